```python
import math
import jax, jax.numpy as jnp
from jax import lax
import numpy as np

D_MODEL = 4096
BATCH = 4
SEQ = 2048
DEPTH = 1
DEC_BATCH = 128
DEC_SEQ = 8
PAST_LEN = 16384
PAGE_SIZE = 128

D_LRU = D_MODEL // 2
LRU_HEADS = 16
LRU_BLK = D_LRU // LRU_HEADS
CONV_W = 4
C_RG = 8.0
D_SSM = D_MODEL // 2
SSM_CG = 16
SSM_GROUPS = D_SSM // SSM_CG
SSM_P = 64
D_FF = ((8 * D_MODEL // 3) + 255) // 256 * 256
D_IN = 2 * D_LRU + D_SSM + 2 * D_MODEL
EPS = 1e-6

kernel_name = 'hawk_s5_macaron_sandwich_decode_step'


def _rms_norm(x, g):
    xf = x.astype(jnp.float32)
    y = xf * lax.rsqrt(jnp.mean(xf * xf, axis=-1, keepdims=True) + EPS)
    return (y * g.astype(jnp.float32)).astype(x.dtype)


def _half_ffn(x, pre_g, post_g, w_gate, w_up, w_down):
    h = _rms_norm(x, pre_g)
    f = (jax.nn.silu(h @ w_gate) * (h @ w_up)) @ w_down
    return x + 0.5 * _rms_norm(f, post_g)


def _lin_combine(e1, e2):
    a1, b1 = e1
    a2, b2 = e2
    return a1 * a2, a2 * b1 + b2


def _cplx_combine(e1, e2):
    ar1, ai1, br1, bi1 = e1
    ar2, ai2, br2, bi2 = e2
    return (ar1 * ar2 - ai1 * ai2,
            ar1 * ai2 + ai1 * ar2,
            ar2 * br1 - ai2 * bi1 + br2,
            ar2 * bi1 + ai2 * br1 + bi2)


def _rg_lru_branch(xa, conv_buf, h0, conv_w, conv_b, w_rg, b_rg, w_ig, b_ig, lam):
    n, t, _ = xa.shape
    xp = jnp.concatenate([conv_buf.astype(xa.dtype), xa], axis=1)
    new_buf = xp[:, -(CONV_W - 1):]
    xc = conv_b + xp[:, 0:t] * conv_w[0]
    for k in range(1, CONV_W):
        xc = xc + xp[:, k:k + t] * conv_w[k]
    xh = xc.reshape(n, t, LRU_HEADS, LRU_BLK)
    r = jax.nn.sigmoid((jnp.einsum('nthi,hij->nthj', xh, w_rg).reshape(n, t, D_LRU) + b_rg).astype(jnp.float32))
    i = jax.nn.sigmoid((jnp.einsum('nthi,hij->nthj', xh, w_ig).reshape(n, t, D_LRU) + b_ig).astype(jnp.float32))
    log_a = -C_RG * r * jax.nn.softplus(-lam.astype(jnp.float32))
    a = jnp.exp(log_a)
    mult = jnp.sqrt(-jnp.expm1(2.0 * log_a))
    b = mult * (i * xc.astype(jnp.float32))
    a_t = jnp.swapaxes(a, 0, 1)
    b_t = jnp.swapaxes(b, 0, 1)
    b_t = b_t.at[0].add(a_t[0] * h0.astype(jnp.float32))
    _, h = lax.associative_scan(_lin_combine, (a_t, b_t), axis=0)
    y = jnp.swapaxes(h, 0, 1).astype(xa.dtype)
    return y, h[-1], new_buf


def _s5_branch(u, s_re, s_im, a_re, a_im, log_dt, b_re, b_im, c_re, c_im, d_skip, w_glu, b_glu):
    f32 = jnp.float32
    n, t, _ = u.shape
    a_re = a_re.astype(f32)
    a_im = a_im.astype(f32)
    dt = jnp.exp(log_dt.astype(f32))[:, None]
    mag = jnp.exp(a_re * dt)
    abar_re = mag * jnp.cos(a_im * dt)
    abar_im = mag * jnp.sin(a_im * dt)
    den = a_re * a_re + a_im * a_im
    nr = abar_re - 1.0
    ni = abar_im
    coef_re = ((nr * a_re + ni * a_im) / den)[..., None]
    coef_im = ((ni * a_re - nr * a_im) / den)[..., None]
    b_re = b_re.astype(f32)
    b_im = b_im.astype(f32)
    bb_re = coef_re * b_re - coef_im * b_im
    bb_im = coef_re * b_im + coef_im * b_re
    uf = u.astype(f32)
    ug = uf.reshape(n, t, SSM_GROUPS, SSM_CG)
    bu_re = jnp.einsum('ntgc,gpc->tngp', ug, bb_re)
    bu_im = jnp.einsum('ntgc,gpc->tngp', ug, bb_im)
    h0r = s_re.astype(f32)
    h0i = s_im.astype(f32)
    bu_re = bu_re.at[0].add(abar_re * h0r - abar_im * h0i)
    bu_im = bu_im.at[0].add(abar_re * h0i + abar_im * h0r)
    ar_t = jnp.broadcast_to(abar_re, (t, 1, SSM_GROUPS, SSM_P))
    ai_t = jnp.broadcast_to(abar_im, (t, 1, SSM_GROUPS, SSM_P))
    _, _, h_re, h_im = lax.associative_scan(_cplx_combine, (ar_t, ai_t, bu_re, bu_im), axis=0)
    y = (jnp.einsum('tngp,gcp->ntgc', h_re, c_re.astype(f32))
         - jnp.einsum('tngp,gcp->ntgc', h_im, c_im.astype(f32)))
    y = y.reshape(n, t, D_SSM) + d_skip.astype(f32) * uf
    g = jax.nn.gelu(y)
    out = g * jax.nn.sigmoid(g @ w_glu.astype(f32) + b_glu.astype(f32))
    return out.astype(u.dtype), h_re[-1], h_im[-1]


def _decoder_layer(x, lru_h, conv_buf, ssm_re, ssm_im, p):
    x = _half_ffn(x, p['ffn1_pre_g'], p['ffn1_post_g'], p['ffn1_w_gate'], p['ffn1_w_up'], p['ffn1_w_down'])
    u = _rms_norm(x, p['mix_pre_g'])
    z = u @ p['w_in']
    xa = z[..., :D_LRU]
    ga = z[..., D_LRU:2 * D_LRU]
    xb = z[..., 2 * D_LRU:2 * D_LRU + D_SSM]
    gl = z[..., 2 * D_LRU + D_SSM:]
    ya, new_h, new_buf = _rg_lru_branch(xa, conv_buf, lru_h, p['conv_w'], p['conv_b'], p['w_rg'], p['b_rg'],
                                        p['w_ig'], p['b_ig'], p['lru_lambda'])
    ya = ya * jax.nn.gelu(ga)
    yb, new_re, new_im = _s5_branch(xb, ssm_re, ssm_im, p['ssm_a_re'], p['ssm_a_im'], p['ssm_log_dt'],
                                    p['ssm_b_re'], p['ssm_b_im'], p['ssm_c_re'], p['ssm_c_im'], p['ssm_d'],
                                    p['w_glu'], p['b_glu'])
    gates = jax.nn.sigmoid(gl.astype(jnp.float32)).astype(x.dtype)
    merged = gates[..., :D_MODEL] * (ya @ p['w_out_a']) + gates[..., D_MODEL:] * (yb @ p['w_out_b'])
    x = x + _rms_norm(merged @ p['w_o'], p['mix_post_g'])
    x = _half_ffn(x, p['ffn2_pre_g'], p['ffn2_post_g'], p['ffn2_w_gate'], p['ffn2_w_up'], p['ffn2_w_down'])
    return x, new_h, new_buf, new_re, new_im


def setup_inputs(seed: int = 0) -> dict:
    key = jax.random.key(seed)
    ks = iter(jax.random.split(key, 48))
    L = DEPTH
    f32 = jnp.float32

    def nrm(shape, scale):
        return jax.random.normal(next(ks), shape, f32) * scale

    def gain(shape):
        return 1.0 + 0.05 * jax.random.normal(next(ks), shape, f32)

    d = {}
    d['x_prompt'] = nrm((BATCH, SEQ, D_MODEL), 1.0)
    d['x_sample'] = nrm((DEC_BATCH, DEC_SEQ, D_MODEL), 1.0)
    d['state_lru_h'] = nrm((L, DEC_BATCH, D_LRU), 0.5)
    d['state_conv'] = nrm((L, DEC_BATCH, CONV_W - 1, D_LRU), 1.0)
    d['state_ssm_re'] = nrm((L, DEC_BATCH, SSM_GROUPS, SSM_P), 0.1)
    d['state_ssm_im'] = nrm((L, DEC_BATCH, SSM_GROUPS, SSM_P), 0.1)
    d['ffn1_pre_g'] = gain((L, D_MODEL))
    d['ffn1_post_g'] = gain((L, D_MODEL))
    d['ffn1_w_gate'] = nrm((L, D_MODEL, D_FF), D_MODEL ** -0.5)
    d['ffn1_w_up'] = nrm((L, D_MODEL, D_FF), D_MODEL ** -0.5)
    d['ffn1_w_down'] = nrm((L, D_FF, D_MODEL), D_FF ** -0.5)
    d['mix_pre_g'] = gain((L, D_MODEL))
    d['mix_post_g'] = gain((L, D_MODEL))
    d['w_in'] = nrm((L, D_MODEL, D_IN), D_MODEL ** -0.5)
    d['conv_w'] = nrm((L, CONV_W, D_LRU), CONV_W ** -0.5)
    d['conv_b'] = nrm((L, D_LRU), 0.02)
    d['w_rg'] = nrm((L, LRU_HEADS, LRU_BLK, LRU_BLK), LRU_BLK ** -0.5)
    d['b_rg'] = nrm((L, D_LRU), 0.02)
    d['w_ig'] = nrm((L, LRU_HEADS, LRU_BLK, LRU_BLK), LRU_BLK ** -0.5)
    d['b_ig'] = nrm((L, D_LRU), 0.02)
    a_pow = jax.random.uniform(next(ks), (L, D_LRU), f32, minval=0.9, maxval=0.999)
    s = a_pow ** (1.0 / C_RG)
    d['lru_lambda'] = jnp.log(s) - jnp.log1p(-s)
    d['ssm_a_re'] = -0.5 + 0.01 * jax.random.normal(next(ks), (L, SSM_GROUPS, SSM_P), f32)
    d['ssm_a_im'] = (math.pi * jnp.arange(SSM_P, dtype=f32))[None, None, :] + 0.01 * jax.random.normal(next(ks), (L, SSM_GROUPS, SSM_P), f32)
    d['ssm_log_dt'] = jax.random.uniform(next(ks), (L, SSM_GROUPS), f32, minval=math.log(1e-3), maxval=math.log(1e-1))
    d['ssm_b_re'] = nrm((L, SSM_GROUPS, SSM_P, SSM_CG), (2 * SSM_CG) ** -0.5)
    d['ssm_b_im'] = nrm((L, SSM_GROUPS, SSM_P, SSM_CG), (2 * SSM_CG) ** -0.5)
    d['ssm_c_re'] = nrm((L, SSM_GROUPS, SSM_CG, SSM_P), (2 * SSM_P) ** -0.5)
    d['ssm_c_im'] = nrm((L, SSM_GROUPS, SSM_CG, SSM_P), (2 * SSM_P) ** -0.5)
    d['ssm_d'] = nrm((L, D_SSM), 1.0)
    d['w_glu'] = nrm((L, D_SSM, D_SSM), D_SSM ** -0.5)
    d['b_glu'] = nrm((L, D_SSM), 0.02)
    d['w_out_a'] = nrm((L, D_LRU, D_MODEL), D_LRU ** -0.5)
    d['w_out_b'] = nrm((L, D_SSM, D_MODEL), D_SSM ** -0.5)
    d['w_o'] = nrm((L, D_MODEL, D_MODEL), D_MODEL ** -0.5)
    d['ffn2_pre_g'] = gain((L, D_MODEL))
    d['ffn2_post_g'] = gain((L, D_MODEL))
    d['ffn2_w_gate'] = nrm((L, D_MODEL, D_FF), D_MODEL ** -0.5)
    d['ffn2_w_up'] = nrm((L, D_MODEL, D_FF), D_MODEL ** -0.5)
    d['ffn2_w_down'] = nrm((L, D_FF, D_MODEL), D_FF ** -0.5)
    return d


def reference(x_prompt, x_sample, state_lru_h, state_conv, state_ssm_re, state_ssm_im,
              ffn1_pre_g, ffn1_post_g, ffn1_w_gate, ffn1_w_up, ffn1_w_down,
              mix_pre_g, mix_post_g, w_in, conv_w, conv_b, w_rg, b_rg, w_ig, b_ig, lru_lambda,
              ssm_a_re, ssm_a_im, ssm_log_dt, ssm_b_re, ssm_b_im, ssm_c_re, ssm_c_im, ssm_d,
              w_glu, b_glu, w_out_a, w_out_b, w_o,
              ffn2_pre_g, ffn2_post_g, ffn2_w_gate, ffn2_w_up, ffn2_w_down):
    n_p = x_prompt.shape[0]
    sdt = state_lru_h.dtype
    yp = x_prompt
    ys = x_sample
    p_h, p_c, p_re, p_im = [], [], [], []
    s_h, s_c, s_re, s_im = [], [], [], []
    for l in range(DEPTH):
        p = dict(ffn1_pre_g=ffn1_pre_g[l], ffn1_post_g=ffn1_post_g[l], ffn1_w_gate=ffn1_w_gate[l],
                 ffn1_w_up=ffn1_w_up[l], ffn1_w_down=ffn1_w_down[l],
                 mix_pre_g=mix_pre_g[l], mix_post_g=mix_post_g[l], w_in=w_in[l],
                 conv_w=conv_w[l], conv_b=conv_b[l], w_rg=w_rg[l], b_rg=b_rg[l], w_ig=w_ig[l], b_ig=b_ig[l],
                 lru_lambda=lru_lambda[l], ssm_a_re=ssm_a_re[l], ssm_a_im=ssm_a_im[l], ssm_log_dt=ssm_log_dt[l],
                 ssm_b_re=ssm_b_re[l], ssm_b_im=ssm_b_im[l], ssm_c_re=ssm_c_re[l], ssm_c_im=ssm_c_im[l],
                 ssm_d=ssm_d[l], w_glu=w_glu[l], b_glu=b_glu[l], w_out_a=w_out_a[l], w_out_b=w_out_b[l],
                 w_o=w_o[l], ffn2_pre_g=ffn2_pre_g[l], ffn2_post_g=ffn2_post_g[l], ffn2_w_gate=ffn2_w_gate[l],
                 ffn2_w_up=ffn2_w_up[l], ffn2_w_down=ffn2_w_down[l])
        yp, h1, c1, r1, i1 = _decoder_layer(
            yp, jnp.zeros((n_p, D_LRU), sdt), jnp.zeros((n_p, CONV_W - 1, D_LRU), sdt),
            jnp.zeros((n_p, SSM_GROUPS, SSM_P), sdt), jnp.zeros((n_p, SSM_GROUPS, SSM_P), sdt), p)
        ys, h2, c2, r2, i2 = _decoder_layer(
            ys, state_lru_h[l], state_conv[l], state_ssm_re[l], state_ssm_im[l], p)
        p_h.append(h1.astype(sdt)); p_c.append(c1.astype(sdt)); p_re.append(r1.astype(sdt)); p_im.append(i1.astype(sdt))
        s_h.append(h2.astype(sdt)); s_c.append(c2.astype(sdt)); s_re.append(r2.astype(sdt)); s_im.append(i2.astype(sdt))
    prompt_lru_h = jnp.stack(p_h, axis=0)
    prompt_conv = jnp.stack(p_c, axis=0)
    prompt_ssm_re = jnp.stack(p_re, axis=0)
    prompt_ssm_im = jnp.stack(p_im, axis=0)
    sample_lru_h = jnp.stack(s_h, axis=0)
    sample_conv = jnp.stack(s_c, axis=0)
    sample_ssm_re = jnp.stack(s_re, axis=0)
    sample_ssm_im = jnp.stack(s_im, axis=0)
    return (yp, ys, prompt_lru_h, prompt_conv, prompt_ssm_re, prompt_ssm_im,
            sample_lru_h, sample_conv, sample_ssm_re, sample_ssm_im)
```

```python
import functools
import math

import jax
import jax.numpy as jnp
from jax import lax
from jax.experimental import pallas as pl
from jax.experimental.pallas import tpu as pltpu

F32 = jnp.float32
BF16 = jnp.bfloat16

EPS = 1e-6
C_RG = 8.0
CONV_W = 4
LANES = 128
SUBLANES = 8
LRU_BLK = 128
VMEM_LIMIT = 56 * 1024 * 1024


def _cparams(*sem):
    return pltpu.CompilerParams(dimension_semantics=sem, vmem_limit_bytes=VMEM_LIMIT)


def _pick(n, candidates):
    for c in candidates:
        if n % c == 0:
            return c
    raise ValueError(f"no tile in {candidates} divides {n}")


def _rms(x, g):
    return x * lax.rsqrt(jnp.mean(x * x, axis=-1, keepdims=True) + EPS) * g


def _softplus(x):
    return jnp.maximum(x, 0.0) + jnp.log1p(jnp.exp(-jnp.abs(x)))


def _two_group_rows(i, n_prompt_tiles, xp_ref, xs_ref, body):
    @pl.when(i < n_prompt_tiles)
    def _():
        body(xp_ref[...])

    @pl.when(i >= n_prompt_tiles)
    def _():
        body(xs_ref[...])


def _prenorm_kernel(xp_ref, xs_ref, g_ref, h_ref, *, n_prompt_tiles):
    def body(x):
        h_ref[...] = _rms(x, g_ref[...]).astype(h_ref.dtype)

    _two_group_rows(pl.program_id(0), n_prompt_tiles, xp_ref, xs_ref, body)


def _post_ffn1_kernel(xp_ref, xs_ref, f_ref, gpost_ref, gpre_ref, x1_ref, u_ref, *, n_prompt_tiles):
    def body(x):
        x1 = x + 0.5 * _rms(f_ref[...], gpost_ref[...])
        x1_ref[...] = x1
        u_ref[...] = _rms(x1, gpre_ref[...]).astype(u_ref.dtype)

    _two_group_rows(pl.program_id(0), n_prompt_tiles, xp_ref, xs_ref, body)


def _post_mix_kernel(x1_ref, o_ref, gpost_ref, gpre_ref, x2_ref, h_ref):
    x2 = x1_ref[...] + _rms(o_ref[...], gpost_ref[...])
    x2_ref[...] = x2
    h_ref[...] = _rms(x2, gpre_ref[...]).astype(h_ref.dtype)


def _final_kernel(x2_ref, f_ref, g_ref, y_ref):
    y_ref[...] = x2_ref[...] + 0.5 * _rms(f_ref[...], g_ref[...])


def _ffn_up_kernel(h_ref, wg_ref, wu_ref, a_ref):
    h = h_ref[...]
    g = jnp.dot(h, wg_ref[...], preferred_element_type=F32)
    u = jnp.dot(h, wu_ref[...], preferred_element_type=F32)
    a_ref[...] = (jax.nn.silu(g) * u).astype(a_ref.dtype)


def _ffn_down_kernel(a_ref, wd_ref, o_ref):
    p = jnp.dot(a_ref[...], wd_ref[...], preferred_element_type=F32)

    @pl.when(pl.program_id(2) == 0)
    def _():
        o_ref[...] = p

    @pl.when(pl.program_id(2) != 0)
    def _():
        o_ref[...] += p


def _mm_kernel(x_ref, w_ref, o_ref):
    o_ref[...] = jnp.dot(x_ref[...], w_ref[...], preferred_element_type=F32).astype(o_ref.dtype)


def _glu_kernel(y_ref, ycol_ref, w_ref, b_ref, o_ref, g_scr):
    @pl.when(pl.program_id(1) == 0)
    def _():
        g_scr[...] = jax.nn.gelu(y_ref[...]).astype(g_scr.dtype)

    s = jnp.dot(g_scr[...], w_ref[...], preferred_element_type=F32) + b_ref[...]
    o_ref[...] = (jax.nn.gelu(ycol_ref[...]) * jax.nn.sigmoid(s)).astype(o_ref.dtype)


def _merge_kernel(ya_ref, yb_ref, wa_ref, wb_ref, gla_ref, glb_ref, o_ref):
    pa = jnp.dot(ya_ref[...], wa_ref[...], preferred_element_type=F32)
    pb = jnp.dot(yb_ref[...], wb_ref[...], preferred_element_type=F32)
    m = jax.nn.sigmoid(gla_ref[...]) * pa + jax.nn.sigmoid(glb_ref[...]) * pb
    o_ref[...] = m.astype(o_ref.dtype)


def _lru_gates(xc, wrg_ref, brg_ref, wig_ref, big_ref, lam_ref):
    xcb = xc.astype(BF16)
    rs, gs = [], []
    for hh in range(wrg_ref.shape[0]):
        xh = xcb[:, hh * LRU_BLK:(hh + 1) * LRU_BLK]
        rs.append(jnp.dot(xh, wrg_ref[hh], preferred_element_type=F32))
        gs.append(jnp.dot(xh, wig_ref[hh], preferred_element_type=F32))
    r = jax.nn.sigmoid(jnp.concatenate(rs, axis=1) + brg_ref[...])
    i = jax.nn.sigmoid(jnp.concatenate(gs, axis=1) + big_ref[...])
    log_a = -C_RG * r * _softplus(-lam_ref[...])
    a = jnp.exp(log_a)
    mult = jnp.sqrt(-jnp.tanh(log_a) * (a * a + 1.0))
    return a, mult * (i * xc)


def _rglru_prompt_kernel(xa_ref, ga_ref, cw_ref, cb_ref, wrg_ref, brg_ref, wig_ref, big_ref, lam_ref,
                         ya_ref, hl_ref, xcar, hcar):
    tm = xa_ref.shape[0]

    @pl.when(pl.program_id(2) == 0)
    def _():
        xcar[...] = jnp.zeros_like(xcar)
        hcar[...] = jnp.zeros_like(hcar)

    x = xa_ref[...]
    xfull = jnp.concatenate([xcar[...], x], axis=0)
    xcar[...] = x[tm - SUBLANES:, :]
    cw = cw_ref[...]
    xc = cb_ref[...]
    for k in range(CONV_W - 1):
        xs = pltpu.roll(xfull, CONV_W - 1 - k, 0)[SUBLANES:, :]
        xc = xc + xs * cw[k:k + 1, :]
    xc = xc + x * cw[CONV_W - 1:CONV_W, :]

    a, b = _lru_gates(xc, wrg_ref, brg_ref, wig_ref, big_ref, lam_ref)
    row = lax.broadcasted_iota(jnp.int32, a.shape, 0)
    b = jnp.where(row == 0, b + a * hcar[...], b)
    d = 1
    while d < tm:
        m = row >= d
        a_s = jnp.where(m, pltpu.roll(a, d, 0), 1.0)
        b_s = jnp.where(m, pltpu.roll(b, d, 0), 0.0)
        b = a * b_s + b
        a = a * a_s
        d *= 2
    h_last = b[tm - 1:tm, :]
    hcar[...] = h_last
    hl_ref[...] = h_last
    ya_ref[...] = (b * jax.nn.gelu(ga_ref[...])).astype(ya_ref.dtype)


def _rglru_sample_kernel(xa_ref, ga_ref, cst_ref, h0_ref, cw_ref, cb_ref, wrg_ref, brg_ref, wig_ref,
                         big_ref, lam_ref, ya_ref, hl_ref, *, n_seq, n_steps):
    x = xa_ref[...]
    xp = jnp.concatenate([cst_ref[k] for k in range(CONV_W - 1)] + [x], axis=0)
    rows = n_seq * n_steps
    cw = cw_ref[...]
    xc = cb_ref[...]
    for k in range(CONV_W):
        xc = xc + xp[k * n_seq:k * n_seq + rows, :] * cw[k:k + 1, :]
    a, b = _lru_gates(xc, wrg_ref, brg_ref, wig_ref, big_ref, lam_ref)
    h = h0_ref[...]
    for t in range(n_steps):
        sl = slice(t * n_seq, (t + 1) * n_seq)
        h = a[sl, :] * h + b[sl, :]
        ya_ref[sl, :] = (h * jax.nn.gelu(ga_ref[sl, :])).astype(ya_ref.dtype)
    hl_ref[...] = h


def _s5_disc_kernel(are_ref, aim_ref, ldt_ref, abr_ref, abi_ref, cfr_ref, cfi_ref):
    a_re = are_ref[...]
    a_im = aim_ref[...]
    dt = jnp.exp(ldt_ref[...])
    mag = jnp.exp(a_re * dt)
    abr = mag * jnp.cos(a_im * dt)
    abi = mag * jnp.sin(a_im * dt)
    den = a_re * a_re + a_im * a_im
    nr = abr - 1.0
    abr_ref[...] = abr
    abi_ref[...] = abi
    cfr_ref[...] = (nr * a_re + abi * a_im) / den
    cfi_ref[...] = (abi * a_re - nr * a_im) / den


def _s5_prompt_kernel(xb_ref, wb_ref, wc_ref, ar_ref, ai_ref, d_ref, y_ref, sre_ref, sim_ref, cre, cim):
    tm = xb_ref.shape[0]
    hw = ar_ref.shape[1]

    @pl.when(pl.program_id(2) == 0)
    def _():
        cre[...] = jnp.zeros_like(cre)
        cim[...] = jnp.zeros_like(cim)

    xb = xb_ref[...]
    bu = jnp.dot(xb.astype(BF16), wb_ref[...], preferred_element_type=F32)
    hr = bu[:, :hw]
    hi = bu[:, hw:]
    pr = ar_ref[...]
    pi = ai_ref[...]
    row = lax.broadcasted_iota(jnp.int32, hr.shape, 0)
    cr = cre[...]
    ci = cim[...]
    hr = jnp.where(row == 0, hr + (pr * cr - pi * ci), hr)
    hi = jnp.where(row == 0, hi + (pr * ci + pi * cr), hi)
    d = 1
    while d < tm:
        m = row >= d
        sr = jnp.where(m, pltpu.roll(hr, d, 0), 0.0)
        si = jnp.where(m, pltpu.roll(hi, d, 0), 0.0)
        hr, hi = hr + (pr * sr - pi * si), hi + (pr * si + pi * sr)
        pr, pi = pr * pr - pi * pi, 2.0 * (pr * pi)
        d *= 2
    cre[...] = hr[tm - 1:tm, :]
    cim[...] = hi[tm - 1:tm, :]
    sre_ref[...] = hr[tm - 1:tm, :]
    sim_ref[...] = hi[tm - 1:tm, :]
    hcat = jnp.concatenate([hr, hi], axis=1).astype(BF16)
    y = jnp.dot(hcat, wc_ref[...], preferred_element_type=F32)
    y_ref[...] = y + d_ref[...] * xb


def _s5_sample_kernel(xb_ref, wb_ref, wc_ref, ar_ref, ai_ref, d_ref, s0r_ref, s0i_ref,
                      y_ref, sre_ref, sim_ref, bu_scr, h_scr, *, n_seq, n_steps):
    hw = ar_ref.shape[1]
    xb = xb_ref[...]
    bu_scr[...] = jnp.dot(xb.astype(BF16), wb_ref[...], preferred_element_type=F32)
    ar = ar_ref[...]
    ai = ai_ref[...]
    hr = s0r_ref[...]
    hi = s0i_ref[...]
    for t in range(n_steps):
        sl = slice(t * n_seq, (t + 1) * n_seq)
        hr, hi = (ar * hr - ai * hi) + bu_scr[sl, :hw], (ar * hi + ai * hr) + bu_scr[sl, hw:]
        h_scr[sl, :hw] = hr.astype(h_scr.dtype)
        h_scr[sl, hw:] = hi.astype(h_scr.dtype)
    sre_ref[...] = hr
    sim_ref[...] = hi
    y = jnp.dot(h_scr[...], wc_ref[...], preferred_element_type=F32)
    y_ref[...] = y + d_ref[...] * xb


def _half_ffn_matmuls(h, wg, wu, wd, *, bm):
    m, dm = h.shape
    dff = wg.shape[1]
    bn_up = _pick(dff, (256, 128))
    a = pl.pallas_call(
        _ffn_up_kernel,
        grid=(m // bm, dff // bn_up),
        in_specs=[pl.BlockSpec((bm, dm), lambda i, j: (i, 0)),
                  pl.BlockSpec((dm, bn_up), lambda i, j: (0, j)),
                  pl.BlockSpec((dm, bn_up), lambda i, j: (0, j))],
        out_specs=pl.BlockSpec((bm, bn_up), lambda i, j: (i, j)),
        out_shape=jax.ShapeDtypeStruct((m, dff), BF16),
        compiler_params=_cparams("parallel", "arbitrary"),
        name="ffn_up",
    )(h, wg, wu)

    bm_d = _pick(m, (1024, 512, 256, 128))
    bn_d = _pick(dm, (512, 256, 128))
    bk = dff // 2 if (dff // 2) % LANES == 0 else dff
    return pl.pallas_call(
        _ffn_down_kernel,
        grid=(m // bm_d, dm // bn_d, dff // bk),
        in_specs=[pl.BlockSpec((bm_d, bk), lambda i, j, k: (i, k)),
                  pl.BlockSpec((bk, bn_d), lambda i, j, k: (k, j))],
        out_specs=pl.BlockSpec((bm_d, bn_d), lambda i, j, k: (i, j)),
        out_shape=jax.ShapeDtypeStruct((m, dm), F32),
        compiler_params=_cparams("parallel", "parallel", "arbitrary"),
        name="ffn_down",
    )(a, wd)


def _matmul(x, w, *, bm, bn, out_dtype, name):
    m, k = x.shape
    n = w.shape[1]
    return pl.pallas_call(
        _mm_kernel,
        grid=(m // bm, n // bn),
        in_specs=[pl.BlockSpec((bm, k), lambda i, j: (i, 0)),
                  pl.BlockSpec((k, bn), lambda i, j: (0, j))],
        out_specs=pl.BlockSpec((bm, bn), lambda i, j: (i, j)),
        out_shape=jax.ShapeDtypeStruct((m, n), out_dtype),
        compiler_params=_cparams("parallel", "arbitrary"),
        name=name,
    )(x, w)


def kernel(x_prompt, x_sample, state_lru_h, state_conv, state_ssm_re, state_ssm_im, ffn1_pre_g, ffn1_post_g, ffn1_w_gate, ffn1_w_up, ffn1_w_down, mix_pre_g, mix_post_g, w_in, conv_w, conv_b, w_rg, b_rg, w_ig, b_ig, lru_lambda, ssm_a_re, ssm_a_im, ssm_log_dt, ssm_b_re, ssm_b_im, ssm_c_re, ssm_c_im, ssm_d, w_glu, b_glu, w_out_a, w_out_b, w_o, ffn2_pre_g, ffn2_post_g, ffn2_w_gate, ffn2_w_up, ffn2_w_down):
    nb, seq, dm = x_prompt.shape
    db, dseq, _ = x_sample.shape
    depth = state_lru_h.shape[0]
    assert depth == 1, "one decoder layer"
    d_lru = state_lru_h.shape[2]
    n_grp, ssm_p = state_ssm_re.shape[2], state_ssm_re.shape[3]
    d_ssm = ssm_d.shape[1]
    ssm_cg = d_ssm // n_grp
    d_in = w_in.shape[2]
    assert w_rg.shape[2] == LRU_BLK and conv_w.shape[1] == CONV_W
    assert d_in == 2 * d_lru + d_ssm + 2 * dm and seq >= CONV_W - 1 and dseq >= CONV_W - 1

    mp = nb * seq
    ms = db * dseq
    m = mp + ms
    sdt = state_lru_h.dtype

    row2 = lambda v: v.reshape(1, -1)
    bf = lambda v: v[0].astype(BF16)

    xp2 = x_prompt.reshape(mp, dm)
    xs2 = jnp.swapaxes(x_sample, 0, 1).reshape(ms, dm)

    tr = _pick(math.gcd(mp, ms), (256, 128, 64, 32, 16, 8))
    npt = mp // tr
    xp_spec = pl.BlockSpec((tr, dm), lambda i: (jnp.minimum(i, npt - 1), 0))
    xs_spec = pl.BlockSpec((tr, dm), lambda i: (jnp.maximum(i - npt, 0), 0))
    row_spec = pl.BlockSpec((tr, dm), lambda i: (i, 0))
    g_spec = pl.BlockSpec((1, dm), lambda i: (0, 0))

    h1 = pl.pallas_call(
        functools.partial(_prenorm_kernel, n_prompt_tiles=npt),
        grid=(m // tr,),
        in_specs=[xp_spec, xs_spec, g_spec],
        out_specs=row_spec,
        out_shape=jax.ShapeDtypeStruct((m, dm), BF16),
        compiler_params=_cparams("parallel"),
        name="prenorm1",
    )(xp2, xs2, row2(ffn1_pre_g))

    bm = _pick(m, (1536, 1024, 768, 512, 384, 256, 128))
    f1 = _half_ffn_matmuls(h1, bf(ffn1_w_gate), bf(ffn1_w_up), bf(ffn1_w_down), bm=bm)

    x1, u = pl.pallas_call(
        functools.partial(_post_ffn1_kernel, n_prompt_tiles=npt),
        grid=(m // tr,),
        in_specs=[xp_spec, xs_spec, row_spec, g_spec, g_spec],
        out_specs=[row_spec, row_spec],
        out_shape=[jax.ShapeDtypeStruct((m, dm), F32), jax.ShapeDtypeStruct((m, dm), BF16)],
        compiler_params=_cparams("parallel"),
        name="post_ffn1",
    )(xp2, xs2, f1, row2(ffn1_post_g), row2(mix_pre_g))

    bn_in = _pick(math.gcd(d_lru, dm), (512, 256, 128))
    z = _matmul(u, bf(w_in), bm=bm, bn=bn_in, out_dtype=F32, name="in_proj")

    cb = _pick(d_lru, (512, 256, 128))
    ncb = d_lru // cb
    hpb = cb // LRU_BLK
    wrg = bf(w_rg)
    wig = bf(w_ig)
    tm = _pick(seq, (256, 128, 64, 32, 16, 8))
    rpt = seq // tm
    par3 = lambda shape: pl.BlockSpec(shape, lambda c, n, r: (0, c))
    lru_params = (conv_w[0], row2(conv_b), wrg, row2(b_rg), wig, row2(b_ig), row2(lru_lambda))
    ya_p, hl_p = pl.pallas_call(
        _rglru_prompt_kernel,
        grid=(ncb, nb, rpt),
        in_specs=[pl.BlockSpec((tm, cb), lambda c, n, r: (n * rpt + r, c)),
                  pl.BlockSpec((tm, cb), lambda c, n, r: (n * rpt + r, ncb + c)),
                  par3((CONV_W, cb)), par3((1, cb)),
                  pl.BlockSpec((hpb, LRU_BLK, LRU_BLK), lambda c, n, r: (c, 0, 0)), par3((1, cb)),
                  pl.BlockSpec((hpb, LRU_BLK, LRU_BLK), lambda c, n, r: (c, 0, 0)), par3((1, cb)),
                  par3((1, cb))],
        out_specs=[pl.BlockSpec((tm, cb), lambda c, n, r: (n * rpt + r, c)),
                   pl.BlockSpec((None, 1, cb), lambda c, n, r: (n, 0, c))],
        out_shape=[jax.ShapeDtypeStruct((mp, d_lru), BF16), jax.ShapeDtypeStruct((nb, 1, d_lru), F32)],
        scratch_shapes=[pltpu.VMEM((SUBLANES, cb), F32), pltpu.VMEM((1, cb), F32)],
        compiler_params=_cparams("parallel", "arbitrary", "arbitrary"),
        name="rglru_prompt",
    )(z, z, *lru_params)

    assert mp % ms == 0, "sample rows must tile the unified row axis"
    s_blk = mp // ms
    par1 = lambda shape: pl.BlockSpec(shape, lambda c: (0, c))
    cst = jnp.swapaxes(state_conv[0], 0, 1)
    ya_s, hl_s = pl.pallas_call(
        functools.partial(_rglru_sample_kernel, n_seq=db, n_steps=dseq),
        grid=(ncb,),
        in_specs=[pl.BlockSpec((ms, cb), lambda c: (s_blk, c)),
                  pl.BlockSpec((ms, cb), lambda c: (s_blk, ncb + c)),
                  pl.BlockSpec((CONV_W - 1, db, cb), lambda c: (0, 0, c)),
                  par1((db, cb)),
                  par1((CONV_W, cb)), par1((1, cb)),
                  pl.BlockSpec((hpb, LRU_BLK, LRU_BLK), lambda c: (c, 0, 0)), par1((1, cb)),
                  pl.BlockSpec((hpb, LRU_BLK, LRU_BLK), lambda c: (c, 0, 0)), par1((1, cb)),
                  par1((1, cb))],
        out_specs=[pl.BlockSpec((ms, cb), lambda c: (0, c)), par1((db, cb))],
        out_shape=[jax.ShapeDtypeStruct((ms, d_lru), BF16), jax.ShapeDtypeStruct((db, d_lru), F32)],
        compiler_params=_cparams("parallel"),
        name="rglru_sample",
    )(z, z, cst, state_lru_h[0], *lru_params)
    ya = jnp.concatenate([ya_p, ya_s], axis=0)

    abr, abi, cfr, cfi = pl.pallas_call(
        _s5_disc_kernel,
        out_shape=[jax.ShapeDtypeStruct((n_grp, ssm_p), F32)] * 4,
        name="s5_discretise",
    )(ssm_a_re[0], ssm_a_im[0], ssm_log_dt[0].reshape(n_grp, 1))
    bb_re = cfr[..., None] * ssm_b_re[0] - cfi[..., None] * ssm_b_im[0]
    bb_im = cfr[..., None] * ssm_b_im[0] + cfi[..., None] * ssm_b_re[0]

    gpb = LANES // ssm_cg
    nj = n_grp // gpb
    hw = gpb * ssm_p
    eye = jnp.eye(gpb, dtype=F32)

    def blockdiag_in(w):
        w5 = w.reshape(nj, gpb, 1, ssm_p, ssm_cg) * eye[None, :, :, None, None]
        return jnp.transpose(w5, (0, 1, 4, 2, 3)).reshape(nj, gpb * ssm_cg, hw)

    def blockdiag_out(w):
        w5 = w.reshape(nj, gpb, 1, ssm_cg, ssm_p) * eye[None, :, :, None, None]
        return jnp.transpose(w5, (0, 2, 4, 1, 3)).reshape(nj, hw, gpb * ssm_cg)

    wb = jnp.concatenate([blockdiag_in(bb_re), blockdiag_in(bb_im)], axis=2).astype(BF16)
    wc = jnp.concatenate([blockdiag_out(ssm_c_re[0]), -blockdiag_out(ssm_c_im[0])], axis=1).astype(BF16)
    abr3 = abr.reshape(nj, 1, hw)
    abi3 = abi.reshape(nj, 1, hw)
    xb_blk = (2 * d_lru) // LANES
    dsk = row2(ssm_d)

    s5_w_specs3 = [pl.BlockSpec((None, LANES, 2 * hw), lambda j, n, r: (j, 0, 0)),
                   pl.BlockSpec((None, 2 * hw, LANES), lambda j, n, r: (j, 0, 0)),
                   pl.BlockSpec((None, 1, hw), lambda j, n, r: (j, 0, 0)),
                   pl.BlockSpec((None, 1, hw), lambda j, n, r: (j, 0, 0)),
                   pl.BlockSpec((1, LANES), lambda j, n, r: (0, j))]
    y_p, sre_p, sim_p = pl.pallas_call(
        _s5_prompt_kernel,
        grid=(nj, nb, rpt),
        in_specs=[pl.BlockSpec((tm, LANES), lambda j, n, r: (n * rpt + r, xb_blk + j))] + s5_w_specs3,
        out_specs=[pl.BlockSpec((tm, LANES), lambda j, n, r: (n * rpt + r, j)),
                   pl.BlockSpec((None, 1, hw), lambda j, n, r: (n, 0, j)),
                   pl.BlockSpec((None, 1, hw), lambda j, n, r: (n, 0, j))],
        out_shape=[jax.ShapeDtypeStruct((mp, d_ssm), F32),
                   jax.ShapeDtypeStruct((nb, 1, n_grp * ssm_p), F32),
                   jax.ShapeDtypeStruct((nb, 1, n_grp * ssm_p), F32)],
        scratch_shapes=[pltpu.VMEM((1, hw), F32), pltpu.VMEM((1, hw), F32)],
        compiler_params=_cparams("parallel", "arbitrary", "arbitrary"),
        name="s5_prompt",
    )(z, wb, wc, abr3, abi3, dsk)

    s5_w_specs1 = [pl.BlockSpec((None, LANES, 2 * hw), lambda j: (j, 0, 0)),
                   pl.BlockSpec((None, 2 * hw, LANES), lambda j: (j, 0, 0)),
                   pl.BlockSpec((None, 1, hw), lambda j: (j, 0, 0)),
                   pl.BlockSpec((None, 1, hw), lambda j: (j, 0, 0)),
                   pl.BlockSpec((1, LANES), lambda j: (0, j))]
    y_s, sre_s, sim_s = pl.pallas_call(
        functools.partial(_s5_sample_kernel, n_seq=db, n_steps=dseq),
        grid=(nj,),
        in_specs=[pl.BlockSpec((ms, LANES), lambda j: (s_blk, xb_blk + j))] + s5_w_specs1
                 + [pl.BlockSpec((db, hw), lambda j: (0, j)), pl.BlockSpec((db, hw), lambda j: (0, j))],
        out_specs=[pl.BlockSpec((ms, LANES), lambda j: (0, j)),
                   pl.BlockSpec((db, hw), lambda j: (0, j)),
                   pl.BlockSpec((db, hw), lambda j: (0, j))],
        out_shape=[jax.ShapeDtypeStruct((ms, d_ssm), F32),
                   jax.ShapeDtypeStruct((db, n_grp * ssm_p), F32),
                   jax.ShapeDtypeStruct((db, n_grp * ssm_p), F32)],
        scratch_shapes=[pltpu.VMEM((ms, 2 * hw), F32), pltpu.VMEM((ms, 2 * hw), BF16)],
        compiler_params=_cparams("parallel"),
        name="s5_sample",
    )(z, wb, wc, abr3, abi3, dsk,
      state_ssm_re[0].reshape(db, n_grp * ssm_p), state_ssm_im[0].reshape(db, n_grp * ssm_p))
    y = jnp.concatenate([y_p, y_s], axis=0)

    bm_g = _pick(m, (1024, 512, 256, 128))
    bn_g = _pick(d_ssm, (512, 256, 128))
    yb = pl.pallas_call(
        _glu_kernel,
        grid=(m // bm_g, d_ssm // bn_g),
        in_specs=[pl.BlockSpec((bm_g, d_ssm), lambda i, j: (i, 0)),
                  pl.BlockSpec((bm_g, bn_g), lambda i, j: (i, j)),
                  pl.BlockSpec((d_ssm, bn_g), lambda i, j: (0, j)),
                  pl.BlockSpec((1, bn_g), lambda i, j: (0, j))],
        out_specs=pl.BlockSpec((bm_g, bn_g), lambda i, j: (i, j)),
        out_shape=jax.ShapeDtypeStruct((m, d_ssm), BF16),
        scratch_shapes=[pltpu.VMEM((bm_g, d_ssm), BF16)],
        compiler_params=_cparams("parallel", "arbitrary"),
        name="s5_glu",
    )(y, y, bf(w_glu), row2(b_glu))

    bn_m = bn_in
    gla_blk = (2 * d_lru + d_ssm) // bn_m
    glb_blk = gla_blk + dm // bn_m
    merged = pl.pallas_call(
        _merge_kernel,
        grid=(m // bm_g, dm // bn_m),
        in_specs=[pl.BlockSpec((bm_g, d_lru), lambda i, j: (i, 0)),
                  pl.BlockSpec((bm_g, d_ssm), lambda i, j: (i, 0)),
                  pl.BlockSpec((d_lru, bn_m), lambda i, j: (0, j)),
                  pl.BlockSpec((d_ssm, bn_m), lambda i, j: (0, j)),
                  pl.BlockSpec((bm_g, bn_m), lambda i, j: (i, gla_blk + j)),
                  pl.BlockSpec((bm_g, bn_m), lambda i, j: (i, glb_blk + j))],
        out_specs=pl.BlockSpec((bm_g, bn_m), lambda i, j: (i, j)),
        out_shape=jax.ShapeDtypeStruct((m, dm), BF16),
        compiler_params=_cparams("parallel", "arbitrary"),
        name="gated_merge",
    )(ya, yb, bf(w_out_a), bf(w_out_b), z, z)

    o = _matmul(merged, bf(w_o), bm=bm, bn=bn_in, out_dtype=F32, name="o_proj")

    x2, h2 = pl.pallas_call(
        _post_mix_kernel,
        grid=(m // tr,),
        in_specs=[row_spec, row_spec, g_spec, g_spec],
        out_specs=[row_spec, row_spec],
        out_shape=[jax.ShapeDtypeStruct((m, dm), F32), jax.ShapeDtypeStruct((m, dm), BF16)],
        compiler_params=_cparams("parallel"),
        name="post_mix",
    )(x1, o, row2(mix_post_g), row2(ffn2_pre_g))

    f2 = _half_ffn_matmuls(h2, bf(ffn2_w_gate), bf(ffn2_w_up), bf(ffn2_w_down), bm=bm)

    def final(rows, first_tile):
        return pl.pallas_call(
            _final_kernel,
            grid=(rows // tr,),
            in_specs=[pl.BlockSpec((tr, dm), lambda i: (first_tile + i, 0)),
                      pl.BlockSpec((tr, dm), lambda i: (first_tile + i, 0)),
                      g_spec],
            out_specs=row_spec,
            out_shape=jax.ShapeDtypeStruct((rows, dm), F32),
            compiler_params=_cparams("parallel"),
            name="final_residual",
        )(x2, f2, row2(ffn2_post_g))

    y_prompt = final(mp, 0).reshape(nb, seq, dm)
    y_sample = jnp.swapaxes(final(ms, npt).reshape(dseq, db, dm), 0, 1)

    za_p = z[:mp, :d_lru].reshape(nb, seq, d_lru)
    za_s = z[mp:, :d_lru].reshape(dseq, db, d_lru)
    prompt_conv = za_p[:, seq - (CONV_W - 1):, :]
    sample_conv = jnp.swapaxes(za_s[dseq - (CONV_W - 1):], 0, 1)
    st = lambda v, n: v.reshape(1, n, n_grp, ssm_p).astype(sdt)
    return (y_prompt, y_sample,
            hl_p.reshape(1, nb, d_lru).astype(sdt), prompt_conv[None].astype(sdt),
            st(sre_p, nb), st(sim_p, nb),
            hl_s.reshape(1, db, d_lru).astype(sdt), sample_conv[None].astype(sdt),
            st(sre_s, db), st(sim_s, db))
```

```python
import functools
import math

import jax
import jax.numpy as jnp
from jax import lax
from jax.experimental import pallas as pl
from jax.experimental.pallas import tpu as pltpu

F32 = jnp.float32
BF16 = jnp.bfloat16

EPS = 1e-6
C_RG = 8.0
CONV_W = 4
LANES = 128
SUBLANES = 8
LRU_BLK = 128
VMEM_LIMIT = 56 * 1024 * 1024


def _cparams(*sem):
    return pltpu.CompilerParams(dimension_semantics=sem, vmem_limit_bytes=VMEM_LIMIT)


def _pick(n, candidates):
    for c in candidates:
        if n % c == 0:
            return c
    raise ValueError(f"no tile in {candidates} divides {n}")


def _rms(x, g):
    return x * lax.rsqrt(jnp.mean(x * x, axis=-1, keepdims=True) + EPS) * g


def _softplus(x):
    return jnp.maximum(x, 0.0) + jnp.log1p(jnp.exp(-jnp.abs(x)))


def _two_group_rows(i, n_prompt_tiles, xp_ref, xs_ref, body):
    @pl.when(i < n_prompt_tiles)
    def _():
        body(xp_ref[...])

    @pl.when(i >= n_prompt_tiles)
    def _():
        body(xs_ref[...])


def _prenorm_kernel(xp_ref, xs_ref, g_ref, h_ref, *, n_prompt_tiles):
    def body(x):
        h_ref[...] = _rms(x, g_ref[...]).astype(h_ref.dtype)

    _two_group_rows(pl.program_id(0), n_prompt_tiles, xp_ref, xs_ref, body)


def _post_ffn1_kernel(xp_ref, xs_ref, f_ref, gpost_ref, gpre_ref, x1_ref, u_ref, *, n_prompt_tiles):
    def body(x):
        x1 = x + 0.5 * _rms(f_ref[...], gpost_ref[...])
        x1_ref[...] = x1
        u_ref[...] = _rms(x1, gpre_ref[...]).astype(u_ref.dtype)

    _two_group_rows(pl.program_id(0), n_prompt_tiles, xp_ref, xs_ref, body)


def _post_mix_kernel(x1_ref, o_ref, gpost_ref, gpre_ref, x2_ref, h_ref):
    x2 = x1_ref[...] + _rms(o_ref[...], gpost_ref[...])
    x2_ref[...] = x2
    h_ref[...] = _rms(x2, gpre_ref[...]).astype(h_ref.dtype)


def _final_kernel(x2_ref, f_ref, g_ref, y_ref):
    y_ref[...] = x2_ref[...] + 0.5 * _rms(f_ref[...], g_ref[...])


def _ffn_up_kernel(h_ref, wg_ref, wu_ref, a_ref):
    h = h_ref[...]
    g = jnp.dot(h, wg_ref[...].astype(BF16), preferred_element_type=F32)
    u = jnp.dot(h, wu_ref[...].astype(BF16), preferred_element_type=F32)
    a_ref[...] = (jax.nn.silu(g) * u).astype(a_ref.dtype)


def _ffn_down_kernel(a_ref, wd_ref, o_ref):
    p = jnp.dot(a_ref[...], wd_ref[...], preferred_element_type=F32)

    @pl.when(pl.program_id(2) == 0)
    def _():
        o_ref[...] = p

    @pl.when(pl.program_id(2) != 0)
    def _():
        o_ref[...] += p


def _mm_kernel(x_ref, w_ref, o_ref):
    w = w_ref[...].astype(BF16)
    o_ref[...] = jnp.dot(x_ref[...], w, preferred_element_type=F32).astype(o_ref.dtype)


def _glu_kernel(yp_ref, ys_ref, ypc_ref, ysc_ref, w_ref, b_ref, o_ref, g_scr, *, n_prompt_tiles):
    i = pl.program_id(0)

    def gelu_rows(y):
        @pl.when(pl.program_id(1) == 0)
        def _():
            g_scr[...] = jax.nn.gelu(y).astype(g_scr.dtype)

    _two_group_rows(i, n_prompt_tiles, yp_ref, ys_ref, gelu_rows)
    s = jnp.dot(g_scr[...], w_ref[...].astype(BF16), preferred_element_type=F32) + b_ref[...]
    gate = jax.nn.sigmoid(s)

    def emit(ycol):
        o_ref[...] = (jax.nn.gelu(ycol) * gate).astype(o_ref.dtype)

    _two_group_rows(i, n_prompt_tiles, ypc_ref, ysc_ref, emit)


def _merge_kernel(yap_ref, yas_ref, yb_ref, wa_ref, wb_ref, gla_ref, glb_ref, o_ref, *, n_prompt_tiles):
    pb = jnp.dot(yb_ref[...], wb_ref[...].astype(BF16), preferred_element_type=F32)
    gb = jax.nn.sigmoid(glb_ref[...]) * pb

    def body(ya):
        pa = jnp.dot(ya, wa_ref[...].astype(BF16), preferred_element_type=F32)
        o_ref[...] = (jax.nn.sigmoid(gla_ref[...]) * pa + gb).astype(o_ref.dtype)

    _two_group_rows(pl.program_id(0), n_prompt_tiles, yap_ref, yas_ref, body)


def _lru_gates(xc, wrg_ref, brg_ref, wig_ref, big_ref, lam_ref):
    xcb = xc.astype(BF16)
    rs, gs = [], []
    for hh in range(wrg_ref.shape[0]):
        xh = xcb[:, hh * LRU_BLK:(hh + 1) * LRU_BLK]
        rs.append(jnp.dot(xh, wrg_ref[hh], preferred_element_type=F32))
        gs.append(jnp.dot(xh, wig_ref[hh], preferred_element_type=F32))
    r = jax.nn.sigmoid(jnp.concatenate(rs, axis=1) + brg_ref[...])
    i = jax.nn.sigmoid(jnp.concatenate(gs, axis=1) + big_ref[...])
    log_a = -C_RG * r * _softplus(-lam_ref[...])
    a = jnp.exp(log_a)
    mult = jnp.sqrt(-jnp.tanh(log_a) * (a * a + 1.0))
    return a, mult * (i * xc)


def _rglru_prompt_kernel(xa_ref, ga_ref, cw_ref, cb_ref, wrg_ref, brg_ref, wig_ref, big_ref, lam_ref,
                         ya_ref, hl_ref, xcar, hcar):
    tm = xa_ref.shape[0]

    @pl.when(pl.program_id(2) == 0)
    def _():
        xcar[...] = jnp.zeros_like(xcar)
        hcar[...] = jnp.zeros_like(hcar)

    x = xa_ref[...]
    xfull = jnp.concatenate([xcar[...], x], axis=0)
    xcar[...] = x[tm - SUBLANES:, :]
    cw = cw_ref[...]
    xc = cb_ref[...]
    for k in range(CONV_W - 1):
        xs = pltpu.roll(xfull, CONV_W - 1 - k, 0)[SUBLANES:, :]
        xc = xc + xs * cw[k:k + 1, :]
    xc = xc + x * cw[CONV_W - 1:CONV_W, :]

    a, b = _lru_gates(xc, wrg_ref, brg_ref, wig_ref, big_ref, lam_ref)
    cb = a.shape[1]
    n_slab = tm // SUBLANES
    a = a.reshape(n_slab, SUBLANES, cb)
    b = b.reshape(n_slab, SUBLANES, cb)
    sub = lax.broadcasted_iota(jnp.int32, (1, SUBLANES, cb), 1)
    d = 1
    while d < SUBLANES:
        m = sub >= d
        a_s = jnp.where(m, pltpu.roll(a, d, 1), 1.0)
        b_s = jnp.where(m, pltpu.roll(b, d, 1), 0.0)
        b = a * b_s + b
        a = a * a_s
        d *= 2
    h = hcar[...]
    hs = []
    for s in range(n_slab):
        h_slab = b[s] + a[s] * jnp.broadcast_to(h, (SUBLANES, cb))
        h = h_slab[SUBLANES - 1:SUBLANES, :]
        hs.append(h_slab)
    hcar[...] = h
    hl_ref[...] = h
    ya_ref[...] = (jnp.concatenate(hs, axis=0) * jax.nn.gelu(ga_ref[...])).astype(ya_ref.dtype)


def _rglru_sample_kernel(xa_ref, ga_ref, cst_ref, h0_ref, cw_ref, cb_ref, wrg_ref, brg_ref, wig_ref,
                         big_ref, lam_ref, ya_ref, hl_ref, *, n_seq, n_steps):
    x = xa_ref[...]
    xp = jnp.concatenate([cst_ref[k] for k in range(CONV_W - 1)] + [x], axis=0)
    rows = n_seq * n_steps
    cw = cw_ref[...]
    xc = cb_ref[...]
    for k in range(CONV_W):
        xc = xc + xp[k * n_seq:k * n_seq + rows, :] * cw[k:k + 1, :]
    a, b = _lru_gates(xc, wrg_ref, brg_ref, wig_ref, big_ref, lam_ref)
    h = h0_ref[...]
    for t in range(n_steps):
        sl = slice(t * n_seq, (t + 1) * n_seq)
        h = a[sl, :] * h + b[sl, :]
        ya_ref[sl, :] = (h * jax.nn.gelu(ga_ref[sl, :])).astype(ya_ref.dtype)
    hl_ref[...] = h


def _s5_disc_kernel(are_ref, aim_ref, ldt_ref, abr_ref, abi_ref, cfr_ref, cfi_ref):
    a_re = are_ref[...]
    a_im = aim_ref[...]
    dt = jnp.exp(ldt_ref[...])
    mag = jnp.exp(a_re * dt)
    abr = mag * jnp.cos(a_im * dt)
    abi = mag * jnp.sin(a_im * dt)
    den = a_re * a_re + a_im * a_im
    nr = abr - 1.0
    abr_ref[...] = abr
    abi_ref[...] = abi
    cr = (nr * a_re + abi * a_im) / den
    ci = (abi * a_re - nr * a_im) / den
    for q in range(cfr_ref.shape[0]):
        cfr_ref[q] = cr
        cfi_ref[q] = ci
        cr, ci = abr * cr - abi * ci, abr * ci + abi * cr


def _s5_prompt_kernel(*refs, n_par):
    xb_refs = refs[:n_par]
    wb_ref, wc_ref, ar_ref, ai_ref, d_ref, y_ref, sre_ref, sim_ref, cre, cim = refs[n_par:]

    @pl.when(pl.program_id(2) == 0)
    def _():
        cre[...] = jnp.zeros_like(cre)
        cim[...] = jnp.zeros_like(cim)

    for p in range(n_par):
        y, cr, ci = _s5_prompt_tile(xb_refs[p][...], wb_ref, wc_ref, ar_ref, ai_ref, d_ref, cre[p], cim[p])
        y_ref[p] = y
        cre[p] = cr
        cim[p] = ci
        sre_ref[p] = cr
        sim_ref[p] = ci


def _s5_prompt_tile(xb, wb_ref, wc_ref, ar_ref, ai_ref, d_ref, cr, ci):
    tm = xb.shape[0]
    hw = ar_ref.shape[1]
    n_slab = tm // SUBLANES
    n_lags = wb_ref.shape[0] // xb.shape[1]
    xb3 = xb.reshape(n_slab, SUBLANES, xb.shape[1])
    subx = lax.broadcasted_iota(jnp.int32, (1, SUBLANES, xb.shape[1]), 1)
    lagged = [xb3] + [jnp.where(subx >= q, pltpu.roll(xb3, q, 1), 0.0) for q in range(1, n_lags)]
    xs = jnp.concatenate(lagged, axis=2).reshape(tm, n_lags * xb.shape[1])
    bu = jnp.dot(xs.astype(BF16), wb_ref[...], preferred_element_type=F32)
    hr = bu[:, :hw].reshape(n_slab, SUBLANES, hw)
    hi = bu[:, hw:].reshape(n_slab, SUBLANES, hw)
    sub = lax.broadcasted_iota(jnp.int32, (1, SUBLANES, hw), 1)
    pr = ar_ref[...].reshape(1, 1, hw)
    pi = ai_ref[...].reshape(1, 1, hw)
    tr = jnp.where(sub == 0, pr, 0.0)
    ti = jnp.where(sub == 0, pi, 0.0)
    d = 1
    while d < SUBLANES:
        mr = jnp.where(sub >= d, pr, 0.0)
        mi = jnp.where(sub >= d, pi, 0.0)
        if d >= n_lags:
            sr = pltpu.roll(hr, d, 1)
            si = pltpu.roll(hi, d, 1)
            hr, hi = hr + (mr * sr - mi * si), hi + (mr * si + mi * sr)
        sr = pltpu.roll(tr, d, 1)
        si = pltpu.roll(ti, d, 1)
        tr, ti = tr + (mr * sr - mi * si), ti + (mr * si + mi * sr)
        pr, pi = pr * pr - pi * pi, 2.0 * (pr * pi)
        d *= 2
    tr = tr[0]
    ti = ti[0]
    hrs, his = [], []
    for s in range(n_slab):
        cbr = jnp.broadcast_to(cr, (SUBLANES, hw))
        cbi = jnp.broadcast_to(ci, (SUBLANES, hw))
        sr = hr[s] + (tr * cbr - ti * cbi)
        si = hi[s] + (tr * cbi + ti * cbr)
        cr = sr[SUBLANES - 1:SUBLANES, :]
        ci = si[SUBLANES - 1:SUBLANES, :]
        hrs.append(sr)
        his.append(si)
    hcat = jnp.concatenate([jnp.concatenate(hrs, axis=0), jnp.concatenate(his, axis=0)], axis=1).astype(BF16)
    y = jnp.dot(hcat, wc_ref[...], preferred_element_type=F32)
    return y + d_ref[...] * xb, cr, ci


def _s5_sample_kernel(xb_ref, wb_ref, wc_ref, ar_ref, ai_ref, d_ref, s0r_ref, s0i_ref,
                      y_ref, sre_ref, sim_ref, bu_scr, h_scr, *, n_seq, n_steps):
    hw = ar_ref.shape[1]
    xb = xb_ref[...]
    bu_scr[...] = jnp.dot(xb.astype(BF16), wb_ref[...], preferred_element_type=F32)
    ar = ar_ref[...]
    ai = ai_ref[...]
    hr = s0r_ref[...]
    hi = s0i_ref[...]
    for t in range(n_steps):
        sl = slice(t * n_seq, (t + 1) * n_seq)
        hr, hi = (ar * hr - ai * hi) + bu_scr[sl, :hw], (ar * hi + ai * hr) + bu_scr[sl, hw:]
        h_scr[sl, :hw] = hr.astype(h_scr.dtype)
        h_scr[sl, hw:] = hi.astype(h_scr.dtype)
    sre_ref[...] = hr
    sim_ref[...] = hi
    y = jnp.dot(h_scr[...], wc_ref[...], preferred_element_type=F32)
    y_ref[...] = y + d_ref[...] * xb


def _half_ffn_matmuls(h, wg, wu, wd, *, bm):
    m, dm = h.shape
    dff = wg.shape[1]
    bn_up = _pick(dff, (256, 128))
    a = pl.pallas_call(
        _ffn_up_kernel,
        grid=(m // bm, dff // bn_up),
        in_specs=[pl.BlockSpec((bm, dm), lambda i, j: (i, 0)),
                  pl.BlockSpec((dm, bn_up), lambda i, j: (0, j)),
                  pl.BlockSpec((dm, bn_up), lambda i, j: (0, j))],
        out_specs=pl.BlockSpec((bm, bn_up), lambda i, j: (i, j)),
        out_shape=jax.ShapeDtypeStruct((m, dff), BF16),
        compiler_params=_cparams("parallel", "arbitrary"),
        name="ffn_up",
    )(h, wg, wu)

    bm_d = _pick(m, (1024, 512, 256, 128))
    bn_d = _pick(dm, (512, 256, 128))
    bk = dff // 2 if (dff // 2) % LANES == 0 else dff
    return pl.pallas_call(
        _ffn_down_kernel,
        grid=(m // bm_d, dm // bn_d, dff // bk),
        in_specs=[pl.BlockSpec((bm_d, bk), lambda i, j, k: (i, k)),
                  pl.BlockSpec((bk, bn_d), lambda i, j, k: (k, j))],
        out_specs=pl.BlockSpec((bm_d, bn_d), lambda i, j, k: (i, j)),
        out_shape=jax.ShapeDtypeStruct((m, dm), F32),
        compiler_params=_cparams("parallel", "parallel", "arbitrary"),
        name="ffn_down",
    )(a, wd)


def _matmul(x, w, *, bm, bn, out_dtype, name):
    m, k = x.shape
    n = w.shape[1]
    return pl.pallas_call(
        _mm_kernel,
        grid=(m // bm, n // bn),
        in_specs=[pl.BlockSpec((bm, k), lambda i, j: (i, 0)),
                  pl.BlockSpec((k, bn), lambda i, j: (0, j))],
        out_specs=pl.BlockSpec((bm, bn), lambda i, j: (i, j)),
        out_shape=jax.ShapeDtypeStruct((m, n), out_dtype),
        compiler_params=_cparams("parallel", "arbitrary"),
        name=name,
    )(x, w)


def kernel(x_prompt, x_sample, state_lru_h, state_conv, state_ssm_re, state_ssm_im, ffn1_pre_g, ffn1_post_g, ffn1_w_gate, ffn1_w_up, ffn1_w_down, mix_pre_g, mix_post_g, w_in, conv_w, conv_b, w_rg, b_rg, w_ig, b_ig, lru_lambda, ssm_a_re, ssm_a_im, ssm_log_dt, ssm_b_re, ssm_b_im, ssm_c_re, ssm_c_im, ssm_d, w_glu, b_glu, w_out_a, w_out_b, w_o, ffn2_pre_g, ffn2_post_g, ffn2_w_gate, ffn2_w_up, ffn2_w_down):
    nb, seq, dm = x_prompt.shape
    db, dseq, _ = x_sample.shape
    depth = state_lru_h.shape[0]
    assert depth == 1, "one decoder layer"
    d_lru = state_lru_h.shape[2]
    n_grp, ssm_p = state_ssm_re.shape[2], state_ssm_re.shape[3]
    d_ssm = ssm_d.shape[1]
    ssm_cg = d_ssm // n_grp
    d_in = w_in.shape[2]
    assert w_rg.shape[2] == LRU_BLK and conv_w.shape[1] == CONV_W
    assert d_in == 2 * d_lru + d_ssm + 2 * dm and seq >= CONV_W - 1 and dseq >= CONV_W - 1

    mp = nb * seq
    ms = db * dseq
    m = mp + ms
    sdt = state_lru_h.dtype

    row2 = lambda v: v.reshape(1, -1)
    bf = lambda v: v[0].astype(BF16)

    xp2 = x_prompt.reshape(mp, dm)
    xs2 = jnp.swapaxes(x_sample, 0, 1).reshape(ms, dm)

    tr = _pick(math.gcd(mp, ms), (256, 128, 64, 32, 16, 8))
    npt = mp // tr
    xp_spec = pl.BlockSpec((tr, dm), lambda i: (jnp.minimum(i, npt - 1), 0))
    xs_spec = pl.BlockSpec((tr, dm), lambda i: (jnp.maximum(i - npt, 0), 0))
    row_spec = pl.BlockSpec((tr, dm), lambda i: (i, 0))
    g_spec = pl.BlockSpec((1, dm), lambda i: (0, 0))

    h1 = pl.pallas_call(
        functools.partial(_prenorm_kernel, n_prompt_tiles=npt),
        grid=(m // tr,),
        in_specs=[xp_spec, xs_spec, g_spec],
        out_specs=row_spec,
        out_shape=jax.ShapeDtypeStruct((m, dm), BF16),
        compiler_params=_cparams("parallel"),
        name="prenorm1",
    )(xp2, xs2, row2(ffn1_pre_g))

    bm = _pick(m, (1536, 1024, 768, 512, 384, 256, 128))
    f1 = _half_ffn_matmuls(h1, ffn1_w_gate[0], ffn1_w_up[0], bf(ffn1_w_down), bm=bm)

    x1, u = pl.pallas_call(
        functools.partial(_post_ffn1_kernel, n_prompt_tiles=npt),
        grid=(m // tr,),
        in_specs=[xp_spec, xs_spec, row_spec, g_spec, g_spec],
        out_specs=[row_spec, row_spec],
        out_shape=[jax.ShapeDtypeStruct((m, dm), F32), jax.ShapeDtypeStruct((m, dm), BF16)],
        compiler_params=_cparams("parallel"),
        name="post_ffn1",
    )(xp2, xs2, f1, row2(ffn1_post_g), row2(mix_pre_g))

    bn_in = _pick(math.gcd(d_lru, dm), (512, 256, 128))
    z = _matmul(u, w_in[0], bm=bm, bn=bn_in, out_dtype=F32, name="in_proj")

    cb = _pick(d_lru, (512, 256, 128))
    ncb = d_lru // cb
    hpb = cb // LRU_BLK
    wrg = bf(w_rg)
    wig = bf(w_ig)
    tm = _pick(seq, (512, 256, 128, 64, 32, 16, 8))
    rpt = seq // tm
    par3 = lambda shape: pl.BlockSpec(shape, lambda c, n, r: (0, c))
    lru_params = (conv_w[0], row2(conv_b), wrg, row2(b_rg), wig, row2(b_ig), row2(lru_lambda))
    ya_p, hl_p = pl.pallas_call(
        _rglru_prompt_kernel,
        grid=(ncb, nb, rpt),
        in_specs=[pl.BlockSpec((tm, cb), lambda c, n, r: (n * rpt + r, c)),
                  pl.BlockSpec((tm, cb), lambda c, n, r: (n * rpt + r, ncb + c)),
                  par3((CONV_W, cb)), par3((1, cb)),
                  pl.BlockSpec((hpb, LRU_BLK, LRU_BLK), lambda c, n, r: (c, 0, 0)), par3((1, cb)),
                  pl.BlockSpec((hpb, LRU_BLK, LRU_BLK), lambda c, n, r: (c, 0, 0)), par3((1, cb)),
                  par3((1, cb))],
        out_specs=[pl.BlockSpec((tm, cb), lambda c, n, r: (n * rpt + r, c)),
                   pl.BlockSpec((None, 1, cb), lambda c, n, r: (n, 0, c))],
        out_shape=[jax.ShapeDtypeStruct((mp, d_lru), BF16), jax.ShapeDtypeStruct((nb, 1, d_lru), F32)],
        scratch_shapes=[pltpu.VMEM((SUBLANES, cb), F32), pltpu.VMEM((1, cb), F32)],
        compiler_params=_cparams("parallel", "arbitrary", "arbitrary"),
        name="rglru_prompt",
    )(z, z, *lru_params)

    assert mp % ms == 0, "sample rows must tile the unified row axis"
    s_blk = mp // ms
    par1 = lambda shape: pl.BlockSpec(shape, lambda c: (0, c))
    cst = jnp.swapaxes(state_conv[0], 0, 1)
    ya_s, hl_s = pl.pallas_call(
        functools.partial(_rglru_sample_kernel, n_seq=db, n_steps=dseq),
        grid=(ncb,),
        in_specs=[pl.BlockSpec((ms, cb), lambda c: (s_blk, c)),
                  pl.BlockSpec((ms, cb), lambda c: (s_blk, ncb + c)),
                  pl.BlockSpec((CONV_W - 1, db, cb), lambda c: (0, 0, c)),
                  par1((db, cb)),
                  par1((CONV_W, cb)), par1((1, cb)),
                  pl.BlockSpec((hpb, LRU_BLK, LRU_BLK), lambda c: (c, 0, 0)), par1((1, cb)),
                  pl.BlockSpec((hpb, LRU_BLK, LRU_BLK), lambda c: (c, 0, 0)), par1((1, cb)),
                  par1((1, cb))],
        out_specs=[pl.BlockSpec((ms, cb), lambda c: (0, c)), par1((db, cb))],
        out_shape=[jax.ShapeDtypeStruct((ms, d_lru), BF16), jax.ShapeDtypeStruct((db, d_lru), F32)],
        compiler_params=_cparams("parallel"),
        name="rglru_sample",
    )(z, z, cst, state_lru_h[0], *lru_params)

    n_lags = 4
    abr, abi, cfr, cfi = pl.pallas_call(
        _s5_disc_kernel,
        out_shape=[jax.ShapeDtypeStruct((n_grp, ssm_p), F32)] * 2
                  + [jax.ShapeDtypeStruct((n_lags, n_grp, ssm_p), F32)] * 2,
        name="s5_discretise",
    )(ssm_a_re[0], ssm_a_im[0], ssm_log_dt[0].reshape(n_grp, 1))
    bb_re = cfr[..., None] * ssm_b_re - cfi[..., None] * ssm_b_im
    bb_im = cfr[..., None] * ssm_b_im + cfi[..., None] * ssm_b_re

    gpb = LANES // ssm_cg
    nj = n_grp // gpb
    hw = gpb * ssm_p
    eye = jnp.eye(gpb, dtype=F32)

    def blockdiag_in(w):
        w6 = w.reshape(n_lags, nj, gpb, 1, ssm_p, ssm_cg) * eye[None, None, :, :, None, None]
        return jnp.transpose(w6, (1, 0, 2, 5, 3, 4)).reshape(nj, n_lags * gpb * ssm_cg, hw)

    def blockdiag_out(w):
        w5 = w.reshape(nj, gpb, 1, ssm_cg, ssm_p) * eye[None, :, :, None, None]
        return jnp.transpose(w5, (0, 2, 4, 1, 3)).reshape(nj, hw, gpb * ssm_cg)

    wb = jnp.concatenate([blockdiag_in(bb_re), blockdiag_in(bb_im)], axis=2).astype(BF16)
    wc = jnp.concatenate([blockdiag_out(ssm_c_re[0]), -blockdiag_out(ssm_c_im[0])], axis=1).astype(BF16)
    abr3 = abr.reshape(nj, 1, hw)
    abi3 = abi.reshape(nj, 1, hw)
    xb_blk = (2 * d_lru) // LANES
    dsk = row2(ssm_d)

    s5_w_specs3 = [pl.BlockSpec((None, n_lags * LANES, 2 * hw), lambda j, n, r: (j, 0, 0)),
                   pl.BlockSpec((None, 2 * hw, LANES), lambda j, n, r: (j, 0, 0)),
                   pl.BlockSpec((None, 1, hw), lambda j, n, r: (j, 0, 0)),
                   pl.BlockSpec((None, 1, hw), lambda j, n, r: (j, 0, 0)),
                   pl.BlockSpec((1, LANES), lambda j, n, r: (0, j))]
    n_par = 2 if nb % 2 == 0 else 1
    xb_specs = [pl.BlockSpec((tm, LANES), functools.partial(
        lambda j, n, r, p: ((n * n_par + p) * rpt + r, xb_blk + j), p=p)) for p in range(n_par)]
    y_p, sre_p, sim_p = pl.pallas_call(
        functools.partial(_s5_prompt_kernel, n_par=n_par),
        grid=(nj, nb // n_par, rpt),
        in_specs=xb_specs + s5_w_specs3,
        out_specs=[pl.BlockSpec((n_par, tm, LANES), lambda j, n, r: (n, r, j)),
                   pl.BlockSpec((n_par, 1, hw), lambda j, n, r: (n, 0, j)),
                   pl.BlockSpec((n_par, 1, hw), lambda j, n, r: (n, 0, j))],
        out_shape=[jax.ShapeDtypeStruct((nb, seq, d_ssm), F32),
                   jax.ShapeDtypeStruct((nb, 1, n_grp * ssm_p), F32),
                   jax.ShapeDtypeStruct((nb, 1, n_grp * ssm_p), F32)],
        scratch_shapes=[pltpu.VMEM((n_par, 1, hw), F32), pltpu.VMEM((n_par, 1, hw), F32)],
        compiler_params=_cparams("parallel", "arbitrary", "arbitrary"),
        name="s5_prompt",
    )(*([z] * n_par), wb, wc, abr3, abi3, dsk)
    y_p = y_p.reshape(mp, d_ssm)

    s5_w_specs1 = [pl.BlockSpec((None, LANES, 2 * hw), lambda j: (j, 0, 0)),
                   pl.BlockSpec((None, 2 * hw, LANES), lambda j: (j, 0, 0)),
                   pl.BlockSpec((None, 1, hw), lambda j: (j, 0, 0)),
                   pl.BlockSpec((None, 1, hw), lambda j: (j, 0, 0)),
                   pl.BlockSpec((1, LANES), lambda j: (0, j))]
    y_s, sre_s, sim_s = pl.pallas_call(
        functools.partial(_s5_sample_kernel, n_seq=db, n_steps=dseq),
        grid=(nj,),
        in_specs=[pl.BlockSpec((ms, LANES), lambda j: (s_blk, xb_blk + j))] + s5_w_specs1
                 + [pl.BlockSpec((db, hw), lambda j: (0, j)), pl.BlockSpec((db, hw), lambda j: (0, j))],
        out_specs=[pl.BlockSpec((ms, LANES), lambda j: (0, j)),
                   pl.BlockSpec((db, hw), lambda j: (0, j)),
                   pl.BlockSpec((db, hw), lambda j: (0, j))],
        out_shape=[jax.ShapeDtypeStruct((ms, d_ssm), F32),
                   jax.ShapeDtypeStruct((db, n_grp * ssm_p), F32),
                   jax.ShapeDtypeStruct((db, n_grp * ssm_p), F32)],
        scratch_shapes=[pltpu.VMEM((ms, 2 * hw), F32), pltpu.VMEM((ms, 2 * hw), BF16)],
        compiler_params=_cparams("parallel"),
        name="s5_sample",
    )(z, wb, wc, abr3, abi3, dsk,
      state_ssm_re[0].reshape(db, n_grp * ssm_p), state_ssm_im[0].reshape(db, n_grp * ssm_p))
    bm_g = _pick(math.gcd(mp, ms), (1024, 512, 256, 128, 64, 32, 16, 8))
    npg = mp // bm_g
    p_rows = lambda i, j: jnp.minimum(i, npg - 1)
    s_rows = lambda i, j: jnp.maximum(i - npg, 0)

    bn_g = _pick(d_ssm, (512, 256, 128))
    bm_u = _pick(bm_g, (512, 256, 128, 64, 32, 16, 8))
    npu = mp // bm_u
    pu_rows = lambda i: jnp.minimum(i, npu - 1)
    su_rows = lambda i: jnp.maximum(i - npu, 0)
    yb = pl.pallas_call(
        functools.partial(_glu_kernel, n_prompt_tiles=npu),
        grid=(m // bm_u, d_ssm // bn_g),
        in_specs=[pl.BlockSpec((bm_u, d_ssm), lambda i, j: (pu_rows(i), 0)),
                  pl.BlockSpec((bm_u, d_ssm), lambda i, j: (su_rows(i), 0)),
                  pl.BlockSpec((bm_u, bn_g), lambda i, j: (pu_rows(i), j)),
                  pl.BlockSpec((bm_u, bn_g), lambda i, j: (su_rows(i), j)),
                  pl.BlockSpec((d_ssm, bn_g), lambda i, j: (0, j)),
                  pl.BlockSpec((1, bn_g), lambda i, j: (0, j))],
        out_specs=pl.BlockSpec((bm_u, bn_g), lambda i, j: (i, j)),
        out_shape=jax.ShapeDtypeStruct((m, d_ssm), BF16),
        scratch_shapes=[pltpu.VMEM((bm_u, d_ssm), BF16)],
        compiler_params=_cparams("parallel", "arbitrary"),
        name="s5_glu",
    )(y_p, y_s, y_p, y_s, w_glu[0], row2(b_glu))

    bn_m = bn_in
    gla_blk = (2 * d_lru + d_ssm) // bn_m
    glb_blk = gla_blk + dm // bn_m
    merged = pl.pallas_call(
        functools.partial(_merge_kernel, n_prompt_tiles=npg),
        grid=(m // bm_g, dm // bn_m),
        in_specs=[pl.BlockSpec((bm_g, d_lru), lambda i, j: (p_rows(i, j), 0)),
                  pl.BlockSpec((bm_g, d_lru), lambda i, j: (s_rows(i, j), 0)),
                  pl.BlockSpec((bm_g, d_ssm), lambda i, j: (i, 0)),
                  pl.BlockSpec((d_lru, bn_m), lambda i, j: (0, j)),
                  pl.BlockSpec((d_ssm, bn_m), lambda i, j: (0, j)),
                  pl.BlockSpec((bm_g, bn_m), lambda i, j: (i, gla_blk + j)),
                  pl.BlockSpec((bm_g, bn_m), lambda i, j: (i, glb_blk + j))],
        out_specs=pl.BlockSpec((bm_g, bn_m), lambda i, j: (i, j)),
        out_shape=jax.ShapeDtypeStruct((m, dm), BF16),
        compiler_params=_cparams("parallel", "arbitrary"),
        name="gated_merge",
    )(ya_p, ya_s, yb, w_out_a[0], w_out_b[0], z, z)

    o = _matmul(merged, w_o[0], bm=bm, bn=bn_in, out_dtype=F32, name="o_proj")

    x2, h2 = pl.pallas_call(
        _post_mix_kernel,
        grid=(m // tr,),
        in_specs=[row_spec, row_spec, g_spec, g_spec],
        out_specs=[row_spec, row_spec],
        out_shape=[jax.ShapeDtypeStruct((m, dm), F32), jax.ShapeDtypeStruct((m, dm), BF16)],
        compiler_params=_cparams("parallel"),
        name="post_mix",
    )(x1, o, row2(mix_post_g), row2(ffn2_pre_g))

    f2 = _half_ffn_matmuls(h2, ffn2_w_gate[0], ffn2_w_up[0], bf(ffn2_w_down), bm=bm)

    def final(rows, first_tile):
        return pl.pallas_call(
            _final_kernel,
            grid=(rows // tr,),
            in_specs=[pl.BlockSpec((tr, dm), lambda i: (first_tile + i, 0)),
                      pl.BlockSpec((tr, dm), lambda i: (first_tile + i, 0)),
                      g_spec],
            out_specs=row_spec,
            out_shape=jax.ShapeDtypeStruct((rows, dm), F32),
            compiler_params=_cparams("parallel"),
            name="final_residual",
        )(x2, f2, row2(ffn2_post_g))

    y_prompt = final(mp, 0).reshape(nb, seq, dm)
    y_sample = jnp.swapaxes(final(ms, npt).reshape(dseq, db, dm), 0, 1)

    nk = CONV_W - 1
    prompt_conv = jnp.stack([lax.slice(z, ((n + 1) * seq - nk, 0), ((n + 1) * seq, d_lru)) for n in range(nb)])
    sample_conv = jnp.swapaxes(lax.slice(z, (m - nk * db, 0), (m, d_lru)).reshape(nk, db, d_lru), 0, 1)
    st = lambda v, n: v.reshape(1, n, n_grp, ssm_p).astype(sdt)
    return (y_prompt, y_sample,
            hl_p.reshape(1, nb, d_lru).astype(sdt), prompt_conv[None].astype(sdt),
            st(sre_p, nb), st(sim_p, nb),
            hl_s.reshape(1, db, d_lru).astype(sdt), sample_conv[None].astype(sdt),
            st(sre_s, db), st(sim_s, db))
```

```python
import functools
import math

import jax
import jax.numpy as jnp
from jax import lax
from jax.experimental import pallas as pl
from jax.experimental.pallas import tpu as pltpu

F32 = jnp.float32
BF16 = jnp.bfloat16

EPS = 1e-6
C_RG = 8.0
CONV_W = 4
LANES = 128
SUBLANES = 8
LRU_BLK = 128
VMEM_LIMIT = 56 * 1024 * 1024


def _cparams(*sem):
    return pltpu.CompilerParams(dimension_semantics=sem, vmem_limit_bytes=VMEM_LIMIT)


def _pick(n, candidates):
    for c in candidates:
        if n % c == 0:
            return c
    raise ValueError(f"no tile in {candidates} divides {n}")


def _rms(x, g):
    return x * lax.rsqrt(jnp.mean(x * x, axis=-1, keepdims=True) + EPS) * g


def _softplus(x):
    return jnp.maximum(x, 0.0) + jnp.log1p(jnp.exp(-jnp.abs(x)))


def _two_group_rows(i, n_prompt_tiles, xp_ref, xs_ref, body):
    @pl.when(i < n_prompt_tiles)
    def _():
        body(xp_ref[...])

    @pl.when(i >= n_prompt_tiles)
    def _():
        body(xs_ref[...])


def _prenorm_kernel(xp_ref, xs_ref, g_ref, h_ref, *, n_prompt_tiles):
    def body(x):
        h_ref[...] = _rms(x, g_ref[...]).astype(h_ref.dtype)

    _two_group_rows(pl.program_id(0), n_prompt_tiles, xp_ref, xs_ref, body)


def _post_ffn1_kernel(xp_ref, xs_ref, f_ref, gpost_ref, gpre_ref, x1_ref, u_ref, *, n_prompt_tiles):
    def body(x):
        x1 = x + 0.5 * _rms(f_ref[...], gpost_ref[...])
        x1_ref[...] = x1
        u_ref[...] = _rms(x1, gpre_ref[...]).astype(u_ref.dtype)

    _two_group_rows(pl.program_id(0), n_prompt_tiles, xp_ref, xs_ref, body)


def _post_mix_kernel(x1_ref, o_ref, gpost_ref, gpre_ref, x2_ref, h_ref):
    x2 = x1_ref[...] + _rms(o_ref[...], gpost_ref[...])
    x2_ref[...] = x2
    h_ref[...] = _rms(x2, gpre_ref[...]).astype(h_ref.dtype)


def _final_kernel(x2_ref, f_ref, g_ref, y_ref):
    y_ref[...] = x2_ref[...] + 0.5 * _rms(f_ref[...], g_ref[...])


def _ffn_up_kernel(h_ref, wg_ref, wu_ref, a_ref):
    h = h_ref[...]
    g = jnp.dot(h, wg_ref[...].astype(BF16), preferred_element_type=F32)
    u = jnp.dot(h, wu_ref[...].astype(BF16), preferred_element_type=F32)
    a_ref[...] = (jax.nn.silu(g) * u).astype(a_ref.dtype)


def _ffn_down_kernel(a_ref, wd_ref, o_ref):
    p = jnp.dot(a_ref[...], wd_ref[...], preferred_element_type=F32)

    @pl.when(pl.program_id(2) == 0)
    def _():
        o_ref[...] = p

    @pl.when(pl.program_id(2) != 0)
    def _():
        o_ref[...] += p


def _mm_kernel(x_ref, w_ref, o_ref):
    w = w_ref[...].astype(BF16)
    o_ref[...] = jnp.dot(x_ref[...], w, preferred_element_type=F32).astype(o_ref.dtype)


def _glu_kernel(g_ref, gcol_ref, w_ref, b_ref, o_ref, g_scr):
    @pl.when(pl.program_id(1) == 0)
    def _():
        g_scr[...] = g_ref[...].astype(g_scr.dtype)

    s = jnp.dot(g_scr[...], w_ref[...].astype(BF16), preferred_element_type=F32) + b_ref[...]
    o_ref[...] = (gcol_ref[...] * jax.nn.sigmoid(s)).astype(o_ref.dtype)


def _merge_kernel(ya_ref, yb_ref, wa_ref, wb_ref, gla_ref, glb_ref, o_ref):
    pa = jnp.dot(ya_ref[...], wa_ref[...].astype(BF16), preferred_element_type=F32)
    pb = jnp.dot(yb_ref[...], wb_ref[...].astype(BF16), preferred_element_type=F32)
    m = jax.nn.sigmoid(gla_ref[...]) * pa + jax.nn.sigmoid(glb_ref[...]) * pb
    o_ref[...] = m.astype(o_ref.dtype)


def _lru_gates(xc, wrg_ref, brg_ref, wig_ref, big_ref, lam_ref):
    xcb = xc.astype(BF16)
    rs, gs = [], []
    for hh in range(wrg_ref.shape[0]):
        xh = xcb[:, hh * LRU_BLK:(hh + 1) * LRU_BLK]
        rs.append(jnp.dot(xh, wrg_ref[hh], preferred_element_type=F32))
        gs.append(jnp.dot(xh, wig_ref[hh], preferred_element_type=F32))
    r = jax.nn.sigmoid(jnp.concatenate(rs, axis=1) + brg_ref[...])
    i = jax.nn.sigmoid(jnp.concatenate(gs, axis=1) + big_ref[...])
    log_a = -C_RG * r * _softplus(-lam_ref[...])
    a = jnp.exp(log_a)
    mult = jnp.sqrt(-jnp.tanh(log_a) * (a * a + 1.0))
    return a, mult * (i * xc)


def _rglru_prompt_kernel(xa_ref, ga_ref, cw_ref, cb_ref, wrg_ref, brg_ref, wig_ref, big_ref, lam_ref,
                         ya_ref, hl_ref, xcar, hcar):
    tm = xa_ref.shape[0]

    @pl.when(pl.program_id(2) == 0)
    def _():
        xcar[...] = jnp.zeros_like(xcar)
        hcar[...] = jnp.zeros_like(hcar)

    x = xa_ref[...]
    xfull = jnp.concatenate([xcar[...], x], axis=0)
    xcar[...] = x[tm - SUBLANES:, :]
    cw = cw_ref[...]
    xc = cb_ref[...]
    for k in range(CONV_W - 1):
        xs = pltpu.roll(xfull, CONV_W - 1 - k, 0)[SUBLANES:, :]
        xc = xc + xs * cw[k:k + 1, :]
    xc = xc + x * cw[CONV_W - 1:CONV_W, :]

    a, b = _lru_gates(xc, wrg_ref, brg_ref, wig_ref, big_ref, lam_ref)
    cb = a.shape[1]
    n_slab = tm // SUBLANES
    a = a.reshape(n_slab, SUBLANES, cb)
    b = b.reshape(n_slab, SUBLANES, cb)
    sub = lax.broadcasted_iota(jnp.int32, (1, SUBLANES, cb), 1)
    d = 1
    while d < SUBLANES:
        m = sub >= d
        a_s = jnp.where(m, pltpu.roll(a, d, 1), 1.0)
        b_s = jnp.where(m, pltpu.roll(b, d, 1), 0.0)
        b = a * b_s + b
        a = a * a_s
        d *= 2
    h = hcar[...]
    hs = []
    for s in range(n_slab):
        h_slab = b[s] + a[s] * jnp.broadcast_to(h, (SUBLANES, cb))
        h = h_slab[SUBLANES - 1:SUBLANES, :]
        hs.append(h_slab)
    hcar[...] = h
    hl_ref[...] = h
    ya_ref[...] = (jnp.concatenate(hs, axis=0) * jax.nn.gelu(ga_ref[...])).astype(ya_ref.dtype)


def _rglru_sample_kernel(xa_ref, ga_ref, cst_ref, h0_ref, cw_ref, cb_ref, wrg_ref, brg_ref, wig_ref,
                         big_ref, lam_ref, ya_all_ref, ya_ref, hl_ref, *, n_seq, n_steps):
    del ya_all_ref
    x = xa_ref[...]
    xp = jnp.concatenate([cst_ref[k] for k in range(CONV_W - 1)] + [x], axis=0)
    rows = n_seq * n_steps
    cw = cw_ref[...]
    xc = cb_ref[...]
    for k in range(CONV_W):
        xc = xc + xp[k * n_seq:k * n_seq + rows, :] * cw[k:k + 1, :]
    a, b = _lru_gates(xc, wrg_ref, brg_ref, wig_ref, big_ref, lam_ref)
    h = h0_ref[...]
    for t in range(n_steps):
        sl = slice(t * n_seq, (t + 1) * n_seq)
        h = a[sl, :] * h + b[sl, :]
        ya_ref[sl, :] = (h * jax.nn.gelu(ga_ref[sl, :])).astype(ya_ref.dtype)
    hl_ref[...] = h


def _s5_disc_kernel(are_ref, aim_ref, ldt_ref, abr_ref, abi_ref, cfr_ref, cfi_ref):
    a_re = are_ref[...]
    a_im = aim_ref[...]
    dt = jnp.exp(ldt_ref[...])
    mag = jnp.exp(a_re * dt)
    abr = mag * jnp.cos(a_im * dt)
    abi = mag * jnp.sin(a_im * dt)
    den = a_re * a_re + a_im * a_im
    nr = abr - 1.0
    abr_ref[...] = abr
    abi_ref[...] = abi
    cr = (nr * a_re + abi * a_im) / den
    ci = (abi * a_re - nr * a_im) / den
    for q in range(cfr_ref.shape[0]):
        cfr_ref[q] = cr
        cfi_ref[q] = ci
        cr, ci = abr * cr - abi * ci, abr * ci + abi * cr


def _s5_prompt_kernel(xb_ref, wb_ref, wc_ref, ar_ref, ai_ref, d_ref, g_ref, sre_ref, sim_ref, cre, cim):
    n_par, _, hw = ar_ref.shape

    @pl.when(pl.program_id(2) == 0)
    def _():
        cre[...] = jnp.zeros_like(cre)
        cim[...] = jnp.zeros_like(cim)

    for p in range(n_par):
        cols = slice(p * LANES, (p + 1) * LANES)
        y, cr, ci = _s5_prompt_tile(xb_ref[:, cols], wb_ref.at[p], wc_ref.at[p], ar_ref[p], ai_ref[p],
                                    d_ref[:, cols], cre[p], cim[p])
        g_ref[:, cols] = jax.nn.gelu(y)
        cre[p] = cr
        cim[p] = ci
        sre_ref[:, p * hw:(p + 1) * hw] = cr
        sim_ref[:, p * hw:(p + 1) * hw] = ci


def _s5_prompt_tile(xb, wb_ref, wc_ref, ar, ai, dsk, cr, ci):
    tm = xb.shape[0]
    hw = ar.shape[1]
    n_slab = tm // SUBLANES
    n_lags = wb_ref.shape[0] // xb.shape[1]
    xb3 = xb.reshape(n_slab, SUBLANES, xb.shape[1])
    subx = lax.broadcasted_iota(jnp.int32, (1, SUBLANES, xb.shape[1]), 1)
    lagged = [xb3] + [jnp.where(subx >= q, pltpu.roll(xb3, q, 1), 0.0) for q in range(1, n_lags)]
    xs = jnp.concatenate(lagged, axis=2).reshape(tm, n_lags * xb.shape[1])
    bu = jnp.dot(xs.astype(BF16), wb_ref[...], preferred_element_type=F32)
    hr = bu[:, :hw].reshape(n_slab, SUBLANES, hw)
    hi = bu[:, hw:].reshape(n_slab, SUBLANES, hw)
    sub = lax.broadcasted_iota(jnp.int32, (1, SUBLANES, hw), 1)
    pr = ar.reshape(1, 1, hw)
    pi = ai.reshape(1, 1, hw)
    tr = jnp.where(sub == 0, pr, 0.0)
    ti = jnp.where(sub == 0, pi, 0.0)
    d = 1
    while d < SUBLANES:
        mr = jnp.where(sub >= d, pr, 0.0)
        mi = jnp.where(sub >= d, pi, 0.0)
        if d >= n_lags:
            sr = pltpu.roll(hr, d, 1)
            si = pltpu.roll(hi, d, 1)
            hr, hi = hr + (mr * sr - mi * si), hi + (mr * si + mi * sr)
        sr = pltpu.roll(tr, d, 1)
        si = pltpu.roll(ti, d, 1)
        tr, ti = tr + (mr * sr - mi * si), ti + (mr * si + mi * sr)
        pr, pi = pr * pr - pi * pi, 2.0 * (pr * pi)
        d *= 2
    tr = tr[0]
    ti = ti[0]
    hrs, his = [], []
    for s in range(n_slab):
        cbr = jnp.broadcast_to(cr, (SUBLANES, hw))
        cbi = jnp.broadcast_to(ci, (SUBLANES, hw))
        sr = hr[s] + (tr * cbr - ti * cbi)
        si = hi[s] + (tr * cbi + ti * cbr)
        cr = sr[SUBLANES - 1:SUBLANES, :]
        ci = si[SUBLANES - 1:SUBLANES, :]
        hrs.append(sr)
        his.append(si)
    hcat = jnp.concatenate([jnp.concatenate(hrs, axis=0), jnp.concatenate(his, axis=0)], axis=1).astype(BF16)
    y = jnp.dot(hcat, wc_ref[...], preferred_element_type=F32)
    return y + dsk * xb, cr, ci


def _s5_sample_kernel(xb_ref, wb_ref, wc_ref, ar_ref, ai_ref, d_ref, s0r_ref, s0i_ref, g_all_ref,
                      g_ref, sre_ref, sim_ref, bu_scr, h_scr, *, n_seq, n_steps):
    del g_all_ref
    hw = ar_ref.shape[1]
    xb = xb_ref[...]
    bu_scr[...] = jnp.dot(xb.astype(BF16), wb_ref[...], preferred_element_type=F32)
    ar = ar_ref[...]
    ai = ai_ref[...]
    hr = s0r_ref[...]
    hi = s0i_ref[...]
    for t in range(n_steps):
        sl = slice(t * n_seq, (t + 1) * n_seq)
        hr, hi = (ar * hr - ai * hi) + bu_scr[sl, :hw], (ar * hi + ai * hr) + bu_scr[sl, hw:]
        h_scr[sl, :hw] = hr.astype(h_scr.dtype)
        h_scr[sl, hw:] = hi.astype(h_scr.dtype)
    sre_ref[...] = hr
    sim_ref[...] = hi
    y = jnp.dot(h_scr[...], wc_ref[...], preferred_element_type=F32)
    g_ref[...] = jax.nn.gelu(y + d_ref[...] * xb)


def _half_ffn_matmuls(h, wg, wu, wd, *, bm):
    m, dm = h.shape
    dff = wg.shape[1]
    bn_up = _pick(dff, (256, 128))
    a = pl.pallas_call(
        _ffn_up_kernel,
        grid=(m // bm, dff // bn_up),
        in_specs=[pl.BlockSpec((bm, dm), lambda i, j: (i, 0)),
                  pl.BlockSpec((dm, bn_up), lambda i, j: (0, j)),
                  pl.BlockSpec((dm, bn_up), lambda i, j: (0, j))],
        out_specs=pl.BlockSpec((bm, bn_up), lambda i, j: (i, j)),
        out_shape=jax.ShapeDtypeStruct((m, dff), BF16),
        compiler_params=_cparams("parallel", "arbitrary"),
        name="ffn_up",
    )(h, wg, wu)

    bm_d = _pick(m, (1536, 1024, 512, 256, 128))
    bn_d = _pick(dm, (512, 256, 128))
    bk = dff // 2 if (dff // 2) % LANES == 0 else dff
    return pl.pallas_call(
        _ffn_down_kernel,
        grid=(m // bm_d, dm // bn_d, dff // bk),
        in_specs=[pl.BlockSpec((bm_d, bk), lambda i, j, k: (i, k)),
                  pl.BlockSpec((bk, bn_d), lambda i, j, k: (k, j))],
        out_specs=pl.BlockSpec((bm_d, bn_d), lambda i, j, k: (i, j)),
        out_shape=jax.ShapeDtypeStruct((m, dm), F32),
        compiler_params=_cparams("parallel", "parallel", "arbitrary"),
        name="ffn_down",
    )(a, wd)


def _matmul(x, w, *, bm, bn, out_dtype, name):
    m, k = x.shape
    n = w.shape[1]
    return pl.pallas_call(
        _mm_kernel,
        grid=(m // bm, n // bn),
        in_specs=[pl.BlockSpec((bm, k), lambda i, j: (i, 0)),
                  pl.BlockSpec((k, bn), lambda i, j: (0, j))],
        out_specs=pl.BlockSpec((bm, bn), lambda i, j: (i, j)),
        out_shape=jax.ShapeDtypeStruct((m, n), out_dtype),
        compiler_params=_cparams("parallel", "arbitrary"),
        name=name,
    )(x, w)


def kernel(x_prompt, x_sample, state_lru_h, state_conv, state_ssm_re, state_ssm_im, ffn1_pre_g, ffn1_post_g, ffn1_w_gate, ffn1_w_up, ffn1_w_down, mix_pre_g, mix_post_g, w_in, conv_w, conv_b, w_rg, b_rg, w_ig, b_ig, lru_lambda, ssm_a_re, ssm_a_im, ssm_log_dt, ssm_b_re, ssm_b_im, ssm_c_re, ssm_c_im, ssm_d, w_glu, b_glu, w_out_a, w_out_b, w_o, ffn2_pre_g, ffn2_post_g, ffn2_w_gate, ffn2_w_up, ffn2_w_down):
    nb, seq, dm = x_prompt.shape
    db, dseq, _ = x_sample.shape
    depth = state_lru_h.shape[0]
    assert depth == 1, "one decoder layer"
    d_lru = state_lru_h.shape[2]
    n_grp, ssm_p = state_ssm_re.shape[2], state_ssm_re.shape[3]
    d_ssm = ssm_d.shape[1]
    ssm_cg = d_ssm // n_grp
    d_in = w_in.shape[2]
    assert w_rg.shape[2] == LRU_BLK and conv_w.shape[1] == CONV_W
    assert d_in == 2 * d_lru + d_ssm + 2 * dm and seq >= CONV_W - 1 and dseq >= CONV_W - 1

    mp = nb * seq
    ms = db * dseq
    m = mp + ms
    sdt = state_lru_h.dtype

    row2 = lambda v: v.reshape(1, -1)
    bf = lambda v: v[0].astype(BF16)

    xp2 = x_prompt.reshape(mp, dm)
    xs2 = jnp.swapaxes(x_sample, 0, 1).reshape(ms, dm)

    tr = _pick(math.gcd(mp, ms), (256, 128, 64, 32, 16, 8))
    npt = mp // tr
    xp_spec = pl.BlockSpec((tr, dm), lambda i: (jnp.minimum(i, npt - 1), 0))
    xs_spec = pl.BlockSpec((tr, dm), lambda i: (jnp.maximum(i - npt, 0), 0))
    row_spec = pl.BlockSpec((tr, dm), lambda i: (i, 0))
    g_spec = pl.BlockSpec((1, dm), lambda i: (0, 0))

    h1 = pl.pallas_call(
        functools.partial(_prenorm_kernel, n_prompt_tiles=npt),
        grid=(m // tr,),
        in_specs=[xp_spec, xs_spec, g_spec],
        out_specs=row_spec,
        out_shape=jax.ShapeDtypeStruct((m, dm), BF16),
        compiler_params=_cparams("parallel"),
        name="prenorm1",
    )(xp2, xs2, row2(ffn1_pre_g))

    bm = _pick(m, (1536, 1024, 768, 512, 384, 256, 128))
    f1 = _half_ffn_matmuls(h1, ffn1_w_gate[0], ffn1_w_up[0], bf(ffn1_w_down), bm=bm)

    x1, u = pl.pallas_call(
        functools.partial(_post_ffn1_kernel, n_prompt_tiles=npt),
        grid=(m // tr,),
        in_specs=[xp_spec, xs_spec, row_spec, g_spec, g_spec],
        out_specs=[row_spec, row_spec],
        out_shape=[jax.ShapeDtypeStruct((m, dm), F32), jax.ShapeDtypeStruct((m, dm), BF16)],
        compiler_params=_cparams("parallel"),
        name="post_ffn1",
    )(xp2, xs2, f1, row2(ffn1_post_g), row2(mix_pre_g))

    bn_in = _pick(math.gcd(d_lru, dm), (512, 256, 128))
    z = _matmul(u, w_in[0], bm=bm, bn=bn_in, out_dtype=F32, name="in_proj")

    cb = _pick(d_lru, (512, 256, 128))
    ncb = d_lru // cb
    hpb = cb // LRU_BLK
    wrg = bf(w_rg)
    wig = bf(w_ig)
    tm = _pick(seq, (512, 256, 128, 64, 32, 16, 8))
    rpt = seq // tm
    par3 = lambda shape: pl.BlockSpec(shape, lambda c, n, r: (0, c))
    lru_params = (conv_w[0], row2(conv_b), wrg, row2(b_rg), wig, row2(b_ig), row2(lru_lambda))
    ya, hl_p = pl.pallas_call(
        _rglru_prompt_kernel,
        grid=(ncb, nb, rpt),
        in_specs=[pl.BlockSpec((tm, cb), lambda c, n, r: (n * rpt + r, c)),
                  pl.BlockSpec((tm, cb), lambda c, n, r: (n * rpt + r, ncb + c)),
                  par3((CONV_W, cb)), par3((1, cb)),
                  pl.BlockSpec((hpb, LRU_BLK, LRU_BLK), lambda c, n, r: (c, 0, 0)), par3((1, cb)),
                  pl.BlockSpec((hpb, LRU_BLK, LRU_BLK), lambda c, n, r: (c, 0, 0)), par3((1, cb)),
                  par3((1, cb))],
        out_specs=[pl.BlockSpec((tm, cb), lambda c, n, r: (n * rpt + r, c)),
                   pl.BlockSpec((None, 1, cb), lambda c, n, r: (n, 0, c))],
        out_shape=[jax.ShapeDtypeStruct((m, d_lru), BF16), jax.ShapeDtypeStruct((nb, 1, d_lru), F32)],
        scratch_shapes=[pltpu.VMEM((SUBLANES, cb), F32), pltpu.VMEM((1, cb), F32)],
        compiler_params=_cparams("parallel", "arbitrary", "arbitrary"),
        name="rglru_prompt",
    )(z, z, *lru_params)

    assert mp % ms == 0, "sample rows must tile the unified row axis"
    s_blk = mp // ms
    par1 = lambda shape: pl.BlockSpec(shape, lambda c: (0, c))
    cst = jnp.swapaxes(state_conv[0], 0, 1)
    any_spec = pl.BlockSpec(memory_space=pl.ANY)
    ya, hl_s = pl.pallas_call(
        functools.partial(_rglru_sample_kernel, n_seq=db, n_steps=dseq),
        grid=(ncb,),
        in_specs=[pl.BlockSpec((ms, cb), lambda c: (s_blk, c)),
                  pl.BlockSpec((ms, cb), lambda c: (s_blk, ncb + c)),
                  pl.BlockSpec((CONV_W - 1, db, cb), lambda c: (0, 0, c)),
                  par1((db, cb)),
                  par1((CONV_W, cb)), par1((1, cb)),
                  pl.BlockSpec((hpb, LRU_BLK, LRU_BLK), lambda c: (c, 0, 0)), par1((1, cb)),
                  pl.BlockSpec((hpb, LRU_BLK, LRU_BLK), lambda c: (c, 0, 0)), par1((1, cb)),
                  par1((1, cb)), any_spec],
        out_specs=[pl.BlockSpec((ms, cb), lambda c: (s_blk, c)), par1((db, cb))],
        out_shape=[jax.ShapeDtypeStruct((m, d_lru), BF16), jax.ShapeDtypeStruct((db, d_lru), F32)],
        input_output_aliases={4 + len(lru_params): 0},
        compiler_params=_cparams("parallel"),
        name="rglru_sample",
    )(z, z, cst, state_lru_h[0], *lru_params, ya)

    n_lags = 4
    abr, abi, cfr, cfi = pl.pallas_call(
        _s5_disc_kernel,
        out_shape=[jax.ShapeDtypeStruct((n_grp, ssm_p), F32)] * 2
                  + [jax.ShapeDtypeStruct((n_lags, n_grp, ssm_p), F32)] * 2,
        name="s5_discretise",
    )(ssm_a_re[0], ssm_a_im[0], ssm_log_dt[0].reshape(n_grp, 1))
    bb_re = cfr[..., None] * ssm_b_re - cfi[..., None] * ssm_b_im
    bb_im = cfr[..., None] * ssm_b_im + cfi[..., None] * ssm_b_re

    gpb = LANES // ssm_cg
    nj = n_grp // gpb
    hw = gpb * ssm_p
    eye = jnp.eye(gpb, dtype=F32)

    def blockdiag_in(w):
        w6 = w.reshape(n_lags, nj, gpb, 1, ssm_p, ssm_cg) * eye[None, None, :, :, None, None]
        return jnp.transpose(w6, (1, 0, 2, 5, 3, 4)).reshape(nj, n_lags * gpb * ssm_cg, hw)

    def blockdiag_out(w):
        w5 = w.reshape(nj, gpb, 1, ssm_cg, ssm_p) * eye[None, :, :, None, None]
        return jnp.transpose(w5, (0, 2, 4, 1, 3)).reshape(nj, hw, gpb * ssm_cg)

    wb = jnp.concatenate([blockdiag_in(bb_re), blockdiag_in(bb_im)], axis=2).astype(BF16)
    wc = jnp.concatenate([blockdiag_out(ssm_c_re[0]), -blockdiag_out(ssm_c_im[0])], axis=1).astype(BF16)
    abr3 = abr.reshape(nj, 1, hw)
    abi3 = abi.reshape(nj, 1, hw)
    xb_blk = (2 * d_lru) // LANES
    dsk = row2(ssm_d)

    n_par = 2 if nj % 2 == 0 and xb_blk % 2 == 0 else 1
    g_all, sre_p, sim_p = pl.pallas_call(
        _s5_prompt_kernel,
        grid=(nj // n_par, nb, rpt),
        in_specs=[pl.BlockSpec((tm, n_par * LANES), lambda j, n, r: (n * rpt + r, xb_blk // n_par + j)),
                  pl.BlockSpec((n_par, n_lags * LANES, 2 * hw), lambda j, n, r: (j, 0, 0)),
                  pl.BlockSpec((n_par, 2 * hw, LANES), lambda j, n, r: (j, 0, 0)),
                  pl.BlockSpec((n_par, 1, hw), lambda j, n, r: (j, 0, 0)),
                  pl.BlockSpec((n_par, 1, hw), lambda j, n, r: (j, 0, 0)),
                  pl.BlockSpec((1, n_par * LANES), lambda j, n, r: (0, j))],
        out_specs=[pl.BlockSpec((tm, n_par * LANES), lambda j, n, r: (n * rpt + r, j)),
                   pl.BlockSpec((None, 1, n_par * hw), lambda j, n, r: (n, 0, j)),
                   pl.BlockSpec((None, 1, n_par * hw), lambda j, n, r: (n, 0, j))],
        out_shape=[jax.ShapeDtypeStruct((m, d_ssm), F32),
                   jax.ShapeDtypeStruct((nb, 1, n_grp * ssm_p), F32),
                   jax.ShapeDtypeStruct((nb, 1, n_grp * ssm_p), F32)],
        scratch_shapes=[pltpu.VMEM((n_par, 1, hw), F32), pltpu.VMEM((n_par, 1, hw), F32)],
        compiler_params=_cparams("parallel", "arbitrary", "arbitrary"),
        name="s5_prompt",
    )(z, wb, wc, abr3, abi3, dsk)

    s5_w_specs1 = [pl.BlockSpec((None, LANES, 2 * hw), lambda j: (j, 0, 0)),
                   pl.BlockSpec((None, 2 * hw, LANES), lambda j: (j, 0, 0)),
                   pl.BlockSpec((None, 1, hw), lambda j: (j, 0, 0)),
                   pl.BlockSpec((None, 1, hw), lambda j: (j, 0, 0)),
                   pl.BlockSpec((1, LANES), lambda j: (0, j))]
    g_all, sre_s, sim_s = pl.pallas_call(
        functools.partial(_s5_sample_kernel, n_seq=db, n_steps=dseq),
        grid=(nj,),
        in_specs=[pl.BlockSpec((ms, LANES), lambda j: (s_blk, xb_blk + j))] + s5_w_specs1
                 + [pl.BlockSpec((db, hw), lambda j: (0, j)), pl.BlockSpec((db, hw), lambda j: (0, j)), any_spec],
        out_specs=[pl.BlockSpec((ms, LANES), lambda j: (s_blk, j)),
                   pl.BlockSpec((db, hw), lambda j: (0, j)),
                   pl.BlockSpec((db, hw), lambda j: (0, j))],
        out_shape=[jax.ShapeDtypeStruct((m, d_ssm), F32),
                   jax.ShapeDtypeStruct((db, n_grp * ssm_p), F32),
                   jax.ShapeDtypeStruct((db, n_grp * ssm_p), F32)],
        scratch_shapes=[pltpu.VMEM((ms, 2 * hw), F32), pltpu.VMEM((ms, 2 * hw), BF16)],
        input_output_aliases={8: 0},
        compiler_params=_cparams("parallel"),
        name="s5_sample",
    )(z, wb, wc, abr3, abi3, dsk,
      state_ssm_re[0].reshape(db, n_grp * ssm_p), state_ssm_im[0].reshape(db, n_grp * ssm_p), g_all)

    bm_g = _pick(m, (1024, 512, 256, 128))
    bn_g = _pick(d_ssm, (512, 256, 128))
    yb = pl.pallas_call(
        _glu_kernel,
        grid=(m // bm_g, d_ssm // bn_g),
        in_specs=[pl.BlockSpec((bm_g, d_ssm), lambda i, j: (i, 0)),
                  pl.BlockSpec((bm_g, bn_g), lambda i, j: (i, j)),
                  pl.BlockSpec((d_ssm, bn_g), lambda i, j: (0, j)),
                  pl.BlockSpec((1, bn_g), lambda i, j: (0, j))],
        out_specs=pl.BlockSpec((bm_g, bn_g), lambda i, j: (i, j)),
        out_shape=jax.ShapeDtypeStruct((m, d_ssm), BF16),
        scratch_shapes=[pltpu.VMEM((bm_g, d_ssm), BF16)],
        compiler_params=_cparams("parallel", "arbitrary"),
        name="s5_glu",
    )(g_all, g_all, w_glu[0], row2(b_glu))

    bn_m = bn_in
    gla_blk = (2 * d_lru + d_ssm) // bn_m
    glb_blk = gla_blk + dm // bn_m
    merged = pl.pallas_call(
        _merge_kernel,
        grid=(m // bm_g, dm // bn_m),
        in_specs=[pl.BlockSpec((bm_g, d_lru), lambda i, j: (i, 0)),
                  pl.BlockSpec((bm_g, d_ssm), lambda i, j: (i, 0)),
                  pl.BlockSpec((d_lru, bn_m), lambda i, j: (0, j)),
                  pl.BlockSpec((d_ssm, bn_m), lambda i, j: (0, j)),
                  pl.BlockSpec((bm_g, bn_m), lambda i, j: (i, gla_blk + j)),
                  pl.BlockSpec((bm_g, bn_m), lambda i, j: (i, glb_blk + j))],
        out_specs=pl.BlockSpec((bm_g, bn_m), lambda i, j: (i, j)),
        out_shape=jax.ShapeDtypeStruct((m, dm), BF16),
        compiler_params=_cparams("parallel", "arbitrary"),
        name="gated_merge",
    )(ya, yb, w_out_a[0], w_out_b[0], z, z)

    o = _matmul(merged, w_o[0], bm=bm, bn=bn_in, out_dtype=F32, name="o_proj")

    x2, h2 = pl.pallas_call(
        _post_mix_kernel,
        grid=(m // tr,),
        in_specs=[row_spec, row_spec, g_spec, g_spec],
        out_specs=[row_spec, row_spec],
        out_shape=[jax.ShapeDtypeStruct((m, dm), F32), jax.ShapeDtypeStruct((m, dm), BF16)],
        compiler_params=_cparams("parallel"),
        name="post_mix",
    )(x1, o, row2(mix_post_g), row2(ffn2_pre_g))

    f2 = _half_ffn_matmuls(h2, ffn2_w_gate[0], ffn2_w_up[0], bf(ffn2_w_down), bm=bm)

    def final(rows, first_tile):
        return pl.pallas_call(
            _final_kernel,
            grid=(rows // tr,),
            in_specs=[pl.BlockSpec((tr, dm), lambda i: (first_tile + i, 0)),
                      pl.BlockSpec((tr, dm), lambda i: (first_tile + i, 0)),
                      g_spec],
            out_specs=row_spec,
            out_shape=jax.ShapeDtypeStruct((rows, dm), F32),
            compiler_params=_cparams("parallel"),
            name="final_residual",
        )(x2, f2, row2(ffn2_post_g))

    y_prompt = final(mp, 0).reshape(nb, seq, dm)
    y_sample = jnp.swapaxes(final(ms, npt).reshape(dseq, db, dm), 0, 1)

    nk = CONV_W - 1
    prompt_conv = jnp.stack([lax.slice(z, ((n + 1) * seq - nk, 0), ((n + 1) * seq, d_lru)) for n in range(nb)])
    sample_conv = jnp.swapaxes(lax.slice(z, (m - nk * db, 0), (m, d_lru)).reshape(nk, db, d_lru), 0, 1)
    st = lambda v, n: v.reshape(1, n, n_grp, ssm_p).astype(sdt)
    return (y_prompt, y_sample,
            hl_p.reshape(1, nb, d_lru).astype(sdt), prompt_conv[None].astype(sdt),
            st(sre_p, nb), st(sim_p, nb),
            hl_s.reshape(1, db, d_lru).astype(sdt), sample_conv[None].astype(sdt),
            st(sre_s, db), st(sim_s, db))
```

```python
import functools
import math

import jax
import jax.numpy as jnp
from jax import lax
from jax.experimental import pallas as pl
from jax.experimental.pallas import tpu as pltpu

F32 = jnp.float32
BF16 = jnp.bfloat16

EPS = 1e-6
C_RG = 8.0
CONV_W = 4
LANES = 128
SUBLANES = 8
LRU_BLK = 128
VMEM_LIMIT = 56 * 1024 * 1024


def _cparams(*sem):
    return pltpu.CompilerParams(dimension_semantics=sem, vmem_limit_bytes=VMEM_LIMIT)


def _pick(n, candidates):
    for c in candidates:
        if n % c == 0:
            return c
    raise ValueError(f"no tile in {candidates} divides {n}")


def _rms(x, g):
    return x * lax.rsqrt(jnp.mean(x * x, axis=-1, keepdims=True) + EPS) * g


def _softplus(x):
    return jnp.maximum(x, 0.0) + jnp.log1p(jnp.exp(-jnp.abs(x)))


def _two_group_rows(i, n_prompt_tiles, xp_ref, xs_ref, body):
    @pl.when(i < n_prompt_tiles)
    def _():
        body(xp_ref[...])

    @pl.when(i >= n_prompt_tiles)
    def _():
        body(xs_ref[...])


def _prenorm_kernel(xp_ref, xs_ref, g_ref, h_ref, *, n_prompt_tiles):
    def body(x):
        h_ref[...] = _rms(x, g_ref[...]).astype(h_ref.dtype)

    _two_group_rows(pl.program_id(0), n_prompt_tiles, xp_ref, xs_ref, body)


def _post_ffn1_kernel(xp_ref, xs_ref, f_ref, gpost_ref, gpre_ref, x1_ref, u_ref, *, n_prompt_tiles):
    def body(x):
        x1 = x + 0.5 * _rms(f_ref[...], gpost_ref[...])
        x1_ref[...] = x1
        u_ref[...] = _rms(x1, gpre_ref[...]).astype(u_ref.dtype)

    _two_group_rows(pl.program_id(0), n_prompt_tiles, xp_ref, xs_ref, body)


def _post_mix_kernel(x1_ref, o_ref, gpost_ref, gpre_ref, x2_ref, h_ref):
    x2 = x1_ref[...] + _rms(o_ref[...], gpost_ref[...])
    x2_ref[...] = x2
    h_ref[...] = _rms(x2, gpre_ref[...]).astype(h_ref.dtype)


def _final_kernel(x2_ref, f_ref, g_ref, y_ref):
    y_ref[...] = x2_ref[...] + 0.5 * _rms(f_ref[...], g_ref[...])


def _ffn_up_kernel(h_ref, wg_ref, wu_ref, wd_ref, a_ref, wd16_ref):
    wd16_ref[...] = wd_ref[...].astype(wd16_ref.dtype)
    h = h_ref[...]
    g = jnp.dot(h, wg_ref[...].astype(BF16), preferred_element_type=F32)
    u = jnp.dot(h, wu_ref[...].astype(BF16), preferred_element_type=F32)
    a_ref[...] = (jax.nn.silu(g) * u).astype(a_ref.dtype)


def _ffn_down_kernel(a_ref, wd_ref, o_ref):
    p = jnp.dot(a_ref[...], wd_ref[...], preferred_element_type=F32)

    @pl.when(pl.program_id(2) == 0)
    def _():
        o_ref[...] = p

    @pl.when(pl.program_id(2) != 0)
    def _():
        o_ref[...] += p


def _mm_kernel(x_ref, w_ref, o_ref):
    w = w_ref[...].astype(BF16)
    o_ref[...] = jnp.dot(x_ref[...], w, preferred_element_type=F32).astype(o_ref.dtype)


def _glu_kernel(g_ref, gcol_ref, w_ref, b_ref, o_ref, g_scr):
    @pl.when(pl.program_id(1) == 0)
    def _():
        g_scr[...] = g_ref[...].astype(g_scr.dtype)

    s = jnp.dot(g_scr[...], w_ref[...].astype(BF16), preferred_element_type=F32) + b_ref[...]
    o_ref[...] = (gcol_ref[...] * jax.nn.sigmoid(s)).astype(o_ref.dtype)


def _merge_kernel(ya_ref, yb_ref, wa_ref, wb_ref, gla_ref, glb_ref, o_ref):
    pa = jnp.dot(ya_ref[...], wa_ref[...].astype(BF16), preferred_element_type=F32)
    pb = jnp.dot(yb_ref[...], wb_ref[...].astype(BF16), preferred_element_type=F32)
    m = jax.nn.sigmoid(gla_ref[...]) * pa + jax.nn.sigmoid(glb_ref[...]) * pb
    o_ref[...] = m.astype(o_ref.dtype)


def _lru_gates(xc, wrg_ref, brg_ref, wig_ref, big_ref, lam_ref):
    xcb = xc.astype(BF16)
    rs, gs = [], []
    for hh in range(wrg_ref.shape[0]):
        xh = xcb[:, hh * LRU_BLK:(hh + 1) * LRU_BLK]
        rs.append(jnp.dot(xh, wrg_ref[hh], preferred_element_type=F32))
        gs.append(jnp.dot(xh, wig_ref[hh], preferred_element_type=F32))
    r = jax.nn.sigmoid(jnp.concatenate(rs, axis=1) + brg_ref[...])
    i = jax.nn.sigmoid(jnp.concatenate(gs, axis=1) + big_ref[...])
    log_a = -C_RG * r * _softplus(-lam_ref[...])
    a = jnp.exp(log_a)
    mult = jnp.sqrt(-jnp.tanh(log_a) * (a * a + 1.0))
    return a, mult * (i * xc)


def _rglru_prompt_kernel(xa_ref, ga_ref, cw_ref, cb_ref, wrg_ref, brg_ref, wig_ref, big_ref, lam_ref,
                         ya_ref, hl_ref, xcar, hcar):
    tm = xa_ref.shape[0]

    @pl.when(pl.program_id(2) == 0)
    def _():
        xcar[...] = jnp.zeros_like(xcar)
        hcar[...] = jnp.zeros_like(hcar)

    x = xa_ref[...]
    xfull = jnp.concatenate([xcar[...], x], axis=0)
    xcar[...] = x[tm - SUBLANES:, :]
    cw = cw_ref[...]
    xc = cb_ref[...]
    for k in range(CONV_W - 1):
        xs = pltpu.roll(xfull, CONV_W - 1 - k, 0)[SUBLANES:, :]
        xc = xc + xs * cw[k:k + 1, :]
    xc = xc + x * cw[CONV_W - 1:CONV_W, :]

    a, b = _lru_gates(xc, wrg_ref, brg_ref, wig_ref, big_ref, lam_ref)
    cb = a.shape[1]
    n_slab = tm // SUBLANES
    a = a.reshape(n_slab, SUBLANES, cb)
    b = b.reshape(n_slab, SUBLANES, cb)
    sub = lax.broadcasted_iota(jnp.int32, (1, SUBLANES, cb), 1)
    d = 1
    while d < SUBLANES:
        m = sub >= d
        a_s = jnp.where(m, pltpu.roll(a, d, 1), 1.0)
        b_s = jnp.where(m, pltpu.roll(b, d, 1), 0.0)
        b = a * b_s + b
        a = a * a_s
        d *= 2
    h = hcar[...]
    hs = []
    for s in range(n_slab):
        h_slab = b[s] + a[s] * jnp.broadcast_to(h, (SUBLANES, cb))
        h = h_slab[SUBLANES - 1:SUBLANES, :]
        hs.append(h_slab)
    hcar[...] = h
    hl_ref[...] = h
    ya_ref[...] = (jnp.concatenate(hs, axis=0) * jax.nn.gelu(ga_ref[...])).astype(ya_ref.dtype)


def _rglru_sample_kernel(xa_ref, ga_ref, cst_ref, h0_ref, cw_ref, cb_ref, wrg_ref, brg_ref, wig_ref,
                         big_ref, lam_ref, ya_all_ref, ya_ref, hl_ref, *, n_seq, n_steps):
    del ya_all_ref
    x = xa_ref[...]
    xp = jnp.concatenate([cst_ref[k] for k in range(CONV_W - 1)] + [x], axis=0)
    rows = n_seq * n_steps
    cw = cw_ref[...]
    xc = cb_ref[...]
    for k in range(CONV_W):
        xc = xc + xp[k * n_seq:k * n_seq + rows, :] * cw[k:k + 1, :]
    a, b = _lru_gates(xc, wrg_ref, brg_ref, wig_ref, big_ref, lam_ref)
    h = h0_ref[...]
    for t in range(n_steps):
        sl = slice(t * n_seq, (t + 1) * n_seq)
        h = a[sl, :] * h + b[sl, :]
        ya_ref[sl, :] = (h * jax.nn.gelu(ga_ref[sl, :])).astype(ya_ref.dtype)
    hl_ref[...] = h


def _s5_disc_kernel(are_ref, aim_ref, ldt_ref, abr_ref, abi_ref, cfr_ref, cfi_ref):
    a_re = are_ref[...]
    a_im = aim_ref[...]
    dt = jnp.exp(ldt_ref[...])
    mag = jnp.exp(a_re * dt)
    abr = mag * jnp.cos(a_im * dt)
    abi = mag * jnp.sin(a_im * dt)
    den = a_re * a_re + a_im * a_im
    nr = abr - 1.0
    abr_ref[...] = abr
    abi_ref[...] = abi
    cr = (nr * a_re + abi * a_im) / den
    ci = (abi * a_re - nr * a_im) / den
    for q in range(cfr_ref.shape[0]):
        cfr_ref[q] = cr
        cfi_ref[q] = ci
        cr, ci = abr * cr - abi * ci, abr * ci + abi * cr


def _s5_prompt_kernel(xb_ref, wb_ref, wc_ref, ar_ref, ai_ref, d_ref, g_ref, sre_ref, sim_ref, cre, cim):
    n_par, _, hw = ar_ref.shape

    @pl.when(pl.program_id(2) == 0)
    def _():
        cre[...] = jnp.zeros_like(cre)
        cim[...] = jnp.zeros_like(cim)

    for p in range(n_par):
        cols = slice(p * LANES, (p + 1) * LANES)
        y, cr, ci = _s5_prompt_tile(xb_ref[:, cols], wb_ref.at[p], wc_ref.at[p], ar_ref[p], ai_ref[p],
                                    d_ref[:, cols], cre[p], cim[p])
        g_ref[:, cols] = jax.nn.gelu(y)
        cre[p] = cr
        cim[p] = ci
        sre_ref[:, p * hw:(p + 1) * hw] = cr
        sim_ref[:, p * hw:(p + 1) * hw] = ci


def _s5_prompt_tile(xb, wb_ref, wc_ref, ar, ai, dsk, cr, ci):
    tm = xb.shape[0]
    hw = ar.shape[1]
    n_slab = tm // SUBLANES
    n_lags = wb_ref.shape[0] // xb.shape[1]
    xb3 = xb.reshape(n_slab, SUBLANES, xb.shape[1])
    subx = lax.broadcasted_iota(jnp.int32, (1, SUBLANES, xb.shape[1]), 1)
    lagged = [xb3] + [jnp.where(subx >= q, pltpu.roll(xb3, q, 1), 0.0) for q in range(1, n_lags)]
    xs = jnp.concatenate(lagged, axis=2).reshape(tm, n_lags * xb.shape[1])
    bu = jnp.dot(xs.astype(BF16), wb_ref[...], preferred_element_type=F32)
    hr = bu[:, :hw].reshape(n_slab, SUBLANES, hw)
    hi = bu[:, hw:].reshape(n_slab, SUBLANES, hw)
    sub = lax.broadcasted_iota(jnp.int32, (1, SUBLANES, hw), 1)
    pr = ar.reshape(1, 1, hw)
    pi = ai.reshape(1, 1, hw)
    tr = jnp.where(sub == 0, pr, 0.0)
    ti = jnp.where(sub == 0, pi, 0.0)
    d = 1
    while d < SUBLANES:
        mr = jnp.where(sub >= d, pr, 0.0)
        mi = jnp.where(sub >= d, pi, 0.0)
        if d >= n_lags:
            sr = pltpu.roll(hr, d, 1)
            si = pltpu.roll(hi, d, 1)
            hr, hi = hr + (mr * sr - mi * si), hi + (mr * si + mi * sr)
        sr = pltpu.roll(tr, d, 1)
        si = pltpu.roll(ti, d, 1)
        tr, ti = tr + (mr * sr - mi * si), ti + (mr * si + mi * sr)
        pr, pi = pr * pr - pi * pi, 2.0 * (pr * pi)
        d *= 2
    tr = tr[0]
    ti = ti[0]
    hrs, his = [], []
    for s in range(n_slab):
        cbr = jnp.broadcast_to(cr, (SUBLANES, hw))
        cbi = jnp.broadcast_to(ci, (SUBLANES, hw))
        sr = hr[s] + (tr * cbr - ti * cbi)
        si = hi[s] + (tr * cbi + ti * cbr)
        cr = sr[SUBLANES - 1:SUBLANES, :]
        ci = si[SUBLANES - 1:SUBLANES, :]
        hrs.append(sr)
        his.append(si)
    hcat = jnp.concatenate([jnp.concatenate(hrs, axis=0), jnp.concatenate(his, axis=0)], axis=1).astype(BF16)
    y = jnp.dot(hcat, wc_ref[...], preferred_element_type=F32)
    return y + dsk * xb, cr, ci


def _s5_sample_kernel(xb_ref, wb_ref, wc_ref, ar_ref, ai_ref, d_ref, s0r_ref, s0i_ref, g_all_ref,
                      g_ref, sre_ref, sim_ref, bu_scr, h_scr, *, n_seq, n_steps):
    del g_all_ref
    hw = ar_ref.shape[1]
    xb = xb_ref[...]
    bu_scr[...] = jnp.dot(xb.astype(BF16), wb_ref[...], preferred_element_type=F32)
    ar = ar_ref[...]
    ai = ai_ref[...]
    hr = s0r_ref[...]
    hi = s0i_ref[...]
    for t in range(n_steps):
        sl = slice(t * n_seq, (t + 1) * n_seq)
        hr, hi = (ar * hr - ai * hi) + bu_scr[sl, :hw], (ar * hi + ai * hr) + bu_scr[sl, hw:]
        h_scr[sl, :hw] = hr.astype(h_scr.dtype)
        h_scr[sl, hw:] = hi.astype(h_scr.dtype)
    sre_ref[...] = hr
    sim_ref[...] = hi
    y = jnp.dot(h_scr[...], wc_ref[...], preferred_element_type=F32)
    g_ref[...] = jax.nn.gelu(y + d_ref[...] * xb)


def _half_ffn_matmuls(h, wg, wu, wd, *, bm):
    m, dm = h.shape
    dff = wg.shape[1]
    bn_up = _pick(dff, (256, 128))
    n_i, n_j = m // bm, dff // bn_up
    rb = LANES
    n_rb = dff // rb
    assert n_i * n_j >= n_rb, "not enough up-projection steps to cover the down-projection weight"
    wd_blk = lambda i, j: (jnp.minimum(i * n_j + j, n_rb - 1), 0)
    a, wd = pl.pallas_call(
        _ffn_up_kernel,
        grid=(n_i, n_j),
        in_specs=[pl.BlockSpec((bm, dm), lambda i, j: (i, 0)),
                  pl.BlockSpec((dm, bn_up), lambda i, j: (0, j)),
                  pl.BlockSpec((dm, bn_up), lambda i, j: (0, j)),
                  pl.BlockSpec((rb, dm), wd_blk)],
        out_specs=[pl.BlockSpec((bm, bn_up), lambda i, j: (i, j)),
                   pl.BlockSpec((rb, dm), wd_blk)],
        out_shape=[jax.ShapeDtypeStruct((m, dff), BF16), jax.ShapeDtypeStruct((dff, dm), BF16)],
        compiler_params=_cparams("arbitrary", "arbitrary"),
        name="ffn_up",
    )(h, wg, wu, wd)

    bm_d = _pick(m, (1536, 1024, 512, 256, 128))
    bn_d = _pick(dm, (512, 256, 128))
    bk = dff // 2 if (dff // 2) % LANES == 0 else dff
    return pl.pallas_call(
        _ffn_down_kernel,
        grid=(m // bm_d, dm // bn_d, dff // bk),
        in_specs=[pl.BlockSpec((bm_d, bk), lambda i, j, k: (i, k)),
                  pl.BlockSpec((bk, bn_d), lambda i, j, k: (k, j))],
        out_specs=pl.BlockSpec((bm_d, bn_d), lambda i, j, k: (i, j)),
        out_shape=jax.ShapeDtypeStruct((m, dm), F32),
        compiler_params=_cparams("parallel", "parallel", "arbitrary"),
        name="ffn_down",
    )(a, wd)


def _matmul(x, w, *, bm, bn, out_dtype, name):
    m, k = x.shape
    n = w.shape[1]
    return pl.pallas_call(
        _mm_kernel,
        grid=(m // bm, n // bn),
        in_specs=[pl.BlockSpec((bm, k), lambda i, j: (i, 0)),
                  pl.BlockSpec((k, bn), lambda i, j: (0, j))],
        out_specs=pl.BlockSpec((bm, bn), lambda i, j: (i, j)),
        out_shape=jax.ShapeDtypeStruct((m, n), out_dtype),
        compiler_params=_cparams("parallel", "arbitrary"),
        name=name,
    )(x, w)


def kernel(x_prompt, x_sample, state_lru_h, state_conv, state_ssm_re, state_ssm_im, ffn1_pre_g, ffn1_post_g, ffn1_w_gate, ffn1_w_up, ffn1_w_down, mix_pre_g, mix_post_g, w_in, conv_w, conv_b, w_rg, b_rg, w_ig, b_ig, lru_lambda, ssm_a_re, ssm_a_im, ssm_log_dt, ssm_b_re, ssm_b_im, ssm_c_re, ssm_c_im, ssm_d, w_glu, b_glu, w_out_a, w_out_b, w_o, ffn2_pre_g, ffn2_post_g, ffn2_w_gate, ffn2_w_up, ffn2_w_down):
    nb, seq, dm = x_prompt.shape
    db, dseq, _ = x_sample.shape
    depth = state_lru_h.shape[0]
    assert depth == 1, "one decoder layer"
    d_lru = state_lru_h.shape[2]
    n_grp, ssm_p = state_ssm_re.shape[2], state_ssm_re.shape[3]
    d_ssm = ssm_d.shape[1]
    ssm_cg = d_ssm // n_grp
    d_in = w_in.shape[2]
    assert w_rg.shape[2] == LRU_BLK and conv_w.shape[1] == CONV_W
    assert d_in == 2 * d_lru + d_ssm + 2 * dm and seq >= CONV_W - 1 and dseq >= CONV_W - 1

    mp = nb * seq
    ms = db * dseq
    m = mp + ms
    sdt = state_lru_h.dtype

    row2 = lambda v: v.reshape(1, -1)
    bf = lambda v: v[0].astype(BF16)

    xp2 = x_prompt.reshape(mp, dm)
    xs2 = jnp.swapaxes(x_sample, 0, 1).reshape(ms, dm)

    tr = _pick(math.gcd(mp, ms), (256, 128, 64, 32, 16, 8))
    npt = mp // tr
    xp_spec = pl.BlockSpec((tr, dm), lambda i: (jnp.minimum(i, npt - 1), 0))
    xs_spec = pl.BlockSpec((tr, dm), lambda i: (jnp.maximum(i - npt, 0), 0))
    row_spec = pl.BlockSpec((tr, dm), lambda i: (i, 0))
    g_spec = pl.BlockSpec((1, dm), lambda i: (0, 0))

    h1 = pl.pallas_call(
        functools.partial(_prenorm_kernel, n_prompt_tiles=npt),
        grid=(m // tr,),
        in_specs=[xp_spec, xs_spec, g_spec],
        out_specs=row_spec,
        out_shape=jax.ShapeDtypeStruct((m, dm), BF16),
        compiler_params=_cparams("parallel"),
        name="prenorm1",
    )(xp2, xs2, row2(ffn1_pre_g))

    bm = _pick(m, (1536, 1024, 768, 512, 384, 256, 128))
    f1 = _half_ffn_matmuls(h1, ffn1_w_gate[0], ffn1_w_up[0], ffn1_w_down[0], bm=bm)

    x1, u = pl.pallas_call(
        functools.partial(_post_ffn1_kernel, n_prompt_tiles=npt),
        grid=(m // tr,),
        in_specs=[xp_spec, xs_spec, row_spec, g_spec, g_spec],
        out_specs=[row_spec, row_spec],
        out_shape=[jax.ShapeDtypeStruct((m, dm), F32), jax.ShapeDtypeStruct((m, dm), BF16)],
        compiler_params=_cparams("parallel"),
        name="post_ffn1",
    )(xp2, xs2, f1, row2(ffn1_post_g), row2(mix_pre_g))

    bn_in = _pick(math.gcd(d_lru, dm), (512, 256, 128))
    z = _matmul(u, w_in[0], bm=bm, bn=bn_in, out_dtype=F32, name="in_proj")

    cb = _pick(d_lru, (512, 256, 128))
    ncb = d_lru // cb
    hpb = cb // LRU_BLK
    wrg = bf(w_rg)
    wig = bf(w_ig)
    tm = _pick(seq, (512, 256, 128, 64, 32, 16, 8))
    rpt = seq // tm
    par3 = lambda shape: pl.BlockSpec(shape, lambda c, n, r: (0, c))
    lru_params = (conv_w[0], row2(conv_b), wrg, row2(b_rg), wig, row2(b_ig), row2(lru_lambda))
    ya, hl_p = pl.pallas_call(
        _rglru_prompt_kernel,
        grid=(ncb, nb, rpt),
        in_specs=[pl.BlockSpec((tm, cb), lambda c, n, r: (n * rpt + r, c)),
                  pl.BlockSpec((tm, cb), lambda c, n, r: (n * rpt + r, ncb + c)),
                  par3((CONV_W, cb)), par3((1, cb)),
                  pl.BlockSpec((hpb, LRU_BLK, LRU_BLK), lambda c, n, r: (c, 0, 0)), par3((1, cb)),
                  pl.BlockSpec((hpb, LRU_BLK, LRU_BLK), lambda c, n, r: (c, 0, 0)), par3((1, cb)),
                  par3((1, cb))],
        out_specs=[pl.BlockSpec((tm, cb), lambda c, n, r: (n * rpt + r, c)),
                   pl.BlockSpec((None, 1, cb), lambda c, n, r: (n, 0, c))],
        out_shape=[jax.ShapeDtypeStruct((m, d_lru), BF16), jax.ShapeDtypeStruct((nb, 1, d_lru), F32)],
        scratch_shapes=[pltpu.VMEM((SUBLANES, cb), F32), pltpu.VMEM((1, cb), F32)],
        compiler_params=_cparams("parallel", "arbitrary", "arbitrary"),
        name="rglru_prompt",
    )(z, z, *lru_params)

    assert mp % ms == 0, "sample rows must tile the unified row axis"
    s_blk = mp // ms
    par1 = lambda shape: pl.BlockSpec(shape, lambda c: (0, c))
    cst = jnp.swapaxes(state_conv[0], 0, 1)
    any_spec = pl.BlockSpec(memory_space=pl.ANY)
    ya, hl_s = pl.pallas_call(
        functools.partial(_rglru_sample_kernel, n_seq=db, n_steps=dseq),
        grid=(ncb,),
        in_specs=[pl.BlockSpec((ms, cb), lambda c: (s_blk, c)),
                  pl.BlockSpec((ms, cb), lambda c: (s_blk, ncb + c)),
                  pl.BlockSpec((CONV_W - 1, db, cb), lambda c: (0, 0, c)),
                  par1((db, cb)),
                  par1((CONV_W, cb)), par1((1, cb)),
                  pl.BlockSpec((hpb, LRU_BLK, LRU_BLK), lambda c: (c, 0, 0)), par1((1, cb)),
                  pl.BlockSpec((hpb, LRU_BLK, LRU_BLK), lambda c: (c, 0, 0)), par1((1, cb)),
                  par1((1, cb)), any_spec],
        out_specs=[pl.BlockSpec((ms, cb), lambda c: (s_blk, c)), par1((db, cb))],
        out_shape=[jax.ShapeDtypeStruct((m, d_lru), BF16), jax.ShapeDtypeStruct((db, d_lru), F32)],
        input_output_aliases={4 + len(lru_params): 0},
        compiler_params=_cparams("parallel"),
        name="rglru_sample",
    )(z, z, cst, state_lru_h[0], *lru_params, ya)

    n_lags = 4
    abr, abi, cfr, cfi = pl.pallas_call(
        _s5_disc_kernel,
        out_shape=[jax.ShapeDtypeStruct((n_grp, ssm_p), F32)] * 2
                  + [jax.ShapeDtypeStruct((n_lags, n_grp, ssm_p), F32)] * 2,
        name="s5_discretise",
    )(ssm_a_re[0], ssm_a_im[0], ssm_log_dt[0].reshape(n_grp, 1))
    bb_re = cfr[..., None] * ssm_b_re - cfi[..., None] * ssm_b_im
    bb_im = cfr[..., None] * ssm_b_im + cfi[..., None] * ssm_b_re

    gpb = LANES // ssm_cg
    nj = n_grp // gpb
    hw = gpb * ssm_p
    eye = jnp.eye(gpb, dtype=F32)

    def blockdiag_in(w):
        w6 = w.reshape(n_lags, nj, gpb, 1, ssm_p, ssm_cg) * eye[None, None, :, :, None, None]
        return jnp.transpose(w6, (1, 0, 2, 5, 3, 4)).reshape(nj, n_lags * gpb * ssm_cg, hw)

    def blockdiag_out(w):
        w5 = w.reshape(nj, gpb, 1, ssm_cg, ssm_p) * eye[None, :, :, None, None]
        return jnp.transpose(w5, (0, 2, 4, 1, 3)).reshape(nj, hw, gpb * ssm_cg)

    wb = jnp.concatenate([blockdiag_in(bb_re), blockdiag_in(bb_im)], axis=2).astype(BF16)
    wc = jnp.concatenate([blockdiag_out(ssm_c_re[0]), -blockdiag_out(ssm_c_im[0])], axis=1).astype(BF16)
    abr3 = abr.reshape(nj, 1, hw)
    abi3 = abi.reshape(nj, 1, hw)
    xb_blk = (2 * d_lru) // LANES
    dsk = row2(ssm_d)

    n_par = 2 if nj % 2 == 0 and xb_blk % 2 == 0 else 1
    g_all, sre_p, sim_p = pl.pallas_call(
        _s5_prompt_kernel,
        grid=(nj // n_par, nb, rpt),
        in_specs=[pl.BlockSpec((tm, n_par * LANES), lambda j, n, r: (n * rpt + r, xb_blk // n_par + j)),
                  pl.BlockSpec((n_par, n_lags * LANES, 2 * hw), lambda j, n, r: (j, 0, 0)),
                  pl.BlockSpec((n_par, 2 * hw, LANES), lambda j, n, r: (j, 0, 0)),
                  pl.BlockSpec((n_par, 1, hw), lambda j, n, r: (j, 0, 0)),
                  pl.BlockSpec((n_par, 1, hw), lambda j, n, r: (j, 0, 0)),
                  pl.BlockSpec((1, n_par * LANES), lambda j, n, r: (0, j))],
        out_specs=[pl.BlockSpec((tm, n_par * LANES), lambda j, n, r: (n * rpt + r, j)),
                   pl.BlockSpec((None, 1, n_par * hw), lambda j, n, r: (n, 0, j)),
                   pl.BlockSpec((None, 1, n_par * hw), lambda j, n, r: (n, 0, j))],
        out_shape=[jax.ShapeDtypeStruct((m, d_ssm), F32),
                   jax.ShapeDtypeStruct((nb, 1, n_grp * ssm_p), F32),
                   jax.ShapeDtypeStruct((nb, 1, n_grp * ssm_p), F32)],
        scratch_shapes=[pltpu.VMEM((n_par, 1, hw), F32), pltpu.VMEM((n_par, 1, hw), F32)],
        compiler_params=_cparams("parallel", "arbitrary", "arbitrary"),
        name="s5_prompt",
    )(z, wb, wc, abr3, abi3, dsk)

    s5_w_specs1 = [pl.BlockSpec((None, LANES, 2 * hw), lambda j: (j, 0, 0)),
                   pl.BlockSpec((None, 2 * hw, LANES), lambda j: (j, 0, 0)),
                   pl.BlockSpec((None, 1, hw), lambda j: (j, 0, 0)),
                   pl.BlockSpec((None, 1, hw), lambda j: (j, 0, 0)),
                   pl.BlockSpec((1, LANES), lambda j: (0, j))]
    g_all, sre_s, sim_s = pl.pallas_call(
        functools.partial(_s5_sample_kernel, n_seq=db, n_steps=dseq),
        grid=(nj,),
        in_specs=[pl.BlockSpec((ms, LANES), lambda j: (s_blk, xb_blk + j))] + s5_w_specs1
                 + [pl.BlockSpec((db, hw), lambda j: (0, j)), pl.BlockSpec((db, hw), lambda j: (0, j)), any_spec],
        out_specs=[pl.BlockSpec((ms, LANES), lambda j: (s_blk, j)),
                   pl.BlockSpec((db, hw), lambda j: (0, j)),
                   pl.BlockSpec((db, hw), lambda j: (0, j))],
        out_shape=[jax.ShapeDtypeStruct((m, d_ssm), F32),
                   jax.ShapeDtypeStruct((db, n_grp * ssm_p), F32),
                   jax.ShapeDtypeStruct((db, n_grp * ssm_p), F32)],
        scratch_shapes=[pltpu.VMEM((ms, 2 * hw), F32), pltpu.VMEM((ms, 2 * hw), BF16)],
        input_output_aliases={8: 0},
        compiler_params=_cparams("parallel"),
        name="s5_sample",
    )(z, wb, wc, abr3, abi3, dsk,
      state_ssm_re[0].reshape(db, n_grp * ssm_p), state_ssm_im[0].reshape(db, n_grp * ssm_p), g_all)

    bm_g = _pick(m, (1024, 512, 256, 128))
    bn_g = _pick(d_ssm, (512, 256, 128))
    yb = pl.pallas_call(
        _glu_kernel,
        grid=(m // bm_g, d_ssm // bn_g),
        in_specs=[pl.BlockSpec((bm_g, d_ssm), lambda i, j: (i, 0)),
                  pl.BlockSpec((bm_g, bn_g), lambda i, j: (i, j)),
                  pl.BlockSpec((d_ssm, bn_g), lambda i, j: (0, j)),
                  pl.BlockSpec((1, bn_g), lambda i, j: (0, j))],
        out_specs=pl.BlockSpec((bm_g, bn_g), lambda i, j: (i, j)),
        out_shape=jax.ShapeDtypeStruct((m, d_ssm), BF16),
        scratch_shapes=[pltpu.VMEM((bm_g, d_ssm), BF16)],
        compiler_params=_cparams("parallel", "arbitrary"),
        name="s5_glu",
    )(g_all, g_all, w_glu[0], row2(b_glu))

    bn_m = bn_in
    gla_blk = (2 * d_lru + d_ssm) // bn_m
    glb_blk = gla_blk + dm // bn_m
    merged = pl.pallas_call(
        _merge_kernel,
        grid=(m // bm_g, dm // bn_m),
        in_specs=[pl.BlockSpec((bm_g, d_lru), lambda i, j: (i, 0)),
                  pl.BlockSpec((bm_g, d_ssm), lambda i, j: (i, 0)),
                  pl.BlockSpec((d_lru, bn_m), lambda i, j: (0, j)),
                  pl.BlockSpec((d_ssm, bn_m), lambda i, j: (0, j)),
                  pl.BlockSpec((bm_g, bn_m), lambda i, j: (i, gla_blk + j)),
                  pl.BlockSpec((bm_g, bn_m), lambda i, j: (i, glb_blk + j))],
        out_specs=pl.BlockSpec((bm_g, bn_m), lambda i, j: (i, j)),
        out_shape=jax.ShapeDtypeStruct((m, dm), BF16),
        compiler_params=_cparams("parallel", "arbitrary"),
        name="gated_merge",
    )(ya, yb, w_out_a[0], w_out_b[0], z, z)

    o = _matmul(merged, w_o[0], bm=bm, bn=bn_in, out_dtype=F32, name="o_proj")

    x2, h2 = pl.pallas_call(
        _post_mix_kernel,
        grid=(m // tr,),
        in_specs=[row_spec, row_spec, g_spec, g_spec],
        out_specs=[row_spec, row_spec],
        out_shape=[jax.ShapeDtypeStruct((m, dm), F32), jax.ShapeDtypeStruct((m, dm), BF16)],
        compiler_params=_cparams("parallel"),
        name="post_mix",
    )(x1, o, row2(mix_post_g), row2(ffn2_pre_g))

    f2 = _half_ffn_matmuls(h2, ffn2_w_gate[0], ffn2_w_up[0], ffn2_w_down[0], bm=bm)

    def final(rows, first_tile):
        return pl.pallas_call(
            _final_kernel,
            grid=(rows // tr,),
            in_specs=[pl.BlockSpec((tr, dm), lambda i: (first_tile + i, 0)),
                      pl.BlockSpec((tr, dm), lambda i: (first_tile + i, 0)),
                      g_spec],
            out_specs=row_spec,
            out_shape=jax.ShapeDtypeStruct((rows, dm), F32),
            compiler_params=_cparams("parallel"),
            name="final_residual",
        )(x2, f2, row2(ffn2_post_g))

    y_prompt = final(mp, 0).reshape(nb, seq, dm)
    y_sample = jnp.swapaxes(final(ms, npt).reshape(dseq, db, dm), 0, 1)

    nk = CONV_W - 1
    prompt_conv = jnp.stack([lax.slice(z, ((n + 1) * seq - nk, 0), ((n + 1) * seq, d_lru)) for n in range(nb)])
    sample_conv = jnp.swapaxes(lax.slice(z, (m - nk * db, 0), (m, d_lru)).reshape(nk, db, d_lru), 0, 1)
    st = lambda v, n: v.reshape(1, n, n_grp, ssm_p).astype(sdt)
    return (y_prompt, y_sample,
            hl_p.reshape(1, nb, d_lru).astype(sdt), prompt_conv[None].astype(sdt),
            st(sre_p, nb), st(sim_p, nb),
            hl_s.reshape(1, db, d_lru).astype(sdt), sample_conv[None].astype(sdt),
            st(sre_s, db), st(sim_s, db))
```

```python
import functools
import math

import jax
import jax.numpy as jnp
from jax import lax
from jax.experimental import pallas as pl
from jax.experimental.pallas import tpu as pltpu

F32 = jnp.float32
BF16 = jnp.bfloat16

EPS = 1e-6
C_RG = 8.0
CONV_W = 4
LANES = 128
SUBLANES = 8
LRU_BLK = 128
VMEM_LIMIT = 56 * 1024 * 1024


def _cparams(*sem):
    return pltpu.CompilerParams(dimension_semantics=sem, vmem_limit_bytes=VMEM_LIMIT)


def _pick(n, candidates):
    for c in candidates:
        if n % c == 0:
            return c
    raise ValueError(f"no tile in {candidates} divides {n}")


def _rms(x, g):
    return x * lax.rsqrt(jnp.mean(x * x, axis=-1, keepdims=True) + EPS) * g


def _softplus(x):
    return jnp.maximum(x, 0.0) + jnp.log1p(jnp.exp(-jnp.abs(x)))


def _two_group_rows(i, n_prompt_tiles, xp_ref, xs_ref, body):
    @pl.when(i < n_prompt_tiles)
    def _():
        body(xp_ref[...])

    @pl.when(i >= n_prompt_tiles)
    def _():
        body(xs_ref[...])


def _prenorm_kernel(xp_ref, xs_ref, g_ref, h_ref, *, n_prompt_tiles):
    def body(x):
        h_ref[...] = _rms(x, g_ref[...]).astype(h_ref.dtype)

    _two_group_rows(pl.program_id(0), n_prompt_tiles, xp_ref, xs_ref, body)


def _post_ffn1_kernel(xp_ref, xs_ref, f_ref, gpost_ref, gpre_ref, x1_ref, u_ref, *, n_prompt_tiles):
    def body(x):
        x1 = x + 0.5 * _rms(f_ref[...], gpost_ref[...])
        x1_ref[...] = x1
        u_ref[...] = _rms(x1, gpre_ref[...]).astype(u_ref.dtype)

    _two_group_rows(pl.program_id(0), n_prompt_tiles, xp_ref, xs_ref, body)


def _post_mix_kernel(x1_ref, o_ref, gpost_ref, gpre_ref, x2_ref, h_ref):
    x2 = x1_ref[...] + _rms(o_ref[...], gpost_ref[...])
    x2_ref[...] = x2
    h_ref[...] = _rms(x2, gpre_ref[...]).astype(h_ref.dtype)


def _final_kernel(x2_ref, f_ref, g_ref, y_ref):
    y_ref[...] = x2_ref[...] + 0.5 * _rms(f_ref[...], g_ref[...])


def _ffn_up_kernel(h_ref, wg_ref, wu_ref, wd_ref, a_ref, wd16_ref):
    h = h_ref[...]
    g = jnp.dot(h, wg_ref[...].astype(BF16), preferred_element_type=F32)
    u = jnp.dot(h, wu_ref[...].astype(BF16), preferred_element_type=F32)
    a_ref[...] = (jax.nn.silu(g) * u).astype(a_ref.dtype)
    wd16_ref[...] = wd_ref[...].astype(wd16_ref.dtype)


def _ffn_down_kernel(a_ref, wd_ref, o_ref):
    p = jnp.dot(a_ref[...], wd_ref[...], preferred_element_type=F32)

    @pl.when(pl.program_id(2) == 0)
    def _():
        o_ref[...] = p

    @pl.when(pl.program_id(2) != 0)
    def _():
        o_ref[...] += p


def _mm_kernel(x_ref, w_ref, o_ref):
    w = w_ref[...].astype(BF16)
    o_ref[...] = jnp.dot(x_ref[...], w, preferred_element_type=F32).astype(o_ref.dtype)


def _glu_kernel(g_ref, gcol_ref, w_ref, b_ref, o_ref, g_scr):
    @pl.when(pl.program_id(1) == 0)
    def _():
        g_scr[...] = g_ref[...].astype(g_scr.dtype)

    s = jnp.dot(g_scr[...], w_ref[...].astype(BF16), preferred_element_type=F32) + b_ref[...]
    o_ref[...] = (gcol_ref[...] * jax.nn.sigmoid(s)).astype(o_ref.dtype)


def _merge_kernel(ya_ref, yb_ref, wa_ref, wb_ref, gla_ref, glb_ref, o_ref):
    pa = jnp.dot(ya_ref[...], wa_ref[...].astype(BF16), preferred_element_type=F32)
    pb = jnp.dot(yb_ref[...], wb_ref[...].astype(BF16), preferred_element_type=F32)
    m = jax.nn.sigmoid(gla_ref[...]) * pa + jax.nn.sigmoid(glb_ref[...]) * pb
    o_ref[...] = m.astype(o_ref.dtype)


def _lru_gates(xc, wrg_ref, brg_ref, wig_ref, big_ref, lam_ref):
    xcb = xc.astype(BF16)
    rs, gs = [], []
    for hh in range(wrg_ref.shape[0]):
        xh = xcb[:, hh * LRU_BLK:(hh + 1) * LRU_BLK]
        rs.append(jnp.dot(xh, wrg_ref[hh], preferred_element_type=F32))
        gs.append(jnp.dot(xh, wig_ref[hh], preferred_element_type=F32))
    r = jax.nn.sigmoid(jnp.concatenate(rs, axis=1) + brg_ref[...])
    i = jax.nn.sigmoid(jnp.concatenate(gs, axis=1) + big_ref[...])
    log_a = -C_RG * r * _softplus(-lam_ref[...])
    a = jnp.exp(log_a)
    mult = jnp.sqrt(-jnp.tanh(log_a) * (a * a + 1.0))
    return a, mult * (i * xc)


def _rglru_prompt_kernel(xa_ref, ga_ref, cw_ref, cb_ref, wrg_ref, brg_ref, wig_ref, big_ref, lam_ref,
                         ya_ref, hl_ref, xcar, hcar):
    tm = xa_ref.shape[0]

    @pl.when(pl.program_id(2) == 0)
    def _():
        xcar[...] = jnp.zeros_like(xcar)
        hcar[...] = jnp.zeros_like(hcar)

    x = xa_ref[...]
    xfull = jnp.concatenate([xcar[...], x], axis=0)
    xcar[...] = x[tm - SUBLANES:, :]
    cw = cw_ref[...]
    xc = cb_ref[...]
    for k in range(CONV_W - 1):
        xs = pltpu.roll(xfull, CONV_W - 1 - k, 0)[SUBLANES:, :]
        xc = xc + xs * cw[k:k + 1, :]
    xc = xc + x * cw[CONV_W - 1:CONV_W, :]

    a, b = _lru_gates(xc, wrg_ref, brg_ref, wig_ref, big_ref, lam_ref)
    cb = a.shape[1]
    n_slab = tm // SUBLANES
    a = a.reshape(n_slab, SUBLANES, cb)
    b = b.reshape(n_slab, SUBLANES, cb)
    sub = lax.broadcasted_iota(jnp.int32, (1, SUBLANES, cb), 1)
    d = 1
    while d < SUBLANES:
        m = sub >= d
        a_s = jnp.where(m, pltpu.roll(a, d, 1), 1.0)
        b_s = jnp.where(m, pltpu.roll(b, d, 1), 0.0)
        b = a * b_s + b
        a = a * a_s
        d *= 2
    h = hcar[...]
    hs = []
    for s in range(n_slab):
        h_slab = b[s] + a[s] * jnp.broadcast_to(h, (SUBLANES, cb))
        h = h_slab[SUBLANES - 1:SUBLANES, :]
        hs.append(h_slab)
    hcar[...] = h
    hl_ref[...] = h
    ya_ref[...] = (jnp.concatenate(hs, axis=0) * jax.nn.gelu(ga_ref[...])).astype(ya_ref.dtype)


def _rglru_sample_kernel(xa_ref, ga_ref, cst_ref, h0_ref, cw_ref, cb_ref, wrg_ref, brg_ref, wig_ref,
                         big_ref, lam_ref, ya_all_ref, ya_ref, hl_ref, *, n_seq, n_steps):
    del ya_all_ref
    x = xa_ref[...]
    xp = jnp.concatenate([cst_ref[k] for k in range(CONV_W - 1)] + [x], axis=0)
    rows = n_seq * n_steps
    cw = cw_ref[...]
    xc = cb_ref[...]
    for k in range(CONV_W):
        xc = xc + xp[k * n_seq:k * n_seq + rows, :] * cw[k:k + 1, :]
    a, b = _lru_gates(xc, wrg_ref, brg_ref, wig_ref, big_ref, lam_ref)
    h = h0_ref[...]
    for t in range(n_steps):
        sl = slice(t * n_seq, (t + 1) * n_seq)
        h = a[sl, :] * h + b[sl, :]
        ya_ref[sl, :] = (h * jax.nn.gelu(ga_ref[sl, :])).astype(ya_ref.dtype)
    hl_ref[...] = h


def _s5_disc_kernel(are_ref, aim_ref, ldt_ref, abr_ref, abi_ref, cfr_ref, cfi_ref):
    a_re = are_ref[...]
    a_im = aim_ref[...]
    dt = jnp.exp(ldt_ref[...])
    mag = jnp.exp(a_re * dt)
    abr = mag * jnp.cos(a_im * dt)
    abi = mag * jnp.sin(a_im * dt)
    den = a_re * a_re + a_im * a_im
    nr = abr - 1.0
    abr_ref[...] = abr
    abi_ref[...] = abi
    cr = (nr * a_re + abi * a_im) / den
    ci = (abi * a_re - nr * a_im) / den
    for q in range(cfr_ref.shape[0]):
        cfr_ref[q] = cr
        cfi_ref[q] = ci
        cr, ci = abr * cr - abi * ci, abr * ci + abi * cr


def _gates_s5_prompt_kernel(u_ref, w_ref, xb_ref, wb_ref, wc_ref, ar_ref, ai_ref, d_ref,
                            gl_ref, g_ref, sre_ref, sim_ref, cre, cim):
    n_par, _, hw = ar_ref.shape

    @pl.when(pl.program_id(2) == 0)
    def _():
        cre[...] = jnp.zeros_like(cre)
        cim[...] = jnp.zeros_like(cim)

    split = (gl_ref.shape[0] * 9 // 16) // (2 * SUBLANES) * (2 * SUBLANES)
    w = w_ref[...].astype(BF16)
    bus = [_s5_project_in(xb_ref[:, p * LANES:(p + 1) * LANES], wb_ref.at[p]) for p in range(n_par)]
    gl_ref[:split, :] = jnp.dot(u_ref[:split, :], w, preferred_element_type=F32)
    scans = [_s5_slab_scan(bus[p], ar_ref[p], ai_ref[p], cre[p], cim[p]) for p in range(n_par)]
    y = jnp.dot(jnp.concatenate([s[0] for s in scans], axis=1), wc_ref[...], preferred_element_type=F32)
    gl_ref[split:, :] = jnp.dot(u_ref[split:, :], w, preferred_element_type=F32)
    g_ref[...] = jax.nn.gelu(y + d_ref[...] * xb_ref[...])
    for p in range(n_par):
        _, cr, ci = scans[p]
        cre[p] = cr
        cim[p] = ci
        sre_ref[:, p * hw:(p + 1) * hw] = cr
        sim_ref[:, p * hw:(p + 1) * hw] = ci


def _s5_project_in(xb, wb_ref):
    tm, width = xb.shape
    n_slab = tm // SUBLANES
    n_lags = wb_ref.shape[0] // width
    xb3 = xb.reshape(n_slab, SUBLANES, width)
    subx = lax.broadcasted_iota(jnp.int32, (1, SUBLANES, width), 1)
    lagged = [xb3] + [jnp.where(subx >= q, pltpu.roll(xb3, q, 1), 0.0) for q in range(1, n_lags)]
    xs = jnp.concatenate(lagged, axis=2).reshape(tm, n_lags * width)
    return jnp.dot(xs.astype(BF16), wb_ref[...], preferred_element_type=F32), n_lags


def _s5_slab_scan(bu_lags, ar, ai, cr, ci):
    bu, n_lags = bu_lags
    tm = bu.shape[0]
    hw = ar.shape[1]
    n_slab = tm // SUBLANES
    hr = bu[:, :hw].reshape(n_slab, SUBLANES, hw)
    hi = bu[:, hw:].reshape(n_slab, SUBLANES, hw)
    sub = lax.broadcasted_iota(jnp.int32, (1, SUBLANES, hw), 1)
    pr = ar.reshape(1, 1, hw)
    pi = ai.reshape(1, 1, hw)
    tr = jnp.where(sub == 0, pr, 0.0)
    ti = jnp.where(sub == 0, pi, 0.0)
    d = 1
    while d < SUBLANES:
        mr = jnp.where(sub >= d, pr, 0.0)
        mi = jnp.where(sub >= d, pi, 0.0)
        if d >= n_lags:
            sr = pltpu.roll(hr, d, 1)
            si = pltpu.roll(hi, d, 1)
            hr, hi = hr + (mr * sr - mi * si), hi + (mr * si + mi * sr)
        sr = pltpu.roll(tr, d, 1)
        si = pltpu.roll(ti, d, 1)
        tr, ti = tr + (mr * sr - mi * si), ti + (mr * si + mi * sr)
        pr, pi = pr * pr - pi * pi, 2.0 * (pr * pi)
        d *= 2
    tr = tr[0]
    ti = ti[0]
    hrs, his = [], []
    for s in range(n_slab):
        cbr = jnp.broadcast_to(cr, (SUBLANES, hw))
        cbi = jnp.broadcast_to(ci, (SUBLANES, hw))
        sr = hr[s] + (tr * cbr - ti * cbi)
        si = hi[s] + (tr * cbi + ti * cbr)
        cr = sr[SUBLANES - 1:SUBLANES, :]
        ci = si[SUBLANES - 1:SUBLANES, :]
        hrs.append(sr)
        his.append(si)
    hcat = jnp.concatenate([jnp.concatenate(hrs, axis=0), jnp.concatenate(his, axis=0)], axis=1).astype(BF16)
    return hcat, cr, ci


def _s5_sample_kernel(xb_ref, wb_ref, wc_ref, ar_ref, ai_ref, d_ref, s0r_ref, s0i_ref, g_all_ref,
                      g_ref, sre_ref, sim_ref, bu_scr, h_scr, *, n_seq, n_steps):
    del g_all_ref
    hw = ar_ref.shape[1]
    xb = xb_ref[...]
    bu_scr[...] = jnp.dot(xb.astype(BF16), wb_ref[...], preferred_element_type=F32)
    ar = ar_ref[...]
    ai = ai_ref[...]
    hr = s0r_ref[...]
    hi = s0i_ref[...]
    for t in range(n_steps):
        sl = slice(t * n_seq, (t + 1) * n_seq)
        hr, hi = (ar * hr - ai * hi) + bu_scr[sl, :hw], (ar * hi + ai * hr) + bu_scr[sl, hw:]
        h_scr[sl, :hw] = hr.astype(h_scr.dtype)
        h_scr[sl, hw:] = hi.astype(h_scr.dtype)
    sre_ref[...] = hr
    sim_ref[...] = hi
    y = jnp.dot(h_scr[...], wc_ref[...], preferred_element_type=F32)
    g_ref[...] = jax.nn.gelu(y + d_ref[...] * xb)


def _half_ffn_matmuls(h, wg, wu, wd, *, bm):
    m, dm = h.shape
    dff = wg.shape[1]
    bn_up = _pick(dff, (256, 128))
    n_i, n_j = m // bm, dff // bn_up
    rb = LANES
    n_rb = dff // rb
    assert n_i * n_j >= n_rb, "not enough up-projection steps to cover the down-projection weight"
    wd_blk = lambda i, j: (jnp.minimum(i * n_j + j, n_rb - 1), 0)
    a, wd = pl.pallas_call(
        _ffn_up_kernel,
        grid=(n_i, n_j),
        in_specs=[pl.BlockSpec((bm, dm), lambda i, j: (i, 0)),
                  pl.BlockSpec((dm, bn_up), lambda i, j: (0, j)),
                  pl.BlockSpec((dm, bn_up), lambda i, j: (0, j)),
                  pl.BlockSpec((rb, dm), wd_blk)],
        out_specs=[pl.BlockSpec((bm, bn_up), lambda i, j: (i, j)),
                   pl.BlockSpec((rb, dm), wd_blk)],
        out_shape=[jax.ShapeDtypeStruct((m, dff), BF16), jax.ShapeDtypeStruct((dff, dm), BF16)],
        compiler_params=_cparams("arbitrary", "arbitrary"),
        name="ffn_up",
    )(h, wg, wu, wd)

    bm_d = _pick(m, (1536, 1024, 512, 256, 128))
    bn_d = _pick(dm, (512, 256, 128))
    bk = dff // 2 if (dff // 2) % LANES == 0 else dff
    return pl.pallas_call(
        _ffn_down_kernel,
        grid=(m // bm_d, dm // bn_d, dff // bk),
        in_specs=[pl.BlockSpec((bm_d, bk), lambda i, j, k: (i, k)),
                  pl.BlockSpec((bk, bn_d), lambda i, j, k: (k, j))],
        out_specs=pl.BlockSpec((bm_d, bn_d), lambda i, j, k: (i, j)),
        out_shape=jax.ShapeDtypeStruct((m, dm), F32),
        compiler_params=_cparams("parallel", "parallel", "arbitrary"),
        name="ffn_down",
    )(a, wd)


def _matmul(x, w, *, bm, bn, out_dtype, name, n=None):
    m, k = x.shape
    n = w.shape[1] if n is None else n
    return pl.pallas_call(
        _mm_kernel,
        grid=(m // bm, n // bn),
        in_specs=[pl.BlockSpec((bm, k), lambda i, j: (i, 0)),
                  pl.BlockSpec((k, bn), lambda i, j: (0, j))],
        out_specs=pl.BlockSpec((bm, bn), lambda i, j: (i, j)),
        out_shape=jax.ShapeDtypeStruct((m, n), out_dtype),
        compiler_params=_cparams("parallel", "arbitrary"),
        name=name,
    )(x, w)


def kernel(x_prompt, x_sample, state_lru_h, state_conv, state_ssm_re, state_ssm_im, ffn1_pre_g, ffn1_post_g, ffn1_w_gate, ffn1_w_up, ffn1_w_down, mix_pre_g, mix_post_g, w_in, conv_w, conv_b, w_rg, b_rg, w_ig, b_ig, lru_lambda, ssm_a_re, ssm_a_im, ssm_log_dt, ssm_b_re, ssm_b_im, ssm_c_re, ssm_c_im, ssm_d, w_glu, b_glu, w_out_a, w_out_b, w_o, ffn2_pre_g, ffn2_post_g, ffn2_w_gate, ffn2_w_up, ffn2_w_down):
    nb, seq, dm = x_prompt.shape
    db, dseq, _ = x_sample.shape
    depth = state_lru_h.shape[0]
    assert depth == 1, "one decoder layer"
    d_lru = state_lru_h.shape[2]
    n_grp, ssm_p = state_ssm_re.shape[2], state_ssm_re.shape[3]
    d_ssm = ssm_d.shape[1]
    ssm_cg = d_ssm // n_grp
    d_in = w_in.shape[2]
    assert w_rg.shape[2] == LRU_BLK and conv_w.shape[1] == CONV_W
    assert d_in == 2 * d_lru + d_ssm + 2 * dm and seq >= CONV_W - 1 and dseq >= CONV_W - 1

    mp = nb * seq
    ms = db * dseq
    m = mp + ms
    sdt = state_lru_h.dtype

    row2 = lambda v: v.reshape(1, -1)
    bf = lambda v: v[0].astype(BF16)

    xp2 = x_prompt.reshape(mp, dm)
    xs2 = jnp.swapaxes(x_sample, 0, 1).reshape(ms, dm)

    tr = _pick(math.gcd(mp, ms), (256, 128, 64, 32, 16, 8))
    npt = mp // tr
    xp_spec = pl.BlockSpec((tr, dm), lambda i: (jnp.minimum(i, npt - 1), 0))
    xs_spec = pl.BlockSpec((tr, dm), lambda i: (jnp.maximum(i - npt, 0), 0))
    row_spec = pl.BlockSpec((tr, dm), lambda i: (i, 0))
    g_spec = pl.BlockSpec((1, dm), lambda i: (0, 0))

    h1 = pl.pallas_call(
        functools.partial(_prenorm_kernel, n_prompt_tiles=npt),
        grid=(m // tr,),
        in_specs=[xp_spec, xs_spec, g_spec],
        out_specs=row_spec,
        out_shape=jax.ShapeDtypeStruct((m, dm), BF16),
        compiler_params=_cparams("parallel"),
        name="prenorm1",
    )(xp2, xs2, row2(ffn1_pre_g))

    bm = _pick(m, (1536, 1024, 768, 512, 384, 256, 128))
    f1 = _half_ffn_matmuls(h1, ffn1_w_gate[0], ffn1_w_up[0], ffn1_w_down[0], bm=bm)

    x1, u = pl.pallas_call(
        functools.partial(_post_ffn1_kernel, n_prompt_tiles=npt),
        grid=(m // tr,),
        in_specs=[xp_spec, xs_spec, row_spec, g_spec, g_spec],
        out_specs=[row_spec, row_spec],
        out_shape=[jax.ShapeDtypeStruct((m, dm), F32), jax.ShapeDtypeStruct((m, dm), BF16)],
        compiler_params=_cparams("parallel"),
        name="post_ffn1",
    )(xp2, xs2, f1, row2(ffn1_post_g), row2(mix_pre_g))

    bn_in = _pick(math.gcd(d_lru, dm), (512, 256, 128))
    n_a = 2 * d_lru + d_ssm
    z = _matmul(u, w_in[0], bm=bm, bn=bn_in, out_dtype=F32, name="in_proj", n=n_a)

    cb = _pick(d_lru, (512, 256, 128))
    ncb = d_lru // cb
    hpb = cb // LRU_BLK
    wrg = bf(w_rg)
    wig = bf(w_ig)
    tm = _pick(seq, (512, 256, 128, 64, 32, 16, 8))
    rpt = seq // tm
    par3 = lambda shape: pl.BlockSpec(shape, lambda c, n, r: (0, c))
    lru_params = (conv_w[0], row2(conv_b), wrg, row2(b_rg), wig, row2(b_ig), row2(lru_lambda))
    ya, hl_p = pl.pallas_call(
        _rglru_prompt_kernel,
        grid=(ncb, nb, rpt),
        in_specs=[pl.BlockSpec((tm, cb), lambda c, n, r: (n * rpt + r, c)),
                  pl.BlockSpec((tm, cb), lambda c, n, r: (n * rpt + r, ncb + c)),
                  par3((CONV_W, cb)), par3((1, cb)),
                  pl.BlockSpec((hpb, LRU_BLK, LRU_BLK), lambda c, n, r: (c, 0, 0)), par3((1, cb)),
                  pl.BlockSpec((hpb, LRU_BLK, LRU_BLK), lambda c, n, r: (c, 0, 0)), par3((1, cb)),
                  par3((1, cb))],
        out_specs=[pl.BlockSpec((tm, cb), lambda c, n, r: (n * rpt + r, c)),
                   pl.BlockSpec((None, 1, cb), lambda c, n, r: (n, 0, c))],
        out_shape=[jax.ShapeDtypeStruct((m, d_lru), BF16), jax.ShapeDtypeStruct((nb, 1, d_lru), F32)],
        scratch_shapes=[pltpu.VMEM((SUBLANES, cb), F32), pltpu.VMEM((1, cb), F32)],
        compiler_params=_cparams("parallel", "arbitrary", "arbitrary"),
        name="rglru_prompt",
    )(z, z, *lru_params)

    assert mp % ms == 0, "sample rows must tile the unified row axis"
    s_blk = mp // ms
    par1 = lambda shape: pl.BlockSpec(shape, lambda c: (0, c))
    cst = jnp.swapaxes(state_conv[0], 0, 1)
    any_spec = pl.BlockSpec(memory_space=pl.ANY)
    ya, hl_s = pl.pallas_call(
        functools.partial(_rglru_sample_kernel, n_seq=db, n_steps=dseq),
        grid=(ncb,),
        in_specs=[pl.BlockSpec((ms, cb), lambda c: (s_blk, c)),
                  pl.BlockSpec((ms, cb), lambda c: (s_blk, ncb + c)),
                  pl.BlockSpec((CONV_W - 1, db, cb), lambda c: (0, 0, c)),
                  par1((db, cb)),
                  par1((CONV_W, cb)), par1((1, cb)),
                  pl.BlockSpec((hpb, LRU_BLK, LRU_BLK), lambda c: (c, 0, 0)), par1((1, cb)),
                  pl.BlockSpec((hpb, LRU_BLK, LRU_BLK), lambda c: (c, 0, 0)), par1((1, cb)),
                  par1((1, cb)), any_spec],
        out_specs=[pl.BlockSpec((ms, cb), lambda c: (s_blk, c)), par1((db, cb))],
        out_shape=[jax.ShapeDtypeStruct((m, d_lru), BF16), jax.ShapeDtypeStruct((db, d_lru), F32)],
        input_output_aliases={4 + len(lru_params): 0},
        compiler_params=_cparams("parallel"),
        name="rglru_sample",
    )(z, z, cst, state_lru_h[0], *lru_params, ya)

    n_lags = 2
    abr, abi, cfr, cfi = pl.pallas_call(
        _s5_disc_kernel,
        out_shape=[jax.ShapeDtypeStruct((n_grp, ssm_p), F32)] * 2
                  + [jax.ShapeDtypeStruct((n_lags, n_grp, ssm_p), F32)] * 2,
        name="s5_discretise",
    )(ssm_a_re[0], ssm_a_im[0], ssm_log_dt[0].reshape(n_grp, 1))
    bb_re = cfr[..., None] * ssm_b_re - cfi[..., None] * ssm_b_im
    bb_im = cfr[..., None] * ssm_b_im + cfi[..., None] * ssm_b_re

    gpb = LANES // ssm_cg
    nj = n_grp // gpb
    hw = gpb * ssm_p
    eye = jnp.eye(gpb, dtype=F32)

    def blockdiag_in(w):
        w6 = w.reshape(n_lags, nj, gpb, 1, ssm_p, ssm_cg) * eye[None, None, :, :, None, None]
        return jnp.transpose(w6, (1, 0, 2, 5, 3, 4)).reshape(nj, n_lags * gpb * ssm_cg, hw)

    def blockdiag_out(w):
        w5 = w.reshape(nj, gpb, 1, ssm_cg, ssm_p) * eye[None, :, :, None, None]
        return jnp.transpose(w5, (0, 2, 4, 1, 3)).reshape(nj, hw, gpb * ssm_cg)

    wb = jnp.concatenate([blockdiag_in(bb_re), blockdiag_in(bb_im)], axis=2).astype(BF16)
    wc = jnp.concatenate([blockdiag_out(ssm_c_re[0]), -blockdiag_out(ssm_c_im[0])], axis=1).astype(BF16)
    abr3 = abr.reshape(nj, 1, hw)
    abi3 = abi.reshape(nj, 1, hw)
    xb_blk = (2 * d_lru) // LANES
    dsk = row2(ssm_d)

    n_par = 2 if nj % 2 == 0 and xb_blk % 2 == 0 else 1
    eye_p = jnp.eye(n_par, dtype=BF16)
    wc_par = (wc.reshape(nj // n_par, n_par, 1, 2 * hw, LANES) * eye_p[None, :, :, None, None])
    wc_par = jnp.transpose(wc_par, (0, 1, 3, 2, 4)).reshape(nj // n_par, n_par * 2 * hw, n_par * LANES)
    n_gi = nj // n_par
    bm_gl = m // n_gi
    assert m % n_gi == 0 and bm_gl % (2 * SUBLANES) == 0

    def gl_cols_ok(t):
        bn = (2 * dm) // (nb * (seq // t)) if (2 * dm) % (nb * (seq // t)) == 0 else 0
        return bn > 0 and bn % LANES == 0 and n_a % bn == 0

    tm5 = _pick(seq, tuple(t for t in (512, 256, 128, 64, 32, 16, 8) if seq % t == 0 and gl_cols_ok(t)))
    rp5 = seq // tm5
    bn_gl = (2 * dm) // (nb * rp5)
    gl_blk = n_a // bn_gl
    gl, g_all, sre_p, sim_p = pl.pallas_call(
        _gates_s5_prompt_kernel,
        grid=(n_gi, nb, rp5),
        in_specs=[pl.BlockSpec((bm_gl, dm), lambda j, n, r: (j, 0)),
                  pl.BlockSpec((dm, bn_gl), lambda j, n, r: (0, gl_blk + n * rp5 + r)),
                  pl.BlockSpec((tm5, n_par * LANES), lambda j, n, r: (n * rp5 + r, xb_blk // n_par + j)),
                  pl.BlockSpec((n_par, n_lags * LANES, 2 * hw), lambda j, n, r: (j, 0, 0)),
                  pl.BlockSpec((None, n_par * 2 * hw, n_par * LANES), lambda j, n, r: (j, 0, 0)),
                  pl.BlockSpec((n_par, 1, hw), lambda j, n, r: (j, 0, 0)),
                  pl.BlockSpec((n_par, 1, hw), lambda j, n, r: (j, 0, 0)),
                  pl.BlockSpec((1, n_par * LANES), lambda j, n, r: (0, j))],
        out_specs=[pl.BlockSpec((bm_gl, bn_gl), lambda j, n, r: (j, n * rp5 + r)),
                   pl.BlockSpec((tm5, n_par * LANES), lambda j, n, r: (n * rp5 + r, j)),
                   pl.BlockSpec((None, 1, n_par * hw), lambda j, n, r: (n, 0, j)),
                   pl.BlockSpec((None, 1, n_par * hw), lambda j, n, r: (n, 0, j))],
        out_shape=[jax.ShapeDtypeStruct((m, 2 * dm), F32),
                   jax.ShapeDtypeStruct((m, d_ssm), F32),
                   jax.ShapeDtypeStruct((nb, 1, n_grp * ssm_p), F32),
                   jax.ShapeDtypeStruct((nb, 1, n_grp * ssm_p), F32)],
        scratch_shapes=[pltpu.VMEM((n_par, 1, hw), F32), pltpu.VMEM((n_par, 1, hw), F32)],
        compiler_params=_cparams("arbitrary", "arbitrary", "arbitrary"),
        name="gates_s5_prompt",
    )(u, w_in[0], z, wb, wc_par, abr3, abi3, dsk)

    s5_w_specs1 = [pl.BlockSpec((None, LANES, 2 * hw), lambda j: (j, 0, 0)),
                   pl.BlockSpec((None, 2 * hw, LANES), lambda j: (j, 0, 0)),
                   pl.BlockSpec((None, 1, hw), lambda j: (j, 0, 0)),
                   pl.BlockSpec((None, 1, hw), lambda j: (j, 0, 0)),
                   pl.BlockSpec((1, LANES), lambda j: (0, j))]
    g_all, sre_s, sim_s = pl.pallas_call(
        functools.partial(_s5_sample_kernel, n_seq=db, n_steps=dseq),
        grid=(nj,),
        in_specs=[pl.BlockSpec((ms, LANES), lambda j: (s_blk, xb_blk + j))] + s5_w_specs1
                 + [pl.BlockSpec((db, hw), lambda j: (0, j)), pl.BlockSpec((db, hw), lambda j: (0, j)), any_spec],
        out_specs=[pl.BlockSpec((ms, LANES), lambda j: (s_blk, j)),
                   pl.BlockSpec((db, hw), lambda j: (0, j)),
                   pl.BlockSpec((db, hw), lambda j: (0, j))],
        out_shape=[jax.ShapeDtypeStruct((m, d_ssm), F32),
                   jax.ShapeDtypeStruct((db, n_grp * ssm_p), F32),
                   jax.ShapeDtypeStruct((db, n_grp * ssm_p), F32)],
        scratch_shapes=[pltpu.VMEM((ms, 2 * hw), F32), pltpu.VMEM((ms, 2 * hw), BF16)],
        input_output_aliases={8: 0},
        compiler_params=_cparams("parallel"),
        name="s5_sample",
    )(z, wb, wc, abr3, abi3, dsk,
      state_ssm_re[0].reshape(db, n_grp * ssm_p), state_ssm_im[0].reshape(db, n_grp * ssm_p), g_all)

    bm_g = _pick(m, (1024, 512, 256, 128))
    bn_g = _pick(d_ssm, (512, 256, 128))
    yb = pl.pallas_call(
        _glu_kernel,
        grid=(m // bm_g, d_ssm // bn_g),
        in_specs=[pl.BlockSpec((bm_g, d_ssm), lambda i, j: (i, 0)),
                  pl.BlockSpec((bm_g, bn_g), lambda i, j: (i, j)),
                  pl.BlockSpec((d_ssm, bn_g), lambda i, j: (0, j)),
                  pl.BlockSpec((1, bn_g), lambda i, j: (0, j))],
        out_specs=pl.BlockSpec((bm_g, bn_g), lambda i, j: (i, j)),
        out_shape=jax.ShapeDtypeStruct((m, d_ssm), BF16),
        scratch_shapes=[pltpu.VMEM((bm_g, d_ssm), BF16)],
        compiler_params=_cparams("parallel", "arbitrary"),
        name="s5_glu",
    )(g_all, g_all, w_glu[0], row2(b_glu))

    bn_m = bn_in
    gla_blk = 0
    glb_blk = dm // bn_m
    merged = pl.pallas_call(
        _merge_kernel,
        grid=(m // bm_g, dm // bn_m),
        in_specs=[pl.BlockSpec((bm_g, d_lru), lambda i, j: (i, 0)),
                  pl.BlockSpec((bm_g, d_ssm), lambda i, j: (i, 0)),
                  pl.BlockSpec((d_lru, bn_m), lambda i, j: (0, j)),
                  pl.BlockSpec((d_ssm, bn_m), lambda i, j: (0, j)),
                  pl.BlockSpec((bm_g, bn_m), lambda i, j: (i, gla_blk + j)),
                  pl.BlockSpec((bm_g, bn_m), lambda i, j: (i, glb_blk + j))],
        out_specs=pl.BlockSpec((bm_g, bn_m), lambda i, j: (i, j)),
        out_shape=jax.ShapeDtypeStruct((m, dm), BF16),
        compiler_params=_cparams("parallel", "arbitrary"),
        name="gated_merge",
    )(ya, yb, w_out_a[0], w_out_b[0], gl, gl)

    o = _matmul(merged, w_o[0], bm=bm, bn=bn_in, out_dtype=F32, name="o_proj")

    x2, h2 = pl.pallas_call(
        _post_mix_kernel,
        grid=(m // tr,),
        in_specs=[row_spec, row_spec, g_spec, g_spec],
        out_specs=[row_spec, row_spec],
        out_shape=[jax.ShapeDtypeStruct((m, dm), F32), jax.ShapeDtypeStruct((m, dm), BF16)],
        compiler_params=_cparams("parallel"),
        name="post_mix",
    )(x1, o, row2(mix_post_g), row2(ffn2_pre_g))

    f2 = _half_ffn_matmuls(h2, ffn2_w_gate[0], ffn2_w_up[0], ffn2_w_down[0], bm=bm)

    def final(rows, first_tile):
        return pl.pallas_call(
            _final_kernel,
            grid=(rows // tr,),
            in_specs=[pl.BlockSpec((tr, dm), lambda i: (first_tile + i, 0)),
                      pl.BlockSpec((tr, dm), lambda i: (first_tile + i, 0)),
                      g_spec],
            out_specs=row_spec,
            out_shape=jax.ShapeDtypeStruct((rows, dm), F32),
            compiler_params=_cparams("parallel"),
            name="final_residual",
        )(x2, f2, row2(ffn2_post_g))

    y_prompt = final(mp, 0).reshape(nb, seq, dm)
    y_sample = jnp.swapaxes(final(ms, npt).reshape(dseq, db, dm), 0, 1)

    nk = CONV_W - 1
    prompt_conv = jnp.stack([lax.slice(z, ((n + 1) * seq - nk, 0), ((n + 1) * seq, d_lru)) for n in range(nb)])
    sample_conv = jnp.swapaxes(lax.slice(z, (m - nk * db, 0), (m, d_lru)).reshape(nk, db, d_lru), 0, 1)
    st = lambda v, n: v.reshape(1, n, n_grp, ssm_p).astype(sdt)
    return (y_prompt, y_sample,
            hl_p.reshape(1, nb, d_lru).astype(sdt), prompt_conv[None].astype(sdt),
            st(sre_p, nb), st(sim_p, nb),
            hl_s.reshape(1, db, d_lru).astype(sdt), sample_conv[None].astype(sdt),
            st(sre_s, db), st(sim_s, db))
```

```python
import functools
import math

import jax
import jax.numpy as jnp
from jax import lax
from jax.experimental import pallas as pl
from jax.experimental.pallas import tpu as pltpu

F32 = jnp.float32
BF16 = jnp.bfloat16

EPS = 1e-6
C_RG = 8.0
CONV_W = 4
LANES = 128
SUBLANES = 8
LRU_BLK = 128
VMEM_LIMIT = 56 * 1024 * 1024


def _cparams(*sem):
    return pltpu.CompilerParams(dimension_semantics=sem, vmem_limit_bytes=VMEM_LIMIT)


def _pick(n, candidates):
    for c in candidates:
        if n % c == 0:
            return c
    raise ValueError(f"no tile in {candidates} divides {n}")


def _rms(x, g):
    return x * lax.rsqrt(jnp.mean(x * x, axis=-1, keepdims=True) + EPS) * g


def _softplus(x):
    return jnp.maximum(x, 0.0) + jnp.log1p(jnp.exp(-jnp.abs(x)))


def _two_group_rows(i, n_prompt_tiles, xp_ref, xs_ref, body):
    @pl.when(i < n_prompt_tiles)
    def _():
        body(xp_ref[...])

    @pl.when(i >= n_prompt_tiles)
    def _():
        body(xs_ref[...])


def _prenorm_kernel(xp_ref, xs_ref, g_ref, h_ref, *, n_prompt_tiles):
    def body(x):
        h_ref[...] = _rms(x, g_ref[...]).astype(h_ref.dtype)

    _two_group_rows(pl.program_id(0), n_prompt_tiles, xp_ref, xs_ref, body)


def _post_ffn1_kernel(xp_ref, xs_ref, f_ref, gpost_ref, gpre_ref, x1_ref, u_ref, *, n_prompt_tiles):
    def body(x):
        x1 = x + 0.5 * _rms(f_ref[...], gpost_ref[...])
        x1_ref[...] = x1
        u_ref[...] = _rms(x1, gpre_ref[...]).astype(u_ref.dtype)

    _two_group_rows(pl.program_id(0), n_prompt_tiles, xp_ref, xs_ref, body)


def _post_mix_kernel(x1_ref, o_ref, gpost_ref, gpre_ref, x2_ref, h_ref):
    x2 = x1_ref[...] + _rms(o_ref[...], gpost_ref[...])
    x2_ref[...] = x2
    h_ref[...] = _rms(x2, gpre_ref[...]).astype(h_ref.dtype)


def _final_kernel(x2_ref, f_ref, g_ref, y_ref):
    y_ref[...] = x2_ref[...] + 0.5 * _rms(f_ref[...], g_ref[...])


def _ffn_up_kernel(h_ref, wg_ref, wu_ref, wd_ref, a_ref, wd16_ref):
    wg = wg_ref[...].astype(BF16)
    wu = wu_ref[...].astype(BF16)
    half = h_ref.shape[0] // 2
    for rows in (slice(0, half), slice(half, None)):
        h = h_ref[rows, :]
        g = jnp.dot(h, wg, preferred_element_type=F32)
        u = jnp.dot(h, wu, preferred_element_type=F32)
        a_ref[rows, :] = (jax.nn.silu(g) * u).astype(a_ref.dtype)
    wd16_ref[...] = wd_ref[...].astype(wd16_ref.dtype)


def _ffn_down_kernel(a_ref, wd_ref, o_ref):
    p = jnp.dot(a_ref[...], wd_ref[...], preferred_element_type=F32)

    @pl.when(pl.program_id(2) == 0)
    def _():
        o_ref[...] = p

    @pl.when(pl.program_id(2) != 0)
    def _():
        o_ref[...] += p


def _mm_kernel(x_ref, w_ref, o_ref):
    w = w_ref[...].astype(BF16)
    o_ref[...] = jnp.dot(x_ref[...], w, preferred_element_type=F32).astype(o_ref.dtype)


def _glu_kernel(g_ref, gcol_ref, w_ref, b_ref, o_ref, g_scr):
    @pl.when(pl.program_id(1) == 0)
    def _():
        g_scr[...] = g_ref[...].astype(g_scr.dtype)

    s = jnp.dot(g_scr[...], w_ref[...].astype(BF16), preferred_element_type=F32) + b_ref[...]
    o_ref[...] = (gcol_ref[...] * jax.nn.sigmoid(s)).astype(o_ref.dtype)


def _merge_kernel(ya_ref, yb_ref, wa_ref, wb_ref, gla_ref, glb_ref, o_ref):
    wa = wa_ref[...].astype(BF16)
    wb = wb_ref[...].astype(BF16)
    half = o_ref.shape[0] // 2
    for rows in (slice(0, half), slice(half, None)):
        pa = jnp.dot(ya_ref[rows, :], wa, preferred_element_type=F32)
        pb = jnp.dot(yb_ref[rows, :], wb, preferred_element_type=F32)
        m = jax.nn.sigmoid(gla_ref[rows, :]) * pa + jax.nn.sigmoid(glb_ref[rows, :]) * pb
        o_ref[rows, :] = m.astype(o_ref.dtype)


def _lru_gates(xc, wrg_ref, brg_ref, wig_ref, big_ref, lam_ref):
    xcb = xc.astype(BF16)
    rs, gs = [], []
    for hh in range(wrg_ref.shape[0]):
        xh = xcb[:, hh * LRU_BLK:(hh + 1) * LRU_BLK]
        rs.append(jnp.dot(xh, wrg_ref[hh], preferred_element_type=F32))
        gs.append(jnp.dot(xh, wig_ref[hh], preferred_element_type=F32))
    r = jax.nn.sigmoid(jnp.concatenate(rs, axis=1) + brg_ref[...])
    i = jax.nn.sigmoid(jnp.concatenate(gs, axis=1) + big_ref[...])
    log_a = -C_RG * r * _softplus(-lam_ref[...])
    a = jnp.exp(log_a)
    mult = jnp.sqrt(-jnp.tanh(log_a) * (a * a + 1.0))
    return a, mult * (i * xc)


def _rglru_prompt_kernel(xa_ref, ga_ref, cw_ref, cb_ref, wrg_ref, brg_ref, wig_ref, big_ref, lam_ref,
                         ya_all_ref, ya_ref, hl_ref, xcar, hcar):
    del ya_all_ref
    tm = xa_ref.shape[0]

    @pl.when(pl.program_id(2) == 0)
    def _():
        xcar[...] = jnp.zeros_like(xcar)
        hcar[...] = jnp.zeros_like(hcar)

    x = xa_ref[...]
    xfull = jnp.concatenate([xcar[...], x], axis=0)
    xcar[...] = x[tm - SUBLANES:, :]
    cw = cw_ref[...]
    xc = cb_ref[...]
    for k in range(CONV_W - 1):
        xs = pltpu.roll(xfull, CONV_W - 1 - k, 0)[SUBLANES:, :]
        xc = xc + xs * cw[k:k + 1, :]
    xc = xc + x * cw[CONV_W - 1:CONV_W, :]

    a, b = _lru_gates(xc, wrg_ref, brg_ref, wig_ref, big_ref, lam_ref)
    cb = a.shape[1]
    n_slab = tm // SUBLANES
    a = a.reshape(n_slab, SUBLANES, cb)
    b = b.reshape(n_slab, SUBLANES, cb)
    sub = lax.broadcasted_iota(jnp.int32, (1, SUBLANES, cb), 1)
    d = 1
    while d < SUBLANES:
        m = sub >= d
        a_s = jnp.where(m, pltpu.roll(a, d, 1), 1.0)
        b_s = jnp.where(m, pltpu.roll(b, d, 1), 0.0)
        b = a * b_s + b
        a = a * a_s
        d *= 2
    h = hcar[...]
    hs = []
    for s in range(n_slab):
        h_slab = b[s] + a[s] * jnp.broadcast_to(h, (SUBLANES, cb))
        h = h_slab[SUBLANES - 1:SUBLANES, :]
        hs.append(h_slab)
    hcar[...] = h
    hl_ref[...] = h
    ya_ref[...] = (jnp.concatenate(hs, axis=0) * jax.nn.gelu(ga_ref[...])).astype(ya_ref.dtype)


def _rglru_sample_kernel(xa_ref, ga_ref, cst_ref, h0_ref, cw_ref, cb_ref, wrg_ref, brg_ref, wig_ref,
                         big_ref, lam_ref, ya_all_ref, ya_ref, hl_ref, *, n_seq, n_steps):
    del ya_all_ref
    x = xa_ref[...]
    xp = jnp.concatenate([cst_ref[k] for k in range(CONV_W - 1)] + [x], axis=0)
    rows = n_seq * n_steps
    cw = cw_ref[...]
    xc = cb_ref[...]
    for k in range(CONV_W):
        xc = xc + xp[k * n_seq:k * n_seq + rows, :] * cw[k:k + 1, :]
    a, b = _lru_gates(xc, wrg_ref, brg_ref, wig_ref, big_ref, lam_ref)
    h = h0_ref[...]
    for t in range(n_steps):
        sl = slice(t * n_seq, (t + 1) * n_seq)
        h = a[sl, :] * h + b[sl, :]
        ya_ref[sl, :] = (h * jax.nn.gelu(ga_ref[sl, :])).astype(ya_ref.dtype)
    hl_ref[...] = h


def _s5_disc_kernel(are_ref, aim_ref, ldt_ref, abr_ref, abi_ref, cfr_ref, cfi_ref):
    a_re = are_ref[...]
    a_im = aim_ref[...]
    dt = jnp.exp(ldt_ref[...])
    mag = jnp.exp(a_re * dt)
    abr = mag * jnp.cos(a_im * dt)
    abi = mag * jnp.sin(a_im * dt)
    den = a_re * a_re + a_im * a_im
    nr = abr - 1.0
    abr_ref[...] = abr
    abi_ref[...] = abi
    cr = (nr * a_re + abi * a_im) / den
    ci = (abi * a_re - nr * a_im) / den
    for q in range(cfr_ref.shape[0]):
        cfr_ref[q] = cr
        cfi_ref[q] = ci
        cr, ci = abr * cr - abi * ci, abr * ci + abi * cr


def _gates_s5_prompt_kernel(u_ref, w_ref, xb_ref, wb_ref, wc_ref, ar_ref, ai_ref, d_ref, g_all_ref,
                            gl_ref, g_ref, sre_ref, sim_ref, cre, cim):
    del g_all_ref
    n_par, _, hw = ar_ref.shape

    @pl.when(pl.program_id(2) == 0)
    def _():
        cre[...] = jnp.zeros_like(cre)
        cim[...] = jnp.zeros_like(cim)

    split = (gl_ref.shape[0] * 9 // 16) // (2 * SUBLANES) * (2 * SUBLANES)
    w = w_ref[...].astype(BF16)
    bus = [_s5_project_in(xb_ref[:, p * LANES:(p + 1) * LANES], wb_ref.at[p]) for p in range(n_par)]
    gl_ref[:split, :] = jnp.dot(u_ref[:split, :], w, preferred_element_type=F32)
    scans = [_s5_slab_scan(bus[p], ar_ref[p], ai_ref[p], cre[p], cim[p]) for p in range(n_par)]
    y = jnp.dot(jnp.concatenate([s[0] for s in scans], axis=1), wc_ref[...], preferred_element_type=F32)
    gl_ref[split:, :] = jnp.dot(u_ref[split:, :], w, preferred_element_type=F32)
    g_ref[...] = jax.nn.gelu(y + d_ref[...] * xb_ref[...])
    for p in range(n_par):
        _, cr, ci = scans[p]
        cre[p] = cr
        cim[p] = ci
        sre_ref[:, p * hw:(p + 1) * hw] = cr
        sim_ref[:, p * hw:(p + 1) * hw] = ci


def _s5_project_in(xb, wb_ref):
    tm, width = xb.shape
    n_slab = tm // SUBLANES
    n_lags = wb_ref.shape[0] // width
    xb3 = xb.reshape(n_slab, SUBLANES, width)
    subx = lax.broadcasted_iota(jnp.int32, (1, SUBLANES, width), 1)
    lagged = [xb3] + [jnp.where(subx >= q, pltpu.roll(xb3, q, 1), 0.0) for q in range(1, n_lags)]
    xs = jnp.concatenate(lagged, axis=2).reshape(tm, n_lags * width)
    return jnp.dot(xs.astype(BF16), wb_ref[...], preferred_element_type=F32), n_lags


def _s5_slab_scan(bu_lags, ar, ai, cr, ci):
    bu, n_lags = bu_lags
    tm = bu.shape[0]
    hw = ar.shape[1]
    n_slab = tm // SUBLANES
    hr = bu[:, :hw].reshape(n_slab, SUBLANES, hw)
    hi = bu[:, hw:].reshape(n_slab, SUBLANES, hw)
    sub = lax.broadcasted_iota(jnp.int32, (1, SUBLANES, hw), 1)
    pr = ar.reshape(1, 1, hw)
    pi = ai.reshape(1, 1, hw)
    tr = jnp.where(sub == 0, pr, 0.0)
    ti = jnp.where(sub == 0, pi, 0.0)
    d = 1
    while d < SUBLANES:
        mr = jnp.where(sub >= d, pr, 0.0)
        mi = jnp.where(sub >= d, pi, 0.0)
        if d >= n_lags:
            sr = pltpu.roll(hr, d, 1)
            si = pltpu.roll(hi, d, 1)
            hr, hi = hr + (mr * sr - mi * si), hi + (mr * si + mi * sr)
        sr = pltpu.roll(tr, d, 1)
        si = pltpu.roll(ti, d, 1)
        tr, ti = tr + (mr * sr - mi * si), ti + (mr * si + mi * sr)
        pr, pi = pr * pr - pi * pi, 2.0 * (pr * pi)
        d *= 2
    tr = tr[0]
    ti = ti[0]
    hrs, his = [], []
    for s in range(n_slab):
        cbr = jnp.broadcast_to(cr, (SUBLANES, hw))
        cbi = jnp.broadcast_to(ci, (SUBLANES, hw))
        sr = hr[s] + (tr * cbr - ti * cbi)
        si = hi[s] + (tr * cbi + ti * cbr)
        cr = sr[SUBLANES - 1:SUBLANES, :]
        ci = si[SUBLANES - 1:SUBLANES, :]
        hrs.append(sr)
        his.append(si)
    hcat = jnp.concatenate([jnp.concatenate(hrs, axis=0), jnp.concatenate(his, axis=0)], axis=1).astype(BF16)
    return hcat, cr, ci


def _s5_sample_kernel(xb_ref, wb_ref, wc_ref, ar_ref, ai_ref, d_ref, s0r_ref, s0i_ref, g_all_ref,
                      g_ref, sre_ref, sim_ref, bu_scr, h_scr, *, n_seq, n_steps):
    del g_all_ref
    hw = ar_ref.shape[1]
    xb = xb_ref[...]
    bu_scr[...] = jnp.dot(xb.astype(BF16), wb_ref[...], preferred_element_type=F32)
    ar = ar_ref[...]
    ai = ai_ref[...]
    hr = s0r_ref[...]
    hi = s0i_ref[...]
    for t in range(n_steps):
        sl = slice(t * n_seq, (t + 1) * n_seq)
        hr, hi = (ar * hr - ai * hi) + bu_scr[sl, :hw], (ar * hi + ai * hr) + bu_scr[sl, hw:]
        h_scr[sl, :hw] = hr.astype(h_scr.dtype)
        h_scr[sl, hw:] = hi.astype(h_scr.dtype)
    sre_ref[...] = hr
    sim_ref[...] = hi
    y = jnp.dot(h_scr[...], wc_ref[...], preferred_element_type=F32)
    g_ref[...] = jax.nn.gelu(y + d_ref[...] * xb)


def _half_ffn_matmuls(h, wg, wu, wd, *, bm):
    m, dm = h.shape
    dff = wg.shape[1]
    bn_up = _pick(dff, (256, 128))
    n_i, n_j = m // bm, dff // bn_up
    rb = LANES
    n_rb = dff // rb
    assert n_i * n_j >= n_rb, "not enough up-projection steps to cover the down-projection weight"
    wd_blk = lambda i, j: (jnp.minimum(i * n_j + j, n_rb - 1), 0)
    a, wd = pl.pallas_call(
        _ffn_up_kernel,
        grid=(n_i, n_j),
        in_specs=[pl.BlockSpec((bm, dm), lambda i, j: (i, 0)),
                  pl.BlockSpec((dm, bn_up), lambda i, j: (0, j)),
                  pl.BlockSpec((dm, bn_up), lambda i, j: (0, j)),
                  pl.BlockSpec((rb, dm), wd_blk)],
        out_specs=[pl.BlockSpec((bm, bn_up), lambda i, j: (i, j)),
                   pl.BlockSpec((rb, dm), wd_blk)],
        out_shape=[jax.ShapeDtypeStruct((m, dff), BF16), jax.ShapeDtypeStruct((dff, dm), BF16)],
        compiler_params=_cparams("arbitrary", "arbitrary"),
        name="ffn_up",
    )(h, wg, wu, wd)

    bm_d = _pick(m, (1024, 512, 256, 128))
    bn_d = _pick(dm, (1024, 512, 256, 128))
    bk = dff // 2 if (dff // 2) % LANES == 0 else dff
    return pl.pallas_call(
        _ffn_down_kernel,
        grid=(m // bm_d, dm // bn_d, dff // bk),
        in_specs=[pl.BlockSpec((bm_d, bk), lambda i, j, k: (i, k)),
                  pl.BlockSpec((bk, bn_d), lambda i, j, k: (k, j))],
        out_specs=pl.BlockSpec((bm_d, bn_d), lambda i, j, k: (i, j)),
        out_shape=jax.ShapeDtypeStruct((m, dm), F32),
        compiler_params=_cparams("parallel", "parallel", "arbitrary"),
        name="ffn_down",
    )(a, wd)


def _matmul(x, w, *, bm, bn, out_dtype, name, n=None):
    m, k = x.shape
    n = w.shape[1] if n is None else n
    return pl.pallas_call(
        _mm_kernel,
        grid=(m // bm, n // bn),
        in_specs=[pl.BlockSpec((bm, k), lambda i, j: (i, 0)),
                  pl.BlockSpec((k, bn), lambda i, j: (0, j))],
        out_specs=pl.BlockSpec((bm, bn), lambda i, j: (i, j)),
        out_shape=jax.ShapeDtypeStruct((m, n), out_dtype),
        compiler_params=_cparams("parallel", "arbitrary"),
        name=name,
    )(x, w)


def kernel(x_prompt, x_sample, state_lru_h, state_conv, state_ssm_re, state_ssm_im, ffn1_pre_g, ffn1_post_g, ffn1_w_gate, ffn1_w_up, ffn1_w_down, mix_pre_g, mix_post_g, w_in, conv_w, conv_b, w_rg, b_rg, w_ig, b_ig, lru_lambda, ssm_a_re, ssm_a_im, ssm_log_dt, ssm_b_re, ssm_b_im, ssm_c_re, ssm_c_im, ssm_d, w_glu, b_glu, w_out_a, w_out_b, w_o, ffn2_pre_g, ffn2_post_g, ffn2_w_gate, ffn2_w_up, ffn2_w_down):
    nb, seq, dm = x_prompt.shape
    db, dseq, _ = x_sample.shape
    depth = state_lru_h.shape[0]
    assert depth == 1, "one decoder layer"
    d_lru = state_lru_h.shape[2]
    n_grp, ssm_p = state_ssm_re.shape[2], state_ssm_re.shape[3]
    d_ssm = ssm_d.shape[1]
    ssm_cg = d_ssm // n_grp
    d_in = w_in.shape[2]
    assert w_rg.shape[2] == LRU_BLK and conv_w.shape[1] == CONV_W
    assert d_in == 2 * d_lru + d_ssm + 2 * dm and seq >= CONV_W - 1 and dseq >= CONV_W - 1

    mp = nb * seq
    ms = db * dseq
    m = mp + ms
    sdt = state_lru_h.dtype

    row2 = lambda v: v.reshape(1, -1)
    bf = lambda v: v[0].astype(BF16)

    xp2 = x_prompt.reshape(mp, dm)
    xs2 = jnp.swapaxes(x_sample, 0, 1).reshape(ms, dm)

    tr = _pick(math.gcd(mp, ms), (256, 128, 64, 32, 16, 8))
    npt = mp // tr
    xp_spec = pl.BlockSpec((tr, dm), lambda i: (jnp.minimum(i, npt - 1), 0))
    xs_spec = pl.BlockSpec((tr, dm), lambda i: (jnp.maximum(i - npt, 0), 0))
    row_spec = pl.BlockSpec((tr, dm), lambda i: (i, 0))
    g_spec = pl.BlockSpec((1, dm), lambda i: (0, 0))

    h1 = pl.pallas_call(
        functools.partial(_prenorm_kernel, n_prompt_tiles=npt),
        grid=(m // tr,),
        in_specs=[xp_spec, xs_spec, g_spec],
        out_specs=row_spec,
        out_shape=jax.ShapeDtypeStruct((m, dm), BF16),
        compiler_params=_cparams("parallel"),
        name="prenorm1",
    )(xp2, xs2, row2(ffn1_pre_g))

    bm = _pick(m, (1536, 1024, 768, 512, 384, 256, 128))
    f1 = _half_ffn_matmuls(h1, ffn1_w_gate[0], ffn1_w_up[0], ffn1_w_down[0], bm=bm)

    x1, u = pl.pallas_call(
        functools.partial(_post_ffn1_kernel, n_prompt_tiles=npt),
        grid=(m // tr,),
        in_specs=[xp_spec, xs_spec, row_spec, g_spec, g_spec],
        out_specs=[row_spec, row_spec],
        out_shape=[jax.ShapeDtypeStruct((m, dm), F32), jax.ShapeDtypeStruct((m, dm), BF16)],
        compiler_params=_cparams("parallel"),
        name="post_ffn1",
    )(xp2, xs2, f1, row2(ffn1_post_g), row2(mix_pre_g))

    bn_in = _pick(math.gcd(d_lru, dm), (512, 256, 128))
    n_a = 2 * d_lru + d_ssm
    z = _matmul(u, w_in[0], bm=bm, bn=bn_in, out_dtype=F32, name="in_proj", n=n_a)

    cb = _pick(d_lru, (512, 256, 128))
    ncb = d_lru // cb
    hpb = cb // LRU_BLK
    wrg = bf(w_rg)
    wig = bf(w_ig)
    tm = _pick(seq, (512, 256, 128, 64, 32, 16, 8))
    rpt = seq // tm
    par3 = lambda shape: pl.BlockSpec(shape, lambda c, n, r: (0, c))
    lru_params = (conv_w[0], row2(conv_b), wrg, row2(b_rg), wig, row2(b_ig), row2(lru_lambda))
    assert d_lru <= dm and d_ssm <= dm
    any_spec = pl.BlockSpec(memory_space=pl.ANY)
    ya, hl_p = pl.pallas_call(
        _rglru_prompt_kernel,
        grid=(ncb, nb, rpt),
        in_specs=[pl.BlockSpec((tm, cb), lambda c, n, r: (n * rpt + r, c)),
                  pl.BlockSpec((tm, cb), lambda c, n, r: (n * rpt + r, ncb + c)),
                  par3((CONV_W, cb)), par3((1, cb)),
                  pl.BlockSpec((hpb, LRU_BLK, LRU_BLK), lambda c, n, r: (c, 0, 0)), par3((1, cb)),
                  pl.BlockSpec((hpb, LRU_BLK, LRU_BLK), lambda c, n, r: (c, 0, 0)), par3((1, cb)),
                  par3((1, cb)), any_spec],
        out_specs=[pl.BlockSpec((tm, cb), lambda c, n, r: (n * rpt + r, c)),
                   pl.BlockSpec((None, 1, cb), lambda c, n, r: (n, 0, c))],
        out_shape=[jax.ShapeDtypeStruct((m, dm), BF16), jax.ShapeDtypeStruct((nb, 1, d_lru), F32)],
        scratch_shapes=[pltpu.VMEM((SUBLANES, cb), F32), pltpu.VMEM((1, cb), F32)],
        input_output_aliases={2 + len(lru_params): 0},
        compiler_params=_cparams("parallel", "arbitrary", "arbitrary"),
        name="rglru_prompt",
    )(z, z, *lru_params, h1)

    assert mp % ms == 0, "sample rows must tile the unified row axis"
    s_blk = mp // ms
    par1 = lambda shape: pl.BlockSpec(shape, lambda c: (0, c))
    cst = jnp.swapaxes(state_conv[0], 0, 1)
    ya, hl_s = pl.pallas_call(
        functools.partial(_rglru_sample_kernel, n_seq=db, n_steps=dseq),
        grid=(ncb,),
        in_specs=[pl.BlockSpec((ms, cb), lambda c: (s_blk, c)),
                  pl.BlockSpec((ms, cb), lambda c: (s_blk, ncb + c)),
                  pl.BlockSpec((CONV_W - 1, db, cb), lambda c: (0, 0, c)),
                  par1((db, cb)),
                  par1((CONV_W, cb)), par1((1, cb)),
                  pl.BlockSpec((hpb, LRU_BLK, LRU_BLK), lambda c: (c, 0, 0)), par1((1, cb)),
                  pl.BlockSpec((hpb, LRU_BLK, LRU_BLK), lambda c: (c, 0, 0)), par1((1, cb)),
                  par1((1, cb)), any_spec],
        out_specs=[pl.BlockSpec((ms, cb), lambda c: (s_blk, c)), par1((db, cb))],
        out_shape=[jax.ShapeDtypeStruct((m, dm), BF16), jax.ShapeDtypeStruct((db, d_lru), F32)],
        input_output_aliases={4 + len(lru_params): 0},
        compiler_params=_cparams("parallel"),
        name="rglru_sample",
    )(z, z, cst, state_lru_h[0], *lru_params, ya)

    n_lags = 2
    abr, abi, cfr, cfi = pl.pallas_call(
        _s5_disc_kernel,
        out_shape=[jax.ShapeDtypeStruct((n_grp, ssm_p), F32)] * 2
                  + [jax.ShapeDtypeStruct((n_lags, n_grp, ssm_p), F32)] * 2,
        name="s5_discretise",
    )(ssm_a_re[0], ssm_a_im[0], ssm_log_dt[0].reshape(n_grp, 1))
    bb_re = cfr[..., None] * ssm_b_re - cfi[..., None] * ssm_b_im
    bb_im = cfr[..., None] * ssm_b_im + cfi[..., None] * ssm_b_re

    gpb = LANES // ssm_cg
    nj = n_grp // gpb
    hw = gpb * ssm_p
    eye = jnp.eye(gpb, dtype=F32)

    def blockdiag_in(w):
        w6 = w.reshape(n_lags, nj, gpb, 1, ssm_p, ssm_cg) * eye[None, None, :, :, None, None]
        return jnp.transpose(w6, (1, 0, 2, 5, 3, 4)).reshape(nj, n_lags * gpb * ssm_cg, hw)

    def blockdiag_out(w):
        w5 = w.reshape(nj, gpb, 1, ssm_cg, ssm_p) * eye[None, :, :, None, None]
        return jnp.transpose(w5, (0, 2, 4, 1, 3)).reshape(nj, hw, gpb * ssm_cg)

    wb = jnp.concatenate([blockdiag_in(bb_re), blockdiag_in(bb_im)], axis=2).astype(BF16)
    wc = jnp.concatenate([blockdiag_out(ssm_c_re[0]), -blockdiag_out(ssm_c_im[0])], axis=1).astype(BF16)
    abr3 = abr.reshape(nj, 1, hw)
    abi3 = abi.reshape(nj, 1, hw)
    xb_blk = (2 * d_lru) // LANES
    dsk = row2(ssm_d)

    n_par = 2 if nj % 2 == 0 and xb_blk % 2 == 0 else 1
    eye_p = jnp.eye(n_par, dtype=BF16)
    wc_par = (wc.reshape(nj // n_par, n_par, 1, 2 * hw, LANES) * eye_p[None, :, :, None, None])
    wc_par = jnp.transpose(wc_par, (0, 1, 3, 2, 4)).reshape(nj // n_par, n_par * 2 * hw, n_par * LANES)
    n_gi = nj // n_par
    bm_gl = m // n_gi
    assert m % n_gi == 0 and bm_gl % (2 * SUBLANES) == 0

    def gl_cols_ok(t):
        bn = (2 * dm) // (nb * (seq // t)) if (2 * dm) % (nb * (seq // t)) == 0 else 0
        return bn > 0 and bn % LANES == 0 and n_a % bn == 0

    tm5 = _pick(seq, tuple(t for t in (512, 256, 128, 64, 32, 16, 8) if seq % t == 0 and gl_cols_ok(t)))
    rp5 = seq // tm5
    bn_gl = (2 * dm) // (nb * rp5)
    gl_blk = n_a // bn_gl
    gl, g_all, sre_p, sim_p = pl.pallas_call(
        _gates_s5_prompt_kernel,
        grid=(n_gi, nb, rp5),
        in_specs=[pl.BlockSpec((bm_gl, dm), lambda j, n, r: (j, 0)),
                  pl.BlockSpec((dm, bn_gl), lambda j, n, r: (0, gl_blk + n * rp5 + r)),
                  pl.BlockSpec((tm5, n_par * LANES), lambda j, n, r: (n * rp5 + r, xb_blk // n_par + j)),
                  pl.BlockSpec((n_par, n_lags * LANES, 2 * hw), lambda j, n, r: (j, 0, 0)),
                  pl.BlockSpec((None, n_par * 2 * hw, n_par * LANES), lambda j, n, r: (j, 0, 0)),
                  pl.BlockSpec((n_par, 1, hw), lambda j, n, r: (j, 0, 0)),
                  pl.BlockSpec((n_par, 1, hw), lambda j, n, r: (j, 0, 0)),
                  pl.BlockSpec((1, n_par * LANES), lambda j, n, r: (0, j)), any_spec],
        out_specs=[pl.BlockSpec((bm_gl, bn_gl), lambda j, n, r: (j, n * rp5 + r)),
                   pl.BlockSpec((tm5, n_par * LANES), lambda j, n, r: (n * rp5 + r, j)),
                   pl.BlockSpec((None, 1, n_par * hw), lambda j, n, r: (n, 0, j)),
                   pl.BlockSpec((None, 1, n_par * hw), lambda j, n, r: (n, 0, j))],
        out_shape=[jax.ShapeDtypeStruct((m, 2 * dm), F32),
                   jax.ShapeDtypeStruct((m, dm), F32),
                   jax.ShapeDtypeStruct((nb, 1, n_grp * ssm_p), F32),
                   jax.ShapeDtypeStruct((nb, 1, n_grp * ssm_p), F32)],
        scratch_shapes=[pltpu.VMEM((n_par, 1, hw), F32), pltpu.VMEM((n_par, 1, hw), F32)],
        input_output_aliases={8: 1},
        compiler_params=_cparams("arbitrary", "arbitrary", "arbitrary"),
        name="gates_s5_prompt",
    )(u, w_in[0], z, wb, wc_par, abr3, abi3, dsk, f1)

    s5_w_specs1 = [pl.BlockSpec((None, LANES, 2 * hw), lambda j: (j, 0, 0)),
                   pl.BlockSpec((None, 2 * hw, LANES), lambda j: (j, 0, 0)),
                   pl.BlockSpec((None, 1, hw), lambda j: (j, 0, 0)),
                   pl.BlockSpec((None, 1, hw), lambda j: (j, 0, 0)),
                   pl.BlockSpec((1, LANES), lambda j: (0, j))]
    g_all, sre_s, sim_s = pl.pallas_call(
        functools.partial(_s5_sample_kernel, n_seq=db, n_steps=dseq),
        grid=(nj,),
        in_specs=[pl.BlockSpec((ms, LANES), lambda j: (s_blk, xb_blk + j))] + s5_w_specs1
                 + [pl.BlockSpec((db, hw), lambda j: (0, j)), pl.BlockSpec((db, hw), lambda j: (0, j)), any_spec],
        out_specs=[pl.BlockSpec((ms, LANES), lambda j: (s_blk, j)),
                   pl.BlockSpec((db, hw), lambda j: (0, j)),
                   pl.BlockSpec((db, hw), lambda j: (0, j))],
        out_shape=[jax.ShapeDtypeStruct((m, dm), F32),
                   jax.ShapeDtypeStruct((db, n_grp * ssm_p), F32),
                   jax.ShapeDtypeStruct((db, n_grp * ssm_p), F32)],
        scratch_shapes=[pltpu.VMEM((ms, 2 * hw), F32), pltpu.VMEM((ms, 2 * hw), BF16)],
        input_output_aliases={8: 0},
        compiler_params=_cparams("parallel"),
        name="s5_sample",
    )(z, wb, wc, abr3, abi3, dsk,
      state_ssm_re[0].reshape(db, n_grp * ssm_p), state_ssm_im[0].reshape(db, n_grp * ssm_p), g_all)

    bm_g = _pick(m, (1024, 512, 256, 128))
    bn_g = _pick(d_ssm, (512, 256, 128))
    yb = pl.pallas_call(
        _glu_kernel,
        grid=(m // bm_g, d_ssm // bn_g),
        in_specs=[pl.BlockSpec((bm_g, d_ssm), lambda i, j: (i, 0)),
                  pl.BlockSpec((bm_g, bn_g), lambda i, j: (i, j)),
                  pl.BlockSpec((d_ssm, bn_g), lambda i, j: (0, j)),
                  pl.BlockSpec((1, bn_g), lambda i, j: (0, j))],
        out_specs=pl.BlockSpec((bm_g, bn_g), lambda i, j: (i, j)),
        out_shape=jax.ShapeDtypeStruct((m, d_ssm), BF16),
        scratch_shapes=[pltpu.VMEM((bm_g, d_ssm), BF16)],
        compiler_params=_cparams("parallel", "arbitrary"),
        name="s5_glu",
    )(g_all, g_all, w_glu[0], row2(b_glu))

    bn_m = bn_in
    gla_blk = 0
    glb_blk = dm // bn_m
    merged = pl.pallas_call(
        _merge_kernel,
        grid=(m // bm_g, dm // bn_m),
        in_specs=[pl.BlockSpec((bm_g, d_lru), lambda i, j: (i, 0)),
                  pl.BlockSpec((bm_g, d_ssm), lambda i, j: (i, 0)),
                  pl.BlockSpec((d_lru, bn_m), lambda i, j: (0, j)),
                  pl.BlockSpec((d_ssm, bn_m), lambda i, j: (0, j)),
                  pl.BlockSpec((bm_g, bn_m), lambda i, j: (i, gla_blk + j)),
                  pl.BlockSpec((bm_g, bn_m), lambda i, j: (i, glb_blk + j))],
        out_specs=pl.BlockSpec((bm_g, bn_m), lambda i, j: (i, j)),
        out_shape=jax.ShapeDtypeStruct((m, dm), BF16),
        compiler_params=_cparams("parallel", "arbitrary"),
        name="gated_merge",
    )(ya, yb, w_out_a[0], w_out_b[0], gl, gl)

    o = _matmul(merged, w_o[0], bm=bm, bn=bn_in, out_dtype=F32, name="o_proj")

    x2, h2 = pl.pallas_call(
        _post_mix_kernel,
        grid=(m // tr,),
        in_specs=[row_spec, row_spec, g_spec, g_spec],
        out_specs=[row_spec, row_spec],
        out_shape=[jax.ShapeDtypeStruct((m, dm), F32), jax.ShapeDtypeStruct((m, dm), BF16)],
        compiler_params=_cparams("parallel"),
        name="post_mix",
    )(x1, o, row2(mix_post_g), row2(ffn2_pre_g))

    f2 = _half_ffn_matmuls(h2, ffn2_w_gate[0], ffn2_w_up[0], ffn2_w_down[0], bm=bm)

    def final(rows, first_tile):
        return pl.pallas_call(
            _final_kernel,
            grid=(rows // tr,),
            in_specs=[pl.BlockSpec((tr, dm), lambda i: (first_tile + i, 0)),
                      pl.BlockSpec((tr, dm), lambda i: (first_tile + i, 0)),
                      g_spec],
            out_specs=row_spec,
            out_shape=jax.ShapeDtypeStruct((rows, dm), F32),
            compiler_params=_cparams("parallel"),
            name="final_residual",
        )(x2, f2, row2(ffn2_post_g))

    y_prompt = final(mp, 0).reshape(nb, seq, dm)
    y_sample = jnp.swapaxes(final(ms, npt).reshape(dseq, db, dm), 0, 1)

    nk = CONV_W - 1
    prompt_conv = jnp.stack([lax.slice(z, ((n + 1) * seq - nk, 0), ((n + 1) * seq, d_lru)) for n in range(nb)])
    sample_conv = jnp.swapaxes(lax.slice(z, (m - nk * db, 0), (m, d_lru)).reshape(nk, db, d_lru), 0, 1)
    st = lambda v, n: v.reshape(1, n, n_grp, ssm_p).astype(sdt)
    return (y_prompt, y_sample,
            hl_p.reshape(1, nb, d_lru).astype(sdt), prompt_conv[None].astype(sdt),
            st(sre_p, nb), st(sim_p, nb),
            hl_s.reshape(1, db, d_lru).astype(sdt), sample_conv[None].astype(sdt),
            st(sre_s, db), st(sim_s, db))
```

```python
import functools
import math

import jax
import jax.numpy as jnp
from jax import lax
from jax.experimental import pallas as pl
from jax.experimental.pallas import tpu as pltpu

F32 = jnp.float32
BF16 = jnp.bfloat16

EPS = 1e-6
C_RG = 8.0
CONV_W = 4
LANES = 128
SUBLANES = 8
LRU_BLK = 128
VMEM_LIMIT = 56 * 1024 * 1024


def _cparams(*sem):
    return pltpu.CompilerParams(dimension_semantics=sem, vmem_limit_bytes=VMEM_LIMIT)


def _pick(n, candidates):
    for c in candidates:
        if n % c == 0:
            return c
    raise ValueError(f"no tile in {candidates} divides {n}")


def _rms(x, g):
    return x * lax.rsqrt(jnp.mean(x * x, axis=-1, keepdims=True) + EPS) * g


def _softplus(x):
    return jnp.maximum(x, 0.0) + jnp.log1p(jnp.exp(-jnp.abs(x)))


def _two_group_rows(i, n_prompt_tiles, xp_ref, xs_ref, body):
    @pl.when(i < n_prompt_tiles)
    def _():
        body(xp_ref[...])

    @pl.when(i >= n_prompt_tiles)
    def _():
        body(xs_ref[...])


def _prenorm_kernel(xp_ref, xs_ref, g_ref, h_ref, *, n_prompt_tiles):
    def body(x):
        h_ref[...] = _rms(x, g_ref[...]).astype(h_ref.dtype)

    _two_group_rows(pl.program_id(0), n_prompt_tiles, xp_ref, xs_ref, body)


def _post_ffn1_kernel(xp_ref, xs_ref, f_ref, gpost_ref, gpre_ref, x1_ref, u_ref, *, n_prompt_tiles):
    def body(x):
        x1 = x + 0.5 * _rms(f_ref[...], gpost_ref[...])
        x1_ref[...] = x1
        u_ref[...] = _rms(x1, gpre_ref[...]).astype(u_ref.dtype)

    _two_group_rows(pl.program_id(0), n_prompt_tiles, xp_ref, xs_ref, body)


def _post_mix_kernel(x1_ref, o_ref, gpost_ref, gpre_ref, x2_ref, h_ref):
    x2 = x1_ref[...] + _rms(o_ref[...], gpost_ref[...])
    x2_ref[...] = x2
    h_ref[...] = _rms(x2, gpre_ref[...]).astype(h_ref.dtype)


def _final_kernel(x2_ref, f_ref, g_ref, y_ref):
    y_ref[...] = x2_ref[...] + 0.5 * _rms(f_ref[...], g_ref[...])


def _ffn_up_kernel(h_ref, wg_ref, wu_ref, wd_ref, a_ref, wd16_ref):
    wg = wg_ref[...].astype(BF16)
    wu = wu_ref[...].astype(BF16)
    half = h_ref.shape[0] // 2
    for rows in (slice(0, half), slice(half, None)):
        h = h_ref[rows, :]
        g = jnp.dot(h, wg, preferred_element_type=F32)
        u = jnp.dot(h, wu, preferred_element_type=F32)
        a_ref[rows, :] = (jax.nn.silu(g) * u).astype(a_ref.dtype)
    wd16_ref[...] = wd_ref[...].astype(wd16_ref.dtype)


def _ffn_down_kernel(a_ref, wd_ref, o_ref):
    p = jnp.dot(a_ref[...], wd_ref[...], preferred_element_type=F32)

    @pl.when(pl.program_id(2) == 0)
    def _():
        o_ref[...] = p

    @pl.when(pl.program_id(2) != 0)
    def _():
        o_ref[...] += p


def _mm_kernel(x_ref, w_ref, o_ref):
    w = w_ref[...].astype(BF16)
    o_ref[...] = jnp.dot(x_ref[...], w, preferred_element_type=F32).astype(o_ref.dtype)


def _merge_kernel(ya_ref, yb_ref, wa_ref, wb_ref, gla_ref, glb_ref, o_ref):
    wa = wa_ref[...].astype(BF16)
    wb = wb_ref[...].astype(BF16)
    half = o_ref.shape[0] // 2
    for rows in (slice(0, half), slice(half, None)):
        pa = jnp.dot(ya_ref[rows, :], wa, preferred_element_type=F32)
        pb = jnp.dot(yb_ref[rows, :], wb, preferred_element_type=F32)
        m = jax.nn.sigmoid(gla_ref[rows, :]) * pa + jax.nn.sigmoid(glb_ref[rows, :]) * pb
        o_ref[rows, :] = m.astype(o_ref.dtype)


def _lru_gates(xc, wrg_ref, brg_ref, wig_ref, big_ref, lam_ref):
    xcb = xc.astype(BF16)
    rs, gs = [], []
    for hh in range(wrg_ref.shape[0]):
        xh = xcb[:, hh * LRU_BLK:(hh + 1) * LRU_BLK]
        rs.append(jnp.dot(xh, wrg_ref[hh], preferred_element_type=F32))
        gs.append(jnp.dot(xh, wig_ref[hh], preferred_element_type=F32))
    r = jax.nn.sigmoid(jnp.concatenate(rs, axis=1) + brg_ref[...])
    i = jax.nn.sigmoid(jnp.concatenate(gs, axis=1) + big_ref[...])
    log_a = -C_RG * r * _softplus(-lam_ref[...])
    a = jnp.exp(log_a)
    mult = jnp.sqrt(-jnp.tanh(log_a) * (a * a + 1.0))
    return a, mult * (i * xc)


def _glu_rglru_prompt_kernel(g_ref, gcol_ref, w_ref, b_ref,
                             xa_ref, ga_ref, cw_ref, cb_ref, wrg_ref, brg_ref, wig_ref, big_ref, lam_ref,
                             ya_all_ref, o_ref, ya_ref, hl_ref, g_scr, xcars, hcars, *, tiles_per_seq):
    del ya_all_ref
    i, j = pl.program_id(0), pl.program_id(1)

    @pl.when(j == 0)
    def _():
        g_scr[...] = g_ref[...].astype(g_scr.dtype)

    _rglru_prompt_tile(xa_ref, ga_ref, cw_ref, cb_ref, wrg_ref, brg_ref, wig_ref, big_ref, lam_ref,
                       ya_ref, hl_ref, xcars.at[j], hcars.at[j], i % tiles_per_seq == 0)
    s = jnp.dot(g_scr[...], w_ref[...].astype(BF16), preferred_element_type=F32) + b_ref[...]
    o_ref[...] = (gcol_ref[...] * jax.nn.sigmoid(s)).astype(o_ref.dtype)


def _rglru_prompt_tile(xa_ref, ga_ref, cw_ref, cb_ref, wrg_ref, brg_ref, wig_ref, big_ref, lam_ref,
                       ya_ref, hl_ref, xcar, hcar, starts_sequence):
    tm = xa_ref.shape[0]

    @pl.when(starts_sequence)
    def _():
        xcar[...] = jnp.zeros_like(xcar)
        hcar[...] = jnp.zeros_like(hcar)

    x = xa_ref[...]
    xfull = jnp.concatenate([xcar[...], x], axis=0)
    xcar[...] = x[tm - SUBLANES:, :]
    cw = cw_ref[...]
    xc = cb_ref[...]
    for k in range(CONV_W - 1):
        xs = pltpu.roll(xfull, CONV_W - 1 - k, 0)[SUBLANES:, :]
        xc = xc + xs * cw[k:k + 1, :]
    xc = xc + x * cw[CONV_W - 1:CONV_W, :]

    a, b = _lru_gates(xc, wrg_ref, brg_ref, wig_ref, big_ref, lam_ref)
    cb = a.shape[1]
    n_slab = tm // SUBLANES
    a = a.reshape(n_slab, SUBLANES, cb)
    b = b.reshape(n_slab, SUBLANES, cb)
    sub = lax.broadcasted_iota(jnp.int32, (1, SUBLANES, cb), 1)
    d = 1
    while d < SUBLANES:
        m = sub >= d
        a_s = jnp.where(m, pltpu.roll(a, d, 1), 1.0)
        b_s = jnp.where(m, pltpu.roll(b, d, 1), 0.0)
        b = a * b_s + b
        a = a * a_s
        d *= 2
    h = hcar[...]
    hs = []
    for s in range(n_slab):
        h_slab = b[s] + a[s] * jnp.broadcast_to(h, (SUBLANES, cb))
        h = h_slab[SUBLANES - 1:SUBLANES, :]
        hs.append(h_slab)
    hcar[...] = h
    hl_ref[...] = h
    ya_ref[...] = (jnp.concatenate(hs, axis=0) * jax.nn.gelu(ga_ref[...])).astype(ya_ref.dtype)


def _rglru_sample_kernel(xa_ref, ga_ref, cst_ref, h0_ref, cw_ref, cb_ref, wrg_ref, brg_ref, wig_ref,
                         big_ref, lam_ref, ya_all_ref, ya_ref, hl_ref, *, n_seq, n_steps):
    del ya_all_ref
    x = xa_ref[...]
    xp = jnp.concatenate([cst_ref[k] for k in range(CONV_W - 1)] + [x], axis=0)
    rows = n_seq * n_steps
    cw = cw_ref[...]
    xc = cb_ref[...]
    for k in range(CONV_W):
        xc = xc + xp[k * n_seq:k * n_seq + rows, :] * cw[k:k + 1, :]
    a, b = _lru_gates(xc, wrg_ref, brg_ref, wig_ref, big_ref, lam_ref)
    h = h0_ref[...]
    for t in range(n_steps):
        sl = slice(t * n_seq, (t + 1) * n_seq)
        h = a[sl, :] * h + b[sl, :]
        ya_ref[sl, :] = (h * jax.nn.gelu(ga_ref[sl, :])).astype(ya_ref.dtype)
    hl_ref[...] = h


def _s5_disc_kernel(are_ref, aim_ref, ldt_ref, abr_ref, abi_ref, cfr_ref, cfi_ref):
    a_re = are_ref[...]
    a_im = aim_ref[...]
    dt = jnp.exp(ldt_ref[...])
    mag = jnp.exp(a_re * dt)
    abr = mag * jnp.cos(a_im * dt)
    abi = mag * jnp.sin(a_im * dt)
    den = a_re * a_re + a_im * a_im
    nr = abr - 1.0
    abr_ref[...] = abr
    abi_ref[...] = abi
    cr = (nr * a_re + abi * a_im) / den
    ci = (abi * a_re - nr * a_im) / den
    for q in range(cfr_ref.shape[0]):
        cfr_ref[q] = cr
        cfi_ref[q] = ci
        cr, ci = abr * cr - abi * ci, abr * ci + abi * cr


def _gates_s5_prompt_kernel(u_ref, w_ref, xb_ref, wb_ref, wc_ref, ar_ref, ai_ref, d_ref, g_all_ref,
                            gl_ref, g_ref, sre_ref, sim_ref, cre, cim):
    del g_all_ref
    n_par, _, hw = ar_ref.shape

    @pl.when(pl.program_id(2) == 0)
    def _():
        cre[...] = jnp.zeros_like(cre)
        cim[...] = jnp.zeros_like(cim)

    split = (gl_ref.shape[0] * 9 // 16) // (2 * SUBLANES) * (2 * SUBLANES)
    w = w_ref[...].astype(BF16)
    bus = [_s5_project_in(xb_ref[:, p * LANES:(p + 1) * LANES], wb_ref.at[p]) for p in range(n_par)]
    gl_ref[:split, :] = jnp.dot(u_ref[:split, :], w, preferred_element_type=F32)
    scans = [_s5_slab_scan(bus[p], ar_ref[p], ai_ref[p], cre[p], cim[p]) for p in range(n_par)]
    y = jnp.dot(jnp.concatenate([s[0] for s in scans], axis=1), wc_ref[...], preferred_element_type=F32)
    gl_ref[split:, :] = jnp.dot(u_ref[split:, :], w, preferred_element_type=F32)
    g_ref[...] = jax.nn.gelu(y + d_ref[...] * xb_ref[...])
    for p in range(n_par):
        _, cr, ci = scans[p]
        cre[p] = cr
        cim[p] = ci
        sre_ref[:, p * hw:(p + 1) * hw] = cr
        sim_ref[:, p * hw:(p + 1) * hw] = ci


def _s5_project_in(xb, wb_ref):
    tm, width = xb.shape
    n_slab = tm // SUBLANES
    n_lags = wb_ref.shape[0] // width
    xb3 = xb.reshape(n_slab, SUBLANES, width)
    subx = lax.broadcasted_iota(jnp.int32, (1, SUBLANES, width), 1)
    lagged = [xb3] + [jnp.where(subx >= q, pltpu.roll(xb3, q, 1), 0.0) for q in range(1, n_lags)]
    xs = jnp.concatenate(lagged, axis=2).reshape(tm, n_lags * width)
    return jnp.dot(xs.astype(BF16), wb_ref[...], preferred_element_type=F32), n_lags


def _s5_slab_scan(bu_lags, ar, ai, cr, ci):
    bu, n_lags = bu_lags
    tm = bu.shape[0]
    hw = ar.shape[1]
    n_slab = tm // SUBLANES
    sub = lax.broadcasted_iota(jnp.int32, (SUBLANES, hw), 0)
    pr, pi = ar, ai
    tr = jnp.where(sub == 0, pr, 0.0)
    ti = jnp.where(sub == 0, pi, 0.0)
    steps = []
    d = 1
    while d < SUBLANES:
        mr = jnp.where(sub >= d, pr, 0.0)
        mi = jnp.where(sub >= d, pi, 0.0)
        if d >= n_lags:
            steps.append((d, mr, mi))
        sr = pltpu.roll(tr, d, 0)
        si = pltpu.roll(ti, d, 0)
        tr, ti = tr + (mr * sr - mi * si), ti + (mr * si + mi * sr)
        pr, pi = pr * pr - pi * pi, 2.0 * (pr * pi)
        d *= 2
    slabs = []
    for s in range(n_slab):
        rows = slice(s * SUBLANES, (s + 1) * SUBLANES)
        hr = bu[rows, :hw]
        hi = bu[rows, hw:]
        for d, mr, mi in steps:
            sr = pltpu.roll(hr, d, 0)
            si = pltpu.roll(hi, d, 0)
            hr, hi = hr + (mr * sr - mi * si), hi + (mr * si + mi * sr)
        cbr = jnp.broadcast_to(cr, (SUBLANES, hw))
        cbi = jnp.broadcast_to(ci, (SUBLANES, hw))
        hr, hi = hr + (tr * cbr - ti * cbi), hi + (tr * cbi + ti * cbr)
        cr = hr[SUBLANES - 1:SUBLANES, :]
        ci = hi[SUBLANES - 1:SUBLANES, :]
        slabs.append(jnp.concatenate([hr, hi], axis=1))
    return jnp.concatenate(slabs, axis=0).astype(BF16), cr, ci


def _s5_sample_kernel(xb_ref, wb_ref, wc_ref, ar_ref, ai_ref, d_ref, s0r_ref, s0i_ref, g_all_ref,
                      g_ref, sre_ref, sim_ref, bu_scr, h_scr, *, n_seq, n_steps):
    del g_all_ref
    hw = ar_ref.shape[1]
    xb = xb_ref[...]
    bu_scr[...] = jnp.dot(xb.astype(BF16), wb_ref[...], preferred_element_type=F32)
    ar = ar_ref[...]
    ai = ai_ref[...]
    hr = s0r_ref[...]
    hi = s0i_ref[...]
    for t in range(n_steps):
        sl = slice(t * n_seq, (t + 1) * n_seq)
        hr, hi = (ar * hr - ai * hi) + bu_scr[sl, :hw], (ar * hi + ai * hr) + bu_scr[sl, hw:]
        h_scr[sl, :hw] = hr.astype(h_scr.dtype)
        h_scr[sl, hw:] = hi.astype(h_scr.dtype)
    sre_ref[...] = hr
    sim_ref[...] = hi
    y = jnp.dot(h_scr[...], wc_ref[...], preferred_element_type=F32)
    g_ref[...] = jax.nn.gelu(y + d_ref[...] * xb)


def _half_ffn_matmuls(h, wg, wu, wd, *, bm):
    m, dm = h.shape
    dff = wg.shape[1]
    bn_up = _pick(dff, (256, 128))
    n_i, n_j = m // bm, dff // bn_up
    rb = LANES
    n_rb = dff // rb
    assert n_i * n_j >= n_rb, "not enough up-projection steps to cover the down-projection weight"
    wd_blk = lambda i, j: (jnp.minimum(i * n_j + j, n_rb - 1), 0)
    a, wd = pl.pallas_call(
        _ffn_up_kernel,
        grid=(n_i, n_j),
        in_specs=[pl.BlockSpec((bm, dm), lambda i, j: (i, 0)),
                  pl.BlockSpec((dm, bn_up), lambda i, j: (0, j)),
                  pl.BlockSpec((dm, bn_up), lambda i, j: (0, j)),
                  pl.BlockSpec((rb, dm), wd_blk)],
        out_specs=[pl.BlockSpec((bm, bn_up), lambda i, j: (i, j)),
                   pl.BlockSpec((rb, dm), wd_blk)],
        out_shape=[jax.ShapeDtypeStruct((m, dff), BF16), jax.ShapeDtypeStruct((dff, dm), BF16)],
        compiler_params=_cparams("arbitrary", "arbitrary"),
        name="ffn_up",
    )(h, wg, wu, wd)

    bm_d = _pick(m, (1024, 512, 256, 128))
    bn_d = _pick(dm, (1024, 512, 256, 128))
    bk = dff // 2 if (dff // 2) % LANES == 0 else dff
    return pl.pallas_call(
        _ffn_down_kernel,
        grid=(m // bm_d, dm // bn_d, dff // bk),
        in_specs=[pl.BlockSpec((bm_d, bk), lambda i, j, k: (i, k)),
                  pl.BlockSpec((bk, bn_d), lambda i, j, k: (k, j))],
        out_specs=pl.BlockSpec((bm_d, bn_d), lambda i, j, k: (i, j)),
        out_shape=jax.ShapeDtypeStruct((m, dm), F32),
        compiler_params=_cparams("parallel", "parallel", "arbitrary"),
        name="ffn_down",
    )(a, wd)


def _matmul(x, w, *, bm, bn, out_dtype, name, n=None):
    m, k = x.shape
    n = w.shape[1] if n is None else n
    return pl.pallas_call(
        _mm_kernel,
        grid=(m // bm, n // bn),
        in_specs=[pl.BlockSpec((bm, k), lambda i, j: (i, 0)),
                  pl.BlockSpec((k, bn), lambda i, j: (0, j))],
        out_specs=pl.BlockSpec((bm, bn), lambda i, j: (i, j)),
        out_shape=jax.ShapeDtypeStruct((m, n), out_dtype),
        compiler_params=_cparams("parallel", "arbitrary"),
        name=name,
    )(x, w)


def kernel(x_prompt, x_sample, state_lru_h, state_conv, state_ssm_re, state_ssm_im, ffn1_pre_g, ffn1_post_g, ffn1_w_gate, ffn1_w_up, ffn1_w_down, mix_pre_g, mix_post_g, w_in, conv_w, conv_b, w_rg, b_rg, w_ig, b_ig, lru_lambda, ssm_a_re, ssm_a_im, ssm_log_dt, ssm_b_re, ssm_b_im, ssm_c_re, ssm_c_im, ssm_d, w_glu, b_glu, w_out_a, w_out_b, w_o, ffn2_pre_g, ffn2_post_g, ffn2_w_gate, ffn2_w_up, ffn2_w_down):
    nb, seq, dm = x_prompt.shape
    db, dseq, _ = x_sample.shape
    depth = state_lru_h.shape[0]
    assert depth == 1, "one decoder layer"
    d_lru = state_lru_h.shape[2]
    n_grp, ssm_p = state_ssm_re.shape[2], state_ssm_re.shape[3]
    d_ssm = ssm_d.shape[1]
    ssm_cg = d_ssm // n_grp
    d_in = w_in.shape[2]
    assert w_rg.shape[2] == LRU_BLK and conv_w.shape[1] == CONV_W
    assert d_in == 2 * d_lru + d_ssm + 2 * dm and seq >= CONV_W - 1 and dseq >= CONV_W - 1

    mp = nb * seq
    ms = db * dseq
    m = mp + ms
    sdt = state_lru_h.dtype

    row2 = lambda v: v.reshape(1, -1)
    bf = lambda v: v[0].astype(BF16)

    xp2 = x_prompt.reshape(mp, dm)
    xs2 = jnp.swapaxes(x_sample, 0, 1).reshape(ms, dm)

    tr = _pick(math.gcd(mp, ms), (256, 128, 64, 32, 16, 8))
    npt = mp // tr
    xp_spec = pl.BlockSpec((tr, dm), lambda i: (jnp.minimum(i, npt - 1), 0))
    xs_spec = pl.BlockSpec((tr, dm), lambda i: (jnp.maximum(i - npt, 0), 0))
    row_spec = pl.BlockSpec((tr, dm), lambda i: (i, 0))
    g_spec = pl.BlockSpec((1, dm), lambda i: (0, 0))

    h1 = pl.pallas_call(
        functools.partial(_prenorm_kernel, n_prompt_tiles=npt),
        grid=(m // tr,),
        in_specs=[xp_spec, xs_spec, g_spec],
        out_specs=row_spec,
        out_shape=jax.ShapeDtypeStruct((m, dm), BF16),
        compiler_params=_cparams("parallel"),
        name="prenorm1",
    )(xp2, xs2, row2(ffn1_pre_g))

    bm = _pick(m, (1536, 1024, 768, 512, 384, 256, 128))
    f1 = _half_ffn_matmuls(h1, ffn1_w_gate[0], ffn1_w_up[0], ffn1_w_down[0], bm=bm)

    x1, u = pl.pallas_call(
        functools.partial(_post_ffn1_kernel, n_prompt_tiles=npt),
        grid=(m // tr,),
        in_specs=[xp_spec, xs_spec, row_spec, g_spec, g_spec],
        out_specs=[row_spec, row_spec],
        out_shape=[jax.ShapeDtypeStruct((m, dm), F32), jax.ShapeDtypeStruct((m, dm), BF16)],
        compiler_params=_cparams("parallel"),
        name="post_ffn1",
    )(xp2, xs2, f1, row2(ffn1_post_g), row2(mix_pre_g))

    bn_in = _pick(math.gcd(d_lru, dm), (512, 256, 128))
    n_a = 2 * d_lru + d_ssm
    z = _matmul(u, w_in[0], bm=bm, bn=bn_in, out_dtype=F32, name="in_proj", n=n_a)

    cb = _pick(d_lru, (512, 256, 128))
    ncb = d_lru // cb
    hpb = cb // LRU_BLK
    wrg = bf(w_rg)
    wig = bf(w_ig)
    lru_params = (conv_w[0], row2(conv_b), wrg, row2(b_rg), wig, row2(b_ig), row2(lru_lambda))
    assert d_lru <= dm and d_ssm <= dm
    any_spec = pl.BlockSpec(memory_space=pl.ANY)
    assert mp % ms == 0, "sample rows must tile the unified row axis"
    s_blk = mp // ms

    n_lags = 2
    abr, abi, cfr, cfi = pl.pallas_call(
        _s5_disc_kernel,
        out_shape=[jax.ShapeDtypeStruct((n_grp, ssm_p), F32)] * 2
                  + [jax.ShapeDtypeStruct((n_lags, n_grp, ssm_p), F32)] * 2,
        name="s5_discretise",
    )(ssm_a_re[0], ssm_a_im[0], ssm_log_dt[0].reshape(n_grp, 1))
    bb_re = cfr[..., None] * ssm_b_re - cfi[..., None] * ssm_b_im
    bb_im = cfr[..., None] * ssm_b_im + cfi[..., None] * ssm_b_re

    gpb = LANES // ssm_cg
    nj = n_grp // gpb
    hw = gpb * ssm_p
    eye = jnp.eye(gpb, dtype=BF16)
    bb = jnp.stack([bb_re, bb_im]).reshape(2, n_lags, nj, gpb, ssm_p, ssm_cg).astype(BF16)
    bb = jnp.transpose(bb, (2, 1, 3, 5, 0, 4))
    wb = (bb[:, :, :, :, :, None, :] * eye[None, None, :, None, None, :, None])
    wb = wb.reshape(nj, n_lags * gpb * ssm_cg, 2 * hw)
    cc = jnp.stack([ssm_c_re[0], -ssm_c_im[0]]).reshape(2, nj, gpb, ssm_cg, ssm_p).astype(BF16)
    cc = jnp.transpose(cc, (1, 0, 4, 2, 3))
    wc = (cc[:, :, None, :, :, :] * eye[None, None, :, None, :, None]).reshape(nj, 2 * hw, gpb * ssm_cg)
    abr3 = abr.reshape(nj, 1, hw)
    abi3 = abi.reshape(nj, 1, hw)
    xb_blk = (2 * d_lru) // LANES
    dsk = row2(ssm_d)

    n_par = 2 if nj % 2 == 0 and xb_blk % 2 == 0 else 1
    eye_p = jnp.eye(n_par, dtype=BF16)
    wc_par = wc.reshape(nj // n_par, n_par, 2 * hw, 1, LANES) * eye_p[None, :, None, :, None]
    wc_par = wc_par.reshape(nj // n_par, n_par * 2 * hw, n_par * LANES)
    n_gi = nj // n_par
    bm_gl = m // n_gi
    assert m % n_gi == 0 and bm_gl % (2 * SUBLANES) == 0

    def gl_cols_ok(t):
        bn = (2 * dm) // (nb * (seq // t)) if (2 * dm) % (nb * (seq // t)) == 0 else 0
        return bn > 0 and bn % LANES == 0 and n_a % bn == 0

    tm5 = _pick(seq, tuple(t for t in (512, 256, 128, 64, 32, 16, 8) if seq % t == 0 and gl_cols_ok(t)))
    rp5 = seq // tm5
    bn_gl = (2 * dm) // (nb * rp5)
    gl_blk = n_a // bn_gl
    gl, g_all, sre_p, sim_p = pl.pallas_call(
        _gates_s5_prompt_kernel,
        grid=(n_gi, nb, rp5),
        in_specs=[pl.BlockSpec((bm_gl, dm), lambda j, n, r: (j, 0)),
                  pl.BlockSpec((dm, bn_gl), lambda j, n, r: (0, gl_blk + n * rp5 + r)),
                  pl.BlockSpec((tm5, n_par * LANES), lambda j, n, r: (n * rp5 + r, xb_blk // n_par + j)),
                  pl.BlockSpec((n_par, n_lags * LANES, 2 * hw), lambda j, n, r: (j, 0, 0)),
                  pl.BlockSpec((None, n_par * 2 * hw, n_par * LANES), lambda j, n, r: (j, 0, 0)),
                  pl.BlockSpec((n_par, 1, hw), lambda j, n, r: (j, 0, 0)),
                  pl.BlockSpec((n_par, 1, hw), lambda j, n, r: (j, 0, 0)),
                  pl.BlockSpec((1, n_par * LANES), lambda j, n, r: (0, j)), any_spec],
        out_specs=[pl.BlockSpec((bm_gl, bn_gl), lambda j, n, r: (j, n * rp5 + r)),
                   pl.BlockSpec((tm5, n_par * LANES), lambda j, n, r: (n * rp5 + r, j)),
                   pl.BlockSpec((None, 1, n_par * hw), lambda j, n, r: (n, 0, j)),
                   pl.BlockSpec((None, 1, n_par * hw), lambda j, n, r: (n, 0, j))],
        out_shape=[jax.ShapeDtypeStruct((m, 2 * dm), F32),
                   jax.ShapeDtypeStruct((m, dm), F32),
                   jax.ShapeDtypeStruct((nb, 1, n_grp * ssm_p), F32),
                   jax.ShapeDtypeStruct((nb, 1, n_grp * ssm_p), F32)],
        scratch_shapes=[pltpu.VMEM((n_par, 1, hw), F32), pltpu.VMEM((n_par, 1, hw), F32)],
        input_output_aliases={8: 1},
        compiler_params=_cparams("arbitrary", "arbitrary", "arbitrary"),
        name="gates_s5_prompt",
    )(u, w_in[0], z, wb, wc_par, abr3, abi3, dsk, f1)

    s5_w_specs1 = [pl.BlockSpec((None, LANES, 2 * hw), lambda j: (j, 0, 0)),
                   pl.BlockSpec((None, 2 * hw, LANES), lambda j: (j, 0, 0)),
                   pl.BlockSpec((None, 1, hw), lambda j: (j, 0, 0)),
                   pl.BlockSpec((None, 1, hw), lambda j: (j, 0, 0)),
                   pl.BlockSpec((1, LANES), lambda j: (0, j))]
    g_all, sre_s, sim_s = pl.pallas_call(
        functools.partial(_s5_sample_kernel, n_seq=db, n_steps=dseq),
        grid=(nj,),
        in_specs=[pl.BlockSpec((ms, LANES), lambda j: (s_blk, xb_blk + j))] + s5_w_specs1
                 + [pl.BlockSpec((db, hw), lambda j: (0, j)), pl.BlockSpec((db, hw), lambda j: (0, j)), any_spec],
        out_specs=[pl.BlockSpec((ms, LANES), lambda j: (s_blk, j)),
                   pl.BlockSpec((db, hw), lambda j: (0, j)),
                   pl.BlockSpec((db, hw), lambda j: (0, j))],
        out_shape=[jax.ShapeDtypeStruct((m, dm), F32),
                   jax.ShapeDtypeStruct((db, n_grp * ssm_p), F32),
                   jax.ShapeDtypeStruct((db, n_grp * ssm_p), F32)],
        scratch_shapes=[pltpu.VMEM((ms, 2 * hw), F32), pltpu.VMEM((ms, 2 * hw), BF16)],
        input_output_aliases={8: 0},
        compiler_params=_cparams("parallel"),
        name="s5_sample",
    )(z, wb, wc, abr3, abi3, dsk,
      state_ssm_re[0].reshape(db, n_grp * ssm_p), state_ssm_im[0].reshape(db, n_grp * ssm_p), g_all)

    assert d_ssm // cb == ncb
    tm = _pick(seq, tuple(t for t in (512, 256, 128, 64, 32, 16, 8)
                          if seq % t == 0 and (m * t) % mp == 0 and ((m * t) // mp) % (2 * SUBLANES) == 0))
    rpt = seq // tm
    bm_u = (m * tm) // mp
    col = lambda shape: pl.BlockSpec(shape, lambda i, j: (0, j))
    head_spec = pl.BlockSpec((hpb, LRU_BLK, LRU_BLK), lambda i, j: (j, 0, 0))
    yb, ya, hl_p = pl.pallas_call(
        functools.partial(_glu_rglru_prompt_kernel, tiles_per_seq=rpt),
        grid=(nb * rpt, ncb),
        in_specs=[pl.BlockSpec((bm_u, d_ssm), lambda i, j: (i, 0)),
                  pl.BlockSpec((bm_u, cb), lambda i, j: (i, j)),
                  pl.BlockSpec((d_ssm, cb), lambda i, j: (0, j)),
                  col((1, cb)),
                  pl.BlockSpec((tm, cb), lambda i, j: (i, j)),
                  pl.BlockSpec((tm, cb), lambda i, j: (i, ncb + j)),
                  col((CONV_W, cb)), col((1, cb)), head_spec, col((1, cb)), head_spec, col((1, cb)),
                  col((1, cb)), any_spec],
        out_specs=[pl.BlockSpec((bm_u, cb), lambda i, j: (i, j)),
                   pl.BlockSpec((tm, cb), lambda i, j: (i, j)),
                   pl.BlockSpec((None, 1, cb), lambda i, j: (i // rpt, 0, j))],
        out_shape=[jax.ShapeDtypeStruct((m, d_ssm), BF16),
                   jax.ShapeDtypeStruct((m, dm), BF16),
                   jax.ShapeDtypeStruct((nb, 1, d_lru), F32)],
        scratch_shapes=[pltpu.VMEM((bm_u, d_ssm), BF16),
                        pltpu.VMEM((ncb, SUBLANES, cb), F32), pltpu.VMEM((ncb, 1, cb), F32)],
        input_output_aliases={4 + 2 + len(lru_params): 1},
        compiler_params=_cparams("arbitrary", "arbitrary"),
        name="glu_rglru_prompt",
    )(g_all, g_all, w_glu[0], row2(b_glu), z, z, *lru_params, h1)

    par1 = lambda shape: pl.BlockSpec(shape, lambda c: (0, c))
    cst = jnp.swapaxes(state_conv[0], 0, 1)
    ya, hl_s = pl.pallas_call(
        functools.partial(_rglru_sample_kernel, n_seq=db, n_steps=dseq),
        grid=(ncb,),
        in_specs=[pl.BlockSpec((ms, cb), lambda c: (s_blk, c)),
                  pl.BlockSpec((ms, cb), lambda c: (s_blk, ncb + c)),
                  pl.BlockSpec((CONV_W - 1, db, cb), lambda c: (0, 0, c)),
                  par1((db, cb)),
                  par1((CONV_W, cb)), par1((1, cb)),
                  pl.BlockSpec((hpb, LRU_BLK, LRU_BLK), lambda c: (c, 0, 0)), par1((1, cb)),
                  pl.BlockSpec((hpb, LRU_BLK, LRU_BLK), lambda c: (c, 0, 0)), par1((1, cb)),
                  par1((1, cb)), any_spec],
        out_specs=[pl.BlockSpec((ms, cb), lambda c: (s_blk, c)), par1((db, cb))],
        out_shape=[jax.ShapeDtypeStruct((m, dm), BF16), jax.ShapeDtypeStruct((db, d_lru), F32)],
        input_output_aliases={4 + len(lru_params): 0},
        compiler_params=_cparams("parallel"),
        name="rglru_sample",
    )(z, z, cst, state_lru_h[0], *lru_params, ya)

    bm_g = _pick(m, (1024, 512, 256, 128))

    bn_m = bn_in
    gla_blk = 0
    glb_blk = dm // bn_m
    merged = pl.pallas_call(
        _merge_kernel,
        grid=(m // bm_g, dm // bn_m),
        in_specs=[pl.BlockSpec((bm_g, d_lru), lambda i, j: (i, 0)),
                  pl.BlockSpec((bm_g, d_ssm), lambda i, j: (i, 0)),
                  pl.BlockSpec((d_lru, bn_m), lambda i, j: (0, j)),
                  pl.BlockSpec((d_ssm, bn_m), lambda i, j: (0, j)),
                  pl.BlockSpec((bm_g, bn_m), lambda i, j: (i, gla_blk + j)),
                  pl.BlockSpec((bm_g, bn_m), lambda i, j: (i, glb_blk + j))],
        out_specs=pl.BlockSpec((bm_g, bn_m), lambda i, j: (i, j)),
        out_shape=jax.ShapeDtypeStruct((m, dm), BF16),
        compiler_params=_cparams("parallel", "arbitrary"),
        name="gated_merge",
    )(ya, yb, w_out_a[0], w_out_b[0], gl, gl)

    o = _matmul(merged, w_o[0], bm=bm, bn=bn_in, out_dtype=F32, name="o_proj")

    x2, h2 = pl.pallas_call(
        _post_mix_kernel,
        grid=(m // tr,),
        in_specs=[row_spec, row_spec, g_spec, g_spec],
        out_specs=[row_spec, row_spec],
        out_shape=[jax.ShapeDtypeStruct((m, dm), F32), jax.ShapeDtypeStruct((m, dm), BF16)],
        compiler_params=_cparams("parallel"),
        name="post_mix",
    )(x1, o, row2(mix_post_g), row2(ffn2_pre_g))

    f2 = _half_ffn_matmuls(h2, ffn2_w_gate[0], ffn2_w_up[0], ffn2_w_down[0], bm=bm)

    def final(rows, first_tile):
        return pl.pallas_call(
            _final_kernel,
            grid=(rows // tr,),
            in_specs=[pl.BlockSpec((tr, dm), lambda i: (first_tile + i, 0)),
                      pl.BlockSpec((tr, dm), lambda i: (first_tile + i, 0)),
                      g_spec],
            out_specs=row_spec,
            out_shape=jax.ShapeDtypeStruct((rows, dm), F32),
            compiler_params=_cparams("parallel"),
            name="final_residual",
        )(x2, f2, row2(ffn2_post_g))

    y_prompt = final(mp, 0).reshape(nb, seq, dm)
    y_sample = jnp.swapaxes(final(ms, npt).reshape(dseq, db, dm), 0, 1)

    nk = CONV_W - 1
    prompt_conv = jnp.stack([lax.slice(z, ((n + 1) * seq - nk, 0), ((n + 1) * seq, d_lru)) for n in range(nb)])
    sample_conv = jnp.swapaxes(lax.slice(z, (m - nk * db, 0), (m, d_lru)).reshape(nk, db, d_lru), 0, 1)
    st = lambda v, n: v.reshape(1, n, n_grp, ssm_p).astype(sdt)
    return (y_prompt, y_sample,
            hl_p.reshape(1, nb, d_lru).astype(sdt), prompt_conv[None].astype(sdt),
            st(sre_p, nb), st(sim_p, nb),
            hl_s.reshape(1, db, d_lru).astype(sdt), sample_conv[None].astype(sdt),
            st(sre_s, db), st(sim_s, db))
```

```python
import functools
import math

import jax
import jax.numpy as jnp
from jax import lax
from jax.experimental import pallas as pl
from jax.experimental.pallas import tpu as pltpu

F32 = jnp.float32
BF16 = jnp.bfloat16

EPS = 1e-6
C_RG = 8.0
CONV_W = 4
LANES = 128
SUBLANES = 8
LRU_BLK = 128
VMEM_LIMIT = 56 * 1024 * 1024


def _cparams(*sem):
    return pltpu.CompilerParams(dimension_semantics=sem, vmem_limit_bytes=VMEM_LIMIT)


def _pick(n, candidates):
    for c in candidates:
        if n % c == 0:
            return c
    raise ValueError(f"no tile in {candidates} divides {n}")


def _rms(x, g):
    return x * lax.rsqrt(jnp.mean(x * x, axis=-1, keepdims=True) + EPS) * g


def _softplus(x):
    return jnp.maximum(x, 0.0) + jnp.log1p(jnp.exp(-jnp.abs(x)))


def _two_group_rows(i, n_prompt_tiles, xp_ref, xs_ref, body):
    @pl.when(i < n_prompt_tiles)
    def _():
        body(xp_ref[...])

    @pl.when(i >= n_prompt_tiles)
    def _():
        body(xs_ref[...])


def _prenorm_kernel(xp_ref, xs_ref, g_ref, h_ref, *, n_prompt_tiles):
    def body(x):
        h_ref[...] = _rms(x, g_ref[...]).astype(h_ref.dtype)

    _two_group_rows(pl.program_id(0), n_prompt_tiles, xp_ref, xs_ref, body)


def _post_ffn1_kernel(xp_ref, xs_ref, f_ref, gpost_ref, gpre_ref, x1_ref, u_ref, *, n_prompt_tiles):
    def body(x):
        x1 = x + 0.5 * _rms(f_ref[...], gpost_ref[...])
        x1_ref[...] = x1
        u_ref[...] = _rms(x1, gpre_ref[...]).astype(u_ref.dtype)

    _two_group_rows(pl.program_id(0), n_prompt_tiles, xp_ref, xs_ref, body)


def _post_mix_kernel(x1_ref, o_ref, gpost_ref, gpre_ref, x2_ref, h_ref):
    x2 = x1_ref[...] + _rms(o_ref[...], gpost_ref[...])
    x2_ref[...] = x2
    h_ref[...] = _rms(x2, gpre_ref[...]).astype(h_ref.dtype)


def _final_kernel(x2_ref, f_ref, g_ref, y_ref):
    y_ref[...] = x2_ref[...] + 0.5 * _rms(f_ref[...], g_ref[...])


def _ffn_up_kernel(h_ref, wg_ref, wu_ref, wd_ref, a_ref, wd16_ref):
    wg = wg_ref[...].astype(BF16)
    wu = wu_ref[...].astype(BF16)
    half = h_ref.shape[0] // 2
    for rows in (slice(0, half), slice(half, None)):
        h = h_ref[rows, :]
        g = jnp.dot(h, wg, preferred_element_type=F32)
        u = jnp.dot(h, wu, preferred_element_type=F32)
        a_ref[rows, :] = (jax.nn.silu(g) * u).astype(a_ref.dtype)
    wd16_ref[...] = wd_ref[...].astype(wd16_ref.dtype)


def _ffn_down_kernel(a_ref, wd_ref, o_ref):
    p = jnp.dot(a_ref[...], wd_ref[...], preferred_element_type=F32)

    @pl.when(pl.program_id(2) == 0)
    def _():
        o_ref[...] = p

    @pl.when(pl.program_id(2) != 0)
    def _():
        o_ref[...] += p


def _mm_kernel(x_ref, w_ref, o_ref):
    w = w_ref[...].astype(BF16)
    o_ref[...] = jnp.dot(x_ref[...], w, preferred_element_type=F32).astype(o_ref.dtype)


def _merge_kernel(ya_ref, yb_ref, wa_ref, wb_ref, gla_ref, glb_ref, o_ref):
    wa = wa_ref[...].astype(BF16)
    wb = wb_ref[...].astype(BF16)
    half = o_ref.shape[0] // 2
    for rows in (slice(0, half), slice(half, None)):
        pa = jnp.dot(ya_ref[rows, :], wa, preferred_element_type=F32)
        pb = jnp.dot(yb_ref[rows, :], wb, preferred_element_type=F32)
        m = jax.nn.sigmoid(gla_ref[rows, :]) * pa + jax.nn.sigmoid(glb_ref[rows, :]) * pb
        o_ref[rows, :] = m.astype(o_ref.dtype)


def _lru_gates(xc, wrg_ref, brg_ref, wig_ref, big_ref, lam_ref):
    xcb = xc.astype(BF16)
    rs, gs = [], []
    for hh in range(wrg_ref.shape[0]):
        xh = xcb[:, hh * LRU_BLK:(hh + 1) * LRU_BLK]
        rs.append(jnp.dot(xh, wrg_ref[hh], preferred_element_type=F32))
        gs.append(jnp.dot(xh, wig_ref[hh], preferred_element_type=F32))
    r = jax.nn.sigmoid(jnp.concatenate(rs, axis=1) + brg_ref[...])
    i = jax.nn.sigmoid(jnp.concatenate(gs, axis=1) + big_ref[...])
    log_a = -C_RG * r * _softplus(-lam_ref[...])
    a = jnp.exp(log_a)
    mult = jnp.sqrt(-jnp.tanh(log_a) * (a * a + 1.0))
    return a, mult * (i * xc)


def _glu_rglru_prompt_kernel(g_ref, gcol_ref, w_ref, b_ref,
                             xa_ref, ga_ref, cw_ref, cb_ref, wrg_ref, brg_ref, wig_ref, big_ref, lam_ref,
                             ya_all_ref, o_ref, ya_ref, hl_ref, g_scr, xcars, hcars, *, tiles_per_seq):
    del ya_all_ref
    i, j = pl.program_id(0), pl.program_id(1)

    @pl.when(j == 0)
    def _():
        g_scr[...] = g_ref[...].astype(g_scr.dtype)

    _rglru_prompt_tile(xa_ref, ga_ref, cw_ref, cb_ref, wrg_ref, brg_ref, wig_ref, big_ref, lam_ref,
                       ya_ref, hl_ref, xcars.at[j], hcars.at[j], i % tiles_per_seq == 0)
    s = jnp.dot(g_scr[...], w_ref[...].astype(BF16), preferred_element_type=F32) + b_ref[...]
    o_ref[...] = (gcol_ref[...] * jax.nn.sigmoid(s)).astype(o_ref.dtype)


def _rglru_prompt_tile(xa_ref, ga_ref, cw_ref, cb_ref, wrg_ref, brg_ref, wig_ref, big_ref, lam_ref,
                       ya_ref, hl_ref, xcar, hcar, starts_sequence):
    tm = xa_ref.shape[0]

    @pl.when(starts_sequence)
    def _():
        xcar[...] = jnp.zeros_like(xcar)
        hcar[...] = jnp.zeros_like(hcar)

    x = xa_ref[...]
    xfull = jnp.concatenate([xcar[...], x], axis=0)
    xcar[...] = x[tm - SUBLANES:, :]
    cw = cw_ref[...]
    xc = cb_ref[...]
    for k in range(CONV_W - 1):
        xs = pltpu.roll(xfull, CONV_W - 1 - k, 0)[SUBLANES:, :]
        xc = xc + xs * cw[k:k + 1, :]
    xc = xc + x * cw[CONV_W - 1:CONV_W, :]

    a, b = _lru_gates(xc, wrg_ref, brg_ref, wig_ref, big_ref, lam_ref)
    cb = a.shape[1]
    n_slab = tm // SUBLANES
    a = a.reshape(n_slab, SUBLANES, cb)
    b = b.reshape(n_slab, SUBLANES, cb)
    sub = lax.broadcasted_iota(jnp.int32, (1, SUBLANES, cb), 1)
    d = 1
    while d < SUBLANES:
        m = sub >= d
        a_s = jnp.where(m, pltpu.roll(a, d, 1), 1.0)
        b_s = jnp.where(m, pltpu.roll(b, d, 1), 0.0)
        b = a * b_s + b
        a = a * a_s
        d *= 2
    h = hcar[...]
    hs = []
    for s in range(n_slab):
        h_slab = b[s] + a[s] * jnp.broadcast_to(h, (SUBLANES, cb))
        h = h_slab[SUBLANES - 1:SUBLANES, :]
        hs.append(h_slab)
    hcar[...] = h
    hl_ref[...] = h
    ya_ref[...] = (jnp.concatenate(hs, axis=0) * jax.nn.gelu(ga_ref[...])).astype(ya_ref.dtype)


def _rglru_sample_kernel(xa_ref, ga_ref, cst_ref, h0_ref, cw_ref, cb_ref, wrg_ref, brg_ref, wig_ref,
                         big_ref, lam_ref, ya_all_ref, ya_ref, hl_ref, *, n_seq, n_steps):
    del ya_all_ref
    x = xa_ref[...]
    xp = jnp.concatenate([cst_ref[k] for k in range(CONV_W - 1)] + [x], axis=0)
    rows = n_seq * n_steps
    cw = cw_ref[...]
    xc = cb_ref[...]
    for k in range(CONV_W):
        xc = xc + xp[k * n_seq:k * n_seq + rows, :] * cw[k:k + 1, :]
    a, b = _lru_gates(xc, wrg_ref, brg_ref, wig_ref, big_ref, lam_ref)
    h = h0_ref[...]
    for t in range(n_steps):
        sl = slice(t * n_seq, (t + 1) * n_seq)
        h = a[sl, :] * h + b[sl, :]
        ya_ref[sl, :] = (h * jax.nn.gelu(ga_ref[sl, :])).astype(ya_ref.dtype)
    hl_ref[...] = h


def _s5_disc_kernel(are_ref, aim_ref, ldt_ref, abr_ref, abi_ref, cfr_ref, cfi_ref):
    a_re = are_ref[...]
    a_im = aim_ref[...]
    dt = jnp.exp(ldt_ref[...])
    mag = jnp.exp(a_re * dt)
    abr = mag * jnp.cos(a_im * dt)
    abi = mag * jnp.sin(a_im * dt)
    den = a_re * a_re + a_im * a_im
    nr = abr - 1.0
    abr_ref[...] = abr
    abi_ref[...] = abi
    cr = (nr * a_re + abi * a_im) / den
    ci = (abi * a_re - nr * a_im) / den
    for q in range(cfr_ref.shape[0]):
        cfr_ref[q] = cr
        cfi_ref[q] = ci
        cr, ci = abr * cr - abi * ci, abr * ci + abi * cr


def _gates_s5_prompt_kernel(u_ref, w_ref, xb_ref, wb_ref, wc_ref, ar_ref, ai_ref, d_ref, g_all_ref,
                            gl_ref, g_ref, sre_ref, sim_ref, cre, cim):
    del g_all_ref
    n_par, _, hw = ar_ref.shape

    @pl.when(pl.program_id(2) == 0)
    def _():
        cre[...] = jnp.zeros_like(cre)
        cim[...] = jnp.zeros_like(cim)

    split = (gl_ref.shape[0] * 9 // 16) // (2 * SUBLANES) * (2 * SUBLANES)
    w = w_ref[...].astype(BF16)
    bus = [_s5_project_in(xb_ref[:, p * LANES:(p + 1) * LANES], wb_ref.at[p]) for p in range(n_par)]
    gl_ref[:split, :] = jnp.dot(u_ref[:split, :], w, preferred_element_type=F32)
    scans = [_s5_slab_scan(bus[p], ar_ref[p], ai_ref[p], cre[p], cim[p]) for p in range(n_par)]
    y = jnp.dot(jnp.concatenate([s[0] for s in scans], axis=1), wc_ref[...], preferred_element_type=F32)
    gl_ref[split:, :] = jnp.dot(u_ref[split:, :], w, preferred_element_type=F32)
    g_ref[...] = jax.nn.gelu(y + d_ref[...] * xb_ref[...])
    for p in range(n_par):
        _, cr, ci = scans[p]
        cre[p] = cr
        cim[p] = ci
        sre_ref[:, p * hw:(p + 1) * hw] = cr
        sim_ref[:, p * hw:(p + 1) * hw] = ci


def _s5_project_in(xb, wb_ref):
    tm, width = xb.shape
    n_slab = tm // SUBLANES
    n_lags = wb_ref.shape[0] // width
    xb3 = xb.reshape(n_slab, SUBLANES, width)
    subx = lax.broadcasted_iota(jnp.int32, (1, SUBLANES, width), 1)
    lagged = [xb3] + [jnp.where(subx >= q, pltpu.roll(xb3, q, 1), 0.0) for q in range(1, n_lags)]
    xs = jnp.concatenate(lagged, axis=2).reshape(tm, n_lags * width)
    return jnp.dot(xs.astype(BF16), wb_ref[...], preferred_element_type=F32), n_lags


def _s5_slab_scan(bu_lags, ar, ai, cr, ci):
    bu, n_lags = bu_lags
    tm = bu.shape[0]
    hw = ar.shape[1]
    n_slab = tm // SUBLANES
    sub = lax.broadcasted_iota(jnp.int32, (SUBLANES, hw), 0)
    pr, pi = ar, ai
    tr = jnp.where(sub == 0, pr, 0.0)
    ti = jnp.where(sub == 0, pi, 0.0)
    steps = []
    d = 1
    while d < SUBLANES:
        mr = jnp.where(sub >= d, pr, 0.0)
        mi = jnp.where(sub >= d, pi, 0.0)
        if d >= n_lags:
            steps.append((d, mr, mi))
        sr = pltpu.roll(tr, d, 0)
        si = pltpu.roll(ti, d, 0)
        tr, ti = tr + (mr * sr - mi * si), ti + (mr * si + mi * sr)
        pr, pi = pr * pr - pi * pi, 2.0 * (pr * pi)
        d *= 2
    slabs = []
    for s in range(n_slab):
        rows = slice(s * SUBLANES, (s + 1) * SUBLANES)
        hr = bu[rows, :hw]
        hi = bu[rows, hw:]
        for d, mr, mi in steps:
            sr = pltpu.roll(hr, d, 0)
            si = pltpu.roll(hi, d, 0)
            hr, hi = hr + (mr * sr - mi * si), hi + (mr * si + mi * sr)
        cbr = jnp.broadcast_to(cr, (SUBLANES, hw))
        cbi = jnp.broadcast_to(ci, (SUBLANES, hw))
        hr, hi = hr + (tr * cbr - ti * cbi), hi + (tr * cbi + ti * cbr)
        cr = hr[SUBLANES - 1:SUBLANES, :]
        ci = hi[SUBLANES - 1:SUBLANES, :]
        slabs.append(jnp.concatenate([hr, hi], axis=1))
    return jnp.concatenate(slabs, axis=0).astype(BF16), cr, ci


def _s5_sample_kernel(xb_ref, wb_ref, wc_ref, ar_ref, ai_ref, d_ref, s0r_ref, s0i_ref, g_all_ref,
                      g_ref, sre_ref, sim_ref, bu_scr, h_scr, *, n_seq, n_steps):
    del g_all_ref
    hw = ar_ref.shape[1]
    xb = xb_ref[...]
    bu_scr[...] = jnp.dot(xb.astype(BF16), wb_ref[...], preferred_element_type=F32)
    ar = ar_ref[...]
    ai = ai_ref[...]
    hr = s0r_ref[...]
    hi = s0i_ref[...]
    for t in range(n_steps):
        sl = slice(t * n_seq, (t + 1) * n_seq)
        hr, hi = (ar * hr - ai * hi) + bu_scr[sl, :hw], (ar * hi + ai * hr) + bu_scr[sl, hw:]
        h_scr[sl, :hw] = hr.astype(h_scr.dtype)
        h_scr[sl, hw:] = hi.astype(h_scr.dtype)
    sre_ref[...] = hr
    sim_ref[...] = hi
    y = jnp.dot(h_scr[...], wc_ref[...], preferred_element_type=F32)
    g_ref[...] = jax.nn.gelu(y + d_ref[...] * xb)


def _half_ffn_matmuls(h, wg, wu, wd, *, bm):
    m, dm = h.shape
    dff = wg.shape[1]
    bn_up = _pick(dff, (256, 128))
    n_i, n_j = m // bm, dff // bn_up
    rb = LANES
    n_rb = dff // rb
    assert n_i * n_j >= n_rb, "not enough up-projection steps to cover the down-projection weight"
    wd_blk = lambda i, j: (jnp.minimum(i * n_j + j, n_rb - 1), 0)
    a, wd = pl.pallas_call(
        _ffn_up_kernel,
        grid=(n_i, n_j),
        in_specs=[pl.BlockSpec((bm, dm), lambda i, j: (i, 0)),
                  pl.BlockSpec((dm, bn_up), lambda i, j: (0, j)),
                  pl.BlockSpec((dm, bn_up), lambda i, j: (0, j)),
                  pl.BlockSpec((rb, dm), wd_blk)],
        out_specs=[pl.BlockSpec((bm, bn_up), lambda i, j: (i, j)),
                   pl.BlockSpec((rb, dm), wd_blk)],
        out_shape=[jax.ShapeDtypeStruct((m, dff), BF16), jax.ShapeDtypeStruct((dff, dm), BF16)],
        compiler_params=_cparams("arbitrary", "arbitrary"),
        name="ffn_up",
    )(h, wg, wu, wd)

    bm_d = _pick(m, (1024, 512, 256, 128))
    bn_d = _pick(dm, (1024, 512, 256, 128))
    bk = dff // 2 if (dff // 2) % LANES == 0 else dff
    return pl.pallas_call(
        _ffn_down_kernel,
        grid=(m // bm_d, dm // bn_d, dff // bk),
        in_specs=[pl.BlockSpec((bm_d, bk), lambda i, j, k: (i, k)),
                  pl.BlockSpec((bk, bn_d), lambda i, j, k: (k, j))],
        out_specs=pl.BlockSpec((bm_d, bn_d), lambda i, j, k: (i, j)),
        out_shape=jax.ShapeDtypeStruct((m, dm), F32),
        compiler_params=_cparams("parallel", "parallel", "arbitrary"),
        name="ffn_down",
    )(a, wd)


def _matmul(x, w, *, bm, bn, out_dtype, name, n=None):
    m, k = x.shape
    n = w.shape[1] if n is None else n
    return pl.pallas_call(
        _mm_kernel,
        grid=(m // bm, n // bn),
        in_specs=[pl.BlockSpec((bm, k), lambda i, j: (i, 0)),
                  pl.BlockSpec((k, bn), lambda i, j: (0, j))],
        out_specs=pl.BlockSpec((bm, bn), lambda i, j: (i, j)),
        out_shape=jax.ShapeDtypeStruct((m, n), out_dtype),
        compiler_params=_cparams("parallel", "arbitrary"),
        name=name,
    )(x, w)


def kernel(x_prompt, x_sample, state_lru_h, state_conv, state_ssm_re, state_ssm_im, ffn1_pre_g, ffn1_post_g, ffn1_w_gate, ffn1_w_up, ffn1_w_down, mix_pre_g, mix_post_g, w_in, conv_w, conv_b, w_rg, b_rg, w_ig, b_ig, lru_lambda, ssm_a_re, ssm_a_im, ssm_log_dt, ssm_b_re, ssm_b_im, ssm_c_re, ssm_c_im, ssm_d, w_glu, b_glu, w_out_a, w_out_b, w_o, ffn2_pre_g, ffn2_post_g, ffn2_w_gate, ffn2_w_up, ffn2_w_down):
    nb, seq, dm = x_prompt.shape
    db, dseq, _ = x_sample.shape
    depth = state_lru_h.shape[0]
    assert depth == 1, "one decoder layer"
    d_lru = state_lru_h.shape[2]
    n_grp, ssm_p = state_ssm_re.shape[2], state_ssm_re.shape[3]
    d_ssm = ssm_d.shape[1]
    ssm_cg = d_ssm // n_grp
    d_in = w_in.shape[2]
    assert w_rg.shape[2] == LRU_BLK and conv_w.shape[1] == CONV_W
    assert d_in == 2 * d_lru + d_ssm + 2 * dm and seq >= CONV_W - 1 and dseq >= CONV_W - 1

    mp = nb * seq
    ms = db * dseq
    m = mp + ms
    sdt = state_lru_h.dtype

    row2 = lambda v: v.reshape(1, -1)
    bf = lambda v: v[0].astype(BF16)

    xp2 = x_prompt.reshape(mp, dm)
    xs2 = jnp.swapaxes(x_sample, 0, 1).reshape(ms, dm)

    tr = _pick(math.gcd(mp, ms), (256, 128, 64, 32, 16, 8))
    npt = mp // tr
    xp_spec = pl.BlockSpec((tr, dm), lambda i: (jnp.minimum(i, npt - 1), 0))
    xs_spec = pl.BlockSpec((tr, dm), lambda i: (jnp.maximum(i - npt, 0), 0))
    row_spec = pl.BlockSpec((tr, dm), lambda i: (i, 0))
    g_spec = pl.BlockSpec((1, dm), lambda i: (0, 0))

    h1 = pl.pallas_call(
        functools.partial(_prenorm_kernel, n_prompt_tiles=npt),
        grid=(m // tr,),
        in_specs=[xp_spec, xs_spec, g_spec],
        out_specs=row_spec,
        out_shape=jax.ShapeDtypeStruct((m, dm), BF16),
        compiler_params=_cparams("parallel"),
        name="prenorm1",
    )(xp2, xs2, row2(ffn1_pre_g))

    bm = _pick(m, (1536, 1024, 768, 512, 384, 256, 128))
    f1 = _half_ffn_matmuls(h1, ffn1_w_gate[0], ffn1_w_up[0], ffn1_w_down[0], bm=bm)

    x1, u = pl.pallas_call(
        functools.partial(_post_ffn1_kernel, n_prompt_tiles=npt),
        grid=(m // tr,),
        in_specs=[xp_spec, xs_spec, row_spec, g_spec, g_spec],
        out_specs=[row_spec, row_spec],
        out_shape=[jax.ShapeDtypeStruct((m, dm), F32), jax.ShapeDtypeStruct((m, dm), BF16)],
        compiler_params=_cparams("parallel"),
        name="post_ffn1",
    )(xp2, xs2, f1, row2(ffn1_post_g), row2(mix_pre_g))

    bn_in = _pick(math.gcd(d_lru, dm), (512, 256, 128))
    n_a = 2 * d_lru + d_ssm
    z = _matmul(u, w_in[0], bm=bm, bn=bn_in, out_dtype=F32, name="in_proj", n=n_a)

    cb = _pick(d_lru, (512, 256, 128))
    ncb = d_lru // cb
    hpb = cb // LRU_BLK
    wrg = bf(w_rg)
    wig = bf(w_ig)
    lru_params = (conv_w[0], row2(conv_b), wrg, row2(b_rg), wig, row2(b_ig), row2(lru_lambda))
    assert d_lru <= dm and d_ssm <= dm
    any_spec = pl.BlockSpec(memory_space=pl.ANY)
    assert mp % ms == 0, "sample rows must tile the unified row axis"
    s_blk = mp // ms

    n_lags = 2
    abr, abi, cfr, cfi = pl.pallas_call(
        _s5_disc_kernel,
        out_shape=[jax.ShapeDtypeStruct((n_grp, ssm_p), F32)] * 2
                  + [jax.ShapeDtypeStruct((n_lags, n_grp, ssm_p), F32)] * 2,
        name="s5_discretise",
    )(ssm_a_re[0], ssm_a_im[0], ssm_log_dt[0].reshape(n_grp, 1))
    bb_re = cfr[..., None] * ssm_b_re - cfi[..., None] * ssm_b_im
    bb_im = cfr[..., None] * ssm_b_im + cfi[..., None] * ssm_b_re

    gpb = LANES // ssm_cg
    nj = n_grp // gpb
    hw = gpb * ssm_p
    bb =jnp.stack([bb_re, bb_im]).reshape(2, n_lags, nj, gpb, ssm_p, ssm_cg).astype(BF16)
    bb = jnp.transpose(bb, (2, 1, 3, 5, 0, 4)).reshape(nj, n_lags * LANES, 2, ssm_p)
    wb = jnp.concatenate([bb[:, :, ri, :] for ri in range(2) for _ in range(gpb)], axis=2)
    wb_shape = (1, n_lags * LANES, 2 * hw)
    g_row = (lax.broadcasted_iota(jnp.int32, wb_shape, 1) // ssm_cg) % gpb
    h_col = (lax.broadcasted_iota(jnp.int32, wb_shape, 2) // ssm_p) % gpb
    wb = jnp.where(g_row == h_col, wb, jnp.zeros((), BF16))
    cc = jnp.stack([ssm_c_re[0], -ssm_c_im[0]]).reshape(2, nj, gpb, ssm_cg, ssm_p).astype(BF16)
    cc = jnp.transpose(cc, (1, 0, 4, 2, 3)).reshape(nj, 2, ssm_p, LANES)
    wc = jnp.concatenate([cc[:, ri, :, :] for ri in range(2) for _ in range(gpb)], axis=1)
    wc_shape = (1, 2 * hw, LANES)
    wc_mask = ((lax.broadcasted_iota(jnp.int32, wc_shape, 1) // ssm_p) % gpb
               == lax.broadcasted_iota(jnp.int32, wc_shape, 2) // ssm_cg)
    wc = jnp.where(wc_mask, wc, jnp.zeros((), BF16))
    abr3 = abr.reshape(nj, 1, hw)
    abi3 = abi.reshape(nj, 1, hw)
    xb_blk = (2 * d_lru) // LANES
    dsk = row2(ssm_d)

    n_par = 2 if nj % 2 == 0 and xb_blk % 2 == 0 else 1
    par_shape = (1, n_par * 2 * hw, n_par * LANES)
    par_mask = (lax.broadcasted_iota(jnp.int32, par_shape, 1) // (2 * hw)
                == lax.broadcasted_iota(jnp.int32, par_shape, 2) // LANES)
    wc_par = jnp.concatenate([wc.reshape(nj // n_par, n_par * 2 * hw, LANES)] * n_par, axis=2)
    wc_par = jnp.where(par_mask, wc_par, jnp.zeros((), BF16))
    n_gi = nj // n_par
    bm_gl = m // n_gi
    assert m % n_gi == 0 and bm_gl % (2 * SUBLANES) == 0

    def gl_cols_ok(t):
        bn = (2 * dm) // (nb * (seq // t)) if (2 * dm) % (nb * (seq // t)) == 0 else 0
        return bn > 0 and bn % LANES == 0 and n_a % bn == 0

    tm5 = _pick(seq, tuple(t for t in (512, 256, 128, 64, 32, 16, 8) if seq % t == 0 and gl_cols_ok(t)))
    rp5 = seq // tm5
    bn_gl = (2 * dm) // (nb * rp5)
    gl_blk = n_a // bn_gl
    gl, g_all, sre_p, sim_p = pl.pallas_call(
        _gates_s5_prompt_kernel,
        grid=(n_gi, nb, rp5),
        in_specs=[pl.BlockSpec((bm_gl, dm), lambda j, n, r: (j, 0)),
                  pl.BlockSpec((dm, bn_gl), lambda j, n, r: (0, gl_blk + n * rp5 + r)),
                  pl.BlockSpec((tm5, n_par * LANES), lambda j, n, r: (n * rp5 + r, xb_blk // n_par + j)),
                  pl.BlockSpec((n_par, n_lags * LANES, 2 * hw), lambda j, n, r: (j, 0, 0)),
                  pl.BlockSpec((None, n_par * 2 * hw, n_par * LANES), lambda j, n, r: (j, 0, 0)),
                  pl.BlockSpec((n_par, 1, hw), lambda j, n, r: (j, 0, 0)),
                  pl.BlockSpec((n_par, 1, hw), lambda j, n, r: (j, 0, 0)),
                  pl.BlockSpec((1, n_par * LANES), lambda j, n, r: (0, j)), any_spec],
        out_specs=[pl.BlockSpec((bm_gl, bn_gl), lambda j, n, r: (j, n * rp5 + r)),
                   pl.BlockSpec((tm5, n_par * LANES), lambda j, n, r: (n * rp5 + r, j)),
                   pl.BlockSpec((None, 1, n_par * hw), lambda j, n, r: (n, 0, j)),
                   pl.BlockSpec((None, 1, n_par * hw), lambda j, n, r: (n, 0, j))],
        out_shape=[jax.ShapeDtypeStruct((m, 2 * dm), F32),
                   jax.ShapeDtypeStruct((m, dm), F32),
                   jax.ShapeDtypeStruct((nb, 1, n_grp * ssm_p), F32),
                   jax.ShapeDtypeStruct((nb, 1, n_grp * ssm_p), F32)],
        scratch_shapes=[pltpu.VMEM((n_par, 1, hw), F32), pltpu.VMEM((n_par, 1, hw), F32)],
        input_output_aliases={8: 1},
        compiler_params=_cparams("arbitrary", "arbitrary", "arbitrary"),
        name="gates_s5_prompt",
    )(u, w_in[0], z, wb, wc_par, abr3, abi3, dsk, f1)

    s5_w_specs1 = [pl.BlockSpec((None, LANES, 2 * hw), lambda j: (j, 0, 0)),
                   pl.BlockSpec((None, 2 * hw, LANES), lambda j: (j, 0, 0)),
                   pl.BlockSpec((None, 1, hw), lambda j: (j, 0, 0)),
                   pl.BlockSpec((None, 1, hw), lambda j: (j, 0, 0)),
                   pl.BlockSpec((1, LANES), lambda j: (0, j))]
    g_all, sre_s, sim_s = pl.pallas_call(
        functools.partial(_s5_sample_kernel, n_seq=db, n_steps=dseq),
        grid=(nj,),
        in_specs=[pl.BlockSpec((ms, LANES), lambda j: (s_blk, xb_blk + j))] + s5_w_specs1
                 + [pl.BlockSpec((db, hw), lambda j: (0, j)), pl.BlockSpec((db, hw), lambda j: (0, j)), any_spec],
        out_specs=[pl.BlockSpec((ms, LANES), lambda j: (s_blk, j)),
                   pl.BlockSpec((db, hw), lambda j: (0, j)),
                   pl.BlockSpec((db, hw), lambda j: (0, j))],
        out_shape=[jax.ShapeDtypeStruct((m, dm), F32),
                   jax.ShapeDtypeStruct((db, n_grp * ssm_p), F32),
                   jax.ShapeDtypeStruct((db, n_grp * ssm_p), F32)],
        scratch_shapes=[pltpu.VMEM((ms, 2 * hw), F32), pltpu.VMEM((ms, 2 * hw), BF16)],
        input_output_aliases={8: 0},
        compiler_params=_cparams("parallel"),
        name="s5_sample",
    )(z, wb, wc, abr3, abi3, dsk,
      state_ssm_re[0].reshape(db, n_grp * ssm_p), state_ssm_im[0].reshape(db, n_grp * ssm_p), g_all)

    assert d_ssm // cb == ncb
    tm = _pick(seq, tuple(t for t in (512, 256, 128, 64, 32, 16, 8)
                          if seq % t == 0 and (m * t) % mp == 0 and ((m * t) // mp) % (2 * SUBLANES) == 0))
    rpt = seq // tm
    bm_u = (m * tm) // mp
    col = lambda shape: pl.BlockSpec(shape, lambda i, j: (0, j))
    head_spec = pl.BlockSpec((hpb, LRU_BLK, LRU_BLK), lambda i, j: (j, 0, 0))
    yb, ya, hl_p = pl.pallas_call(
        functools.partial(_glu_rglru_prompt_kernel, tiles_per_seq=rpt),
        grid=(nb * rpt, ncb),
        in_specs=[pl.BlockSpec((bm_u, d_ssm), lambda i, j: (i, 0)),
                  pl.BlockSpec((bm_u, cb), lambda i, j: (i, j)),
                  pl.BlockSpec((d_ssm, cb), lambda i, j: (0, j)),
                  col((1, cb)),
                  pl.BlockSpec((tm, cb), lambda i, j: (i, j)),
                  pl.BlockSpec((tm, cb), lambda i, j: (i, ncb + j)),
                  col((CONV_W, cb)), col((1, cb)), head_spec, col((1, cb)), head_spec, col((1, cb)),
                  col((1, cb)), any_spec],
        out_specs=[pl.BlockSpec((bm_u, cb), lambda i, j: (i, j)),
                   pl.BlockSpec((tm, cb), lambda i, j: (i, j)),
                   pl.BlockSpec((None, 1, cb), lambda i, j: (i // rpt, 0, j))],
        out_shape=[jax.ShapeDtypeStruct((m, d_ssm), BF16),
                   jax.ShapeDtypeStruct((m, dm), BF16),
                   jax.ShapeDtypeStruct((nb, 1, d_lru), F32)],
        scratch_shapes=[pltpu.VMEM((bm_u, d_ssm), BF16),
                        pltpu.VMEM((ncb, SUBLANES, cb), F32), pltpu.VMEM((ncb, 1, cb), F32)],
        input_output_aliases={4 + 2 + len(lru_params): 1},
        compiler_params=_cparams("arbitrary", "arbitrary"),
        name="glu_rglru_prompt",
    )(g_all, g_all, w_glu[0], row2(b_glu), z, z, *lru_params, h1)

    par1 = lambda shape: pl.BlockSpec(shape, lambda c: (0, c))
    cst = jnp.swapaxes(state_conv[0], 0, 1)
    ya, hl_s = pl.pallas_call(
        functools.partial(_rglru_sample_kernel, n_seq=db, n_steps=dseq),
        grid=(ncb,),
        in_specs=[pl.BlockSpec((ms, cb), lambda c: (s_blk, c)),
                  pl.BlockSpec((ms, cb), lambda c: (s_blk, ncb + c)),
                  pl.BlockSpec((CONV_W - 1, db, cb), lambda c: (0, 0, c)),
                  par1((db, cb)),
                  par1((CONV_W, cb)), par1((1, cb)),
                  pl.BlockSpec((hpb, LRU_BLK, LRU_BLK), lambda c: (c, 0, 0)), par1((1, cb)),
                  pl.BlockSpec((hpb, LRU_BLK, LRU_BLK), lambda c: (c, 0, 0)), par1((1, cb)),
                  par1((1, cb)), any_spec],
        out_specs=[pl.BlockSpec((ms, cb), lambda c: (s_blk, c)), par1((db, cb))],
        out_shape=[jax.ShapeDtypeStruct((m, dm), BF16), jax.ShapeDtypeStruct((db, d_lru), F32)],
        input_output_aliases={4 + len(lru_params): 0},
        compiler_params=_cparams("parallel"),
        name="rglru_sample",
    )(z, z, cst, state_lru_h[0], *lru_params, ya)

    bm_g = _pick(m, (1024, 512, 256, 128))

    bn_m = bn_in
    gla_blk = 0
    glb_blk = dm // bn_m
    merged = pl.pallas_call(
        _merge_kernel,
        grid=(m // bm_g, dm // bn_m),
        in_specs=[pl.BlockSpec((bm_g, d_lru), lambda i, j: (i, 0)),
                  pl.BlockSpec((bm_g, d_ssm), lambda i, j: (i, 0)),
                  pl.BlockSpec((d_lru, bn_m), lambda i, j: (0, j)),
                  pl.BlockSpec((d_ssm, bn_m), lambda i, j: (0, j)),
                  pl.BlockSpec((bm_g, bn_m), lambda i, j: (i, gla_blk + j)),
                  pl.BlockSpec((bm_g, bn_m), lambda i, j: (i, glb_blk + j))],
        out_specs=pl.BlockSpec((bm_g, bn_m), lambda i, j: (i, j)),
        out_shape=jax.ShapeDtypeStruct((m, dm), BF16),
        compiler_params=_cparams("parallel", "arbitrary"),
        name="gated_merge",
    )(ya, yb, w_out_a[0], w_out_b[0], gl, gl)

    o = _matmul(merged, w_o[0], bm=bm, bn=bn_in, out_dtype=F32, name="o_proj")

    x2, h2 = pl.pallas_call(
        _post_mix_kernel,
        grid=(m // tr,),
        in_specs=[row_spec, row_spec, g_spec, g_spec],
        out_specs=[row_spec, row_spec],
        out_shape=[jax.ShapeDtypeStruct((m, dm), F32), jax.ShapeDtypeStruct((m, dm), BF16)],
        compiler_params=_cparams("parallel"),
        name="post_mix",
    )(x1, o, row2(mix_post_g), row2(ffn2_pre_g))

    f2 = _half_ffn_matmuls(h2, ffn2_w_gate[0], ffn2_w_up[0], ffn2_w_down[0], bm=bm)

    def final(rows, first_tile):
        return pl.pallas_call(
            _final_kernel,
            grid=(rows // tr,),
            in_specs=[pl.BlockSpec((tr, dm), lambda i: (first_tile + i, 0)),
                      pl.BlockSpec((tr, dm), lambda i: (first_tile + i, 0)),
                      g_spec],
            out_specs=row_spec,
            out_shape=jax.ShapeDtypeStruct((rows, dm), F32),
            compiler_params=_cparams("parallel"),
            name="final_residual",
        )(x2, f2, row2(ffn2_post_g))

    y_prompt = final(mp, 0).reshape(nb, seq, dm)
    y_sample = jnp.swapaxes(final(ms, npt).reshape(dseq, db, dm), 0, 1)

    nk = CONV_W - 1
    prompt_conv = jnp.stack([lax.slice(z, ((n + 1) * seq - nk, 0), ((n + 1) * seq, d_lru)) for n in range(nb)])
    sample_conv = jnp.swapaxes(lax.slice(z, (m - nk * db, 0), (m, d_lru)).reshape(nk, db, d_lru), 0, 1)
    st = lambda v, n: v.reshape(1, n, n_grp, ssm_p).astype(sdt)
    return (y_prompt, y_sample,
            hl_p.reshape(1, nb, d_lru).astype(sdt), prompt_conv[None].astype(sdt),
            st(sre_p, nb), st(sim_p, nb),
            hl_s.reshape(1, db, d_lru).astype(sdt), sample_conv[None].astype(sdt),
            st(sre_s, db), st(sim_s, db))
```

```python
import functools
import math

import jax
import jax.numpy as jnp
from jax import lax
from jax.experimental import pallas as pl
from jax.experimental.pallas import tpu as pltpu

F32 = jnp.float32
BF16 = jnp.bfloat16

EPS = 1e-6
C_RG = 8.0
CONV_W = 4
LANES = 128
SUBLANES = 8
LRU_BLK = 128
VMEM_LIMIT = 56 * 1024 * 1024


def _cparams(*sem):
    return pltpu.CompilerParams(dimension_semantics=sem, vmem_limit_bytes=VMEM_LIMIT)


def _pick(n, candidates):
    for c in candidates:
        if n % c == 0:
            return c
    raise ValueError(f"no tile in {candidates} divides {n}")


def _rms(x, g):
    return x * lax.rsqrt(jnp.mean(x * x, axis=-1, keepdims=True) + EPS) * g


def _softplus(x):
    return jnp.maximum(x, 0.0) + jnp.log1p(jnp.exp(-jnp.abs(x)))


def _two_group_rows(i, n_prompt_tiles, xp_ref, xs_ref, body):
    @pl.when(i < n_prompt_tiles)
    def _():
        body(xp_ref[...])

    @pl.when(i >= n_prompt_tiles)
    def _():
        body(xs_ref[...])


def _prenorm_kernel(xp_ref, xs_ref, g_ref, h_ref, *, n_prompt_tiles):
    def body(x):
        h_ref[...] = _rms(x, g_ref[...]).astype(h_ref.dtype)

    _two_group_rows(pl.program_id(0), n_prompt_tiles, xp_ref, xs_ref, body)


def _post_ffn1_kernel(xp_ref, xs_ref, f_ref, gpost_ref, gpre_ref, x1_ref, u_ref, *, n_prompt_tiles):
    def body(x):
        x1 = x + 0.5 * _rms(f_ref[...], gpost_ref[...])
        x1_ref[...] = x1
        u_ref[...] = _rms(x1, gpre_ref[...]).astype(u_ref.dtype)

    _two_group_rows(pl.program_id(0), n_prompt_tiles, xp_ref, xs_ref, body)


def _post_mix_kernel(x1_ref, o_ref, gpost_ref, gpre_ref, x2_ref, h_ref):
    x2 = x1_ref[...] + _rms(o_ref[...], gpost_ref[...])
    x2_ref[...] = x2
    h_ref[...] = _rms(x2, gpre_ref[...]).astype(h_ref.dtype)


def _final_kernel(x2_ref, f_ref, g_ref, y_ref):
    y_ref[...] = x2_ref[...] + 0.5 * _rms(f_ref[...], g_ref[...])


def _ffn_up_kernel(h_ref, wg_ref, wu_ref, wd_ref, a_ref, wd16_ref):
    wg = wg_ref[...].astype(BF16)
    wu = wu_ref[...].astype(BF16)
    half = h_ref.shape[0] // 2
    for rows in (slice(0, half), slice(half, None)):
        h = h_ref[rows, :]
        g = jnp.dot(h, wg, preferred_element_type=F32)
        u = jnp.dot(h, wu, preferred_element_type=F32)
        a_ref[rows, :] = (jax.nn.silu(g) * u).astype(a_ref.dtype)
    wd16_ref[...] = wd_ref[...].astype(wd16_ref.dtype)


def _ffn_down_kernel(a_ref, wd_ref, o_ref):
    p = jnp.dot(a_ref[...], wd_ref[...], preferred_element_type=F32)

    @pl.when(pl.program_id(2) == 0)
    def _():
        o_ref[...] = p

    @pl.when(pl.program_id(2) != 0)
    def _():
        o_ref[...] += p


def _mm_kernel(x_ref, w_ref, o_ref):
    w = w_ref[...].astype(BF16)
    o_ref[...] = jnp.dot(x_ref[...], w, preferred_element_type=F32).astype(o_ref.dtype)


def _merge_kernel(ya_ref, yb_ref, wa_ref, wb_ref, gla_ref, glb_ref, o_ref):
    wa = wa_ref[...].astype(BF16)
    wb = wb_ref[...].astype(BF16)
    half = o_ref.shape[0] // 2
    for rows in (slice(0, half), slice(half, None)):
        pa = jnp.dot(ya_ref[rows, :], wa, preferred_element_type=F32)
        pb = jnp.dot(yb_ref[rows, :], wb, preferred_element_type=F32)
        m = jax.nn.sigmoid(gla_ref[rows, :]) * pa + jax.nn.sigmoid(glb_ref[rows, :]) * pb
        o_ref[rows, :] = m.astype(o_ref.dtype)


def _lru_gates(xc, wrg_ref, brg_ref, wig_ref, big_ref, lam_ref):
    xcb = xc.astype(BF16)
    rs, gs = [], []
    for hh in range(wrg_ref.shape[0]):
        xh = xcb[:, hh * LRU_BLK:(hh + 1) * LRU_BLK]
        rs.append(jnp.dot(xh, wrg_ref[hh], preferred_element_type=F32))
        gs.append(jnp.dot(xh, wig_ref[hh], preferred_element_type=F32))
    r = jax.nn.sigmoid(jnp.concatenate(rs, axis=1) + brg_ref[...])
    i = jax.nn.sigmoid(jnp.concatenate(gs, axis=1) + big_ref[...])
    log_a = -C_RG * r * _softplus(-lam_ref[...])
    a = jnp.exp(log_a)
    mult = jnp.sqrt(-jnp.tanh(log_a) * (a * a + 1.0))
    return a, mult * (i * xc)


def _glu_rglru_prompt_kernel(g_ref, gcol_ref, w_ref, b_ref,
                             xa_ref, ga_ref, cw_ref, cb_ref, wrg_ref, brg_ref, wig_ref, big_ref, lam_ref,
                             ya_all_ref, o_ref, ya_ref, hl_ref, g_scr, xcars, hcars, *, tiles_per_seq):
    del ya_all_ref
    i, j = pl.program_id(0), pl.program_id(1)

    @pl.when(j == 0)
    def _():
        g_scr[...] = g_ref[...].astype(g_scr.dtype)

    _rglru_prompt_tile(xa_ref, ga_ref, cw_ref, cb_ref, wrg_ref, brg_ref, wig_ref, big_ref, lam_ref,
                       ya_ref, hl_ref, xcars.at[j], hcars.at[j], i % tiles_per_seq == 0)
    s = jnp.dot(g_scr[...], w_ref[...].astype(BF16), preferred_element_type=F32) + b_ref[...]
    o_ref[...] = (gcol_ref[...] * jax.nn.sigmoid(s)).astype(o_ref.dtype)


def _rglru_prompt_tile(xa_ref, ga_ref, cw_ref, cb_ref, wrg_ref, brg_ref, wig_ref, big_ref, lam_ref,
                       ya_ref, hl_ref, xcar, hcar, starts_sequence):
    tm = xa_ref.shape[0]

    @pl.when(starts_sequence)
    def _():
        xcar[...] = jnp.zeros_like(xcar)
        hcar[...] = jnp.zeros_like(hcar)

    x = xa_ref[...]
    xfull = jnp.concatenate([xcar[...], x], axis=0)
    xcar[...] = x[tm - SUBLANES:, :]
    cw = cw_ref[...]
    xc = cb_ref[...]
    for k in range(CONV_W - 1):
        xs = pltpu.roll(xfull, CONV_W - 1 - k, 0)[SUBLANES:, :]
        xc = xc + xs * cw[k:k + 1, :]
    xc = xc + x * cw[CONV_W - 1:CONV_W, :]

    a, b = _lru_gates(xc, wrg_ref, brg_ref, wig_ref, big_ref, lam_ref)
    cb = a.shape[1]
    n_slab = tm // SUBLANES
    a = a.reshape(n_slab, SUBLANES, cb)
    b = b.reshape(n_slab, SUBLANES, cb)
    sub = lax.broadcasted_iota(jnp.int32, (1, SUBLANES, cb), 1)
    d = 1
    while d < SUBLANES:
        m = sub >= d
        a_s = jnp.where(m, pltpu.roll(a, d, 1), 1.0)
        b_s = jnp.where(m, pltpu.roll(b, d, 1), 0.0)
        b = a * b_s + b
        a = a * a_s
        d *= 2
    h = hcar[...]
    hs = []
    for s in range(n_slab):
        h_slab = b[s] + a[s] * jnp.broadcast_to(h, (SUBLANES, cb))
        h = h_slab[SUBLANES - 1:SUBLANES, :]
        hs.append(h_slab)
    hcar[...] = h
    hl_ref[...] = h
    ya_ref[...] = (jnp.concatenate(hs, axis=0) * jax.nn.gelu(ga_ref[...])).astype(ya_ref.dtype)


def _rglru_sample_kernel(xa_ref, ga_ref, cst_ref, h0_ref, cw_ref, cb_ref, wrg_ref, brg_ref, wig_ref,
                         big_ref, lam_ref, ya_all_ref, ya_ref, hl_ref, *, n_seq, n_steps):
    del ya_all_ref
    x = xa_ref[...]
    xp = jnp.concatenate([cst_ref[k] for k in range(CONV_W - 1)] + [x], axis=0)
    rows = n_seq * n_steps
    cw = cw_ref[...]
    xc = cb_ref[...]
    for k in range(CONV_W):
        xc = xc + xp[k * n_seq:k * n_seq + rows, :] * cw[k:k + 1, :]
    a, b = _lru_gates(xc, wrg_ref, brg_ref, wig_ref, big_ref, lam_ref)
    h = h0_ref[...]
    for t in range(n_steps):
        sl = slice(t * n_seq, (t + 1) * n_seq)
        h = a[sl, :] * h + b[sl, :]
        ya_ref[sl, :] = (h * jax.nn.gelu(ga_ref[sl, :])).astype(ya_ref.dtype)
    hl_ref[...] = h


def _s5_disc_kernel(are_ref, aim_ref, ldt_ref, abr_ref, abi_ref, cfr_ref, cfi_ref):
    a_re = are_ref[...]
    a_im = aim_ref[...]
    dt = jnp.exp(ldt_ref[...])
    mag = jnp.exp(a_re * dt)
    abr = mag * jnp.cos(a_im * dt)
    abi = mag * jnp.sin(a_im * dt)
    den = a_re * a_re + a_im * a_im
    nr = abr - 1.0
    abr_ref[...] = abr
    abi_ref[...] = abi
    cr = (nr * a_re + abi * a_im) / den
    ci = (abi * a_re - nr * a_im) / den
    for q in range(cfr_ref.shape[0]):
        cfr_ref[q] = cr
        cfi_ref[q] = ci
        cr, ci = abr * cr - abi * ci, abr * ci + abi * cr


def _gates_s5_prompt_kernel(u_ref, w_ref, xb_ref, wb_ref, wc_ref, ar_ref, ai_ref, d_ref, g_all_ref,
                            gl_ref, g_ref, sre_ref, sim_ref, cre, cim):
    del g_all_ref
    n_par, _, hw = ar_ref.shape

    @pl.when(pl.program_id(2) == 0)
    def _():
        cre[...] = jnp.zeros_like(cre)
        cim[...] = jnp.zeros_like(cim)

    split = (gl_ref.shape[0] * 9 // 16) // (2 * SUBLANES) * (2 * SUBLANES)
    w = w_ref[...].astype(BF16)
    bus = [_s5_project_in(xb_ref[:, p * LANES:(p + 1) * LANES], wb_ref.at[p]) for p in range(n_par)]
    gl_ref[:split, :] = jnp.dot(u_ref[:split, :], w, preferred_element_type=F32)
    scans = [_s5_slab_scan(bus[p], ar_ref[p], ai_ref[p], cre[p], cim[p]) for p in range(n_par)]
    y = jnp.dot(jnp.concatenate([s[0] for s in scans], axis=1), wc_ref[...], preferred_element_type=F32)
    gl_ref[split:, :] = jnp.dot(u_ref[split:, :], w, preferred_element_type=F32)
    g_ref[...] = jax.nn.gelu(y + d_ref[...] * xb_ref[...])
    for p in range(n_par):
        _, cr, ci = scans[p]
        cre[p] = cr
        cim[p] = ci
        sre_ref[:, p * hw:(p + 1) * hw] = cr
        sim_ref[:, p * hw:(p + 1) * hw] = ci


def _s5_project_in(xb, wb_ref):
    tm, width = xb.shape
    n_slab = tm // SUBLANES
    n_lags = wb_ref.shape[0] // width
    xb3 = xb.reshape(n_slab, SUBLANES, width)
    subx = lax.broadcasted_iota(jnp.int32, (1, SUBLANES, width), 1)
    lagged = [xb3] + [jnp.where(subx >= q, pltpu.roll(xb3, q, 1), 0.0) for q in range(1, n_lags)]
    xs = jnp.concatenate(lagged, axis=2).reshape(tm, n_lags * width)
    return jnp.dot(xs.astype(BF16), wb_ref[...], preferred_element_type=F32), n_lags


def _s5_slab_scan(bu_lags, ar, ai, cr, ci):
    bu, n_lags = bu_lags
    tm = bu.shape[0]
    hw = ar.shape[1]
    n_slab = tm // SUBLANES
    sub = lax.broadcasted_iota(jnp.int32, (SUBLANES, hw), 0)
    pr, pi = ar, ai
    tr = jnp.where(sub == 0, pr, 0.0)
    ti = jnp.where(sub == 0, pi, 0.0)
    steps = []
    d = 1
    while d < SUBLANES:
        mr = jnp.where(sub >= d, pr, 0.0)
        mi = jnp.where(sub >= d, pi, 0.0)
        if d >= n_lags:
            steps.append((d, mr, mi))
        sr = pltpu.roll(tr, d, 0)
        si = pltpu.roll(ti, d, 0)
        tr, ti = tr + (mr * sr - mi * si), ti + (mr * si + mi * sr)
        pr, pi = pr * pr - pi * pi, 2.0 * (pr * pi)
        d *= 2
    slabs = []
    for s in range(n_slab):
        rows = slice(s * SUBLANES, (s + 1) * SUBLANES)
        hr = bu[rows, :hw]
        hi = bu[rows, hw:]
        for d, mr, mi in steps:
            sr = pltpu.roll(hr, d, 0)
            si = pltpu.roll(hi, d, 0)
            hr, hi = hr + (mr * sr - mi * si), hi + (mr * si + mi * sr)
        cbr = jnp.broadcast_to(cr, (SUBLANES, hw))
        cbi = jnp.broadcast_to(ci, (SUBLANES, hw))
        hr, hi = hr + (tr * cbr - ti * cbi), hi + (tr * cbi + ti * cbr)
        cr = hr[SUBLANES - 1:SUBLANES, :]
        ci = hi[SUBLANES - 1:SUBLANES, :]
        slabs.append(jnp.concatenate([hr, hi], axis=1))
    return jnp.concatenate(slabs, axis=0).astype(BF16), cr, ci


def _s5_sample_kernel(xb_ref, wb_ref, wc_ref, ar_ref, ai_ref, d_ref, s0r_ref, s0i_ref, g_all_ref,
                      g_ref, sre_ref, sim_ref, bu_scr, h_scr, *, n_seq, n_steps):
    del g_all_ref
    hw = ar_ref.shape[1]
    xb = xb_ref[...]
    bu_scr[...] = jnp.dot(xb.astype(BF16), wb_ref[...], preferred_element_type=F32)
    ar = ar_ref[...]
    ai = ai_ref[...]
    hr = s0r_ref[...]
    hi = s0i_ref[...]
    for t in range(n_steps):
        sl = slice(t * n_seq, (t + 1) * n_seq)
        hr, hi = (ar * hr - ai * hi) + bu_scr[sl, :hw], (ar * hi + ai * hr) + bu_scr[sl, hw:]
        h_scr[sl, :hw] = hr.astype(h_scr.dtype)
        h_scr[sl, hw:] = hi.astype(h_scr.dtype)
    sre_ref[...] = hr
    sim_ref[...] = hi
    y = jnp.dot(h_scr[...], wc_ref[...], preferred_element_type=F32)
    g_ref[...] = jax.nn.gelu(y + d_ref[...] * xb)


def _half_ffn_matmuls(h, wg, wu, wd, *, bm):
    m, dm = h.shape
    dff = wg.shape[1]
    bn_up = _pick(dff, (256, 128))
    n_i, n_j = m // bm, dff // bn_up
    rb = LANES
    n_rb = dff // rb
    assert n_i * n_j >= n_rb, "not enough up-projection steps to cover the down-projection weight"
    wd_blk = lambda i, j: (jnp.minimum(i * n_j + j, n_rb - 1), 0)
    a, wd = pl.pallas_call(
        _ffn_up_kernel,
        grid=(n_i, n_j),
        in_specs=[pl.BlockSpec((bm, dm), lambda i, j: (i, 0)),
                  pl.BlockSpec((dm, bn_up), lambda i, j: (0, j)),
                  pl.BlockSpec((dm, bn_up), lambda i, j: (0, j)),
                  pl.BlockSpec((rb, dm), wd_blk)],
        out_specs=[pl.BlockSpec((bm, bn_up), lambda i, j: (i, j)),
                   pl.BlockSpec((rb, dm), wd_blk)],
        out_shape=[jax.ShapeDtypeStruct((m, dff), BF16), jax.ShapeDtypeStruct((dff, dm), BF16)],
        compiler_params=_cparams("arbitrary", "arbitrary"),
        name="ffn_up",
    )(h, wg, wu, wd)

    bm_d = _pick(m, (1024, 512, 256, 128))
    bn_d = _pick(dm, (1024, 512, 256, 128))
    bk = dff // 2 if (dff // 2) % LANES == 0 else dff
    return pl.pallas_call(
        _ffn_down_kernel,
        grid=(m // bm_d, dm // bn_d, dff // bk),
        in_specs=[pl.BlockSpec((bm_d, bk), lambda i, j, k: (i, k)),
                  pl.BlockSpec((bk, bn_d), lambda i, j, k: (k, j))],
        out_specs=pl.BlockSpec((bm_d, bn_d), lambda i, j, k: (i, j)),
        out_shape=jax.ShapeDtypeStruct((m, dm), F32),
        compiler_params=_cparams("parallel", "parallel", "arbitrary"),
        name="ffn_down",
    )(a, wd)


def _matmul(x, w, *, bm, bn, out_dtype, name, n=None):
    m, k = x.shape
    n = w.shape[1] if n is None else n
    return pl.pallas_call(
        _mm_kernel,
        grid=(m // bm, n // bn),
        in_specs=[pl.BlockSpec((bm, k), lambda i, j: (i, 0)),
                  pl.BlockSpec((k, bn), lambda i, j: (0, j))],
        out_specs=pl.BlockSpec((bm, bn), lambda i, j: (i, j)),
        out_shape=jax.ShapeDtypeStruct((m, n), out_dtype),
        compiler_params=_cparams("parallel", "arbitrary"),
        name=name,
    )(x, w)


def kernel(x_prompt, x_sample, state_lru_h, state_conv, state_ssm_re, state_ssm_im, ffn1_pre_g, ffn1_post_g, ffn1_w_gate, ffn1_w_up, ffn1_w_down, mix_pre_g, mix_post_g, w_in, conv_w, conv_b, w_rg, b_rg, w_ig, b_ig, lru_lambda, ssm_a_re, ssm_a_im, ssm_log_dt, ssm_b_re, ssm_b_im, ssm_c_re, ssm_c_im, ssm_d, w_glu, b_glu, w_out_a, w_out_b, w_o, ffn2_pre_g, ffn2_post_g, ffn2_w_gate, ffn2_w_up, ffn2_w_down):
    nb, seq, dm = x_prompt.shape
    db, dseq, _ = x_sample.shape
    depth = state_lru_h.shape[0]
    assert depth == 1, "one decoder layer"
    d_lru = state_lru_h.shape[2]
    n_grp, ssm_p = state_ssm_re.shape[2], state_ssm_re.shape[3]
    d_ssm = ssm_d.shape[1]
    ssm_cg = d_ssm // n_grp
    d_in = w_in.shape[2]
    assert w_rg.shape[2] == LRU_BLK and conv_w.shape[1] == CONV_W
    assert d_in == 2 * d_lru + d_ssm + 2 * dm and seq >= CONV_W - 1 and dseq >= CONV_W - 1

    mp = nb * seq
    ms = db * dseq
    m = mp + ms
    sdt = state_lru_h.dtype

    row2 = lambda v: v.reshape(1, -1)
    bf = lambda v: v[0].astype(BF16)

    xp2 = x_prompt.reshape(mp, dm)
    xs2 = jnp.swapaxes(x_sample, 0, 1).reshape(ms, dm)

    tr = _pick(math.gcd(mp, ms), (256, 128, 64, 32, 16, 8))
    npt = mp // tr
    xp_spec = pl.BlockSpec((tr, dm), lambda i: (jnp.minimum(i, npt - 1), 0))
    xs_spec = pl.BlockSpec((tr, dm), lambda i: (jnp.maximum(i - npt, 0), 0))
    row_spec = pl.BlockSpec((tr, dm), lambda i: (i, 0))
    g_spec = pl.BlockSpec((1, dm), lambda i: (0, 0))

    h1 = pl.pallas_call(
        functools.partial(_prenorm_kernel, n_prompt_tiles=npt),
        grid=(m // tr,),
        in_specs=[xp_spec, xs_spec, g_spec],
        out_specs=row_spec,
        out_shape=jax.ShapeDtypeStruct((m, dm), BF16),
        compiler_params=_cparams("parallel"),
        name="prenorm1",
    )(xp2, xs2, row2(ffn1_pre_g))

    bm = _pick(m, (1536, 1024, 768, 512, 384, 256, 128))
    f1 = _half_ffn_matmuls(h1, ffn1_w_gate[0], ffn1_w_up[0], ffn1_w_down[0], bm=bm)

    x1, u = pl.pallas_call(
        functools.partial(_post_ffn1_kernel, n_prompt_tiles=npt),
        grid=(m // tr,),
        in_specs=[xp_spec, xs_spec, row_spec, g_spec, g_spec],
        out_specs=[row_spec, row_spec],
        out_shape=[jax.ShapeDtypeStruct((m, dm), F32), jax.ShapeDtypeStruct((m, dm), BF16)],
        compiler_params=_cparams("parallel"),
        name="post_ffn1",
    )(xp2, xs2, f1, row2(ffn1_post_g), row2(mix_pre_g))

    bn_in = _pick(math.gcd(d_lru, dm), (512, 256, 128))
    n_a = 2 * d_lru + d_ssm
    z = _matmul(u, w_in[0], bm=bm, bn=bn_in, out_dtype=F32, name="in_proj", n=n_a)

    cb = _pick(d_lru, (512, 256, 128))
    ncb = d_lru // cb
    hpb = cb // LRU_BLK
    wrg = bf(w_rg)
    wig = bf(w_ig)
    lru_params = (conv_w[0], row2(conv_b), wrg, row2(b_rg), wig, row2(b_ig), row2(lru_lambda))
    assert d_lru <= dm and d_ssm <= dm
    any_spec = pl.BlockSpec(memory_space=pl.ANY)
    assert mp % ms == 0, "sample rows must tile the unified row axis"
    s_blk = mp // ms

    n_lags = 2
    abr, abi, cfr, cfi = pl.pallas_call(
        _s5_disc_kernel,
        out_shape=[jax.ShapeDtypeStruct((n_grp, ssm_p), F32)] * 2
                  + [jax.ShapeDtypeStruct((n_lags, n_grp, ssm_p), F32)] * 2,
        name="s5_discretise",
    )(ssm_a_re[0], ssm_a_im[0], ssm_log_dt[0].reshape(n_grp, 1))
    bb_re = cfr[..., None] * ssm_b_re - cfi[..., None] * ssm_b_im
    bb_im = cfr[..., None] * ssm_b_im + cfi[..., None] * ssm_b_re

    gpb = LANES // ssm_cg
    nj = n_grp // gpb
    hw = gpb * ssm_p
    bb =jnp.stack([bb_re, bb_im]).reshape(2, n_lags, nj, gpb, ssm_p, ssm_cg).astype(BF16)
    bb = jnp.transpose(bb, (2, 1, 3, 5, 0, 4)).reshape(nj, n_lags * LANES, 2, ssm_p)
    assert LANES % ssm_p == 0 and gpb % (LANES // ssm_p) == 0
    hpl = LANES // ssm_p
    bb = jnp.concatenate([bb] * hpl, axis=3)
    wb = jnp.concatenate([bb[:, :, ri, :] for ri in range(2) for _ in range(gpb // hpl)], axis=2)
    wb_shape = (1, n_lags * LANES, 2 * hw)
    g_row = (lax.broadcasted_iota(jnp.int32, wb_shape, 1) // ssm_cg) % gpb
    h_col = (lax.broadcasted_iota(jnp.int32, wb_shape, 2) // ssm_p) % gpb
    wb = jnp.where(g_row == h_col, wb, jnp.zeros((), BF16))
    cc = jnp.stack([ssm_c_re[0], -ssm_c_im[0]]).reshape(2, nj, gpb, ssm_cg, ssm_p).astype(BF16)
    cc = jnp.transpose(cc, (1, 0, 4, 2, 3)).reshape(nj, 2, ssm_p, LANES)
    wc = jnp.concatenate([cc[:, ri, :, :] for ri in range(2) for _ in range(gpb)], axis=1)
    wc_shape = (1, 2 * hw, LANES)
    wc_mask = ((lax.broadcasted_iota(jnp.int32, wc_shape, 1) // ssm_p) % gpb
               == lax.broadcasted_iota(jnp.int32, wc_shape, 2) // ssm_cg)
    wc = jnp.where(wc_mask, wc, jnp.zeros((), BF16))
    abr3 = abr.reshape(nj, 1, hw)
    abi3 = abi.reshape(nj, 1, hw)
    xb_blk = (2 * d_lru) // LANES
    dsk = row2(ssm_d)

    n_par = 2 if nj % 2 == 0 and xb_blk % 2 == 0 else 1
    par_shape = (1, n_par * 2 * hw, n_par * LANES)
    par_mask = (lax.broadcasted_iota(jnp.int32, par_shape, 1) // (2 * hw)
                == lax.broadcasted_iota(jnp.int32, par_shape, 2) // LANES)
    wc_par = jnp.concatenate([wc.reshape(nj // n_par, n_par * 2 * hw, LANES)] * n_par, axis=2)
    wc_par = jnp.where(par_mask, wc_par, jnp.zeros((), BF16))
    n_gi = nj // n_par
    bm_gl = m // n_gi
    assert m % n_gi == 0 and bm_gl % (2 * SUBLANES) == 0

    def gl_cols_ok(t):
        bn = (2 * dm) // (nb * (seq // t)) if (2 * dm) % (nb * (seq // t)) == 0 else 0
        return bn > 0 and bn % LANES == 0 and n_a % bn == 0

    tm5 = _pick(seq, tuple(t for t in (512, 256, 128, 64, 32, 16, 8) if seq % t == 0 and gl_cols_ok(t)))
    rp5 = seq // tm5
    bn_gl = (2 * dm) // (nb * rp5)
    gl_blk = n_a // bn_gl
    gl, g_all, sre_p, sim_p = pl.pallas_call(
        _gates_s5_prompt_kernel,
        grid=(n_gi, nb, rp5),
        in_specs=[pl.BlockSpec((bm_gl, dm), lambda j, n, r: (j, 0)),
                  pl.BlockSpec((dm, bn_gl), lambda j, n, r: (0, gl_blk + n * rp5 + r)),
                  pl.BlockSpec((tm5, n_par * LANES), lambda j, n, r: (n * rp5 + r, xb_blk // n_par + j)),
                  pl.BlockSpec((n_par, n_lags * LANES, 2 * hw), lambda j, n, r: (j, 0, 0)),
                  pl.BlockSpec((None, n_par * 2 * hw, n_par * LANES), lambda j, n, r: (j, 0, 0)),
                  pl.BlockSpec((n_par, 1, hw), lambda j, n, r: (j, 0, 0)),
                  pl.BlockSpec((n_par, 1, hw), lambda j, n, r: (j, 0, 0)),
                  pl.BlockSpec((1, n_par * LANES), lambda j, n, r: (0, j)), any_spec],
        out_specs=[pl.BlockSpec((bm_gl, bn_gl), lambda j, n, r: (j, n * rp5 + r)),
                   pl.BlockSpec((tm5, n_par * LANES), lambda j, n, r: (n * rp5 + r, j)),
                   pl.BlockSpec((None, 1, n_par * hw), lambda j, n, r: (n, 0, j)),
                   pl.BlockSpec((None, 1, n_par * hw), lambda j, n, r: (n, 0, j))],
        out_shape=[jax.ShapeDtypeStruct((m, 2 * dm), F32),
                   jax.ShapeDtypeStruct((m, dm), F32),
                   jax.ShapeDtypeStruct((nb, 1, n_grp * ssm_p), F32),
                   jax.ShapeDtypeStruct((nb, 1, n_grp * ssm_p), F32)],
        scratch_shapes=[pltpu.VMEM((n_par, 1, hw), F32), pltpu.VMEM((n_par, 1, hw), F32)],
        input_output_aliases={8: 1},
        compiler_params=_cparams("arbitrary", "arbitrary", "arbitrary"),
        name="gates_s5_prompt",
    )(u, w_in[0], z, wb, wc_par, abr3, abi3, dsk, f1)

    s5_w_specs1 = [pl.BlockSpec((None, LANES, 2 * hw), lambda j: (j, 0, 0)),
                   pl.BlockSpec((None, 2 * hw, LANES), lambda j: (j, 0, 0)),
                   pl.BlockSpec((None, 1, hw), lambda j: (j, 0, 0)),
                   pl.BlockSpec((None, 1, hw), lambda j: (j, 0, 0)),
                   pl.BlockSpec((1, LANES), lambda j: (0, j))]
    g_all, sre_s, sim_s = pl.pallas_call(
        functools.partial(_s5_sample_kernel, n_seq=db, n_steps=dseq),
        grid=(nj,),
        in_specs=[pl.BlockSpec((ms, LANES), lambda j: (s_blk, xb_blk + j))] + s5_w_specs1
                 + [pl.BlockSpec((db, hw), lambda j: (0, j)), pl.BlockSpec((db, hw), lambda j: (0, j)), any_spec],
        out_specs=[pl.BlockSpec((ms, LANES), lambda j: (s_blk, j)),
                   pl.BlockSpec((db, hw), lambda j: (0, j)),
                   pl.BlockSpec((db, hw), lambda j: (0, j))],
        out_shape=[jax.ShapeDtypeStruct((m, dm), F32),
                   jax.ShapeDtypeStruct((db, n_grp * ssm_p), F32),
                   jax.ShapeDtypeStruct((db, n_grp * ssm_p), F32)],
        scratch_shapes=[pltpu.VMEM((ms, 2 * hw), F32), pltpu.VMEM((ms, 2 * hw), BF16)],
        input_output_aliases={8: 0},
        compiler_params=_cparams("parallel"),
        name="s5_sample",
    )(z, wb, wc, abr3, abi3, dsk,
      state_ssm_re[0].reshape(db, n_grp * ssm_p), state_ssm_im[0].reshape(db, n_grp * ssm_p), g_all)

    assert d_ssm // cb == ncb
    tm = _pick(seq, tuple(t for t in (512, 256, 128, 64, 32, 16, 8)
                          if seq % t == 0 and (m * t) % mp == 0 and ((m * t) // mp) % (2 * SUBLANES) == 0))
    rpt = seq // tm
    bm_u = (m * tm) // mp
    col = lambda shape: pl.BlockSpec(shape, lambda i, j: (0, j))
    head_spec = pl.BlockSpec((hpb, LRU_BLK, LRU_BLK), lambda i, j: (j, 0, 0))
    yb, ya, hl_p = pl.pallas_call(
        functools.partial(_glu_rglru_prompt_kernel, tiles_per_seq=rpt),
        grid=(nb * rpt, ncb),
        in_specs=[pl.BlockSpec((bm_u, d_ssm), lambda i, j: (i, 0)),
                  pl.BlockSpec((bm_u, cb), lambda i, j: (i, j)),
                  pl.BlockSpec((d_ssm, cb), lambda i, j: (0, j)),
                  col((1, cb)),
                  pl.BlockSpec((tm, cb), lambda i, j: (i, j)),
                  pl.BlockSpec((tm, cb), lambda i, j: (i, ncb + j)),
                  col((CONV_W, cb)), col((1, cb)), head_spec, col((1, cb)), head_spec, col((1, cb)),
                  col((1, cb)), any_spec],
        out_specs=[pl.BlockSpec((bm_u, cb), lambda i, j: (i, j)),
                   pl.BlockSpec((tm, cb), lambda i, j: (i, j)),
                   pl.BlockSpec((None, 1, cb), lambda i, j: (i // rpt, 0, j))],
        out_shape=[jax.ShapeDtypeStruct((m, d_ssm), BF16),
                   jax.ShapeDtypeStruct((m, dm), BF16),
                   jax.ShapeDtypeStruct((nb, 1, d_lru), F32)],
        scratch_shapes=[pltpu.VMEM((bm_u, d_ssm), BF16),
                        pltpu.VMEM((ncb, SUBLANES, cb), F32), pltpu.VMEM((ncb, 1, cb), F32)],
        input_output_aliases={4 + 2 + len(lru_params): 1},
        compiler_params=_cparams("arbitrary", "arbitrary"),
        name="glu_rglru_prompt",
    )(g_all, g_all, w_glu[0], row2(b_glu), z, z, *lru_params, h1)

    par1 = lambda shape: pl.BlockSpec(shape, lambda c: (0, c))
    cst = jnp.swapaxes(state_conv[0], 0, 1)
    ya, hl_s = pl.pallas_call(
        functools.partial(_rglru_sample_kernel, n_seq=db, n_steps=dseq),
        grid=(ncb,),
        in_specs=[pl.BlockSpec((ms, cb), lambda c: (s_blk, c)),
                  pl.BlockSpec((ms, cb), lambda c: (s_blk, ncb + c)),
                  pl.BlockSpec((CONV_W - 1, db, cb), lambda c: (0, 0, c)),
                  par1((db, cb)),
                  par1((CONV_W, cb)), par1((1, cb)),
                  pl.BlockSpec((hpb, LRU_BLK, LRU_BLK), lambda c: (c, 0, 0)), par1((1, cb)),
                  pl.BlockSpec((hpb, LRU_BLK, LRU_BLK), lambda c: (c, 0, 0)), par1((1, cb)),
                  par1((1, cb)), any_spec],
        out_specs=[pl.BlockSpec((ms, cb), lambda c: (s_blk, c)), par1((db, cb))],
        out_shape=[jax.ShapeDtypeStruct((m, dm), BF16), jax.ShapeDtypeStruct((db, d_lru), F32)],
        input_output_aliases={4 + len(lru_params): 0},
        compiler_params=_cparams("parallel"),
        name="rglru_sample",
    )(z, z, cst, state_lru_h[0], *lru_params, ya)

    bm_g = bm
    bn_m = bn_in
    gla_blk = 0
    glb_blk = dm // bn_m
    merged = pl.pallas_call(
        _merge_kernel,
        grid=(m // bm_g, dm // bn_m),
        in_specs=[pl.BlockSpec((bm_g, d_lru), lambda i, j: (i, 0), pipeline_mode=pl.Buffered(1)),
                  pl.BlockSpec((bm_g, d_ssm), lambda i, j: (i, 0), pipeline_mode=pl.Buffered(1)),
                  pl.BlockSpec((d_lru, bn_m), lambda i, j: (0, j)),
                  pl.BlockSpec((d_ssm, bn_m), lambda i, j: (0, j)),
                  pl.BlockSpec((bm_g, bn_m), lambda i, j: (i, gla_blk + j)),
                  pl.BlockSpec((bm_g, bn_m), lambda i, j: (i, glb_blk + j))],
        out_specs=pl.BlockSpec((bm_g, bn_m), lambda i, j: (i, j)),
        out_shape=jax.ShapeDtypeStruct((m, dm), BF16),
        compiler_params=_cparams("parallel", "arbitrary"),
        name="gated_merge",
    )(ya, yb, w_out_a[0], w_out_b[0], gl, gl)

    o = _matmul(merged, w_o[0], bm=bm, bn=bn_in, out_dtype=F32, name="o_proj")

    x2, h2 = pl.pallas_call(
        _post_mix_kernel,
        grid=(m // tr,),
        in_specs=[row_spec, row_spec, g_spec, g_spec],
        out_specs=[row_spec, row_spec],
        out_shape=[jax.ShapeDtypeStruct((m, dm), F32), jax.ShapeDtypeStruct((m, dm), BF16)],
        compiler_params=_cparams("parallel"),
        name="post_mix",
    )(x1, o, row2(mix_post_g), row2(ffn2_pre_g))

    f2 = _half_ffn_matmuls(h2, ffn2_w_gate[0], ffn2_w_up[0], ffn2_w_down[0], bm=bm)

    def final(rows, first_tile):
        return pl.pallas_call(
            _final_kernel,
            grid=(rows // tr,),
            in_specs=[pl.BlockSpec((tr, dm), lambda i: (first_tile + i, 0)),
                      pl.BlockSpec((tr, dm), lambda i: (first_tile + i, 0)),
                      g_spec],
            out_specs=row_spec,
            out_shape=jax.ShapeDtypeStruct((rows, dm), F32),
            compiler_params=_cparams("parallel"),
            name="final_residual",
        )(x2, f2, row2(ffn2_post_g))

    y_prompt = final(mp, 0).reshape(nb, seq, dm)
    y_sample = jnp.swapaxes(final(ms, npt).reshape(dseq, db, dm), 0, 1)

    nk = CONV_W - 1
    prompt_conv = jnp.stack([lax.slice(z, ((n + 1) * seq - nk, 0), ((n + 1) * seq, d_lru)) for n in range(nb)])
    sample_conv = jnp.swapaxes(lax.slice(z, (m - nk * db, 0), (m, d_lru)).reshape(nk, db, d_lru), 0, 1)
    st = lambda v, n: v.reshape(1, n, n_grp, ssm_p).astype(sdt)
    return (y_prompt, y_sample,
            hl_p.reshape(1, nb, d_lru).astype(sdt), prompt_conv[None].astype(sdt),
            st(sre_p, nb), st(sim_p, nb),
            hl_s.reshape(1, db, d_lru).astype(sdt), sample_conv[None].astype(sdt),
            st(sre_s, db), st(sim_s, db))
```

```python
import functools
import math

import jax
import jax.numpy as jnp
from jax import lax
from jax.experimental import pallas as pl
from jax.experimental.pallas import tpu as pltpu

F32 = jnp.float32
BF16 = jnp.bfloat16

EPS = 1e-6
C_RG = 8.0
CONV_W = 4
LANES = 128
SUBLANES = 8
LRU_BLK = 128
VMEM_LIMIT = 56 * 1024 * 1024


def _cparams(*sem):
    return pltpu.CompilerParams(dimension_semantics=sem, vmem_limit_bytes=VMEM_LIMIT)


def _pick(n, candidates):
    for c in candidates:
        if n % c == 0:
            return c
    raise ValueError(f"no tile in {candidates} divides {n}")


def _rms(x, g):
    return x * lax.rsqrt(jnp.mean(x * x, axis=-1, keepdims=True) + EPS) * g


def _softplus(x):
    return jnp.maximum(x, 0.0) + jnp.log1p(jnp.exp(-jnp.abs(x)))


def _two_group_rows(i, n_prompt_tiles, xp_ref, xs_ref, body):
    @pl.when(i < n_prompt_tiles)
    def _():
        body(xp_ref[...])

    @pl.when(i >= n_prompt_tiles)
    def _():
        body(xs_ref[...])


def _prenorm_kernel(xp_ref, xs_ref, g_ref, h_ref, *, n_prompt_tiles):
    def body(x):
        h_ref[...] = _rms(x, g_ref[...]).astype(h_ref.dtype)

    _two_group_rows(pl.program_id(0), n_prompt_tiles, xp_ref, xs_ref, body)


def _post_ffn1_kernel(xp_ref, xs_ref, f_ref, gpost_ref, gpre_ref, x1_ref, u_ref, *, n_prompt_tiles):
    def body(x):
        x1 = x + 0.5 * _rms(f_ref[...], gpost_ref[...])
        x1_ref[...] = x1
        u_ref[...] = _rms(x1, gpre_ref[...]).astype(u_ref.dtype)

    _two_group_rows(pl.program_id(0), n_prompt_tiles, xp_ref, xs_ref, body)


def _post_mix_kernel(x1_ref, o_ref, gpost_ref, gpre_ref, x2_ref, h_ref):
    x2 = x1_ref[...] + _rms(o_ref[...], gpost_ref[...])
    x2_ref[...] = x2
    h_ref[...] = _rms(x2, gpre_ref[...]).astype(h_ref.dtype)


def _final_kernel(x2_ref, f_ref, g_ref, y_ref):
    y_ref[...] = x2_ref[...] + 0.5 * _rms(f_ref[...], g_ref[...])


def _ffn_up_kernel(h_ref, wg_ref, wu_ref, wd_ref, a_ref, wd16_ref):
    wg = wg_ref[...].astype(BF16)
    wu = wu_ref[...].astype(BF16)
    half = h_ref.shape[0] // 2
    for rows in (slice(0, half), slice(half, None)):
        h = h_ref[rows, :]
        g = jnp.dot(h, wg, preferred_element_type=F32)
        u = jnp.dot(h, wu, preferred_element_type=F32)
        a_ref[rows, :] = (jax.nn.silu(g) * u).astype(a_ref.dtype)
    wd16_ref[...] = wd_ref[...].astype(wd16_ref.dtype)


def _ffn_down_kernel(a_ref, wd_ref, o_ref):
    p = jnp.dot(a_ref[...], wd_ref[...], preferred_element_type=F32)

    @pl.when(pl.program_id(2) == 0)
    def _():
        o_ref[...] = p

    @pl.when(pl.program_id(2) != 0)
    def _():
        o_ref[...] += p


def _mm_kernel(x_ref, w_ref, o_ref):
    w = w_ref[...].astype(BF16)
    o_ref[...] = jnp.dot(x_ref[...], w, preferred_element_type=F32).astype(o_ref.dtype)


def _merge_kernel(ya_ref, yb_ref, wa_ref, wb_ref, gla_ref, glb_ref, o_ref):
    wa = wa_ref[...].astype(BF16)
    wb = wb_ref[...].astype(BF16)
    half = o_ref.shape[0] // 2
    for rows in (slice(0, half), slice(half, None)):
        pa = jnp.dot(ya_ref[rows, :], wa, preferred_element_type=F32)
        pb = jnp.dot(yb_ref[rows, :], wb, preferred_element_type=F32)
        m = jax.nn.sigmoid(gla_ref[rows, :]) * pa + jax.nn.sigmoid(glb_ref[rows, :]) * pb
        o_ref[rows, :] = m.astype(o_ref.dtype)


def _lru_gates(xc, wrg_ref, brg_ref, wig_ref, big_ref, lam_ref):
    xcb = xc.astype(BF16)
    rs, gs = [], []
    for hh in range(wrg_ref.shape[0]):
        xh = xcb[:, hh * LRU_BLK:(hh + 1) * LRU_BLK]
        rs.append(jnp.dot(xh, wrg_ref[hh], preferred_element_type=F32))
        gs.append(jnp.dot(xh, wig_ref[hh], preferred_element_type=F32))
    r = jax.nn.sigmoid(jnp.concatenate(rs, axis=1) + brg_ref[...])
    i = jax.nn.sigmoid(jnp.concatenate(gs, axis=1) + big_ref[...])
    log_a = -C_RG * r * _softplus(-lam_ref[...])
    a = jnp.exp(log_a)
    mult = jnp.sqrt(-jnp.tanh(log_a) * (a * a + 1.0))
    return a, mult * (i * xc)


def _glu_rglru_prompt_kernel(g_ref, gcol_ref, w_ref, b_ref,
                             xa_ref, ga_ref, cw_ref, cb_ref, wrg_ref, brg_ref, wig_ref, big_ref, lam_ref,
                             ya_all_ref, o_ref, ya_ref, hl_ref, g_scr, xcars, hcars, *, tiles_per_seq):
    del ya_all_ref
    i, j = pl.program_id(0), pl.program_id(1)

    @pl.when(j == 0)
    def _():
        g_scr[...] = g_ref[...].astype(g_scr.dtype)

    _rglru_prompt_tile(xa_ref, ga_ref, cw_ref, cb_ref, wrg_ref, brg_ref, wig_ref, big_ref, lam_ref,
                       ya_ref, hl_ref, xcars.at[j], hcars.at[j], i % tiles_per_seq == 0)
    s = jnp.dot(g_scr[...], w_ref[...].astype(BF16), preferred_element_type=F32) + b_ref[...]
    o_ref[...] = (gcol_ref[...] * jax.nn.sigmoid(s)).astype(o_ref.dtype)


def _rglru_prompt_tile(xa_ref, ga_ref, cw_ref, cb_ref, wrg_ref, brg_ref, wig_ref, big_ref, lam_ref,
                       ya_ref, hl_ref, xcar, hcar, starts_sequence):
    tm = xa_ref.shape[0]

    @pl.when(starts_sequence)
    def _():
        xcar[...] = jnp.zeros_like(xcar)
        hcar[...] = jnp.zeros_like(hcar)

    x = xa_ref[...]
    xfull = jnp.concatenate([xcar[...], x], axis=0)
    xcar[...] = x[tm - SUBLANES:, :]
    cw = cw_ref[...]
    xc = cb_ref[...]
    for k in range(CONV_W - 1):
        xs = pltpu.roll(xfull, CONV_W - 1 - k, 0)[SUBLANES:, :]
        xc = xc + xs * cw[k:k + 1, :]
    xc = xc + x * cw[CONV_W - 1:CONV_W, :]

    a, b = _lru_gates(xc, wrg_ref, brg_ref, wig_ref, big_ref, lam_ref)
    cb = a.shape[1]
    n_slab = tm // SUBLANES
    a = a.reshape(n_slab, SUBLANES, cb)
    b = b.reshape(n_slab, SUBLANES, cb)
    sub = lax.broadcasted_iota(jnp.int32, (1, SUBLANES, cb), 1)
    d = 1
    while d < SUBLANES:
        m = sub >= d
        a_s = jnp.where(m, pltpu.roll(a, d, 1), 1.0)
        b_s = jnp.where(m, pltpu.roll(b, d, 1), 0.0)
        b = a * b_s + b
        a = a * a_s
        d *= 2
    h = hcar[...]
    hs = []
    for s in range(n_slab):
        h_slab = b[s] + a[s] * jnp.broadcast_to(h, (SUBLANES, cb))
        h = h_slab[SUBLANES - 1:SUBLANES, :]
        hs.append(h_slab)
    hcar[...] = h
    hl_ref[...] = h
    ya_ref[...] = (jnp.concatenate(hs, axis=0) * jax.nn.gelu(ga_ref[...])).astype(ya_ref.dtype)


def _rglru_sample_kernel(xa_ref, ga_ref, cst_ref, h0_ref, cw_ref, cb_ref, wrg_ref, brg_ref, wig_ref,
                         big_ref, lam_ref, ya_all_ref, ya_ref, hl_ref, *, n_seq, n_steps):
    del ya_all_ref
    x = xa_ref[...]
    xp = jnp.concatenate([cst_ref[k] for k in range(CONV_W - 1)] + [x], axis=0)
    rows = n_seq * n_steps
    cw = cw_ref[...]
    xc = cb_ref[...]
    for k in range(CONV_W):
        xc = xc + xp[k * n_seq:k * n_seq + rows, :] * cw[k:k + 1, :]
    a, b = _lru_gates(xc, wrg_ref, brg_ref, wig_ref, big_ref, lam_ref)
    h = h0_ref[...]
    for t in range(n_steps):
        sl = slice(t * n_seq, (t + 1) * n_seq)
        h = a[sl, :] * h + b[sl, :]
        ya_ref[sl, :] = (h * jax.nn.gelu(ga_ref[sl, :])).astype(ya_ref.dtype)
    hl_ref[...] = h


def _s5_disc_kernel(are_ref, aim_ref, ldt_ref, abr_ref, abi_ref, cfr_ref, cfi_ref):
    a_re = are_ref[...]
    a_im = aim_ref[...]
    dt = jnp.exp(ldt_ref[...])
    mag = jnp.exp(a_re * dt)
    abr = mag * jnp.cos(a_im * dt)
    abi = mag * jnp.sin(a_im * dt)
    den = a_re * a_re + a_im * a_im
    nr = abr - 1.0
    abr_ref[...] = abr
    abi_ref[...] = abi
    cr = (nr * a_re + abi * a_im) / den
    ci = (abi * a_re - nr * a_im) / den
    for q in range(cfr_ref.shape[0]):
        cfr_ref[q] = cr
        cfi_ref[q] = ci
        cr, ci = abr * cr - abi * ci, abr * ci + abi * cr


def _gates_s5_prompt_kernel(u_ref, w_ref, xb_ref, wb_ref, wc_ref, ar_ref, ai_ref, d_ref, g_all_ref,
                            gl_ref, g_ref, sre_ref, sim_ref, cre, cim):
    del g_all_ref
    n_par, _, hw = ar_ref.shape

    @pl.when(pl.program_id(2) == 0)
    def _():
        cre[...] = jnp.zeros_like(cre)
        cim[...] = jnp.zeros_like(cim)

    split = (gl_ref.shape[0] * 9 // 16) // (2 * SUBLANES) * (2 * SUBLANES)
    w = w_ref[...].astype(BF16)
    bus = [_s5_project_in(xb_ref[:, p * LANES:(p + 1) * LANES], wb_ref.at[p]) for p in range(n_par)]
    gl_ref[:split, :] = jnp.dot(u_ref[:split, :], w, preferred_element_type=F32)
    scans = [_s5_slab_scan(bus[p], ar_ref[p], ai_ref[p], cre[p], cim[p]) for p in range(n_par)]
    y = jnp.dot(jnp.concatenate([s[0] for s in scans], axis=1), wc_ref[...], preferred_element_type=F32)
    gl_ref[split:, :] = jnp.dot(u_ref[split:, :], w, preferred_element_type=F32)
    g_ref[...] = jax.nn.gelu(y + d_ref[...] * xb_ref[...])
    for p in range(n_par):
        _, cr, ci = scans[p]
        cre[p] = cr
        cim[p] = ci
        sre_ref[:, p * hw:(p + 1) * hw] = cr
        sim_ref[:, p * hw:(p + 1) * hw] = ci


def _s5_project_in(xb, wb_ref):
    tm, width = xb.shape
    n_slab = tm // SUBLANES
    n_lags = wb_ref.shape[0] // width
    xb3 = xb.reshape(n_slab, SUBLANES, width)
    subx = lax.broadcasted_iota(jnp.int32, (1, SUBLANES, width), 1)
    lagged = [xb3] + [jnp.where(subx >= q, pltpu.roll(xb3, q, 1), 0.0) for q in range(1, n_lags)]
    xs = jnp.concatenate(lagged, axis=2).reshape(tm, n_lags * width)
    return jnp.dot(xs.astype(BF16), wb_ref[...], preferred_element_type=F32), n_lags


def _s5_slab_scan(bu_lags, ar, ai, cr, ci):
    bu, n_lags = bu_lags
    tm = bu.shape[0]
    hw = ar.shape[1]
    n_slab = tm // SUBLANES
    sub = lax.broadcasted_iota(jnp.int32, (SUBLANES, hw), 0)
    pr, pi = ar, ai
    tr = jnp.where(sub == 0, pr, 0.0)
    ti = jnp.where(sub == 0, pi, 0.0)
    steps = []
    d = 1
    while d < SUBLANES:
        mr = jnp.where(sub >= d, pr, 0.0)
        mi = jnp.where(sub >= d, pi, 0.0)
        if d >= n_lags:
            steps.append((d, mr, mi))
        sr = pltpu.roll(tr, d, 0)
        si = pltpu.roll(ti, d, 0)
        tr, ti = tr + (mr * sr - mi * si), ti + (mr * si + mi * sr)
        pr, pi = pr * pr - pi * pi, 2.0 * (pr * pi)
        d *= 2
    slabs = []
    for s in range(n_slab):
        rows = slice(s * SUBLANES, (s + 1) * SUBLANES)
        hr = bu[rows, :hw]
        hi = bu[rows, hw:]
        for d, mr, mi in steps:
            sr = pltpu.roll(hr, d, 0)
            si = pltpu.roll(hi, d, 0)
            hr, hi = hr + (mr * sr - mi * si), hi + (mr * si + mi * sr)
        cbr = jnp.broadcast_to(cr, (SUBLANES, hw))
        cbi = jnp.broadcast_to(ci, (SUBLANES, hw))
        hr, hi = hr + (tr * cbr - ti * cbi), hi + (tr * cbi + ti * cbr)
        cr = hr[SUBLANES - 1:SUBLANES, :]
        ci = hi[SUBLANES - 1:SUBLANES, :]
        slabs.append(jnp.concatenate([hr, hi], axis=1))
    return jnp.concatenate(slabs, axis=0).astype(BF16), cr, ci


def _s5_sample_kernel(xb_ref, wb_ref, wc_ref, ar_ref, ai_ref, d_ref, s0r_ref, s0i_ref, g_all_ref,
                      g_ref, sre_ref, sim_ref, bu_scr, h_scr, *, n_seq, n_steps):
    del g_all_ref
    hw = ar_ref.shape[1]
    xb = xb_ref[...]
    bu_scr[...] = jnp.dot(xb.astype(BF16), wb_ref[...], preferred_element_type=F32)
    ar = ar_ref[...]
    ai = ai_ref[...]
    hr = s0r_ref[...]
    hi = s0i_ref[...]
    for t in range(n_steps):
        sl = slice(t * n_seq, (t + 1) * n_seq)
        hr, hi = (ar * hr - ai * hi) + bu_scr[sl, :hw], (ar * hi + ai * hr) + bu_scr[sl, hw:]
        h_scr[sl, :hw] = hr.astype(h_scr.dtype)
        h_scr[sl, hw:] = hi.astype(h_scr.dtype)
    sre_ref[...] = hr
    sim_ref[...] = hi
    y = jnp.dot(h_scr[...], wc_ref[...], preferred_element_type=F32)
    g_ref[...] = jax.nn.gelu(y + d_ref[...] * xb)


def _half_ffn_matmuls(h, wg, wu, wd, *, bm):
    m, dm = h.shape
    dff = wg.shape[1]
    bn_up = _pick(dff, (256, 128))
    n_i, n_j = m // bm, dff // bn_up
    rb = LANES
    n_rb = dff // rb
    assert n_i * n_j >= n_rb, "not enough up-projection steps to cover the down-projection weight"
    wd_blk = lambda i, j: (jnp.minimum(i * n_j + j, n_rb - 1), 0)
    a, wd = pl.pallas_call(
        _ffn_up_kernel,
        grid=(n_i, n_j),
        in_specs=[pl.BlockSpec((bm, dm), lambda i, j: (i, 0)),
                  pl.BlockSpec((dm, bn_up), lambda i, j: (0, j)),
                  pl.BlockSpec((dm, bn_up), lambda i, j: (0, j)),
                  pl.BlockSpec((rb, dm), wd_blk)],
        out_specs=[pl.BlockSpec((bm, bn_up), lambda i, j: (i, j)),
                   pl.BlockSpec((rb, dm), wd_blk)],
        out_shape=[jax.ShapeDtypeStruct((m, dff), BF16), jax.ShapeDtypeStruct((dff, dm), BF16)],
        compiler_params=_cparams("arbitrary", "arbitrary"),
        name="ffn_up",
    )(h, wg, wu, wd)

    bm_d = _pick(m, (1024, 512, 256, 128))
    bn_d = _pick(dm, (1024, 512, 256, 128))
    bk = dff // 2 if (dff // 2) % LANES == 0 else dff
    return pl.pallas_call(
        _ffn_down_kernel,
        grid=(m // bm_d, dm // bn_d, dff // bk),
        in_specs=[pl.BlockSpec((bm_d, bk), lambda i, j, k: (i, k)),
                  pl.BlockSpec((bk, bn_d), lambda i, j, k: (k, j))],
        out_specs=pl.BlockSpec((bm_d, bn_d), lambda i, j, k: (i, j)),
        out_shape=jax.ShapeDtypeStruct((m, dm), F32),
        compiler_params=_cparams("parallel", "parallel", "arbitrary"),
        name="ffn_down",
    )(a, wd)


def _matmul(x, w, *, bm, bn, out_dtype, name, n=None):
    m, k = x.shape
    n = w.shape[1] if n is None else n
    return pl.pallas_call(
        _mm_kernel,
        grid=(m // bm, n // bn),
        in_specs=[pl.BlockSpec((bm, k), lambda i, j: (i, 0)),
                  pl.BlockSpec((k, bn), lambda i, j: (0, j))],
        out_specs=pl.BlockSpec((bm, bn), lambda i, j: (i, j)),
        out_shape=jax.ShapeDtypeStruct((m, n), out_dtype),
        compiler_params=_cparams("parallel", "arbitrary"),
        name=name,
    )(x, w)


def kernel(x_prompt, x_sample, state_lru_h, state_conv, state_ssm_re, state_ssm_im, ffn1_pre_g, ffn1_post_g, ffn1_w_gate, ffn1_w_up, ffn1_w_down, mix_pre_g, mix_post_g, w_in, conv_w, conv_b, w_rg, b_rg, w_ig, b_ig, lru_lambda, ssm_a_re, ssm_a_im, ssm_log_dt, ssm_b_re, ssm_b_im, ssm_c_re, ssm_c_im, ssm_d, w_glu, b_glu, w_out_a, w_out_b, w_o, ffn2_pre_g, ffn2_post_g, ffn2_w_gate, ffn2_w_up, ffn2_w_down):
    nb, seq, dm = x_prompt.shape
    db, dseq, _ = x_sample.shape
    depth = state_lru_h.shape[0]
    assert depth == 1, "one decoder layer"
    d_lru = state_lru_h.shape[2]
    n_grp, ssm_p = state_ssm_re.shape[2], state_ssm_re.shape[3]
    d_ssm = ssm_d.shape[1]
    ssm_cg = d_ssm // n_grp
    d_in = w_in.shape[2]
    assert w_rg.shape[2] == LRU_BLK and conv_w.shape[1] == CONV_W
    assert d_in == 2 * d_lru + d_ssm + 2 * dm and seq >= CONV_W - 1 and dseq >= CONV_W - 1

    mp = nb * seq
    ms = db * dseq
    m = mp + ms
    sdt = state_lru_h.dtype

    row2 = lambda v: v.reshape(1, -1)
    bf = lambda v: v[0].astype(BF16)

    xp2 = x_prompt.reshape(mp, dm)
    xs2 = jnp.swapaxes(x_sample, 0, 1).reshape(ms, dm)

    tr = _pick(math.gcd(mp, ms), (256, 128, 64, 32, 16, 8))
    npt = mp // tr
    xp_spec = pl.BlockSpec((tr, dm), lambda i: (jnp.minimum(i, npt - 1), 0))
    xs_spec = pl.BlockSpec((tr, dm), lambda i: (jnp.maximum(i - npt, 0), 0))
    row_spec = pl.BlockSpec((tr, dm), lambda i: (i, 0))
    g_spec = pl.BlockSpec((1, dm), lambda i: (0, 0))

    h1 = pl.pallas_call(
        functools.partial(_prenorm_kernel, n_prompt_tiles=npt),
        grid=(m // tr,),
        in_specs=[xp_spec, xs_spec, g_spec],
        out_specs=row_spec,
        out_shape=jax.ShapeDtypeStruct((m, dm), BF16),
        compiler_params=_cparams("parallel"),
        name="prenorm1",
    )(xp2, xs2, row2(ffn1_pre_g))

    bm = _pick(m, (1536, 1024, 768, 512, 384, 256, 128))
    f1 = _half_ffn_matmuls(h1, ffn1_w_gate[0], ffn1_w_up[0], ffn1_w_down[0], bm=bm)

    x1, u = pl.pallas_call(
        functools.partial(_post_ffn1_kernel, n_prompt_tiles=npt),
        grid=(m // tr,),
        in_specs=[xp_spec, xs_spec, row_spec, g_spec, g_spec],
        out_specs=[row_spec, row_spec],
        out_shape=[jax.ShapeDtypeStruct((m, dm), F32), jax.ShapeDtypeStruct((m, dm), BF16)],
        compiler_params=_cparams("parallel"),
        name="post_ffn1",
    )(xp2, xs2, f1, row2(ffn1_post_g), row2(mix_pre_g))

    bn_in = _pick(math.gcd(d_lru, dm), (512, 256, 128))
    n_a = 2 * d_lru + d_ssm
    z = _matmul(u, w_in[0], bm=bm, bn=bn_in, out_dtype=F32, name="in_proj", n=n_a)

    cb = _pick(d_lru, (512, 256, 128))
    ncb = d_lru // cb
    hpb = cb // LRU_BLK
    wrg = bf(w_rg)
    wig = bf(w_ig)
    lru_params = (conv_w[0], row2(conv_b), wrg, row2(b_rg), wig, row2(b_ig), row2(lru_lambda))
    assert d_lru <= dm and d_ssm <= dm
    any_spec = pl.BlockSpec(memory_space=pl.ANY)
    assert mp % ms == 0, "sample rows must tile the unified row axis"
    s_blk = mp // ms

    n_lags = 2
    abr, abi, cfr, cfi = pl.pallas_call(
        _s5_disc_kernel,
        out_shape=[jax.ShapeDtypeStruct((n_grp, ssm_p), F32)] * 2
                  + [jax.ShapeDtypeStruct((n_lags, n_grp, ssm_p), F32)] * 2,
        name="s5_discretise",
    )(ssm_a_re[0], ssm_a_im[0], ssm_log_dt[0].reshape(n_grp, 1))
    bb_re = cfr[..., None] * ssm_b_re - cfi[..., None] * ssm_b_im
    bb_im = cfr[..., None] * ssm_b_im + cfi[..., None] * ssm_b_re

    gpb = LANES // ssm_cg
    nj = n_grp // gpb
    hw = gpb * ssm_p
    bb =jnp.stack([bb_re, bb_im]).reshape(2, n_lags, nj, gpb, ssm_p, ssm_cg).astype(BF16)
    bb = jnp.transpose(bb, (2, 1, 3, 5, 0, 4)).reshape(nj, n_lags * LANES, 2, ssm_p)
    assert LANES % ssm_p == 0 and gpb % (LANES // ssm_p) == 0
    hpl = LANES // ssm_p
    bb = jnp.concatenate([bb] * hpl, axis=3)
    wb = jnp.concatenate([bb[:, :, ri, :] for ri in range(2) for _ in range(gpb // hpl)], axis=2)
    wb_shape = (1, n_lags * LANES, 2 * hw)
    g_row = (lax.broadcasted_iota(jnp.int32, wb_shape, 1) // ssm_cg) % gpb
    h_col = (lax.broadcasted_iota(jnp.int32, wb_shape, 2) // ssm_p) % gpb
    wb = jnp.where(g_row == h_col, wb, jnp.zeros((), BF16))
    cc = jnp.stack([ssm_c_re[0], -ssm_c_im[0]]).reshape(2, nj, gpb, ssm_cg, ssm_p).astype(BF16)
    cc = jnp.transpose(cc, (1, 0, 4, 2, 3)).reshape(nj, 2, ssm_p, LANES)
    wc = jnp.concatenate([cc[:, ri, :, :] for ri in range(2) for _ in range(gpb)], axis=1)
    wc_shape = (1, 2 * hw, LANES)
    wc_mask = ((lax.broadcasted_iota(jnp.int32, wc_shape, 1) // ssm_p) % gpb
               == lax.broadcasted_iota(jnp.int32, wc_shape, 2) // ssm_cg)
    wc = jnp.where(wc_mask, wc, jnp.zeros((), BF16))
    abr3 = abr.reshape(nj, 1, hw)
    abi3 = abi.reshape(nj, 1, hw)
    xb_blk = (2 * d_lru) // LANES
    dsk = row2(ssm_d)

    n_par = 2 if nj % 2 == 0 and xb_blk % 2 == 0 else 1
    par_shape = (1, n_par * 2 * hw, n_par * LANES)
    par_mask = (lax.broadcasted_iota(jnp.int32, par_shape, 1) // (2 * hw)
                == lax.broadcasted_iota(jnp.int32, par_shape, 2) // LANES)
    wc_par = jnp.concatenate([wc.reshape(nj // n_par, n_par * 2 * hw, LANES)] * n_par, axis=2)
    wc_par = jnp.where(par_mask, wc_par, jnp.zeros((), BF16))
    n_gi = nj // n_par
    bm_gl = m // n_gi
    assert m % n_gi == 0 and bm_gl % (2 * SUBLANES) == 0

    def gl_cols_ok(t):
        bn = (2 * dm) // (nb * (seq // t)) if (2 * dm) % (nb * (seq // t)) == 0 else 0
        return bn > 0 and bn % LANES == 0 and n_a % bn == 0

    tm5 = _pick(seq, tuple(t for t in (512, 256, 128, 64, 32, 16, 8) if seq % t == 0 and gl_cols_ok(t)))
    rp5 = seq // tm5
    bn_gl = (2 * dm) // (nb * rp5)
    gl_blk = n_a // bn_gl
    gl, g_all, sre_p, sim_p = pl.pallas_call(
        _gates_s5_prompt_kernel,
        grid=(n_gi, nb, rp5),
        in_specs=[pl.BlockSpec((bm_gl, dm), lambda j, n, r: (j, 0)),
                  pl.BlockSpec((dm, bn_gl), lambda j, n, r: (0, gl_blk + n * rp5 + r)),
                  pl.BlockSpec((tm5, n_par * LANES), lambda j, n, r: (n * rp5 + r, xb_blk // n_par + j)),
                  pl.BlockSpec((n_par, n_lags * LANES, 2 * hw), lambda j, n, r: (j, 0, 0)),
                  pl.BlockSpec((None, n_par * 2 * hw, n_par * LANES), lambda j, n, r: (j, 0, 0)),
                  pl.BlockSpec((n_par, 1, hw), lambda j, n, r: (j, 0, 0)),
                  pl.BlockSpec((n_par, 1, hw), lambda j, n, r: (j, 0, 0)),
                  pl.BlockSpec((1, n_par * LANES), lambda j, n, r: (0, j)), any_spec],
        out_specs=[pl.BlockSpec((bm_gl, bn_gl), lambda j, n, r: (j, n * rp5 + r)),
                   pl.BlockSpec((tm5, n_par * LANES), lambda j, n, r: (n * rp5 + r, j)),
                   pl.BlockSpec((None, 1, n_par * hw), lambda j, n, r: (n, 0, j)),
                   pl.BlockSpec((None, 1, n_par * hw), lambda j, n, r: (n, 0, j))],
        out_shape=[jax.ShapeDtypeStruct((m, 2 * dm), F32),
                   jax.ShapeDtypeStruct((m, dm), F32),
                   jax.ShapeDtypeStruct((nb, 1, n_grp * ssm_p), F32),
                   jax.ShapeDtypeStruct((nb, 1, n_grp * ssm_p), F32)],
        scratch_shapes=[pltpu.VMEM((n_par, 1, hw), F32), pltpu.VMEM((n_par, 1, hw), F32)],
        input_output_aliases={8: 1},
        compiler_params=_cparams("arbitrary", "arbitrary", "arbitrary"),
        name="gates_s5_prompt",
    )(u, w_in[0], z, wb, wc_par, abr3, abi3, dsk, f1)

    s5_w_specs1 = [pl.BlockSpec((None, LANES, 2 * hw), lambda j: (j, 0, 0)),
                   pl.BlockSpec((None, 2 * hw, LANES), lambda j: (j, 0, 0)),
                   pl.BlockSpec((None, 1, hw), lambda j: (j, 0, 0)),
                   pl.BlockSpec((None, 1, hw), lambda j: (j, 0, 0)),
                   pl.BlockSpec((1, LANES), lambda j: (0, j))]
    g_all, sre_s, sim_s = pl.pallas_call(
        functools.partial(_s5_sample_kernel, n_seq=db, n_steps=dseq),
        grid=(nj,),
        in_specs=[pl.BlockSpec((ms, LANES), lambda j: (s_blk, xb_blk + j))] + s5_w_specs1
                 + [pl.BlockSpec((db, hw), lambda j: (0, j)), pl.BlockSpec((db, hw), lambda j: (0, j)), any_spec],
        out_specs=[pl.BlockSpec((ms, LANES), lambda j: (s_blk, j)),
                   pl.BlockSpec((db, hw), lambda j: (0, j)),
                   pl.BlockSpec((db, hw), lambda j: (0, j))],
        out_shape=[jax.ShapeDtypeStruct((m, dm), F32),
                   jax.ShapeDtypeStruct((db, n_grp * ssm_p), F32),
                   jax.ShapeDtypeStruct((db, n_grp * ssm_p), F32)],
        scratch_shapes=[pltpu.VMEM((ms, 2 * hw), F32), pltpu.VMEM((ms, 2 * hw), BF16)],
        input_output_aliases={8: 0},
        compiler_params=_cparams("parallel"),
        name="s5_sample",
    )(z, wb, wc, abr3, abi3, dsk,
      state_ssm_re[0].reshape(db, n_grp * ssm_p), state_ssm_im[0].reshape(db, n_grp * ssm_p), g_all)

    assert d_ssm // cb == ncb
    tm = _pick(seq, tuple(t for t in (512, 256, 128, 64, 32, 16, 8)
                          if seq % t == 0 and (m * t) % mp == 0 and ((m * t) // mp) % (2 * SUBLANES) == 0))
    rpt = seq // tm
    bm_u = (m * tm) // mp
    col = lambda shape: pl.BlockSpec(shape, lambda i, j: (0, j))
    head_spec = pl.BlockSpec((hpb, LRU_BLK, LRU_BLK), lambda i, j: (j, 0, 0))
    yb, ya, hl_p = pl.pallas_call(
        functools.partial(_glu_rglru_prompt_kernel, tiles_per_seq=rpt),
        grid=(nb * rpt, ncb),
        in_specs=[pl.BlockSpec((bm_u, d_ssm), lambda i, j: (i, 0)),
                  pl.BlockSpec((bm_u, cb), lambda i, j: (i, j)),
                  pl.BlockSpec((d_ssm, cb), lambda i, j: (0, j)),
                  col((1, cb)),
                  pl.BlockSpec((tm, cb), lambda i, j: (i, j)),
                  pl.BlockSpec((tm, cb), lambda i, j: (i, ncb + j)),
                  col((CONV_W, cb)), col((1, cb)), head_spec, col((1, cb)), head_spec, col((1, cb)),
                  col((1, cb)), any_spec],
        out_specs=[pl.BlockSpec((bm_u, cb), lambda i, j: (i, j)),
                   pl.BlockSpec((tm, cb), lambda i, j: (i, j)),
                   pl.BlockSpec((None, 1, cb), lambda i, j: (i // rpt, 0, j))],
        out_shape=[jax.ShapeDtypeStruct((m, d_ssm), BF16),
                   jax.ShapeDtypeStruct((m, dm), BF16),
                   jax.ShapeDtypeStruct((nb, 1, d_lru), F32)],
        scratch_shapes=[pltpu.VMEM((bm_u, d_ssm), BF16),
                        pltpu.VMEM((ncb, SUBLANES, cb), F32), pltpu.VMEM((ncb, 1, cb), F32)],
        input_output_aliases={4 + 2 + len(lru_params): 1},
        compiler_params=_cparams("arbitrary", "arbitrary"),
        name="glu_rglru_prompt",
    )(g_all, g_all, w_glu[0], row2(b_glu), z, z, *lru_params, h1)

    par1 = lambda shape: pl.BlockSpec(shape, lambda c: (0, c))
    cst = jnp.swapaxes(state_conv[0], 0, 1)
    ya, hl_s = pl.pallas_call(
        functools.partial(_rglru_sample_kernel, n_seq=db, n_steps=dseq),
        grid=(ncb,),
        in_specs=[pl.BlockSpec((ms, cb), lambda c: (s_blk, c)),
                  pl.BlockSpec((ms, cb), lambda c: (s_blk, ncb + c)),
                  pl.BlockSpec((CONV_W - 1, db, cb), lambda c: (0, 0, c)),
                  par1((db, cb)),
                  par1((CONV_W, cb)), par1((1, cb)),
                  pl.BlockSpec((hpb, LRU_BLK, LRU_BLK), lambda c: (c, 0, 0)), par1((1, cb)),
                  pl.BlockSpec((hpb, LRU_BLK, LRU_BLK), lambda c: (c, 0, 0)), par1((1, cb)),
                  par1((1, cb)), any_spec],
        out_specs=[pl.BlockSpec((ms, cb), lambda c: (s_blk, c)), par1((db, cb))],
        out_shape=[jax.ShapeDtypeStruct((m, dm), BF16), jax.ShapeDtypeStruct((db, d_lru), F32)],
        input_output_aliases={4 + len(lru_params): 0},
        compiler_params=_cparams("parallel"),
        name="rglru_sample",
    )(z, z, cst, state_lru_h[0], *lru_params, ya)

    bm_g = _pick(m, (1024, 512, 256, 128))
    bn_m = bn_in
    gla_blk = 0
    glb_blk = dm // bn_m
    merged = pl.pallas_call(
        _merge_kernel,
        grid=(m // bm_g, dm // bn_m),
        in_specs=[pl.BlockSpec((bm_g, d_lru), lambda i, j: (i, 0)),
                  pl.BlockSpec((bm_g, d_ssm), lambda i, j: (i, 0)),
                  pl.BlockSpec((d_lru, bn_m), lambda i, j: (0, j)),
                  pl.BlockSpec((d_ssm, bn_m), lambda i, j: (0, j)),
                  pl.BlockSpec((bm_g, bn_m), lambda i, j: (i, gla_blk + j)),
                  pl.BlockSpec((bm_g, bn_m), lambda i, j: (i, glb_blk + j))],
        out_specs=pl.BlockSpec((bm_g, bn_m), lambda i, j: (i, j)),
        out_shape=jax.ShapeDtypeStruct((m, dm), BF16),
        compiler_params=_cparams("parallel", "arbitrary"),
        name="gated_merge",
    )(ya, yb, w_out_a[0], w_out_b[0], gl, gl)

    o = _matmul(merged, w_o[0], bm=bm, bn=bn_in, out_dtype=F32, name="o_proj")

    x2, h2 = pl.pallas_call(
        _post_mix_kernel,
        grid=(m // tr,),
        in_specs=[row_spec, row_spec, g_spec, g_spec],
        out_specs=[row_spec, row_spec],
        out_shape=[jax.ShapeDtypeStruct((m, dm), F32), jax.ShapeDtypeStruct((m, dm), BF16)],
        compiler_params=_cparams("parallel"),
        name="post_mix",
    )(x1, o, row2(mix_post_g), row2(ffn2_pre_g))

    f2 = _half_ffn_matmuls(h2, ffn2_w_gate[0], ffn2_w_up[0], ffn2_w_down[0], bm=bm)

    def final(rows, first_tile):
        return pl.pallas_call(
            _final_kernel,
            grid=(rows // tr,),
            in_specs=[pl.BlockSpec((tr, dm), lambda i: (first_tile + i, 0)),
                      pl.BlockSpec((tr, dm), lambda i: (first_tile + i, 0)),
                      g_spec],
            out_specs=row_spec,
            out_shape=jax.ShapeDtypeStruct((rows, dm), F32),
            compiler_params=_cparams("parallel"),
            name="final_residual",
        )(x2, f2, row2(ffn2_post_g))

    y_prompt = final(mp, 0).reshape(nb, seq, dm)
    y_sample = jnp.swapaxes(final(ms, npt).reshape(dseq, db, dm), 0, 1)

    nk = CONV_W - 1
    prompt_conv = jnp.stack([lax.slice(z, ((n + 1) * seq - nk, 0), ((n + 1) * seq, d_lru)) for n in range(nb)])
    sample_conv = jnp.swapaxes(lax.slice(z, (m - nk * db, 0), (m, d_lru)).reshape(nk, db, d_lru), 0, 1)
    st = lambda v, n: v.reshape(1, n, n_grp, ssm_p).astype(sdt)
    return (y_prompt, y_sample,
            hl_p.reshape(1, nb, d_lru).astype(sdt), prompt_conv[None].astype(sdt),
            st(sre_p, nb), st(sim_p, nb),
            hl_s.reshape(1, db, d_lru).astype(sdt), sample_conv[None].astype(sdt),
            st(sre_s, db), st(sim_s, db))
```

```python
import functools
import math

import jax
import jax.numpy as jnp
from jax import lax
from jax.experimental import pallas as pl
from jax.experimental.pallas import tpu as pltpu

F32 = jnp.float32
BF16 = jnp.bfloat16

EPS = 1e-6
C_RG = 8.0
CONV_W = 4
LANES = 128
SUBLANES = 8
LRU_BLK = 128
VMEM_LIMIT = 56 * 1024 * 1024


def _cparams(*sem):
    return pltpu.CompilerParams(dimension_semantics=sem, vmem_limit_bytes=VMEM_LIMIT)


def _pick(n, candidates):
    for c in candidates:
        if n % c == 0:
            return c
    raise ValueError(f"no tile in {candidates} divides {n}")


def _rms(x, g):
    return x * lax.rsqrt(jnp.mean(x * x, axis=-1, keepdims=True) + EPS) * g


def _softplus(x):
    return jnp.maximum(x, 0.0) + jnp.log1p(jnp.exp(-jnp.abs(x)))


def _two_group_rows(i, n_prompt_tiles, xp_ref, xs_ref, body):
    @pl.when(i < n_prompt_tiles)
    def _():
        body(xp_ref[...])

    @pl.when(i >= n_prompt_tiles)
    def _():
        body(xs_ref[...])


def _prenorm_kernel(xp_ref, xs_ref, g_ref, h_ref, *, n_prompt_tiles):
    def body(x):
        h_ref[...] = _rms(x, g_ref[...]).astype(h_ref.dtype)

    _two_group_rows(pl.program_id(0), n_prompt_tiles, xp_ref, xs_ref, body)


def _post_ffn1_kernel(xp_ref, xs_ref, f_ref, gpost_ref, gpre_ref, x1_ref, u_ref, *, n_prompt_tiles):
    def body(x):
        x1 = x + 0.5 * _rms(f_ref[...], gpost_ref[...])
        x1_ref[...] = x1
        u_ref[...] = _rms(x1, gpre_ref[...]).astype(u_ref.dtype)

    _two_group_rows(pl.program_id(0), n_prompt_tiles, xp_ref, xs_ref, body)


def _post_mix_kernel(x1_ref, o_ref, gpost_ref, gpre_ref, x2_ref, h_ref):
    x2 = x1_ref[...] + _rms(o_ref[...], gpost_ref[...])
    x2_ref[...] = x2
    h_ref[...] = _rms(x2, gpre_ref[...]).astype(h_ref.dtype)


def _final_kernel(x2_ref, f_ref, g_ref, y_ref):
    y_ref[...] = x2_ref[...] + 0.5 * _rms(f_ref[...], g_ref[...])


def _ffn_up_kernel(h_ref, wg_ref, wu_ref, wd_ref, a_ref, wd16_ref):
    wg = wg_ref[...].astype(BF16)
    wu = wu_ref[...].astype(BF16)
    half = h_ref.shape[0] // 2
    for rows in (slice(0, half), slice(half, None)):
        h = h_ref[rows, :]
        g = jnp.dot(h, wg, preferred_element_type=F32)
        u = jnp.dot(h, wu, preferred_element_type=F32)
        a_ref[rows, :] = (jax.nn.silu(g) * u).astype(a_ref.dtype)
    wd16_ref[...] = wd_ref[...].astype(wd16_ref.dtype)


def _ffn_down_kernel(a_ref, wd_ref, o_ref):
    p = jnp.dot(a_ref[...], wd_ref[...], preferred_element_type=F32)

    @pl.when(pl.program_id(2) == 0)
    def _():
        o_ref[...] = p

    @pl.when(pl.program_id(2) != 0)
    def _():
        o_ref[...] += p


def _mm_kernel(x_ref, w_ref, o_ref):
    w = w_ref[...].astype(BF16)
    o_ref[...] = jnp.dot(x_ref[...], w, preferred_element_type=F32).astype(o_ref.dtype)


def _merge_kernel(ya_ref, yb_ref, wa_ref, wb_ref, gla_ref, glb_ref, o_ref):
    wa = wa_ref[...].astype(BF16)
    wb = wb_ref[...].astype(BF16)
    half = o_ref.shape[0] // 2
    for rows in (slice(0, half), slice(half, None)):
        pa = jnp.dot(ya_ref[rows, :], wa, preferred_element_type=F32)
        pb = jnp.dot(yb_ref[rows, :], wb, preferred_element_type=F32)
        m = jax.nn.sigmoid(gla_ref[rows, :]) * pa + jax.nn.sigmoid(glb_ref[rows, :]) * pb
        o_ref[rows, :] = m.astype(o_ref.dtype)


def _lru_gates(xc, wrg_ref, brg_ref, wig_ref, big_ref, lam_ref):
    xcb = xc.astype(BF16)
    rs, gs = [], []
    for hh in range(wrg_ref.shape[0]):
        xh = xcb[:, hh * LRU_BLK:(hh + 1) * LRU_BLK]
        rs.append(jnp.dot(xh, wrg_ref[hh], preferred_element_type=F32))
        gs.append(jnp.dot(xh, wig_ref[hh], preferred_element_type=F32))
    r = jax.nn.sigmoid(jnp.concatenate(rs, axis=1) + brg_ref[...])
    i = jax.nn.sigmoid(jnp.concatenate(gs, axis=1) + big_ref[...])
    log_a = -C_RG * r * _softplus(-lam_ref[...])
    a = jnp.exp(log_a)
    mult = jnp.sqrt(-jnp.tanh(log_a) * (a * a + 1.0))
    return a, mult * (i * xc)


def _glu_rglru_prompt_kernel(g_ref, gcol_ref, w_ref, b_ref,
                             xa_ref, ga_ref, cw_ref, cb_ref, wrg_ref, brg_ref, wig_ref, big_ref, lam_ref,
                             ya_all_ref, o_ref, ya_ref, hl_ref, g_scr, xcars, hcars, *, tiles_per_seq):
    del ya_all_ref
    i, j = pl.program_id(0), pl.program_id(1)

    @pl.when(j == 0)
    def _():
        g_scr[...] = g_ref[...].astype(g_scr.dtype)

    _rglru_prompt_tile(xa_ref, ga_ref, cw_ref, cb_ref, wrg_ref, brg_ref, wig_ref, big_ref, lam_ref,
                       ya_ref, hl_ref, xcars.at[j], hcars.at[j], i % tiles_per_seq == 0)
    s = jnp.dot(g_scr[...], w_ref[...].astype(BF16), preferred_element_type=F32) + b_ref[...]
    o_ref[...] = (gcol_ref[...] * jax.nn.sigmoid(s)).astype(o_ref.dtype)


def _rglru_prompt_tile(xa_ref, ga_ref, cw_ref, cb_ref, wrg_ref, brg_ref, wig_ref, big_ref, lam_ref,
                       ya_ref, hl_ref, xcar, hcar, starts_sequence):
    tm = xa_ref.shape[0]

    @pl.when(starts_sequence)
    def _():
        xcar[...] = jnp.zeros_like(xcar)
        hcar[...] = jnp.zeros_like(hcar)

    x = xa_ref[...]
    xfull = jnp.concatenate([xcar[...], x], axis=0)
    xcar[...] = x[tm - SUBLANES:, :]
    cw = cw_ref[...]
    xc = cb_ref[...]
    for k in range(CONV_W - 1):
        xs = pltpu.roll(xfull, CONV_W - 1 - k, 0)[SUBLANES:, :]
        xc = xc + xs * cw[k:k + 1, :]
    xc = xc + x * cw[CONV_W - 1:CONV_W, :]

    a, b = _lru_gates(xc, wrg_ref, brg_ref, wig_ref, big_ref, lam_ref)
    cb = a.shape[1]
    n_slab = tm // SUBLANES
    a = a.reshape(n_slab, SUBLANES, cb)
    b = b.reshape(n_slab, SUBLANES, cb)
    sub = lax.broadcasted_iota(jnp.int32, (1, SUBLANES, cb), 1)
    d = 1
    while d < SUBLANES:
        m = sub >= d
        a_s = jnp.where(m, pltpu.roll(a, d, 1), 1.0)
        b_s = jnp.where(m, pltpu.roll(b, d, 1), 0.0)
        b = a * b_s + b
        a = a * a_s
        d *= 2
    h = hcar[...]
    hs = []
    for s in range(n_slab):
        h_slab = b[s] + a[s] * jnp.broadcast_to(h, (SUBLANES, cb))
        h = h_slab[SUBLANES - 1:SUBLANES, :]
        hs.append(h_slab)
    hcar[...] = h
    hl_ref[...] = h
    ya_ref[...] = (jnp.concatenate(hs, axis=0) * jax.nn.gelu(ga_ref[...])).astype(ya_ref.dtype)


def _rglru_sample_kernel(xa_ref, ga_ref, cst_ref, h0_ref, cw_ref, cb_ref, wrg_ref, brg_ref, wig_ref,
                         big_ref, lam_ref, ya_all_ref, ya_ref, hl_ref, *, n_seq, n_steps):
    del ya_all_ref
    x = xa_ref[...]
    xp = jnp.concatenate([cst_ref[k] for k in range(CONV_W - 1)] + [x], axis=0)
    rows = n_seq * n_steps
    cw = cw_ref[...]
    xc = cb_ref[...]
    for k in range(CONV_W):
        xc = xc + xp[k * n_seq:k * n_seq + rows, :] * cw[k:k + 1, :]
    a, b = _lru_gates(xc, wrg_ref, brg_ref, wig_ref, big_ref, lam_ref)
    h = h0_ref[...]
    for t in range(n_steps):
        sl = slice(t * n_seq, (t + 1) * n_seq)
        h = a[sl, :] * h + b[sl, :]
        ya_ref[sl, :] = (h * jax.nn.gelu(ga_ref[sl, :])).astype(ya_ref.dtype)
    hl_ref[...] = h


def _s5_disc_kernel(are_ref, aim_ref, ldt_ref, abr_ref, abi_ref, cfr_ref, cfi_ref):
    a_re = are_ref[...]
    a_im = aim_ref[...]
    dt = jnp.exp(ldt_ref[...])
    mag = jnp.exp(a_re * dt)
    abr = mag * jnp.cos(a_im * dt)
    abi = mag * jnp.sin(a_im * dt)
    den = a_re * a_re + a_im * a_im
    nr = abr - 1.0
    abr_ref[...] = abr
    abi_ref[...] = abi
    cr = (nr * a_re + abi * a_im) / den
    ci = (abi * a_re - nr * a_im) / den
    for q in range(cfr_ref.shape[0]):
        cfr_ref[q] = cr
        cfi_ref[q] = ci
        cr, ci = abr * cr - abi * ci, abr * ci + abi * cr


def _gates_s5_prompt_kernel(u_ref, w_ref, xb_ref, wb_ref, wc_ref, ar_ref, ai_ref, d_ref, g_all_ref,
                            gl_ref, g_ref, sre_ref, sim_ref, cre, cim):
    del g_all_ref
    n_par, _, hw = ar_ref.shape

    @pl.when(pl.program_id(2) == 0)
    def _():
        cre[...] = jnp.zeros_like(cre)
        cim[...] = jnp.zeros_like(cim)

    split = (gl_ref.shape[0] * 9 // 16) // (2 * SUBLANES) * (2 * SUBLANES)
    w = w_ref[...].astype(BF16)
    bus = [_s5_project_in(xb_ref[:, p * LANES:(p + 1) * LANES], wb_ref.at[p]) for p in range(n_par)]
    gl_ref[:split, :] = jnp.dot(u_ref[:split, :], w, preferred_element_type=F32)
    scans = [_s5_slab_scan(bus[p], ar_ref[p], ai_ref[p], cre[p], cim[p]) for p in range(n_par)]
    y = jnp.dot(jnp.concatenate([s[0] for s in scans], axis=1), wc_ref[...], preferred_element_type=F32)
    gl_ref[split:, :] = jnp.dot(u_ref[split:, :], w, preferred_element_type=F32)
    g_ref[...] = jax.nn.gelu(y + d_ref[...] * xb_ref[...])
    for p in range(n_par):
        _, cr, ci = scans[p]
        cre[p] = cr
        cim[p] = ci
        sre_ref[:, p * hw:(p + 1) * hw] = cr
        sim_ref[:, p * hw:(p + 1) * hw] = ci


def _s5_project_in(xb, wb_ref):
    tm, width = xb.shape
    n_slab = tm // SUBLANES
    n_lags = wb_ref.shape[0] // width
    xb3 = xb.reshape(n_slab, SUBLANES, width)
    subx = lax.broadcasted_iota(jnp.int32, (1, SUBLANES, width), 1)
    lagged = [xb3] + [jnp.where(subx >= q, pltpu.roll(xb3, q, 1), 0.0) for q in range(1, n_lags)]
    xs = jnp.concatenate(lagged, axis=2).reshape(tm, n_lags * width)
    return jnp.dot(xs.astype(BF16), wb_ref[...], preferred_element_type=F32), n_lags


def _s5_slab_scan(bu_lags, ar, ai, cr, ci):
    bu, n_lags = bu_lags
    tm = bu.shape[0]
    hw = ar.shape[1]
    n_slab = tm // SUBLANES
    sub = lax.broadcasted_iota(jnp.int32, (SUBLANES, hw), 0)
    pr, pi = ar, ai
    tr = jnp.where(sub == 0, pr, 0.0)
    ti = jnp.where(sub == 0, pi, 0.0)
    steps = []
    d = 1
    while d < SUBLANES:
        mr = jnp.where(sub >= d, pr, 0.0)
        mi = jnp.where(sub >= d, pi, 0.0)
        if d >= n_lags:
            steps.append((d, mr, mi))
        sr = pltpu.roll(tr, d, 0)
        si = pltpu.roll(ti, d, 0)
        tr, ti = tr + (mr * sr - mi * si), ti + (mr * si + mi * sr)
        pr, pi = pr * pr - pi * pi, 2.0 * (pr * pi)
        d *= 2
    slabs = []
    for s in range(n_slab):
        rows = slice(s * SUBLANES, (s + 1) * SUBLANES)
        hr = bu[rows, :hw]
        hi = bu[rows, hw:]
        for d, mr, mi in steps:
            sr = pltpu.roll(hr, d, 0)
            si = pltpu.roll(hi, d, 0)
            hr, hi = hr + (mr * sr - mi * si), hi + (mr * si + mi * sr)
        cbr = jnp.broadcast_to(cr, (SUBLANES, hw))
        cbi = jnp.broadcast_to(ci, (SUBLANES, hw))
        hr, hi = hr + (tr * cbr - ti * cbi), hi + (tr * cbi + ti * cbr)
        cr = hr[SUBLANES - 1:SUBLANES, :]
        ci = hi[SUBLANES - 1:SUBLANES, :]
        slabs.append(jnp.concatenate([hr, hi], axis=1))
    return jnp.concatenate(slabs, axis=0).astype(BF16), cr, ci


def _s5_sample_kernel(xb_ref, wb_ref, wc_ref, ar_ref, ai_ref, d_ref, s0r_ref, s0i_ref, g_all_ref,
                      g_ref, sre_ref, sim_ref, bu_scr, h_scr, *, n_seq, n_steps):
    del g_all_ref
    hw = ar_ref.shape[1]
    xb = xb_ref[...]
    bu_scr[...] = jnp.dot(xb.astype(BF16), wb_ref[...], preferred_element_type=F32)
    ar = ar_ref[...]
    ai = ai_ref[...]
    hr = s0r_ref[...]
    hi = s0i_ref[...]
    for t in range(n_steps):
        sl = slice(t * n_seq, (t + 1) * n_seq)
        hr, hi = (ar * hr - ai * hi) + bu_scr[sl, :hw], (ar * hi + ai * hr) + bu_scr[sl, hw:]
        h_scr[sl, :hw] = hr.astype(h_scr.dtype)
        h_scr[sl, hw:] = hi.astype(h_scr.dtype)
    sre_ref[...] = hr
    sim_ref[...] = hi
    y = jnp.dot(h_scr[...], wc_ref[...], preferred_element_type=F32)
    g_ref[...] = jax.nn.gelu(y + d_ref[...] * xb)


def _half_ffn_matmuls(h, wg, wu, wd, *, bm):
    m, dm = h.shape
    dff = wg.shape[1]
    bn_up = _pick(dff, (256, 128))
    n_i, n_j = m // bm, dff // bn_up
    rb = LANES
    n_rb = dff // rb
    assert n_i * n_j >= n_rb, "not enough up-projection steps to cover the down-projection weight"
    wd_blk = lambda i, j: (jnp.minimum(i * n_j + j, n_rb - 1), 0)
    a, wd = pl.pallas_call(
        _ffn_up_kernel,
        grid=(n_i, n_j),
        in_specs=[pl.BlockSpec((bm, dm), lambda i, j: (i, 0)),
                  pl.BlockSpec((dm, bn_up), lambda i, j: (0, j)),
                  pl.BlockSpec((dm, bn_up), lambda i, j: (0, j)),
                  pl.BlockSpec((rb, dm), wd_blk)],
        out_specs=[pl.BlockSpec((bm, bn_up), lambda i, j: (i, j)),
                   pl.BlockSpec((rb, dm), wd_blk)],
        out_shape=[jax.ShapeDtypeStruct((m, dff), BF16), jax.ShapeDtypeStruct((dff, dm), BF16)],
        compiler_params=_cparams("arbitrary", "arbitrary"),
        name="ffn_up",
    )(h, wg, wu, wd)

    bm_d = _pick(m, (1024, 512, 256, 128))
    bn_d = _pick(dm, (1024, 512, 256, 128))
    bk = dff // 2 if (dff // 2) % LANES == 0 else dff
    return pl.pallas_call(
        _ffn_down_kernel,
        grid=(m // bm_d, dm // bn_d, dff // bk),
        in_specs=[pl.BlockSpec((bm_d, bk), lambda i, j, k: (i, k)),
                  pl.BlockSpec((bk, bn_d), lambda i, j, k: (k, j))],
        out_specs=pl.BlockSpec((bm_d, bn_d), lambda i, j, k: (i, j)),
        out_shape=jax.ShapeDtypeStruct((m, dm), F32),
        compiler_params=_cparams("parallel", "parallel", "arbitrary"),
        name="ffn_down",
    )(a, wd)


def _matmul(x, w, *, bm, bn, out_dtype, name, n=None):
    m, k = x.shape
    n = w.shape[1] if n is None else n
    return pl.pallas_call(
        _mm_kernel,
        grid=(m // bm, n // bn),
        in_specs=[pl.BlockSpec((bm, k), lambda i, j: (i, 0)),
                  pl.BlockSpec((k, bn), lambda i, j: (0, j))],
        out_specs=pl.BlockSpec((bm, bn), lambda i, j: (i, j)),
        out_shape=jax.ShapeDtypeStruct((m, n), out_dtype),
        compiler_params=_cparams("parallel", "arbitrary"),
        name=name,
    )(x, w)


def kernel(x_prompt, x_sample, state_lru_h, state_conv, state_ssm_re, state_ssm_im, ffn1_pre_g, ffn1_post_g, ffn1_w_gate, ffn1_w_up, ffn1_w_down, mix_pre_g, mix_post_g, w_in, conv_w, conv_b, w_rg, b_rg, w_ig, b_ig, lru_lambda, ssm_a_re, ssm_a_im, ssm_log_dt, ssm_b_re, ssm_b_im, ssm_c_re, ssm_c_im, ssm_d, w_glu, b_glu, w_out_a, w_out_b, w_o, ffn2_pre_g, ffn2_post_g, ffn2_w_gate, ffn2_w_up, ffn2_w_down):
    nb, seq, dm = x_prompt.shape
    db, dseq, _ = x_sample.shape
    depth = state_lru_h.shape[0]
    assert depth == 1, "one decoder layer"
    d_lru = state_lru_h.shape[2]
    n_grp, ssm_p = state_ssm_re.shape[2], state_ssm_re.shape[3]
    d_ssm = ssm_d.shape[1]
    ssm_cg = d_ssm // n_grp
    d_in = w_in.shape[2]
    assert w_rg.shape[2] == LRU_BLK and conv_w.shape[1] == CONV_W
    assert d_in == 2 * d_lru + d_ssm + 2 * dm and seq >= CONV_W - 1 and dseq >= CONV_W - 1

    mp = nb * seq
    ms = db * dseq
    m = mp + ms
    sdt = state_lru_h.dtype

    row2 = lambda v: v.reshape(1, -1)
    bf = lambda v: v[0].astype(BF16)

    xp2 = x_prompt.reshape(mp, dm)
    xs2 = jnp.swapaxes(x_sample, 0, 1).reshape(ms, dm)

    tr = _pick(math.gcd(mp, ms), (256, 128, 64, 32, 16, 8))
    npt = mp // tr
    xp_spec = pl.BlockSpec((tr, dm), lambda i: (jnp.minimum(i, npt - 1), 0))
    xs_spec = pl.BlockSpec((tr, dm), lambda i: (jnp.maximum(i - npt, 0), 0))
    row_spec = pl.BlockSpec((tr, dm), lambda i: (i, 0))
    g_spec = pl.BlockSpec((1, dm), lambda i: (0, 0))

    h1 = pl.pallas_call(
        functools.partial(_prenorm_kernel, n_prompt_tiles=npt),
        grid=(m // tr,),
        in_specs=[xp_spec, xs_spec, g_spec],
        out_specs=row_spec,
        out_shape=jax.ShapeDtypeStruct((m, dm), BF16),
        compiler_params=_cparams("parallel"),
        name="prenorm1",
    )(xp2, xs2, row2(ffn1_pre_g))

    bm = _pick(m, (1536, 1024, 768, 512, 384, 256, 128))
    f1 = _half_ffn_matmuls(h1, ffn1_w_gate[0], ffn1_w_up[0], ffn1_w_down[0], bm=bm)

    x1, u = pl.pallas_call(
        functools.partial(_post_ffn1_kernel, n_prompt_tiles=npt),
        grid=(m // tr,),
        in_specs=[xp_spec, xs_spec, row_spec, g_spec, g_spec],
        out_specs=[row_spec, row_spec],
        out_shape=[jax.ShapeDtypeStruct((m, dm), F32), jax.ShapeDtypeStruct((m, dm), BF16)],
        compiler_params=_cparams("parallel"),
        name="post_ffn1",
    )(xp2, xs2, f1, row2(ffn1_post_g), row2(mix_pre_g))

    bn_in = _pick(math.gcd(d_lru, dm), (512, 256, 128))
    n_a = 2 * d_lru + d_ssm
    z = _matmul(u, w_in[0], bm=bm, bn=bn_in, out_dtype=F32, name="in_proj", n=n_a)

    cb = _pick(d_lru, (512, 256, 128))
    ncb = d_lru // cb
    hpb = cb // LRU_BLK
    wrg = bf(w_rg)
    wig = bf(w_ig)
    lru_params = (conv_w[0], row2(conv_b), wrg, row2(b_rg), wig, row2(b_ig), row2(lru_lambda))
    assert d_lru <= dm and d_ssm <= dm
    any_spec = pl.BlockSpec(memory_space=pl.ANY)
    assert mp % ms == 0, "sample rows must tile the unified row axis"
    s_blk = mp // ms

    n_lags = 2
    abr, abi, cfr, cfi = pl.pallas_call(
        _s5_disc_kernel,
        out_shape=[jax.ShapeDtypeStruct((n_grp, ssm_p), F32)] * 2
                  + [jax.ShapeDtypeStruct((n_lags, n_grp, ssm_p), F32)] * 2,
        name="s5_discretise",
    )(ssm_a_re[0], ssm_a_im[0], ssm_log_dt[0].reshape(n_grp, 1))
    bb_re = cfr[..., None] * ssm_b_re - cfi[..., None] * ssm_b_im
    bb_im = cfr[..., None] * ssm_b_im + cfi[..., None] * ssm_b_re

    gpb = LANES // ssm_cg
    nj = n_grp // gpb
    hw = gpb * ssm_p
    bb =jnp.stack([bb_re, bb_im]).reshape(2, n_lags, nj, gpb, ssm_p, ssm_cg).astype(BF16)
    bb = jnp.transpose(bb, (2, 1, 3, 5, 0, 4)).reshape(nj, n_lags * LANES, 2, ssm_p)
    assert LANES % ssm_p == 0 and gpb % (LANES // ssm_p) == 0
    hpl = LANES // ssm_p
    bb = jnp.concatenate([bb] * hpl, axis=3)
    wb = jnp.concatenate([bb[:, :, ri, :] for ri in range(2) for _ in range(gpb // hpl)], axis=2)
    wb_shape = (1, n_lags * LANES, 2 * hw)
    g_row = (lax.broadcasted_iota(jnp.int32, wb_shape, 1) // ssm_cg) % gpb
    h_col = (lax.broadcasted_iota(jnp.int32, wb_shape, 2) // ssm_p) % gpb
    wb = jnp.where(g_row == h_col, wb, jnp.zeros((), BF16))
    cc = jnp.stack([ssm_c_re[0], -ssm_c_im[0]]).reshape(2, nj, gpb, ssm_cg, ssm_p).astype(BF16)
    cc = jnp.transpose(cc, (1, 0, 4, 2, 3)).reshape(nj, 2, ssm_p, LANES)
    wc = jnp.concatenate([cc[:, ri, :, :] for ri in range(2) for _ in range(gpb)], axis=1)
    wc_shape = (1, 2 * hw, LANES)
    wc_mask = ((lax.broadcasted_iota(jnp.int32, wc_shape, 1) // ssm_p) % gpb
               == lax.broadcasted_iota(jnp.int32, wc_shape, 2) // ssm_cg)
    wc = jnp.where(wc_mask, wc, jnp.zeros((), BF16))
    abr3 = abr.reshape(nj, 1, hw)
    abi3 = abi.reshape(nj, 1, hw)
    xb_blk = (2 * d_lru) // LANES
    dsk = row2(ssm_d)

    n_par = 2 if nj % 2 == 0 and xb_blk % 2 == 0 else 1
    par_shape = (1, n_par * 2 * hw, n_par * LANES)
    par_mask = (lax.broadcasted_iota(jnp.int32, par_shape, 1) // (2 * hw)
                == lax.broadcasted_iota(jnp.int32, par_shape, 2) // LANES)
    wc_par = jnp.concatenate([wc.reshape(nj // n_par, n_par * 2 * hw, LANES)] * n_par, axis=2)
    wc_par = jnp.where(par_mask, wc_par, jnp.zeros((), BF16))
    n_gi = nj // n_par
    bm_gl = m // n_gi
    assert m % n_gi == 0 and bm_gl % (2 * SUBLANES) == 0

    def gl_cols_ok(t):
        bn = (2 * dm) // (nb * (seq // t)) if (2 * dm) % (nb * (seq // t)) == 0 else 0
        return bn > 0 and bn % LANES == 0 and n_a % bn == 0

    tm5 = _pick(seq, tuple(t for t in (512, 256, 128, 64, 32, 16, 8) if seq % t == 0 and gl_cols_ok(t)))
    rp5 = seq // tm5
    bn_gl = (2 * dm) // (nb * rp5)
    gl_blk = n_a // bn_gl
    gl, g_all, sre_p, sim_p = pl.pallas_call(
        _gates_s5_prompt_kernel,
        grid=(n_gi, nb, rp5),
        in_specs=[pl.BlockSpec((bm_gl, dm), lambda j, n, r: (j, 0)),
                  pl.BlockSpec((dm, bn_gl), lambda j, n, r: (0, gl_blk + n * rp5 + r)),
                  pl.BlockSpec((tm5, n_par * LANES), lambda j, n, r: (n * rp5 + r, xb_blk // n_par + j)),
                  pl.BlockSpec((n_par, n_lags * LANES, 2 * hw), lambda j, n, r: (j, 0, 0)),
                  pl.BlockSpec((None, n_par * 2 * hw, n_par * LANES), lambda j, n, r: (j, 0, 0)),
                  pl.BlockSpec((n_par, 1, hw), lambda j, n, r: (j, 0, 0)),
                  pl.BlockSpec((n_par, 1, hw), lambda j, n, r: (j, 0, 0)),
                  pl.BlockSpec((1, n_par * LANES), lambda j, n, r: (0, j)), any_spec],
        out_specs=[pl.BlockSpec((bm_gl, bn_gl), lambda j, n, r: (j, n * rp5 + r)),
                   pl.BlockSpec((tm5, n_par * LANES), lambda j, n, r: (n * rp5 + r, j)),
                   pl.BlockSpec((None, 1, n_par * hw), lambda j, n, r: (n, 0, j)),
                   pl.BlockSpec((None, 1, n_par * hw), lambda j, n, r: (n, 0, j))],
        out_shape=[jax.ShapeDtypeStruct((m, 2 * dm), F32),
                   jax.ShapeDtypeStruct((m, dm), F32),
                   jax.ShapeDtypeStruct((nb, 1, n_grp * ssm_p), F32),
                   jax.ShapeDtypeStruct((nb, 1, n_grp * ssm_p), F32)],
        scratch_shapes=[pltpu.VMEM((n_par, 1, hw), F32), pltpu.VMEM((n_par, 1, hw), F32)],
        input_output_aliases={8: 1},
        compiler_params=_cparams("arbitrary", "arbitrary", "arbitrary"),
        name="gates_s5_prompt",
    )(u, w_in[0], z, wb, wc_par, abr3, abi3, dsk, f1)

    s5_w_specs1 = [pl.BlockSpec((None, LANES, 2 * hw), lambda j: (j, 0, 0)),
                   pl.BlockSpec((None, 2 * hw, LANES), lambda j: (j, 0, 0)),
                   pl.BlockSpec((None, 1, hw), lambda j: (j, 0, 0)),
                   pl.BlockSpec((None, 1, hw), lambda j: (j, 0, 0)),
                   pl.BlockSpec((1, LANES), lambda j: (0, j))]
    g_all, sre_s, sim_s = pl.pallas_call(
        functools.partial(_s5_sample_kernel, n_seq=db, n_steps=dseq),
        grid=(nj,),
        in_specs=[pl.BlockSpec((ms, LANES), lambda j: (s_blk, xb_blk + j))] + s5_w_specs1
                 + [pl.BlockSpec((db, hw), lambda j: (0, j)), pl.BlockSpec((db, hw), lambda j: (0, j)), any_spec],
        out_specs=[pl.BlockSpec((ms, LANES), lambda j: (s_blk, j)),
                   pl.BlockSpec((db, hw), lambda j: (0, j)),
                   pl.BlockSpec((db, hw), lambda j: (0, j))],
        out_shape=[jax.ShapeDtypeStruct((m, dm), F32),
                   jax.ShapeDtypeStruct((db, n_grp * ssm_p), F32),
                   jax.ShapeDtypeStruct((db, n_grp * ssm_p), F32)],
        scratch_shapes=[pltpu.VMEM((ms, 2 * hw), F32), pltpu.VMEM((ms, 2 * hw), BF16)],
        input_output_aliases={8: 0},
        compiler_params=_cparams("parallel"),
        name="s5_sample",
    )(z, wb, wc, abr3, abi3, dsk,
      state_ssm_re[0].reshape(db, n_grp * ssm_p), state_ssm_im[0].reshape(db, n_grp * ssm_p), g_all)

    assert d_ssm // cb == ncb
    tm = _pick(seq, tuple(t for t in (512, 256, 128, 64, 32, 16, 8)
                          if seq % t == 0 and (m * t) % mp == 0 and ((m * t) // mp) % (2 * SUBLANES) == 0))
    rpt = seq // tm
    bm_u = (m * tm) // mp
    col = lambda shape: pl.BlockSpec(shape, lambda i, j: (0, j))
    head_spec = pl.BlockSpec((hpb, LRU_BLK, LRU_BLK), lambda i, j: (j, 0, 0))
    yb, ya, hl_p = pl.pallas_call(
        functools.partial(_glu_rglru_prompt_kernel, tiles_per_seq=rpt),
        grid=(nb * rpt, ncb),
        in_specs=[pl.BlockSpec((bm_u, d_ssm), lambda i, j: (i, 0)),
                  pl.BlockSpec((bm_u, cb), lambda i, j: (i, j)),
                  pl.BlockSpec((d_ssm, cb), lambda i, j: (0, j)),
                  col((1, cb)),
                  pl.BlockSpec((tm, cb), lambda i, j: (i, j)),
                  pl.BlockSpec((tm, cb), lambda i, j: (i, ncb + j)),
                  col((CONV_W, cb)), col((1, cb)), head_spec, col((1, cb)), head_spec, col((1, cb)),
                  col((1, cb)), any_spec],
        out_specs=[pl.BlockSpec((bm_u, cb), lambda i, j: (i, j)),
                   pl.BlockSpec((tm, cb), lambda i, j: (i, j)),
                   pl.BlockSpec((None, 1, cb), lambda i, j: (i, 0, j))],
        out_shape=[jax.ShapeDtypeStruct((m, d_ssm), BF16),
                   jax.ShapeDtypeStruct((m, dm), BF16),
                   jax.ShapeDtypeStruct((nb * rpt, 1, d_lru), F32)],
        scratch_shapes=[pltpu.VMEM((bm_u, d_ssm), BF16),
                        pltpu.VMEM((ncb, SUBLANES, cb), F32), pltpu.VMEM((ncb, 1, cb), F32)],
        input_output_aliases={4 + 2 + len(lru_params): 1},
        compiler_params=_cparams("arbitrary", "arbitrary"),
        name="glu_rglru_prompt",
    )(g_all, g_all, w_glu[0], row2(b_glu), z, z, *lru_params, h1)

    par1 = lambda shape: pl.BlockSpec(shape, lambda c: (0, c))
    cst = jnp.swapaxes(state_conv[0], 0, 1)
    ya, hl_s = pl.pallas_call(
        functools.partial(_rglru_sample_kernel, n_seq=db, n_steps=dseq),
        grid=(ncb,),
        in_specs=[pl.BlockSpec((ms, cb), lambda c: (s_blk, c)),
                  pl.BlockSpec((ms, cb), lambda c: (s_blk, ncb + c)),
                  pl.BlockSpec((CONV_W - 1, db, cb), lambda c: (0, 0, c)),
                  par1((db, cb)),
                  par1((CONV_W, cb)), par1((1, cb)),
                  pl.BlockSpec((hpb, LRU_BLK, LRU_BLK), lambda c: (c, 0, 0)), par1((1, cb)),
                  pl.BlockSpec((hpb, LRU_BLK, LRU_BLK), lambda c: (c, 0, 0)), par1((1, cb)),
                  par1((1, cb)), any_spec],
        out_specs=[pl.BlockSpec((ms, cb), lambda c: (s_blk, c)), par1((db, cb))],
        out_shape=[jax.ShapeDtypeStruct((m, dm), BF16), jax.ShapeDtypeStruct((db, d_lru), F32)],
        input_output_aliases={4 + len(lru_params): 0},
        compiler_params=_cparams("parallel"),
        name="rglru_sample",
    )(z, z, cst, state_lru_h[0], *lru_params, ya)

    bm_g = _pick(m, (1024, 512, 256, 128))
    bn_m = bn_in
    gla_blk = 0
    glb_blk = dm // bn_m
    merged = pl.pallas_call(
        _merge_kernel,
        grid=(m // bm_g, dm // bn_m),
        in_specs=[pl.BlockSpec((bm_g, d_lru), lambda i, j: (i, 0)),
                  pl.BlockSpec((bm_g, d_ssm), lambda i, j: (i, 0)),
                  pl.BlockSpec((d_lru, bn_m), lambda i, j: (0, j)),
                  pl.BlockSpec((d_ssm, bn_m), lambda i, j: (0, j)),
                  pl.BlockSpec((bm_g, bn_m), lambda i, j: (i, gla_blk + j)),
                  pl.BlockSpec((bm_g, bn_m), lambda i, j: (i, glb_blk + j))],
        out_specs=pl.BlockSpec((bm_g, bn_m), lambda i, j: (i, j)),
        out_shape=jax.ShapeDtypeStruct((m, dm), BF16),
        compiler_params=_cparams("parallel", "arbitrary"),
        name="gated_merge",
    )(ya, yb, w_out_a[0], w_out_b[0], gl, gl)

    o = _matmul(merged, w_o[0], bm=bm, bn=bn_in, out_dtype=F32, name="o_proj")

    x2, h2 = pl.pallas_call(
        _post_mix_kernel,
        grid=(m // tr,),
        in_specs=[row_spec, row_spec, g_spec, g_spec],
        out_specs=[row_spec, row_spec],
        out_shape=[jax.ShapeDtypeStruct((m, dm), F32), jax.ShapeDtypeStruct((m, dm), BF16)],
        compiler_params=_cparams("parallel"),
        name="post_mix",
    )(x1, o, row2(mix_post_g), row2(ffn2_pre_g))

    f2 = _half_ffn_matmuls(h2, ffn2_w_gate[0], ffn2_w_up[0], ffn2_w_down[0], bm=bm)

    def final(rows, first_tile):
        return pl.pallas_call(
            _final_kernel,
            grid=(rows // tr,),
            in_specs=[pl.BlockSpec((tr, dm), lambda i: (first_tile + i, 0)),
                      pl.BlockSpec((tr, dm), lambda i: (first_tile + i, 0)),
                      g_spec],
            out_specs=row_spec,
            out_shape=jax.ShapeDtypeStruct((rows, dm), F32),
            compiler_params=_cparams("parallel"),
            name="final_residual",
        )(x2, f2, row2(ffn2_post_g))

    y_prompt = final(mp, 0).reshape(nb, seq, dm)
    y_sample = jnp.swapaxes(final(ms, npt).reshape(dseq, db, dm), 0, 1)

    nk = CONV_W - 1
    prompt_conv = jnp.stack([lax.slice(z, ((n + 1) * seq - nk, 0), ((n + 1) * seq, d_lru)) for n in range(nb)])
    sample_conv = jnp.swapaxes(lax.slice(z, (m - nk * db, 0), (m, d_lru)).reshape(nk, db, d_lru), 0, 1)
    st = lambda v, n: v.reshape(1, n, n_grp, ssm_p).astype(sdt)
    return (y_prompt, y_sample,
            hl_p.reshape(nb, rpt, d_lru)[:, rpt - 1][None].astype(sdt), prompt_conv[None].astype(sdt),
            st(sre_p, nb), st(sim_p, nb),
            hl_s.reshape(1, db, d_lru).astype(sdt), sample_conv[None].astype(sdt),
            st(sre_s, db), st(sim_s, db))
```

```python
import functools
import math

import jax
import jax.numpy as jnp
from jax import lax
from jax.experimental import pallas as pl
from jax.experimental.pallas import tpu as pltpu

F32 = jnp.float32
BF16 = jnp.bfloat16

EPS = 1e-6
C_RG = 8.0
CONV_W = 4
LANES = 128
SUBLANES = 8
LRU_BLK = 128
VMEM_LIMIT = 56 * 1024 * 1024


def _cparams(*sem):
    return pltpu.CompilerParams(dimension_semantics=sem, vmem_limit_bytes=VMEM_LIMIT)


def _pick(n, candidates):
    for c in candidates:
        if n % c == 0:
            return c
    raise ValueError(f"no tile in {candidates} divides {n}")


def _rms(x, g):
    return x * lax.rsqrt(jnp.mean(x * x, axis=-1, keepdims=True) + EPS) * g


def _softplus(x):
    return jnp.maximum(x, 0.0) + jnp.log1p(jnp.exp(-jnp.abs(x)))


def _two_group_rows(i, n_prompt_tiles, xp_ref, xs_ref, body):
    @pl.when(i < n_prompt_tiles)
    def _():
        body(xp_ref[...])

    @pl.when(i >= n_prompt_tiles)
    def _():
        body(xs_ref[...])


def _prenorm_kernel(xp_ref, xs_ref, g_ref, h_ref, *, n_prompt_tiles):
    def body(x):
        h_ref[...] = _rms(x, g_ref[...]).astype(h_ref.dtype)

    _two_group_rows(pl.program_id(0), n_prompt_tiles, xp_ref, xs_ref, body)


def _post_ffn1_kernel(xp_ref, xs_ref, f_ref, gpost_ref, gpre_ref, x1_ref, u_ref, *, n_prompt_tiles):
    def body(x):
        x1 = x + 0.5 * _rms(f_ref[...], gpost_ref[...])
        x1_ref[...] = x1
        u_ref[...] = _rms(x1, gpre_ref[...]).astype(u_ref.dtype)

    _two_group_rows(pl.program_id(0), n_prompt_tiles, xp_ref, xs_ref, body)


def _post_mix_kernel(x1_ref, o_ref, gpost_ref, gpre_ref, x2_ref, h_ref):
    x2 = x1_ref[...] + _rms(o_ref[...], gpost_ref[...])
    x2_ref[...] = x2
    h_ref[...] = _rms(x2, gpre_ref[...]).astype(h_ref.dtype)


def _final_kernel(x2_ref, f_ref, g_ref, y_ref):
    y_ref[...] = x2_ref[...] + 0.5 * _rms(f_ref[...], g_ref[...])


def _ffn_up_kernel(h_ref, wg_ref, wu_ref, wd_ref, a_ref, wd16_ref):
    wg = wg_ref[...].astype(BF16)
    wu = wu_ref[...].astype(BF16)
    half = h_ref.shape[0] // 2
    for rows in (slice(0, half), slice(half, None)):
        h = h_ref[rows, :]
        g = jnp.dot(h, wg, preferred_element_type=F32)
        u = jnp.dot(h, wu, preferred_element_type=F32)
        a_ref[rows, :] = (jax.nn.silu(g) * u).astype(a_ref.dtype)
    wd16_ref[...] = wd_ref[...].astype(wd16_ref.dtype)


def _ffn_down_kernel(a_ref, wd_ref, o_ref):
    @pl.when(pl.program_id(2) == 0)
    def _():
        o_ref[...] = jnp.dot(a_ref[...], wd_ref[...], preferred_element_type=F32)

    @pl.when(pl.program_id(2) != 0)
    def _():
        o_ref[...] += jnp.dot(a_ref[...], wd_ref[...], preferred_element_type=F32)


def _mm_kernel(x_ref, w_ref, o_ref):
    w = w_ref[...].astype(BF16)
    o_ref[...] = jnp.dot(x_ref[...], w, preferred_element_type=F32).astype(o_ref.dtype)


def _merge_kernel(ya_ref, yb_ref, wa_ref, wb_ref, gla_ref, glb_ref, o_ref):
    wa = wa_ref[...].astype(BF16)
    wb = wb_ref[...].astype(BF16)
    half = o_ref.shape[0] // 2
    for rows in (slice(0, half), slice(half, None)):
        pa = jnp.dot(ya_ref[rows, :], wa, preferred_element_type=F32)
        pb = jnp.dot(yb_ref[rows, :], wb, preferred_element_type=F32)
        m = jax.nn.sigmoid(gla_ref[rows, :]) * pa + jax.nn.sigmoid(glb_ref[rows, :]) * pb
        o_ref[rows, :] = m.astype(o_ref.dtype)


def _lru_gates(xc, wrg_ref, brg_ref, wig_ref, big_ref, lam_ref):
    xcb = xc.astype(BF16)
    rs, gs = [], []
    for hh in range(wrg_ref.shape[0]):
        xh = xcb[:, hh * LRU_BLK:(hh + 1) * LRU_BLK]
        rs.append(jnp.dot(xh, wrg_ref[hh], preferred_element_type=F32))
        gs.append(jnp.dot(xh, wig_ref[hh], preferred_element_type=F32))
    r = jax.nn.sigmoid(jnp.concatenate(rs, axis=1) + brg_ref[...])
    i = jax.nn.sigmoid(jnp.concatenate(gs, axis=1) + big_ref[...])
    log_a = -C_RG * r * _softplus(-lam_ref[...])
    a = jnp.exp(log_a)
    mult = jnp.sqrt(-jnp.tanh(log_a) * (a * a + 1.0))
    return a, mult * (i * xc)


def _glu_rglru_prompt_kernel(g_ref, gcol_ref, w_ref, b_ref,
                             xa_ref, ga_ref, cw_ref, cb_ref, wrg_ref, brg_ref, wig_ref, big_ref, lam_ref,
                             ya_all_ref, o_ref, ya_ref, hl_ref, g_scr, xcars, hcars, *, tiles_per_seq):
    del ya_all_ref
    i, j = pl.program_id(0), pl.program_id(1)

    @pl.when(j == 0)
    def _():
        g_scr[...] = g_ref[...].astype(g_scr.dtype)

    _rglru_prompt_tile(xa_ref, ga_ref, cw_ref, cb_ref, wrg_ref, brg_ref, wig_ref, big_ref, lam_ref,
                       ya_ref, hl_ref, xcars.at[j], hcars.at[j], i % tiles_per_seq == 0)
    s = jnp.dot(g_scr[...], w_ref[...].astype(BF16), preferred_element_type=F32) + b_ref[...]
    o_ref[...] = (gcol_ref[...] * jax.nn.sigmoid(s)).astype(o_ref.dtype)


def _rglru_prompt_tile(xa_ref, ga_ref, cw_ref, cb_ref, wrg_ref, brg_ref, wig_ref, big_ref, lam_ref,
                       ya_ref, hl_ref, xcar, hcar, starts_sequence):
    tm = xa_ref.shape[0]

    @pl.when(starts_sequence)
    def _():
        xcar[...] = jnp.zeros_like(xcar)
        hcar[...] = jnp.zeros_like(hcar)

    x = xa_ref[...]
    xfull = jnp.concatenate([xcar[...], x], axis=0)
    xcar[...] = x[tm - SUBLANES:, :]
    cw = cw_ref[...]
    xc = cb_ref[...]
    for k in range(CONV_W - 1):
        xs = pltpu.roll(xfull, CONV_W - 1 - k, 0)[SUBLANES:, :]
        xc = xc + xs * cw[k:k + 1, :]
    xc = xc + x * cw[CONV_W - 1:CONV_W, :]

    a, b = _lru_gates(xc, wrg_ref, brg_ref, wig_ref, big_ref, lam_ref)
    cb = a.shape[1]
    n_slab = tm // SUBLANES
    a = a.reshape(n_slab, SUBLANES, cb)
    b = b.reshape(n_slab, SUBLANES, cb)
    sub = lax.broadcasted_iota(jnp.int32, (1, SUBLANES, cb), 1)
    d = 1
    while d < SUBLANES:
        m = sub >= d
        a_s = jnp.where(m, pltpu.roll(a, d, 1), 1.0)
        b_s = jnp.where(m, pltpu.roll(b, d, 1), 0.0)
        b = a * b_s + b
        a = a * a_s
        d *= 2
    h = hcar[...]
    hs = []
    for s in range(n_slab):
        h_slab = b[s] + a[s] * jnp.broadcast_to(h, (SUBLANES, cb))
        h = h_slab[SUBLANES - 1:SUBLANES, :]
        hs.append(h_slab)
    hcar[...] = h
    hl_ref[...] = h
    ya_ref[...] = (jnp.concatenate(hs, axis=0) * jax.nn.gelu(ga_ref[...])).astype(ya_ref.dtype)


def _rglru_sample_kernel(xa_ref, ga_ref, cst_ref, h0_ref, cw_ref, cb_ref, wrg_ref, brg_ref, wig_ref,
                         big_ref, lam_ref, ya_all_ref, ya_ref, hl_ref, *, n_seq, n_steps):
    del ya_all_ref
    x = xa_ref[...]
    xp = jnp.concatenate([cst_ref[k] for k in range(CONV_W - 1)] + [x], axis=0)
    rows = n_seq * n_steps
    cw = cw_ref[...]
    xc = cb_ref[...]
    for k in range(CONV_W):
        xc = xc + xp[k * n_seq:k * n_seq + rows, :] * cw[k:k + 1, :]
    a, b = _lru_gates(xc, wrg_ref, brg_ref, wig_ref, big_ref, lam_ref)
    h = h0_ref[...]
    for t in range(n_steps):
        sl = slice(t * n_seq, (t + 1) * n_seq)
        h = a[sl, :] * h + b[sl, :]
        ya_ref[sl, :] = (h * jax.nn.gelu(ga_ref[sl, :])).astype(ya_ref.dtype)
    hl_ref[...] = h


def _s5_disc_kernel(are_ref, aim_ref, ldt_ref, abr_ref, abi_ref, cfr_ref, cfi_ref):
    a_re = are_ref[...]
    a_im = aim_ref[...]
    dt = jnp.exp(ldt_ref[...])
    mag = jnp.exp(a_re * dt)
    abr = mag * jnp.cos(a_im * dt)
    abi = mag * jnp.sin(a_im * dt)
    den = a_re * a_re + a_im * a_im
    nr = abr - 1.0
    abr_ref[...] = abr
    abi_ref[...] = abi
    cr = (nr * a_re + abi * a_im) / den
    ci = (abi * a_re - nr * a_im) / den
    for q in range(cfr_ref.shape[0]):
        cfr_ref[q] = cr
        cfi_ref[q] = ci
        cr, ci = abr * cr - abi * ci, abr * ci + abi * cr


def _gates_s5_prompt_kernel(u_ref, w_ref, xb_ref, wb_ref, wc_ref, ar_ref, ai_ref, d_ref, g_all_ref,
                            gl_ref, g_ref, sre_ref, sim_ref, cre, cim):
    del g_all_ref
    n_par, _, hw = ar_ref.shape

    @pl.when(pl.program_id(2) == 0)
    def _():
        cre[...] = jnp.zeros_like(cre)
        cim[...] = jnp.zeros_like(cim)

    split = (gl_ref.shape[0] * 9 // 16) // (2 * SUBLANES) * (2 * SUBLANES)
    w = w_ref[...].astype(BF16)
    bus = [_s5_project_in(xb_ref[:, p * LANES:(p + 1) * LANES], wb_ref.at[p]) for p in range(n_par)]
    gl_ref[:split, :] = jnp.dot(u_ref[:split, :], w, preferred_element_type=F32)
    scans = [_s5_slab_scan(bus[p], ar_ref[p], ai_ref[p], cre[p], cim[p]) for p in range(n_par)]
    y = jnp.dot(jnp.concatenate([s[0] for s in scans], axis=1), wc_ref[...], preferred_element_type=F32)
    gl_ref[split:, :] = jnp.dot(u_ref[split:, :], w, preferred_element_type=F32)
    g_ref[...] = jax.nn.gelu(y + d_ref[...] * xb_ref[...])
    for p in range(n_par):
        _, cr, ci = scans[p]
        cre[p] = cr
        cim[p] = ci
        sre_ref[:, p * hw:(p + 1) * hw] = cr
        sim_ref[:, p * hw:(p + 1) * hw] = ci


def _s5_project_in(xb, wb_ref):
    tm, width = xb.shape
    n_slab = tm // SUBLANES
    n_lags = wb_ref.shape[0] // width
    xb3 = xb.reshape(n_slab, SUBLANES, width)
    subx = lax.broadcasted_iota(jnp.int32, (1, SUBLANES, width), 1)
    lagged = [xb3] + [jnp.where(subx >= q, pltpu.roll(xb3, q, 1), 0.0) for q in range(1, n_lags)]
    xs = jnp.concatenate(lagged, axis=2).reshape(tm, n_lags * width)
    return jnp.dot(xs.astype(BF16), wb_ref[...], preferred_element_type=F32), n_lags


def _s5_slab_scan(bu_lags, ar, ai, cr, ci):
    bu, n_lags = bu_lags
    tm = bu.shape[0]
    hw = ar.shape[1]
    n_slab = tm // SUBLANES
    sub = lax.broadcasted_iota(jnp.int32, (SUBLANES, hw), 0)
    pr, pi = ar, ai
    tr = jnp.where(sub == 0, pr, 0.0)
    ti = jnp.where(sub == 0, pi, 0.0)
    steps = []
    d = 1
    while d < SUBLANES:
        mr = jnp.where(sub >= d, pr, 0.0)
        mi = jnp.where(sub >= d, pi, 0.0)
        if d >= n_lags:
            steps.append((d, mr, mi))
        sr = pltpu.roll(tr, d, 0)
        si = pltpu.roll(ti, d, 0)
        tr, ti = tr + (mr * sr - mi * si), ti + (mr * si + mi * sr)
        pr, pi = pr * pr - pi * pi, 2.0 * (pr * pi)
        d *= 2
    slabs = []
    for s in range(n_slab):
        rows = slice(s * SUBLANES, (s + 1) * SUBLANES)
        hr = bu[rows, :hw]
        hi = bu[rows, hw:]
        for d, mr, mi in steps:
            sr = pltpu.roll(hr, d, 0)
            si = pltpu.roll(hi, d, 0)
            hr, hi = hr + (mr * sr - mi * si), hi + (mr * si + mi * sr)
        cbr = jnp.broadcast_to(cr, (SUBLANES, hw))
        cbi = jnp.broadcast_to(ci, (SUBLANES, hw))
        hr, hi = hr + (tr * cbr - ti * cbi), hi + (tr * cbi + ti * cbr)
        cr = hr[SUBLANES - 1:SUBLANES, :]
        ci = hi[SUBLANES - 1:SUBLANES, :]
        slabs.append(jnp.concatenate([hr, hi], axis=1))
    return jnp.concatenate(slabs, axis=0).astype(BF16), cr, ci


def _s5_sample_kernel(xb_ref, wb_ref, wc_ref, ar_ref, ai_ref, d_ref, s0r_ref, s0i_ref, g_all_ref,
                      g_ref, sre_ref, sim_ref, bu_scr, h_scr, *, n_seq, n_steps):
    del g_all_ref
    hw = ar_ref.shape[1]
    xb = xb_ref[...]
    bu_scr[...] = jnp.dot(xb.astype(BF16), wb_ref[...], preferred_element_type=F32)
    ar = ar_ref[...]
    ai = ai_ref[...]
    hr = s0r_ref[...]
    hi = s0i_ref[...]
    for t in range(n_steps):
        sl = slice(t * n_seq, (t + 1) * n_seq)
        hr, hi = (ar * hr - ai * hi) + bu_scr[sl, :hw], (ar * hi + ai * hr) + bu_scr[sl, hw:]
        h_scr[sl, :hw] = hr.astype(h_scr.dtype)
        h_scr[sl, hw:] = hi.astype(h_scr.dtype)
    sre_ref[...] = hr
    sim_ref[...] = hi
    y = jnp.dot(h_scr[...], wc_ref[...], preferred_element_type=F32)
    g_ref[...] = jax.nn.gelu(y + d_ref[...] * xb)


def _half_ffn_matmuls(h, wg, wu, wd, *, bm):
    m, dm = h.shape
    dff = wg.shape[1]
    bn_up = _pick(dff, (256, 128))
    n_i, n_j = m // bm, dff // bn_up
    rb = LANES
    n_rb = dff // rb
    assert n_i * n_j >= n_rb, "not enough up-projection steps to cover the down-projection weight"
    wd_blk = lambda i, j: (jnp.minimum(i * n_j + j, n_rb - 1), 0)
    a, wd = pl.pallas_call(
        _ffn_up_kernel,
        grid=(n_i, n_j),
        in_specs=[pl.BlockSpec((bm, dm), lambda i, j: (i, 0)),
                  pl.BlockSpec((dm, bn_up), lambda i, j: (0, j)),
                  pl.BlockSpec((dm, bn_up), lambda i, j: (0, j)),
                  pl.BlockSpec((rb, dm), wd_blk)],
        out_specs=[pl.BlockSpec((bm, bn_up), lambda i, j: (i, j)),
                   pl.BlockSpec((rb, dm), wd_blk)],
        out_shape=[jax.ShapeDtypeStruct((m, dff), BF16), jax.ShapeDtypeStruct((dff, dm), BF16)],
        compiler_params=_cparams("arbitrary", "arbitrary"),
        name="ffn_up",
    )(h, wg, wu, wd)

    bm_d = _pick(m, (1024, 512, 256, 128))
    bn_d = _pick(dm, (1024, 512, 256, 128))
    bk = dff // 2 if (dff // 2) % LANES == 0 else dff
    return pl.pallas_call(
        _ffn_down_kernel,
        grid=(m // bm_d, dm // bn_d, dff // bk),
        in_specs=[pl.BlockSpec((bm_d, bk), lambda i, j, k: (i, k)),
                  pl.BlockSpec((bk, bn_d), lambda i, j, k: (k, j))],
        out_specs=pl.BlockSpec((bm_d, bn_d), lambda i, j, k: (i, j)),
        out_shape=jax.ShapeDtypeStruct((m, dm), F32),
        compiler_params=_cparams("parallel", "parallel", "arbitrary"),
        name="ffn_down",
    )(a, wd)


def _matmul(x, w, *, bm, bn, out_dtype, name, n=None):
    m, k = x.shape
    n = w.shape[1] if n is None else n
    return pl.pallas_call(
        _mm_kernel,
        grid=(m // bm, n // bn),
        in_specs=[pl.BlockSpec((bm, k), lambda i, j: (i, 0)),
                  pl.BlockSpec((k, bn), lambda i, j: (0, j))],
        out_specs=pl.BlockSpec((bm, bn), lambda i, j: (i, j)),
        out_shape=jax.ShapeDtypeStruct((m, n), out_dtype),
        compiler_params=_cparams("parallel", "arbitrary"),
        name=name,
    )(x, w)


def kernel(x_prompt, x_sample, state_lru_h, state_conv, state_ssm_re, state_ssm_im, ffn1_pre_g, ffn1_post_g, ffn1_w_gate, ffn1_w_up, ffn1_w_down, mix_pre_g, mix_post_g, w_in, conv_w, conv_b, w_rg, b_rg, w_ig, b_ig, lru_lambda, ssm_a_re, ssm_a_im, ssm_log_dt, ssm_b_re, ssm_b_im, ssm_c_re, ssm_c_im, ssm_d, w_glu, b_glu, w_out_a, w_out_b, w_o, ffn2_pre_g, ffn2_post_g, ffn2_w_gate, ffn2_w_up, ffn2_w_down):
    nb, seq, dm = x_prompt.shape
    db, dseq, _ = x_sample.shape
    depth = state_lru_h.shape[0]
    assert depth == 1, "one decoder layer"
    d_lru = state_lru_h.shape[2]
    n_grp, ssm_p = state_ssm_re.shape[2], state_ssm_re.shape[3]
    d_ssm = ssm_d.shape[1]
    ssm_cg = d_ssm // n_grp
    d_in = w_in.shape[2]
    assert w_rg.shape[2] == LRU_BLK and conv_w.shape[1] == CONV_W
    assert d_in == 2 * d_lru + d_ssm + 2 * dm and seq >= CONV_W - 1 and dseq >= CONV_W - 1

    mp = nb * seq
    ms = db * dseq
    m = mp + ms
    sdt = state_lru_h.dtype

    row2 = lambda v: v.reshape(1, -1)
    bf = lambda v: v[0].astype(BF16)

    xp2 = x_prompt.reshape(mp, dm)
    xs2 = jnp.swapaxes(x_sample, 0, 1).reshape(ms, dm)

    tr = _pick(math.gcd(mp, ms), (256, 128, 64, 32, 16, 8))
    npt = mp // tr
    xp_spec = pl.BlockSpec((tr, dm), lambda i: (jnp.minimum(i, npt - 1), 0))
    xs_spec = pl.BlockSpec((tr, dm), lambda i: (jnp.maximum(i - npt, 0), 0))
    row_spec = pl.BlockSpec((tr, dm), lambda i: (i, 0))
    g_spec = pl.BlockSpec((1, dm), lambda i: (0, 0))

    h1 = pl.pallas_call(
        functools.partial(_prenorm_kernel, n_prompt_tiles=npt),
        grid=(m // tr,),
        in_specs=[xp_spec, xs_spec, g_spec],
        out_specs=row_spec,
        out_shape=jax.ShapeDtypeStruct((m, dm), BF16),
        compiler_params=_cparams("parallel"),
        name="prenorm1",
    )(xp2, xs2, row2(ffn1_pre_g))

    bm = _pick(m, (1536, 1024, 768, 512, 384, 256, 128))
    f1 = _half_ffn_matmuls(h1, ffn1_w_gate[0], ffn1_w_up[0], ffn1_w_down[0], bm=bm)

    x1, u = pl.pallas_call(
        functools.partial(_post_ffn1_kernel, n_prompt_tiles=npt),
        grid=(m // tr,),
        in_specs=[xp_spec, xs_spec, row_spec, g_spec, g_spec],
        out_specs=[row_spec, row_spec],
        out_shape=[jax.ShapeDtypeStruct((m, dm), F32), jax.ShapeDtypeStruct((m, dm), BF16)],
        compiler_params=_cparams("parallel"),
        name="post_ffn1",
    )(xp2, xs2, f1, row2(ffn1_post_g), row2(mix_pre_g))

    bn_in = _pick(math.gcd(d_lru, dm), (512, 256, 128))
    n_a = 2 * d_lru + d_ssm
    z = _matmul(u, w_in[0], bm=bm, bn=bn_in, out_dtype=F32, name="in_proj", n=n_a)

    cb = _pick(d_lru, (512, 256, 128))
    ncb = d_lru // cb
    hpb = cb // LRU_BLK
    wrg = bf(w_rg)
    wig = bf(w_ig)
    lru_params = (conv_w[0], row2(conv_b), wrg, row2(b_rg), wig, row2(b_ig), row2(lru_lambda))
    assert d_lru <= dm and d_ssm <= dm
    any_spec = pl.BlockSpec(memory_space=pl.ANY)
    assert mp % ms == 0, "sample rows must tile the unified row axis"
    s_blk = mp // ms

    n_lags = 2
    abr, abi, cfr, cfi = pl.pallas_call(
        _s5_disc_kernel,
        out_shape=[jax.ShapeDtypeStruct((n_grp, ssm_p), F32)] * 2
                  + [jax.ShapeDtypeStruct((n_lags, n_grp, ssm_p), F32)] * 2,
        name="s5_discretise",
    )(ssm_a_re[0], ssm_a_im[0], ssm_log_dt[0].reshape(n_grp, 1))
    bb_re = cfr[..., None] * ssm_b_re - cfi[..., None] * ssm_b_im
    bb_im = cfr[..., None] * ssm_b_im + cfi[..., None] * ssm_b_re

    gpb = LANES // ssm_cg
    nj = n_grp // gpb
    hw = gpb * ssm_p
    bb =jnp.stack([bb_re, bb_im]).reshape(2, n_lags, nj, gpb, ssm_p, ssm_cg).astype(BF16)
    bb = jnp.transpose(bb, (2, 1, 3, 5, 0, 4)).reshape(nj, n_lags * LANES, 2, ssm_p)
    assert LANES % ssm_p == 0 and gpb % (LANES // ssm_p) == 0
    hpl = LANES // ssm_p
    bb = jnp.concatenate([bb] * hpl, axis=3)
    wb = jnp.concatenate([bb[:, :, ri, :] for ri in range(2) for _ in range(gpb // hpl)], axis=2)
    wb_shape = (1, n_lags * LANES, 2 * hw)
    g_row = (lax.broadcasted_iota(jnp.int32, wb_shape, 1) // ssm_cg) % gpb
    h_col = (lax.broadcasted_iota(jnp.int32, wb_shape, 2) // ssm_p) % gpb
    wb = jnp.where(g_row == h_col, wb, jnp.zeros((), BF16))
    cc = jnp.stack([ssm_c_re[0], -ssm_c_im[0]]).reshape(2, nj, gpb, ssm_cg, ssm_p).astype(BF16)
    cc = jnp.transpose(cc, (1, 0, 4, 2, 3)).reshape(nj, 2, ssm_p, LANES)
    wc = jnp.concatenate([cc[:, ri, :, :] for ri in range(2) for _ in range(gpb)], axis=1)
    wc_shape = (1, 2 * hw, LANES)
    wc_mask = ((lax.broadcasted_iota(jnp.int32, wc_shape, 1) // ssm_p) % gpb
               == lax.broadcasted_iota(jnp.int32, wc_shape, 2) // ssm_cg)
    wc = jnp.where(wc_mask, wc, jnp.zeros((), BF16))
    abr3 = abr.reshape(nj, 1, hw)
    abi3 = abi.reshape(nj, 1, hw)
    xb_blk = (2 * d_lru) // LANES
    dsk = row2(ssm_d)

    n_par = 2 if nj % 2 == 0 and xb_blk % 2 == 0 else 1
    par_shape = (1, n_par * 2 * hw, n_par * LANES)
    par_mask = (lax.broadcasted_iota(jnp.int32, par_shape, 1) // (2 * hw)
                == lax.broadcasted_iota(jnp.int32, par_shape, 2) // LANES)
    wc_par = jnp.concatenate([wc.reshape(nj // n_par, n_par * 2 * hw, LANES)] * n_par, axis=2)
    wc_par = jnp.where(par_mask, wc_par, jnp.zeros((), BF16))
    n_gi = nj // n_par
    bm_gl = m // n_gi
    assert m % n_gi == 0 and bm_gl % (2 * SUBLANES) == 0

    def gl_cols_ok(t):
        bn = (2 * dm) // (nb * (seq // t)) if (2 * dm) % (nb * (seq // t)) == 0 else 0
        return bn > 0 and bn % LANES == 0 and n_a % bn == 0

    tm5 = _pick(seq, tuple(t for t in (512, 256, 128, 64, 32, 16, 8) if seq % t == 0 and gl_cols_ok(t)))
    rp5 = seq // tm5
    bn_gl = (2 * dm) // (nb * rp5)
    gl_blk = n_a // bn_gl
    gl, g_all, sre_p, sim_p = pl.pallas_call(
        _gates_s5_prompt_kernel,
        grid=(n_gi, nb, rp5),
        in_specs=[pl.BlockSpec((bm_gl, dm), lambda j, n, r: (j, 0)),
                  pl.BlockSpec((dm, bn_gl), lambda j, n, r: (0, gl_blk + n * rp5 + r)),
                  pl.BlockSpec((tm5, n_par * LANES), lambda j, n, r: (n * rp5 + r, xb_blk // n_par + j)),
                  pl.BlockSpec((n_par, n_lags * LANES, 2 * hw), lambda j, n, r: (j, 0, 0)),
                  pl.BlockSpec((None, n_par * 2 * hw, n_par * LANES), lambda j, n, r: (j, 0, 0)),
                  pl.BlockSpec((n_par, 1, hw), lambda j, n, r: (j, 0, 0)),
                  pl.BlockSpec((n_par, 1, hw), lambda j, n, r: (j, 0, 0)),
                  pl.BlockSpec((1, n_par * LANES), lambda j, n, r: (0, j)), any_spec],
        out_specs=[pl.BlockSpec((bm_gl, bn_gl), lambda j, n, r: (j, n * rp5 + r)),
                   pl.BlockSpec((tm5, n_par * LANES), lambda j, n, r: (n * rp5 + r, j)),
                   pl.BlockSpec((None, 1, n_par * hw), lambda j, n, r: (n, 0, j)),
                   pl.BlockSpec((None, 1, n_par * hw), lambda j, n, r: (n, 0, j))],
        out_shape=[jax.ShapeDtypeStruct((m, 2 * dm), F32),
                   jax.ShapeDtypeStruct((m, dm), F32),
                   jax.ShapeDtypeStruct((nb, 1, n_grp * ssm_p), F32),
                   jax.ShapeDtypeStruct((nb, 1, n_grp * ssm_p), F32)],
        scratch_shapes=[pltpu.VMEM((n_par, 1, hw), F32), pltpu.VMEM((n_par, 1, hw), F32)],
        input_output_aliases={8: 1},
        compiler_params=_cparams("arbitrary", "arbitrary", "arbitrary"),
        name="gates_s5_prompt",
    )(u, w_in[0], z, wb, wc_par, abr3, abi3, dsk, f1)

    s5_w_specs1 = [pl.BlockSpec((None, LANES, 2 * hw), lambda j: (j, 0, 0)),
                   pl.BlockSpec((None, 2 * hw, LANES), lambda j: (j, 0, 0)),
                   pl.BlockSpec((None, 1, hw), lambda j: (j, 0, 0)),
                   pl.BlockSpec((None, 1, hw), lambda j: (j, 0, 0)),
                   pl.BlockSpec((1, LANES), lambda j: (0, j))]
    g_all, sre_s, sim_s = pl.pallas_call(
        functools.partial(_s5_sample_kernel, n_seq=db, n_steps=dseq),
        grid=(nj,),
        in_specs=[pl.BlockSpec((ms, LANES), lambda j: (s_blk, xb_blk + j))] + s5_w_specs1
                 + [pl.BlockSpec((db, hw), lambda j: (0, j)), pl.BlockSpec((db, hw), lambda j: (0, j)), any_spec],
        out_specs=[pl.BlockSpec((ms, LANES), lambda j: (s_blk, j)),
                   pl.BlockSpec((db, hw), lambda j: (0, j)),
                   pl.BlockSpec((db, hw), lambda j: (0, j))],
        out_shape=[jax.ShapeDtypeStruct((m, dm), F32),
                   jax.ShapeDtypeStruct((db, n_grp * ssm_p), F32),
                   jax.ShapeDtypeStruct((db, n_grp * ssm_p), F32)],
        scratch_shapes=[pltpu.VMEM((ms, 2 * hw), F32), pltpu.VMEM((ms, 2 * hw), BF16)],
        input_output_aliases={8: 0},
        compiler_params=_cparams("parallel"),
        name="s5_sample",
    )(z, wb, wc, abr3, abi3, dsk,
      state_ssm_re[0].reshape(db, n_grp * ssm_p), state_ssm_im[0].reshape(db, n_grp * ssm_p), g_all)

    assert d_ssm // cb == ncb
    tm = _pick(seq, tuple(t for t in (512, 256, 128, 64, 32, 16, 8)
                          if seq % t == 0 and (m * t) % mp == 0 and ((m * t) // mp) % (2 * SUBLANES) == 0))
    rpt = seq // tm
    bm_u = (m * tm) // mp
    col = lambda shape: pl.BlockSpec(shape, lambda i, j: (0, j))
    head_spec = pl.BlockSpec((hpb, LRU_BLK, LRU_BLK), lambda i, j: (j, 0, 0))
    yb, ya, hl_p = pl.pallas_call(
        functools.partial(_glu_rglru_prompt_kernel, tiles_per_seq=rpt),
        grid=(nb * rpt, ncb),
        in_specs=[pl.BlockSpec((bm_u, d_ssm), lambda i, j: (i, 0)),
                  pl.BlockSpec((bm_u, cb), lambda i, j: (i, j)),
                  pl.BlockSpec((d_ssm, cb), lambda i, j: (0, j)),
                  col((1, cb)),
                  pl.BlockSpec((tm, cb), lambda i, j: (i, j)),
                  pl.BlockSpec((tm, cb), lambda i, j: (i, ncb + j)),
                  col((CONV_W, cb)), col((1, cb)), head_spec, col((1, cb)), head_spec, col((1, cb)),
                  col((1, cb)), any_spec],
        out_specs=[pl.BlockSpec((bm_u, cb), lambda i, j: (i, j)),
                   pl.BlockSpec((tm, cb), lambda i, j: (i, j)),
                   pl.BlockSpec((None, 1, cb), lambda i, j: (i, 0, j))],
        out_shape=[jax.ShapeDtypeStruct((m, d_ssm), BF16),
                   jax.ShapeDtypeStruct((m, dm), BF16),
                   jax.ShapeDtypeStruct((nb * rpt, 1, d_lru), F32)],
        scratch_shapes=[pltpu.VMEM((bm_u, d_ssm), BF16),
                        pltpu.VMEM((ncb, SUBLANES, cb), F32), pltpu.VMEM((ncb, 1, cb), F32)],
        input_output_aliases={4 + 2 + len(lru_params): 1},
        compiler_params=_cparams("arbitrary", "arbitrary"),
        name="glu_rglru_prompt",
    )(g_all, g_all, w_glu[0], row2(b_glu), z, z, *lru_params, h1)

    par1 = lambda shape: pl.BlockSpec(shape, lambda c: (0, c))
    cst = jnp.swapaxes(state_conv[0], 0, 1)
    ya, hl_s = pl.pallas_call(
        functools.partial(_rglru_sample_kernel, n_seq=db, n_steps=dseq),
        grid=(ncb,),
        in_specs=[pl.BlockSpec((ms, cb), lambda c: (s_blk, c)),
                  pl.BlockSpec((ms, cb), lambda c: (s_blk, ncb + c)),
                  pl.BlockSpec((CONV_W - 1, db, cb), lambda c: (0, 0, c)),
                  par1((db, cb)),
                  par1((CONV_W, cb)), par1((1, cb)),
                  pl.BlockSpec((hpb, LRU_BLK, LRU_BLK), lambda c: (c, 0, 0)), par1((1, cb)),
                  pl.BlockSpec((hpb, LRU_BLK, LRU_BLK), lambda c: (c, 0, 0)), par1((1, cb)),
                  par1((1, cb)), any_spec],
        out_specs=[pl.BlockSpec((ms, cb), lambda c: (s_blk, c)), par1((db, cb))],
        out_shape=[jax.ShapeDtypeStruct((m, dm), BF16), jax.ShapeDtypeStruct((db, d_lru), F32)],
        input_output_aliases={4 + len(lru_params): 0},
        compiler_params=_cparams("parallel"),
        name="rglru_sample",
    )(z, z, cst, state_lru_h[0], *lru_params, ya)

    bm_g = _pick(m, (1024, 512, 256, 128))
    bn_m = bn_in
    gla_blk = 0
    glb_blk = dm // bn_m
    merged = pl.pallas_call(
        _merge_kernel,
        grid=(m // bm_g, dm // bn_m),
        in_specs=[pl.BlockSpec((bm_g, d_lru), lambda i, j: (i, 0)),
                  pl.BlockSpec((bm_g, d_ssm), lambda i, j: (i, 0)),
                  pl.BlockSpec((d_lru, bn_m), lambda i, j: (0, j)),
                  pl.BlockSpec((d_ssm, bn_m), lambda i, j: (0, j)),
                  pl.BlockSpec((bm_g, bn_m), lambda i, j: (i, gla_blk + j)),
                  pl.BlockSpec((bm_g, bn_m), lambda i, j: (i, glb_blk + j))],
        out_specs=pl.BlockSpec((bm_g, bn_m), lambda i, j: (i, j)),
        out_shape=jax.ShapeDtypeStruct((m, dm), BF16),
        compiler_params=_cparams("parallel", "arbitrary"),
        name="gated_merge",
    )(ya, yb, w_out_a[0], w_out_b[0], gl, gl)

    o = _matmul(merged, w_o[0], bm=bm, bn=bn_in, out_dtype=F32, name="o_proj")

    x2, h2 = pl.pallas_call(
        _post_mix_kernel,
        grid=(m // tr,),
        in_specs=[row_spec, row_spec, g_spec, g_spec],
        out_specs=[row_spec, row_spec],
        out_shape=[jax.ShapeDtypeStruct((m, dm), F32), jax.ShapeDtypeStruct((m, dm), BF16)],
        compiler_params=_cparams("parallel"),
        name="post_mix",
    )(x1, o, row2(mix_post_g), row2(ffn2_pre_g))

    f2 = _half_ffn_matmuls(h2, ffn2_w_gate[0], ffn2_w_up[0], ffn2_w_down[0], bm=bm)

    def final(rows, first_tile):
        return pl.pallas_call(
            _final_kernel,
            grid=(rows // tr,),
            in_specs=[pl.BlockSpec((tr, dm), lambda i: (first_tile + i, 0)),
                      pl.BlockSpec((tr, dm), lambda i: (first_tile + i, 0)),
                      g_spec],
            out_specs=row_spec,
            out_shape=jax.ShapeDtypeStruct((rows, dm), F32),
            compiler_params=_cparams("parallel"),
            name="final_residual",
        )(x2, f2, row2(ffn2_post_g))

    y_prompt = final(mp, 0).reshape(nb, seq, dm)
    y_sample = jnp.swapaxes(final(ms, npt).reshape(dseq, db, dm), 0, 1)

    nk = CONV_W - 1
    prompt_conv = jnp.stack([lax.slice(z, ((n + 1) * seq - nk, 0), ((n + 1) * seq, d_lru)) for n in range(nb)])
    sample_conv = jnp.swapaxes(lax.slice(z, (m - nk * db, 0), (m, d_lru)).reshape(nk, db, d_lru), 0, 1)
    st = lambda v, n: v.reshape(1, n, n_grp, ssm_p).astype(sdt)
    return (y_prompt, y_sample,
            hl_p.reshape(nb, rpt, d_lru)[:, rpt - 1][None].astype(sdt), prompt_conv[None].astype(sdt),
            st(sre_p, nb), st(sim_p, nb),
            hl_s.reshape(1, db, d_lru).astype(sdt), sample_conv[None].astype(sdt),
            st(sre_s, db), st(sim_s, db))
```

```python
import functools
import math

import jax
import jax.numpy as jnp
from jax import lax
from jax.experimental import pallas as pl
from jax.experimental.pallas import tpu as pltpu

F32 = jnp.float32
BF16 = jnp.bfloat16

EPS = 1e-6
C_RG = 8.0
CONV_W = 4
LANES = 128
SUBLANES = 8
LRU_BLK = 128
VMEM_LIMIT = 56 * 1024 * 1024


def _cparams(*sem):
    return pltpu.CompilerParams(dimension_semantics=sem, vmem_limit_bytes=VMEM_LIMIT)


def _pick(n, candidates):
    for c in candidates:
        if n % c == 0:
            return c
    raise ValueError(f"no tile in {candidates} divides {n}")


def _rms(x, g):
    return x * lax.rsqrt(jnp.mean(x * x, axis=-1, keepdims=True) + EPS) * g


def _softplus(x):
    return jnp.maximum(x, 0.0) + jnp.log1p(jnp.exp(-jnp.abs(x)))


def _two_group_rows(i, n_prompt_tiles, xp_ref, xs_ref, body):
    @pl.when(i < n_prompt_tiles)
    def _():
        body(xp_ref[...])

    @pl.when(i >= n_prompt_tiles)
    def _():
        body(xs_ref[...])


def _prenorm_kernel(xp_ref, xs_ref, g_ref, h_ref, *, n_prompt_tiles):
    def body(x):
        h_ref[...] = _rms(x, g_ref[...]).astype(h_ref.dtype)

    _two_group_rows(pl.program_id(0), n_prompt_tiles, xp_ref, xs_ref, body)


def _post_ffn1_kernel(xp_ref, xs_ref, f_ref, gpost_ref, gpre_ref, x1_ref, u_ref, *, n_prompt_tiles):
    def body(x):
        x1 = x + 0.5 * _rms(f_ref[...], gpost_ref[...])
        x1_ref[...] = x1
        u_ref[...] = _rms(x1, gpre_ref[...]).astype(u_ref.dtype)

    _two_group_rows(pl.program_id(0), n_prompt_tiles, xp_ref, xs_ref, body)


def _post_mix_kernel(x1_ref, o_ref, gpost_ref, gpre_ref, x2_ref, h_ref):
    x2 = x1_ref[...] + _rms(o_ref[...], gpost_ref[...])
    x2_ref[...] = x2
    h_ref[...] = _rms(x2, gpre_ref[...]).astype(h_ref.dtype)


def _final_kernel(x2_ref, f_ref, g_ref, y_ref):
    y_ref[...] = x2_ref[...] + 0.5 * _rms(f_ref[...], g_ref[...])


def _ffn_up_kernel(h_ref, wg_ref, wu_ref, wd_ref, a_ref, wd16_ref):
    wg = wg_ref[...].astype(BF16)
    wu = wu_ref[...].astype(BF16)
    half = h_ref.shape[0] // 2
    for rows in (slice(0, half), slice(half, None)):
        h = h_ref[rows, :]
        g = jnp.dot(h, wg, preferred_element_type=F32)
        u = jnp.dot(h, wu, preferred_element_type=F32)
        a_ref[rows, :] = (jax.nn.silu(g) * u).astype(a_ref.dtype)
    wd16_ref[...] = wd_ref[...].astype(wd16_ref.dtype)


def _ffn_down_kernel(a_ref, wd_ref, o_ref):
    @pl.when(pl.program_id(2) == 0)
    def _():
        o_ref[...] = jnp.dot(a_ref[...], wd_ref[...], preferred_element_type=F32)

    @pl.when(pl.program_id(2) != 0)
    def _():
        o_ref[...] += jnp.dot(a_ref[...], wd_ref[...], preferred_element_type=F32)


def _mm_kernel(x_ref, w_ref, o_ref):
    w = w_ref[...].astype(BF16)
    o_ref[...] = jnp.dot(x_ref[...], w, preferred_element_type=F32).astype(o_ref.dtype)


def _merge_kernel(ya_ref, yb_ref, wa_ref, wb_ref, gla_ref, glb_ref, o_ref):
    wa = wa_ref[...].astype(BF16)
    wb = wb_ref[...].astype(BF16)
    half = o_ref.shape[0] // 2
    for rows in (slice(0, half), slice(half, None)):
        pa = jnp.dot(ya_ref[rows, :], wa, preferred_element_type=F32)
        pb = jnp.dot(yb_ref[rows, :], wb, preferred_element_type=F32)
        m = jax.nn.sigmoid(gla_ref[rows, :]) * pa + jax.nn.sigmoid(glb_ref[rows, :]) * pb
        o_ref[rows, :] = m.astype(o_ref.dtype)


def _lru_gates(xc, wrg_ref, brg_ref, wig_ref, big_ref, lam_ref):
    xcb = xc.astype(BF16)
    rs, gs = [], []
    for hh in range(wrg_ref.shape[0]):
        xh = xcb[:, hh * LRU_BLK:(hh + 1) * LRU_BLK]
        rs.append(jnp.dot(xh, wrg_ref[hh], preferred_element_type=F32))
        gs.append(jnp.dot(xh, wig_ref[hh], preferred_element_type=F32))
    r = jax.nn.sigmoid(jnp.concatenate(rs, axis=1) + brg_ref[...])
    i = jax.nn.sigmoid(jnp.concatenate(gs, axis=1) + big_ref[...])
    log_a = -C_RG * r * _softplus(-lam_ref[...])
    a = jnp.exp(log_a)
    mult = jnp.sqrt(-jnp.tanh(log_a) * (a * a + 1.0))
    return a, mult * (i * xc)


def _glu_rglru_prompt_kernel(g_ref, gcol_ref, w_ref, b_ref,
                             xa_ref, ga_ref, cw_ref, cb_ref, wrg_ref, brg_ref, wig_ref, big_ref, lam_ref,
                             ya_all_ref, o_ref, ya_ref, hl_ref, g_scr, xcars, hcars, *, tiles_per_seq):
    del ya_all_ref
    i, j = pl.program_id(0), pl.program_id(1)

    @pl.when(j == 0)
    def _():
        g_scr[...] = g_ref[...].astype(g_scr.dtype)

    _rglru_prompt_tile(xa_ref, ga_ref, cw_ref, cb_ref, wrg_ref, brg_ref, wig_ref, big_ref, lam_ref,
                       ya_ref, hl_ref, xcars.at[j], hcars.at[j], i % tiles_per_seq == 0)
    s = jnp.dot(g_scr[...], w_ref[...].astype(BF16), preferred_element_type=F32) + b_ref[...]
    o_ref[...] = (gcol_ref[...] * jax.nn.sigmoid(s)).astype(o_ref.dtype)


def _rglru_prompt_tile(xa_ref, ga_ref, cw_ref, cb_ref, wrg_ref, brg_ref, wig_ref, big_ref, lam_ref,
                       ya_ref, hl_ref, xcar, hcar, starts_sequence):
    tm = xa_ref.shape[0]

    @pl.when(starts_sequence)
    def _():
        xcar[...] = jnp.zeros_like(xcar)
        hcar[...] = jnp.zeros_like(hcar)

    x = xa_ref[...]
    xfull = jnp.concatenate([xcar[...], x], axis=0)
    xcar[...] = x[tm - SUBLANES:, :]
    cw = cw_ref[...]
    xc = cb_ref[...]
    for k in range(CONV_W - 1):
        xs = pltpu.roll(xfull, CONV_W - 1 - k, 0)[SUBLANES:, :]
        xc = xc + xs * cw[k:k + 1, :]
    xc = xc + x * cw[CONV_W - 1:CONV_W, :]

    a, b = _lru_gates(xc, wrg_ref, brg_ref, wig_ref, big_ref, lam_ref)
    cb = a.shape[1]
    n_slab = tm // SUBLANES
    a = a.reshape(n_slab, SUBLANES, cb)
    b = b.reshape(n_slab, SUBLANES, cb)
    sub = lax.broadcasted_iota(jnp.int32, (1, SUBLANES, cb), 1)
    d = 1
    while d < SUBLANES:
        m = sub >= d
        a_s = jnp.where(m, pltpu.roll(a, d, 1), 1.0)
        b_s = jnp.where(m, pltpu.roll(b, d, 1), 0.0)
        b = a * b_s + b
        a = a * a_s
        d *= 2
    h = hcar[...]
    hs = []
    for s in range(n_slab):
        h_slab = b[s] + a[s] * jnp.broadcast_to(h, (SUBLANES, cb))
        h = h_slab[SUBLANES - 1:SUBLANES, :]
        hs.append(h_slab)
    hcar[...] = h
    hl_ref[...] = h
    ya_ref[...] = (jnp.concatenate(hs, axis=0) * jax.nn.gelu(ga_ref[...])).astype(ya_ref.dtype)


def _rglru_sample_kernel(xa_ref, ga_ref, cst_ref, h0_ref, cw_ref, cb_ref, wrg_ref, brg_ref, wig_ref,
                         big_ref, lam_ref, ya_all_ref, ya_ref, hl_ref, *, n_seq, n_steps):
    del ya_all_ref
    x = xa_ref[...]
    xp = jnp.concatenate([cst_ref[k] for k in range(CONV_W - 1)] + [x], axis=0)
    rows = n_seq * n_steps
    cw = cw_ref[...]
    xc = cb_ref[...]
    for k in range(CONV_W):
        xc = xc + xp[k * n_seq:k * n_seq + rows, :] * cw[k:k + 1, :]
    a, b = _lru_gates(xc, wrg_ref, brg_ref, wig_ref, big_ref, lam_ref)
    h = h0_ref[...]
    for t in range(n_steps):
        sl = slice(t * n_seq, (t + 1) * n_seq)
        h = a[sl, :] * h + b[sl, :]
        ya_ref[sl, :] = (h * jax.nn.gelu(ga_ref[sl, :])).astype(ya_ref.dtype)
    hl_ref[...] = h


def _s5_disc_kernel(are_ref, aim_ref, ldt_ref, abr_ref, abi_ref, cfr_ref, cfi_ref):
    a_re = are_ref[...]
    a_im = aim_ref[...]
    dt = jnp.exp(ldt_ref[...])
    mag = jnp.exp(a_re * dt)
    abr = mag * jnp.cos(a_im * dt)
    abi = mag * jnp.sin(a_im * dt)
    den = a_re * a_re + a_im * a_im
    nr = abr - 1.0
    abr_ref[...] = abr
    abi_ref[...] = abi
    cr = (nr * a_re + abi * a_im) / den
    ci = (abi * a_re - nr * a_im) / den
    for q in range(cfr_ref.shape[0]):
        cfr_ref[q] = cr
        cfi_ref[q] = ci
        cr, ci = abr * cr - abi * ci, abr * ci + abi * cr


def _gates_s5_prompt_kernel(u_ref, w_ref, xb_ref, wb_ref, wc_ref, ar_ref, ai_ref, d_ref, g_all_ref,
                            gl_ref, g_ref, sre_ref, sim_ref, cre, cim):
    del g_all_ref
    n_par, _, hw = ar_ref.shape

    @pl.when(pl.program_id(2) == 0)
    def _():
        cre[...] = jnp.zeros_like(cre)
        cim[...] = jnp.zeros_like(cim)

    split = (gl_ref.shape[0] * 9 // 16) // (2 * SUBLANES) * (2 * SUBLANES)
    w = w_ref[...].astype(BF16)
    bus = [_s5_project_in(xb_ref[:, p * LANES:(p + 1) * LANES], wb_ref.at[p]) for p in range(n_par)]
    gl_ref[:split, :] = jnp.dot(u_ref[:split, :], w, preferred_element_type=F32)
    scans = [_s5_slab_scan(bus[p], ar_ref[p], ai_ref[p], cre[p], cim[p]) for p in range(n_par)]
    y = jnp.dot(jnp.concatenate([s[0] for s in scans], axis=1), wc_ref[...], preferred_element_type=F32)
    gl_ref[split:, :] = jnp.dot(u_ref[split:, :], w, preferred_element_type=F32)
    g_ref[...] = jax.nn.gelu(y + d_ref[...] * xb_ref[...])
    for p in range(n_par):
        _, cr, ci = scans[p]
        cre[p] = cr
        cim[p] = ci
        sre_ref[:, p * hw:(p + 1) * hw] = cr
        sim_ref[:, p * hw:(p + 1) * hw] = ci


def _s5_project_in(xb, wb_ref):
    tm, width = xb.shape
    n_slab = tm // SUBLANES
    n_lags = wb_ref.shape[0] // width
    xb3 = xb.reshape(n_slab, SUBLANES, width)
    subx = lax.broadcasted_iota(jnp.int32, (1, SUBLANES, width), 1)
    lagged = [xb3] + [jnp.where(subx >= q, pltpu.roll(xb3, q, 1), 0.0) for q in range(1, n_lags)]
    xs = jnp.concatenate(lagged, axis=2).reshape(tm, n_lags * width)
    return jnp.dot(xs.astype(BF16), wb_ref[...], preferred_element_type=F32), n_lags


def _s5_slab_scan(bu_lags, ar, ai, cr, ci):
    bu, n_lags = bu_lags
    tm = bu.shape[0]
    hw = ar.shape[1]
    n_slab = tm // SUBLANES
    sub = lax.broadcasted_iota(jnp.int32, (SUBLANES, hw), 0)
    pr, pi = ar, ai
    tr = jnp.where(sub == 0, pr, 0.0)
    ti = jnp.where(sub == 0, pi, 0.0)
    steps = []
    d = 1
    while d < SUBLANES:
        mr = jnp.where(sub >= d, pr, 0.0)
        mi = jnp.where(sub >= d, pi, 0.0)
        if d >= n_lags:
            steps.append((d, mr, mi))
        sr = pltpu.roll(tr, d, 0)
        si = pltpu.roll(ti, d, 0)
        tr, ti = tr + (mr * sr - mi * si), ti + (mr * si + mi * sr)
        pr, pi = pr * pr - pi * pi, 2.0 * (pr * pi)
        d *= 2
    slabs = []
    for s in range(n_slab):
        rows = slice(s * SUBLANES, (s + 1) * SUBLANES)
        hr = bu[rows, :hw]
        hi = bu[rows, hw:]
        for d, mr, mi in steps:
            sr = pltpu.roll(hr, d, 0)
            si = pltpu.roll(hi, d, 0)
            hr, hi = hr + (mr * sr - mi * si), hi + (mr * si + mi * sr)
        cbr = jnp.broadcast_to(cr, (SUBLANES, hw))
        cbi = jnp.broadcast_to(ci, (SUBLANES, hw))
        hr, hi = hr + (tr * cbr - ti * cbi), hi + (tr * cbi + ti * cbr)
        cr = hr[SUBLANES - 1:SUBLANES, :]
        ci = hi[SUBLANES - 1:SUBLANES, :]
        slabs.append(jnp.concatenate([hr, hi], axis=1))
    return jnp.concatenate(slabs, axis=0).astype(BF16), cr, ci


def _s5_sample_kernel(xb_ref, wb_ref, wc_ref, ar_ref, ai_ref, d_ref, s0r_ref, s0i_ref, g_all_ref,
                      g_ref, sre_ref, sim_ref, bu_scr, h_scr, *, n_seq, n_steps):
    del g_all_ref
    hw = ar_ref.shape[1]
    xb = xb_ref[...]
    bu_scr[...] = jnp.dot(xb.astype(BF16), wb_ref[...], preferred_element_type=F32)
    ar = ar_ref[...]
    ai = ai_ref[...]
    hr = s0r_ref[...]
    hi = s0i_ref[...]
    for t in range(n_steps):
        sl = slice(t * n_seq, (t + 1) * n_seq)
        hr, hi = (ar * hr - ai * hi) + bu_scr[sl, :hw], (ar * hi + ai * hr) + bu_scr[sl, hw:]
        h_scr[sl, :hw] = hr.astype(h_scr.dtype)
        h_scr[sl, hw:] = hi.astype(h_scr.dtype)
    sre_ref[...] = hr
    sim_ref[...] = hi
    y = jnp.dot(h_scr[...], wc_ref[...], preferred_element_type=F32)
    g_ref[...] = jax.nn.gelu(y + d_ref[...] * xb)


def _half_ffn_matmuls(h, wg, wu, wd, *, bm):
    m, dm = h.shape
    dff = wg.shape[1]
    bn_up = _pick(dff, (256, 128))
    n_i, n_j = m // bm, dff // bn_up
    rb = _pick(dff, tuple(r for r in (64, 128, 256, 512) if n_i * n_j >= dff // r))
    n_rb = dff // rb
    assert n_i * n_j >= n_rb, "not enough up-projection steps to cover the down-projection weight"
    wd_blk = lambda i, j: (jnp.minimum(i * n_j + j, n_rb - 1), 0)
    a, wd = pl.pallas_call(
        _ffn_up_kernel,
        grid=(n_i, n_j),
        in_specs=[pl.BlockSpec((bm, dm), lambda i, j: (i, 0)),
                  pl.BlockSpec((dm, bn_up), lambda i, j: (0, j)),
                  pl.BlockSpec((dm, bn_up), lambda i, j: (0, j)),
                  pl.BlockSpec((rb, dm), wd_blk)],
        out_specs=[pl.BlockSpec((bm, bn_up), lambda i, j: (i, j)),
                   pl.BlockSpec((rb, dm), wd_blk)],
        out_shape=[jax.ShapeDtypeStruct((m, dff), BF16), jax.ShapeDtypeStruct((dff, dm), BF16)],
        compiler_params=_cparams("arbitrary", "arbitrary"),
        name="ffn_up",
    )(h, wg, wu, wd)

    bm_d = _pick(m, (1024, 512, 256, 128))
    bn_d = _pick(dm, (1024, 512, 256, 128))
    bk = dff // 2 if (dff // 2) % LANES == 0 else dff
    return pl.pallas_call(
        _ffn_down_kernel,
        grid=(m // bm_d, dm // bn_d, dff // bk),
        in_specs=[pl.BlockSpec((bm_d, bk), lambda i, j, k: (i, k)),
                  pl.BlockSpec((bk, bn_d), lambda i, j, k: (k, j))],
        out_specs=pl.BlockSpec((bm_d, bn_d), lambda i, j, k: (i, j)),
        out_shape=jax.ShapeDtypeStruct((m, dm), F32),
        compiler_params=_cparams("parallel", "parallel", "arbitrary"),
        name="ffn_down",
    )(a, wd)


def _matmul(x, w, *, bm, bn, out_dtype, name, n=None):
    m, k = x.shape
    n = w.shape[1] if n is None else n
    return pl.pallas_call(
        _mm_kernel,
        grid=(m // bm, n // bn),
        in_specs=[pl.BlockSpec((bm, k), lambda i, j: (i, 0)),
                  pl.BlockSpec((k, bn), lambda i, j: (0, j))],
        out_specs=pl.BlockSpec((bm, bn), lambda i, j: (i, j)),
        out_shape=jax.ShapeDtypeStruct((m, n), out_dtype),
        compiler_params=_cparams("parallel", "arbitrary"),
        name=name,
    )(x, w)


def kernel(x_prompt, x_sample, state_lru_h, state_conv, state_ssm_re, state_ssm_im, ffn1_pre_g, ffn1_post_g, ffn1_w_gate, ffn1_w_up, ffn1_w_down, mix_pre_g, mix_post_g, w_in, conv_w, conv_b, w_rg, b_rg, w_ig, b_ig, lru_lambda, ssm_a_re, ssm_a_im, ssm_log_dt, ssm_b_re, ssm_b_im, ssm_c_re, ssm_c_im, ssm_d, w_glu, b_glu, w_out_a, w_out_b, w_o, ffn2_pre_g, ffn2_post_g, ffn2_w_gate, ffn2_w_up, ffn2_w_down):
    nb, seq, dm = x_prompt.shape
    db, dseq, _ = x_sample.shape
    depth = state_lru_h.shape[0]
    assert depth == 1, "one decoder layer"
    d_lru = state_lru_h.shape[2]
    n_grp, ssm_p = state_ssm_re.shape[2], state_ssm_re.shape[3]
    d_ssm = ssm_d.shape[1]
    ssm_cg = d_ssm // n_grp
    d_in = w_in.shape[2]
    assert w_rg.shape[2] == LRU_BLK and conv_w.shape[1] == CONV_W
    assert d_in == 2 * d_lru + d_ssm + 2 * dm and seq >= CONV_W - 1 and dseq >= CONV_W - 1

    mp = nb * seq
    ms = db * dseq
    m = mp + ms
    sdt = state_lru_h.dtype

    row2 = lambda v: v.reshape(1, -1)
    bf = lambda v: v[0].astype(BF16)

    xp2 = x_prompt.reshape(mp, dm)
    xs2 = jnp.swapaxes(x_sample, 0, 1).reshape(ms, dm)

    tr = _pick(math.gcd(mp, ms), (256, 128, 64, 32, 16, 8))
    npt = mp // tr
    xp_spec = pl.BlockSpec((tr, dm), lambda i: (jnp.minimum(i, npt - 1), 0))
    xs_spec = pl.BlockSpec((tr, dm), lambda i: (jnp.maximum(i - npt, 0), 0))
    row_spec = pl.BlockSpec((tr, dm), lambda i: (i, 0))
    g_spec = pl.BlockSpec((1, dm), lambda i: (0, 0))

    h1 = pl.pallas_call(
        functools.partial(_prenorm_kernel, n_prompt_tiles=npt),
        grid=(m // tr,),
        in_specs=[xp_spec, xs_spec, g_spec],
        out_specs=row_spec,
        out_shape=jax.ShapeDtypeStruct((m, dm), BF16),
        compiler_params=_cparams("parallel"),
        name="prenorm1",
    )(xp2, xs2, row2(ffn1_pre_g))

    bm = _pick(m, (1536, 1024, 768, 512, 384, 256, 128))
    f1 = _half_ffn_matmuls(h1, ffn1_w_gate[0], ffn1_w_up[0], ffn1_w_down[0], bm=bm)

    x1, u = pl.pallas_call(
        functools.partial(_post_ffn1_kernel, n_prompt_tiles=npt),
        grid=(m // tr,),
        in_specs=[xp_spec, xs_spec, row_spec, g_spec, g_spec],
        out_specs=[row_spec, row_spec],
        out_shape=[jax.ShapeDtypeStruct((m, dm), F32), jax.ShapeDtypeStruct((m, dm), BF16)],
        compiler_params=_cparams("parallel"),
        name="post_ffn1",
    )(xp2, xs2, f1, row2(ffn1_post_g), row2(mix_pre_g))

    bn_in = _pick(math.gcd(d_lru, dm), (512, 256, 128))
    n_a = 2 * d_lru + d_ssm
    z = _matmul(u, w_in[0], bm=bm, bn=bn_in, out_dtype=F32, name="in_proj", n=n_a)

    cb = _pick(d_lru, (512, 256, 128))
    ncb = d_lru // cb
    hpb = cb // LRU_BLK
    wrg = bf(w_rg)
    wig = bf(w_ig)
    lru_params = (conv_w[0], row2(conv_b), wrg, row2(b_rg), wig, row2(b_ig), row2(lru_lambda))
    assert d_lru <= dm and d_ssm <= dm
    any_spec = pl.BlockSpec(memory_space=pl.ANY)
    assert mp % ms == 0, "sample rows must tile the unified row axis"
    s_blk = mp // ms

    n_lags = 2
    abr, abi, cfr, cfi = pl.pallas_call(
        _s5_disc_kernel,
        out_shape=[jax.ShapeDtypeStruct((n_grp, ssm_p), F32)] * 2
                  + [jax.ShapeDtypeStruct((n_lags, n_grp, ssm_p), F32)] * 2,
        name="s5_discretise",
    )(ssm_a_re[0], ssm_a_im[0], ssm_log_dt[0].reshape(n_grp, 1))
    bb_re = cfr[..., None] * ssm_b_re - cfi[..., None] * ssm_b_im
    bb_im = cfr[..., None] * ssm_b_im + cfi[..., None] * ssm_b_re

    gpb = LANES // ssm_cg
    nj = n_grp // gpb
    hw = gpb * ssm_p
    bb =jnp.stack([bb_re, bb_im]).reshape(2, n_lags, nj, gpb, ssm_p, ssm_cg).astype(BF16)
    bb = jnp.transpose(bb, (2, 1, 3, 5, 0, 4)).reshape(nj, n_lags * LANES, 2, ssm_p)
    assert LANES % ssm_p == 0 and gpb % (LANES // ssm_p) == 0
    hpl = LANES // ssm_p
    bb = jnp.concatenate([bb] * hpl, axis=3)
    wb = jnp.concatenate([bb[:, :, ri, :] for ri in range(2) for _ in range(gpb // hpl)], axis=2)
    wb_shape = (1, n_lags * LANES, 2 * hw)
    g_row = (lax.broadcasted_iota(jnp.int32, wb_shape, 1) // ssm_cg) % gpb
    h_col = (lax.broadcasted_iota(jnp.int32, wb_shape, 2) // ssm_p) % gpb
    wb = jnp.where(g_row == h_col, wb, jnp.zeros((), BF16))
    cc = jnp.stack([ssm_c_re[0], -ssm_c_im[0]]).reshape(2, nj, gpb, ssm_cg, ssm_p).astype(BF16)
    cc = jnp.transpose(cc, (1, 0, 4, 2, 3)).reshape(nj, 2, ssm_p, LANES)
    wc = jnp.concatenate([cc[:, ri, :, :] for ri in range(2) for _ in range(gpb)], axis=1)
    wc_shape = (1, 2 * hw, LANES)
    wc_mask = ((lax.broadcasted_iota(jnp.int32, wc_shape, 1) // ssm_p) % gpb
               == lax.broadcasted_iota(jnp.int32, wc_shape, 2) // ssm_cg)
    wc = jnp.where(wc_mask, wc, jnp.zeros((), BF16))
    abr3 = abr.reshape(nj, 1, hw)
    abi3 = abi.reshape(nj, 1, hw)
    xb_blk = (2 * d_lru) // LANES
    dsk = row2(ssm_d)

    n_par = 2 if nj % 2 == 0 and xb_blk % 2 == 0 else 1
    par_shape = (1, n_par * 2 * hw, n_par * LANES)
    par_mask = (lax.broadcasted_iota(jnp.int32, par_shape, 1) // (2 * hw)
                == lax.broadcasted_iota(jnp.int32, par_shape, 2) // LANES)
    wc_par = jnp.concatenate([wc.reshape(nj // n_par, n_par * 2 * hw, LANES)] * n_par, axis=2)
    wc_par = jnp.where(par_mask, wc_par, jnp.zeros((), BF16))
    n_gi = nj // n_par
    bm_gl = m // n_gi
    assert m % n_gi == 0 and bm_gl % (2 * SUBLANES) == 0

    def gl_cols_ok(t):
        bn = (2 * dm) // (nb * (seq // t)) if (2 * dm) % (nb * (seq // t)) == 0 else 0
        return bn > 0 and bn % LANES == 0 and n_a % bn == 0

    tm5 = _pick(seq, tuple(t for t in (512, 256, 128, 64, 32, 16, 8) if seq % t == 0 and gl_cols_ok(t)))
    rp5 = seq // tm5
    bn_gl = (2 * dm) // (nb * rp5)
    gl_blk = n_a // bn_gl
    gl, g_all, sre_p, sim_p = pl.pallas_call(
        _gates_s5_prompt_kernel,
        grid=(n_gi, nb, rp5),
        in_specs=[pl.BlockSpec((bm_gl, dm), lambda j, n, r: (j, 0)),
                  pl.BlockSpec((dm, bn_gl), lambda j, n, r: (0, gl_blk + n * rp5 + r)),
                  pl.BlockSpec((tm5, n_par * LANES), lambda j, n, r: (n * rp5 + r, xb_blk // n_par + j)),
                  pl.BlockSpec((n_par, n_lags * LANES, 2 * hw), lambda j, n, r: (j, 0, 0)),
                  pl.BlockSpec((None, n_par * 2 * hw, n_par * LANES), lambda j, n, r: (j, 0, 0)),
                  pl.BlockSpec((n_par, 1, hw), lambda j, n, r: (j, 0, 0)),
                  pl.BlockSpec((n_par, 1, hw), lambda j, n, r: (j, 0, 0)),
                  pl.BlockSpec((1, n_par * LANES), lambda j, n, r: (0, j)), any_spec],
        out_specs=[pl.BlockSpec((bm_gl, bn_gl), lambda j, n, r: (j, n * rp5 + r)),
                   pl.BlockSpec((tm5, n_par * LANES), lambda j, n, r: (n * rp5 + r, j)),
                   pl.BlockSpec((None, 1, n_par * hw), lambda j, n, r: (n, 0, j)),
                   pl.BlockSpec((None, 1, n_par * hw), lambda j, n, r: (n, 0, j))],
        out_shape=[jax.ShapeDtypeStruct((m, 2 * dm), F32),
                   jax.ShapeDtypeStruct((m, dm), F32),
                   jax.ShapeDtypeStruct((nb, 1, n_grp * ssm_p), F32),
                   jax.ShapeDtypeStruct((nb, 1, n_grp * ssm_p), F32)],
        scratch_shapes=[pltpu.VMEM((n_par, 1, hw), F32), pltpu.VMEM((n_par, 1, hw), F32)],
        input_output_aliases={8: 1},
        compiler_params=_cparams("arbitrary", "arbitrary", "arbitrary"),
        name="gates_s5_prompt",
    )(u, w_in[0], z, wb, wc_par, abr3, abi3, dsk, f1)

    s5_w_specs1 = [pl.BlockSpec((None, LANES, 2 * hw), lambda j: (j, 0, 0)),
                   pl.BlockSpec((None, 2 * hw, LANES), lambda j: (j, 0, 0)),
                   pl.BlockSpec((None, 1, hw), lambda j: (j, 0, 0)),
                   pl.BlockSpec((None, 1, hw), lambda j: (j, 0, 0)),
                   pl.BlockSpec((1, LANES), lambda j: (0, j))]
    g_all, sre_s, sim_s = pl.pallas_call(
        functools.partial(_s5_sample_kernel, n_seq=db, n_steps=dseq),
        grid=(nj,),
        in_specs=[pl.BlockSpec((ms, LANES), lambda j: (s_blk, xb_blk + j))] + s5_w_specs1
                 + [pl.BlockSpec((db, hw), lambda j: (0, j)), pl.BlockSpec((db, hw), lambda j: (0, j)), any_spec],
        out_specs=[pl.BlockSpec((ms, LANES), lambda j: (s_blk, j)),
                   pl.BlockSpec((db, hw), lambda j: (0, j)),
                   pl.BlockSpec((db, hw), lambda j: (0, j))],
        out_shape=[jax.ShapeDtypeStruct((m, dm), F32),
                   jax.ShapeDtypeStruct((db, n_grp * ssm_p), F32),
                   jax.ShapeDtypeStruct((db, n_grp * ssm_p), F32)],
        scratch_shapes=[pltpu.VMEM((ms, 2 * hw), F32), pltpu.VMEM((ms, 2 * hw), BF16)],
        input_output_aliases={8: 0},
        compiler_params=_cparams("parallel"),
        name="s5_sample",
    )(z, wb, wc, abr3, abi3, dsk,
      state_ssm_re[0].reshape(db, n_grp * ssm_p), state_ssm_im[0].reshape(db, n_grp * ssm_p), g_all)

    assert d_ssm // cb == ncb
    tm = _pick(seq, tuple(t for t in (512, 256, 128, 64, 32, 16, 8)
                          if seq % t == 0 and (m * t) % mp == 0 and ((m * t) // mp) % (2 * SUBLANES) == 0))
    rpt = seq // tm
    bm_u = (m * tm) // mp
    col = lambda shape: pl.BlockSpec(shape, lambda i, j: (0, j))
    head_spec = pl.BlockSpec((hpb, LRU_BLK, LRU_BLK), lambda i, j: (j, 0, 0))
    yb, ya, hl_p = pl.pallas_call(
        functools.partial(_glu_rglru_prompt_kernel, tiles_per_seq=rpt),
        grid=(nb * rpt, ncb),
        in_specs=[pl.BlockSpec((bm_u, d_ssm), lambda i, j: (i, 0)),
                  pl.BlockSpec((bm_u, cb), lambda i, j: (i, j)),
                  pl.BlockSpec((d_ssm, cb), lambda i, j: (0, j)),
                  col((1, cb)),
                  pl.BlockSpec((tm, cb), lambda i, j: (i, j)),
                  pl.BlockSpec((tm, cb), lambda i, j: (i, ncb + j)),
                  col((CONV_W, cb)), col((1, cb)), head_spec, col((1, cb)), head_spec, col((1, cb)),
                  col((1, cb)), any_spec],
        out_specs=[pl.BlockSpec((bm_u, cb), lambda i, j: (i, j)),
                   pl.BlockSpec((tm, cb), lambda i, j: (i, j)),
                   pl.BlockSpec((None, 1, cb), lambda i, j: (i, 0, j))],
        out_shape=[jax.ShapeDtypeStruct((m, d_ssm), BF16),
                   jax.ShapeDtypeStruct((m, dm), BF16),
                   jax.ShapeDtypeStruct((nb * rpt, 1, d_lru), F32)],
        scratch_shapes=[pltpu.VMEM((bm_u, d_ssm), BF16),
                        pltpu.VMEM((ncb, SUBLANES, cb), F32), pltpu.VMEM((ncb, 1, cb), F32)],
        input_output_aliases={4 + 2 + len(lru_params): 1},
        compiler_params=_cparams("arbitrary", "arbitrary"),
        name="glu_rglru_prompt",
    )(g_all, g_all, w_glu[0], row2(b_glu), z, z, *lru_params, h1)

    par1 = lambda shape: pl.BlockSpec(shape, lambda c: (0, c))
    cst = jnp.swapaxes(state_conv[0], 0, 1)
    ya, hl_s = pl.pallas_call(
        functools.partial(_rglru_sample_kernel, n_seq=db, n_steps=dseq),
        grid=(ncb,),
        in_specs=[pl.BlockSpec((ms, cb), lambda c: (s_blk, c)),
                  pl.BlockSpec((ms, cb), lambda c: (s_blk, ncb + c)),
                  pl.BlockSpec((CONV_W - 1, db, cb), lambda c: (0, 0, c)),
                  par1((db, cb)),
                  par1((CONV_W, cb)), par1((1, cb)),
                  pl.BlockSpec((hpb, LRU_BLK, LRU_BLK), lambda c: (c, 0, 0)), par1((1, cb)),
                  pl.BlockSpec((hpb, LRU_BLK, LRU_BLK), lambda c: (c, 0, 0)), par1((1, cb)),
                  par1((1, cb)), any_spec],
        out_specs=[pl.BlockSpec((ms, cb), lambda c: (s_blk, c)), par1((db, cb))],
        out_shape=[jax.ShapeDtypeStruct((m, dm), BF16), jax.ShapeDtypeStruct((db, d_lru), F32)],
        input_output_aliases={4 + len(lru_params): 0},
        compiler_params=_cparams("parallel"),
        name="rglru_sample",
    )(z, z, cst, state_lru_h[0], *lru_params, ya)

    bm_g = _pick(m, (1024, 512, 256, 128))
    bn_m = bn_in
    gla_blk = 0
    glb_blk = dm // bn_m
    merged = pl.pallas_call(
        _merge_kernel,
        grid=(m // bm_g, dm // bn_m),
        in_specs=[pl.BlockSpec((bm_g, d_lru), lambda i, j: (i, 0)),
                  pl.BlockSpec((bm_g, d_ssm), lambda i, j: (i, 0)),
                  pl.BlockSpec((d_lru, bn_m), lambda i, j: (0, j)),
                  pl.BlockSpec((d_ssm, bn_m), lambda i, j: (0, j)),
                  pl.BlockSpec((bm_g, bn_m), lambda i, j: (i, gla_blk + j)),
                  pl.BlockSpec((bm_g, bn_m), lambda i, j: (i, glb_blk + j))],
        out_specs=pl.BlockSpec((bm_g, bn_m), lambda i, j: (i, j)),
        out_shape=jax.ShapeDtypeStruct((m, dm), BF16),
        compiler_params=_cparams("parallel", "arbitrary"),
        name="gated_merge",
    )(ya, yb, w_out_a[0], w_out_b[0], gl, gl)

    o = _matmul(merged, w_o[0], bm=bm, bn=bn_in, out_dtype=F32, name="o_proj")

    x2, h2 = pl.pallas_call(
        _post_mix_kernel,
        grid=(m // tr,),
        in_specs=[row_spec, row_spec, g_spec, g_spec],
        out_specs=[row_spec, row_spec],
        out_shape=[jax.ShapeDtypeStruct((m, dm), F32), jax.ShapeDtypeStruct((m, dm), BF16)],
        compiler_params=_cparams("parallel"),
        name="post_mix",
    )(x1, o, row2(mix_post_g), row2(ffn2_pre_g))

    f2 = _half_ffn_matmuls(h2, ffn2_w_gate[0], ffn2_w_up[0], ffn2_w_down[0], bm=bm)

    def final(rows, first_tile):
        return pl.pallas_call(
            _final_kernel,
            grid=(rows // tr,),
            in_specs=[pl.BlockSpec((tr, dm), lambda i: (first_tile + i, 0)),
                      pl.BlockSpec((tr, dm), lambda i: (first_tile + i, 0)),
                      g_spec],
            out_specs=row_spec,
            out_shape=jax.ShapeDtypeStruct((rows, dm), F32),
            compiler_params=_cparams("parallel"),
            name="final_residual",
        )(x2, f2, row2(ffn2_post_g))

    y_prompt = final(mp, 0).reshape(nb, seq, dm)
    y_sample = jnp.swapaxes(final(ms, npt).reshape(dseq, db, dm), 0, 1)

    nk = CONV_W - 1
    prompt_conv = jnp.stack([lax.slice(z, ((n + 1) * seq - nk, 0), ((n + 1) * seq, d_lru)) for n in range(nb)])
    sample_conv = jnp.swapaxes(lax.slice(z, (m - nk * db, 0), (m, d_lru)).reshape(nk, db, d_lru), 0, 1)
    st = lambda v, n: v.reshape(1, n, n_grp, ssm_p).astype(sdt)
    return (y_prompt, y_sample,
            hl_p.reshape(nb, rpt, d_lru)[:, rpt - 1][None].astype(sdt), prompt_conv[None].astype(sdt),
            st(sre_p, nb), st(sim_p, nb),
            hl_s.reshape(1, db, d_lru).astype(sdt), sample_conv[None].astype(sdt),
            st(sre_s, db), st(sim_s, db))
```

```python
import functools
import math

import jax
import jax.numpy as jnp
from jax import lax
from jax.experimental import pallas as pl
from jax.experimental.pallas import tpu as pltpu

F32 = jnp.float32
BF16 = jnp.bfloat16

EPS = 1e-6
C_RG = 8.0
CONV_W = 4
LANES = 128
SUBLANES = 8
LRU_BLK = 128
VMEM_LIMIT = 56 * 1024 * 1024


def _cparams(*sem):
    return pltpu.CompilerParams(dimension_semantics=sem, vmem_limit_bytes=VMEM_LIMIT)


def _pick(n, candidates):
    for c in candidates:
        if n % c == 0:
            return c
    raise ValueError(f"no tile in {candidates} divides {n}")


def _rms(x, g):
    return x * lax.rsqrt(jnp.mean(x * x, axis=-1, keepdims=True) + EPS) * g


def _softplus(x):
    return jnp.maximum(x, 0.0) + jnp.log1p(jnp.exp(-jnp.abs(x)))


def _sigmoid(x):
    return 0.5 * jnp.tanh(0.5 * x) + 0.5


def _two_group_rows(i, n_prompt_tiles, xp_ref, xs_ref, body):
    @pl.when(i < n_prompt_tiles)
    def _():
        body(xp_ref[...])

    @pl.when(i >= n_prompt_tiles)
    def _():
        body(xs_ref[...])


def _prenorm_kernel(xp_ref, xs_ref, g_ref, h_ref, *, n_prompt_tiles):
    def body(x):
        h_ref[...] = _rms(x, g_ref[...]).astype(h_ref.dtype)

    _two_group_rows(pl.program_id(0), n_prompt_tiles, xp_ref, xs_ref, body)


def _post_ffn1_kernel(xp_ref, xs_ref, f_ref, gpost_ref, gpre_ref, x1_ref, u_ref, *, n_prompt_tiles):
    def body(x):
        x1 = x + 0.5 * _rms(f_ref[...], gpost_ref[...])
        x1_ref[...] = x1
        u_ref[...] = _rms(x1, gpre_ref[...]).astype(u_ref.dtype)

    _two_group_rows(pl.program_id(0), n_prompt_tiles, xp_ref, xs_ref, body)


def _post_mix_kernel(x1_ref, o_ref, gpost_ref, gpre_ref, x2_ref, h_ref):
    x2 = x1_ref[...] + _rms(o_ref[...], gpost_ref[...])
    x2_ref[...] = x2
    h_ref[...] = _rms(x2, gpre_ref[...]).astype(h_ref.dtype)


def _final_kernel(x2_ref, f_ref, g_ref, y_ref):
    y_ref[...] = x2_ref[...] + 0.5 * _rms(f_ref[...], g_ref[...])


def _ffn_up_kernel(h_ref, wg_ref, wu_ref, wd_ref, a_ref, wd16_ref):
    wg = wg_ref[...].astype(BF16)
    wu = wu_ref[...].astype(BF16)
    half = h_ref.shape[0] // 2
    for rows in (slice(0, half), slice(half, None)):
        h = h_ref[rows, :]
        g = jnp.dot(h, wg, preferred_element_type=F32)
        u = jnp.dot(h, wu, preferred_element_type=F32)
        a_ref[rows, :] = (g * _sigmoid(g) * u).astype(a_ref.dtype)
    wd16_ref[...] = wd_ref[...].astype(wd16_ref.dtype)


def _ffn_down_kernel(a_ref, wd_ref, o_ref):
    @pl.when(pl.program_id(2) == 0)
    def _():
        o_ref[...] = jnp.dot(a_ref[...], wd_ref[...], preferred_element_type=F32)

    @pl.when(pl.program_id(2) != 0)
    def _():
        o_ref[...] += jnp.dot(a_ref[...], wd_ref[...], preferred_element_type=F32)


def _mm_kernel(x_ref, w_ref, o_ref):
    w = w_ref[...].astype(BF16)
    o_ref[...] = jnp.dot(x_ref[...], w, preferred_element_type=F32).astype(o_ref.dtype)


def _merge_kernel(ya_ref, yb_ref, wa_ref, wb_ref, gla_ref, glb_ref, o_ref):
    wa = wa_ref[...].astype(BF16)
    wb = wb_ref[...].astype(BF16)
    half = o_ref.shape[0] // 2
    for rows in (slice(0, half), slice(half, None)):
        pa = jnp.dot(ya_ref[rows, :], wa, preferred_element_type=F32)
        pb = jnp.dot(yb_ref[rows, :], wb, preferred_element_type=F32)
        m = _sigmoid(gla_ref[rows, :]) * pa + _sigmoid(glb_ref[rows, :]) * pb
        o_ref[rows, :] = m.astype(o_ref.dtype)


def _lru_gates(xc, wrg_ref, brg_ref, wig_ref, big_ref, lam_ref):
    xcb = xc.astype(BF16)
    rs, gs = [], []
    for hh in range(wrg_ref.shape[0]):
        xh = xcb[:, hh * LRU_BLK:(hh + 1) * LRU_BLK]
        rs.append(jnp.dot(xh, wrg_ref[hh], preferred_element_type=F32))
        gs.append(jnp.dot(xh, wig_ref[hh], preferred_element_type=F32))
    r = _sigmoid(jnp.concatenate(rs, axis=1) + brg_ref[...])
    i = _sigmoid(jnp.concatenate(gs, axis=1) + big_ref[...])
    log_a = -C_RG * r * _softplus(-lam_ref[...])
    a = jnp.exp(log_a)
    mult = jnp.sqrt(-jnp.tanh(log_a) * (a * a + 1.0))
    return a, mult * (i * xc)


def _glu_rglru_prompt_kernel(g_ref, gcol_ref, w_ref, b_ref,
                             xa_ref, ga_ref, cw_ref, cb_ref, wrg_ref, brg_ref, wig_ref, big_ref, lam_ref,
                             ya_all_ref, o_ref, ya_ref, hl_ref, g_scr, xcars, hcars, *, tiles_per_seq):
    del ya_all_ref
    i, j = pl.program_id(0), pl.program_id(1)

    @pl.when(j == 0)
    def _():
        g_scr[...] = g_ref[...].astype(g_scr.dtype)

    _rglru_prompt_tile(xa_ref, ga_ref, cw_ref, cb_ref, wrg_ref, brg_ref, wig_ref, big_ref, lam_ref,
                       ya_ref, hl_ref, xcars.at[j], hcars.at[j], i % tiles_per_seq == 0)
    s = jnp.dot(g_scr[...], w_ref[...].astype(BF16), preferred_element_type=F32) + b_ref[...]
    o_ref[...] = (gcol_ref[...] * _sigmoid(s)).astype(o_ref.dtype)


def _rglru_prompt_tile(xa_ref, ga_ref, cw_ref, cb_ref, wrg_ref, brg_ref, wig_ref, big_ref, lam_ref,
                       ya_ref, hl_ref, xcar, hcar, starts_sequence):
    tm = xa_ref.shape[0]

    @pl.when(starts_sequence)
    def _():
        xcar[...] = jnp.zeros_like(xcar)
        hcar[...] = jnp.zeros_like(hcar)

    x = xa_ref[...]
    xfull = jnp.concatenate([xcar[...], x], axis=0)
    xcar[...] = x[tm - SUBLANES:, :]
    cw = cw_ref[...]
    xc = cb_ref[...]
    for k in range(CONV_W - 1):
        xs = pltpu.roll(xfull, CONV_W - 1 - k, 0)[SUBLANES:, :]
        xc = xc + xs * cw[k:k + 1, :]
    xc = xc + x * cw[CONV_W - 1:CONV_W, :]

    a, b = _lru_gates(xc, wrg_ref, brg_ref, wig_ref, big_ref, lam_ref)
    cb = a.shape[1]
    n_slab = tm // SUBLANES
    a = a.reshape(n_slab, SUBLANES, cb)
    b = b.reshape(n_slab, SUBLANES, cb)
    sub = lax.broadcasted_iota(jnp.int32, (1, SUBLANES, cb), 1)
    d = 1
    while d < SUBLANES:
        m = sub >= d
        a_s = jnp.where(m, pltpu.roll(a, d, 1), 1.0)
        b_s = jnp.where(m, pltpu.roll(b, d, 1), 0.0)
        b = a * b_s + b
        a = a * a_s
        d *= 2
    h = hcar[...]
    hs = []
    for s in range(n_slab):
        h_slab = b[s] + a[s] * jnp.broadcast_to(h, (SUBLANES, cb))
        h = h_slab[SUBLANES - 1:SUBLANES, :]
        hs.append(h_slab)
    hcar[...] = h
    hl_ref[...] = h
    ya_ref[...] = (jnp.concatenate(hs, axis=0) * jax.nn.gelu(ga_ref[...])).astype(ya_ref.dtype)


def _rglru_sample_kernel(xa_ref, ga_ref, cst_ref, h0_ref, cw_ref, cb_ref, wrg_ref, brg_ref, wig_ref,
                         big_ref, lam_ref, ya_all_ref, ya_ref, hl_ref, *, n_seq, n_steps):
    del ya_all_ref
    x = xa_ref[...]
    xp = jnp.concatenate([cst_ref[k] for k in range(CONV_W - 1)] + [x], axis=0)
    rows = n_seq * n_steps
    cw = cw_ref[...]
    xc = cb_ref[...]
    for k in range(CONV_W):
        xc = xc + xp[k * n_seq:k * n_seq + rows, :] * cw[k:k + 1, :]
    a, b = _lru_gates(xc, wrg_ref, brg_ref, wig_ref, big_ref, lam_ref)
    h = h0_ref[...]
    for t in range(n_steps):
        sl = slice(t * n_seq, (t + 1) * n_seq)
        h = a[sl, :] * h + b[sl, :]
        ya_ref[sl, :] = (h * jax.nn.gelu(ga_ref[sl, :])).astype(ya_ref.dtype)
    hl_ref[...] = h


def _s5_disc_kernel(are_ref, aim_ref, ldt_ref, abr_ref, abi_ref, cfr_ref, cfi_ref):
    a_re = are_ref[...]
    a_im = aim_ref[...]
    dt = jnp.exp(ldt_ref[...])
    mag = jnp.exp(a_re * dt)
    abr = mag * jnp.cos(a_im * dt)
    abi = mag * jnp.sin(a_im * dt)
    den = a_re * a_re + a_im * a_im
    nr = abr - 1.0
    abr_ref[...] = abr
    abi_ref[...] = abi
    cr = (nr * a_re + abi * a_im) / den
    ci = (abi * a_re - nr * a_im) / den
    for q in range(cfr_ref.shape[0]):
        cfr_ref[q] = cr
        cfi_ref[q] = ci
        cr, ci = abr * cr - abi * ci, abr * ci + abi * cr


def _gates_s5_prompt_kernel(u_ref, w_ref, xb_ref, wb_ref, wc_ref, ar_ref, ai_ref, d_ref, g_all_ref,
                            gl_ref, g_ref, sre_ref, sim_ref, cre, cim):
    del g_all_ref
    n_par, _, hw = ar_ref.shape

    @pl.when(pl.program_id(2) == 0)
    def _():
        cre[...] = jnp.zeros_like(cre)
        cim[...] = jnp.zeros_like(cim)

    split = (gl_ref.shape[0] * 9 // 16) // (2 * SUBLANES) * (2 * SUBLANES)
    w = w_ref[...].astype(BF16)
    bus = [_s5_project_in(xb_ref[:, p * LANES:(p + 1) * LANES], wb_ref.at[p]) for p in range(n_par)]
    gl_ref[:split, :] = jnp.dot(u_ref[:split, :], w, preferred_element_type=F32)
    scans = [_s5_slab_scan(bus[p], ar_ref[p], ai_ref[p], cre[p], cim[p]) for p in range(n_par)]
    y = jnp.dot(jnp.concatenate([s[0] for s in scans], axis=1), wc_ref[...], preferred_element_type=F32)
    gl_ref[split:, :] = jnp.dot(u_ref[split:, :], w, preferred_element_type=F32)
    g_ref[...] = jax.nn.gelu(y + d_ref[...] * xb_ref[...])
    for p in range(n_par):
        _, cr, ci = scans[p]
        cre[p] = cr
        cim[p] = ci
        sre_ref[:, p * hw:(p + 1) * hw] = cr
        sim_ref[:, p * hw:(p + 1) * hw] = ci


def _s5_project_in(xb, wb_ref):
    tm, width = xb.shape
    n_slab = tm // SUBLANES
    n_lags = wb_ref.shape[0] // width
    xb3 = xb.reshape(n_slab, SUBLANES, width)
    subx = lax.broadcasted_iota(jnp.int32, (1, SUBLANES, width), 1)
    lagged = [xb3] + [jnp.where(subx >= q, pltpu.roll(xb3, q, 1), 0.0) for q in range(1, n_lags)]
    xs = jnp.concatenate(lagged, axis=2).reshape(tm, n_lags * width)
    return jnp.dot(xs.astype(BF16), wb_ref[...], preferred_element_type=F32), n_lags


def _s5_slab_scan(bu_lags, ar, ai, cr, ci):
    bu, n_lags = bu_lags
    tm = bu.shape[0]
    hw = ar.shape[1]
    n_slab = tm // SUBLANES
    sub = lax.broadcasted_iota(jnp.int32, (SUBLANES, hw), 0)
    pr, pi = ar, ai
    tr = jnp.where(sub == 0, pr, 0.0)
    ti = jnp.where(sub == 0, pi, 0.0)
    steps = []
    d = 1
    while d < SUBLANES:
        mr = jnp.where(sub >= d, pr, 0.0)
        mi = jnp.where(sub >= d, pi, 0.0)
        if d >= n_lags:
            steps.append((d, mr, mi))
        sr = pltpu.roll(tr, d, 0)
        si = pltpu.roll(ti, d, 0)
        tr, ti = tr + (mr * sr - mi * si), ti + (mr * si + mi * sr)
        pr, pi = pr * pr - pi * pi, 2.0 * (pr * pi)
        d *= 2
    slabs = []
    for s in range(n_slab):
        rows = slice(s * SUBLANES, (s + 1) * SUBLANES)
        hr = bu[rows, :hw]
        hi = bu[rows, hw:]
        for d, mr, mi in steps:
            sr = pltpu.roll(hr, d, 0)
            si = pltpu.roll(hi, d, 0)
            hr, hi = hr + (mr * sr - mi * si), hi + (mr * si + mi * sr)
        cbr = jnp.broadcast_to(cr, (SUBLANES, hw))
        cbi = jnp.broadcast_to(ci, (SUBLANES, hw))
        hr, hi = hr + (tr * cbr - ti * cbi), hi + (tr * cbi + ti * cbr)
        cr = hr[SUBLANES - 1:SUBLANES, :]
        ci = hi[SUBLANES - 1:SUBLANES, :]
        slabs.append(jnp.concatenate([hr, hi], axis=1))
    return jnp.concatenate(slabs, axis=0).astype(BF16), cr, ci


def _s5_sample_kernel(xb_ref, wb_ref, wc_ref, ar_ref, ai_ref, d_ref, s0r_ref, s0i_ref, g_all_ref,
                      g_ref, sre_ref, sim_ref, bu_scr, h_scr, *, n_seq, n_steps):
    del g_all_ref
    hw = ar_ref.shape[1]
    xb = xb_ref[...]
    bu_scr[...] = jnp.dot(xb.astype(BF16), wb_ref[...], preferred_element_type=F32)
    ar = ar_ref[...]
    ai = ai_ref[...]
    hr = s0r_ref[...]
    hi = s0i_ref[...]
    for t in range(n_steps):
        sl = slice(t * n_seq, (t + 1) * n_seq)
        hr, hi = (ar * hr - ai * hi) + bu_scr[sl, :hw], (ar * hi + ai * hr) + bu_scr[sl, hw:]
        h_scr[sl, :hw] = hr.astype(h_scr.dtype)
        h_scr[sl, hw:] = hi.astype(h_scr.dtype)
    sre_ref[...] = hr
    sim_ref[...] = hi
    y = jnp.dot(h_scr[...], wc_ref[...], preferred_element_type=F32)
    g_ref[...] = jax.nn.gelu(y + d_ref[...] * xb)


def _half_ffn_matmuls(h, wg, wu, wd, *, bm):
    m, dm = h.shape
    dff = wg.shape[1]
    bn_up = _pick(dff, (256, 128))
    n_i, n_j = m // bm, dff // bn_up
    rb = _pick(dff, tuple(r for r in (64, 128, 256, 512) if n_i * n_j >= dff // r))
    n_rb = dff // rb
    assert n_i * n_j >= n_rb, "not enough up-projection steps to cover the down-projection weight"
    wd_blk = lambda i, j: (jnp.minimum(i * n_j + j, n_rb - 1), 0)
    a, wd = pl.pallas_call(
        _ffn_up_kernel,
        grid=(n_i, n_j),
        in_specs=[pl.BlockSpec((bm, dm), lambda i, j: (i, 0)),
                  pl.BlockSpec((dm, bn_up), lambda i, j: (0, j)),
                  pl.BlockSpec((dm, bn_up), lambda i, j: (0, j)),
                  pl.BlockSpec((rb, dm), wd_blk)],
        out_specs=[pl.BlockSpec((bm, bn_up), lambda i, j: (i, j)),
                   pl.BlockSpec((rb, dm), wd_blk)],
        out_shape=[jax.ShapeDtypeStruct((m, dff), BF16), jax.ShapeDtypeStruct((dff, dm), BF16)],
        compiler_params=_cparams("arbitrary", "arbitrary"),
        name="ffn_up",
    )(h, wg, wu, wd)

    bm_d = _pick(m, (1024, 512, 256, 128))
    bn_d = _pick(dm, (1024, 512, 256, 128))
    bk = dff // 2 if (dff // 2) % LANES == 0 else dff
    return pl.pallas_call(
        _ffn_down_kernel,
        grid=(m // bm_d, dm // bn_d, dff // bk),
        in_specs=[pl.BlockSpec((bm_d, bk), lambda i, j, k: (i, k)),
                  pl.BlockSpec((bk, bn_d), lambda i, j, k: (k, j))],
        out_specs=pl.BlockSpec((bm_d, bn_d), lambda i, j, k: (i, j)),
        out_shape=jax.ShapeDtypeStruct((m, dm), F32),
        compiler_params=_cparams("parallel", "parallel", "arbitrary"),
        name="ffn_down",
    )(a, wd)


def _matmul(x, w, *, bm, bn, out_dtype, name, n=None):
    m, k = x.shape
    n = w.shape[1] if n is None else n
    return pl.pallas_call(
        _mm_kernel,
        grid=(m // bm, n // bn),
        in_specs=[pl.BlockSpec((bm, k), lambda i, j: (i, 0)),
                  pl.BlockSpec((k, bn), lambda i, j: (0, j))],
        out_specs=pl.BlockSpec((bm, bn), lambda i, j: (i, j)),
        out_shape=jax.ShapeDtypeStruct((m, n), out_dtype),
        compiler_params=_cparams("parallel", "arbitrary"),
        name=name,
    )(x, w)


def kernel(x_prompt, x_sample, state_lru_h, state_conv, state_ssm_re, state_ssm_im, ffn1_pre_g, ffn1_post_g, ffn1_w_gate, ffn1_w_up, ffn1_w_down, mix_pre_g, mix_post_g, w_in, conv_w, conv_b, w_rg, b_rg, w_ig, b_ig, lru_lambda, ssm_a_re, ssm_a_im, ssm_log_dt, ssm_b_re, ssm_b_im, ssm_c_re, ssm_c_im, ssm_d, w_glu, b_glu, w_out_a, w_out_b, w_o, ffn2_pre_g, ffn2_post_g, ffn2_w_gate, ffn2_w_up, ffn2_w_down):
    nb, seq, dm = x_prompt.shape
    db, dseq, _ = x_sample.shape
    depth = state_lru_h.shape[0]
    assert depth == 1, "one decoder layer"
    d_lru = state_lru_h.shape[2]
    n_grp, ssm_p = state_ssm_re.shape[2], state_ssm_re.shape[3]
    d_ssm = ssm_d.shape[1]
    ssm_cg = d_ssm // n_grp
    d_in = w_in.shape[2]
    assert w_rg.shape[2] == LRU_BLK and conv_w.shape[1] == CONV_W
    assert d_in == 2 * d_lru + d_ssm + 2 * dm and seq >= CONV_W - 1 and dseq >= CONV_W - 1

    mp = nb * seq
    ms = db * dseq
    m = mp + ms
    sdt = state_lru_h.dtype

    row2 = lambda v: v.reshape(1, -1)
    bf = lambda v: v[0].astype(BF16)

    xp2 = x_prompt.reshape(mp, dm)
    xs2 = jnp.swapaxes(x_sample, 0, 1).reshape(ms, dm)

    tr = _pick(math.gcd(mp, ms), (256, 128, 64, 32, 16, 8))
    npt = mp // tr
    xp_spec = pl.BlockSpec((tr, dm), lambda i: (jnp.minimum(i, npt - 1), 0))
    xs_spec = pl.BlockSpec((tr, dm), lambda i: (jnp.maximum(i - npt, 0), 0))
    row_spec = pl.BlockSpec((tr, dm), lambda i: (i, 0))
    g_spec = pl.BlockSpec((1, dm), lambda i: (0, 0))

    h1 = pl.pallas_call(
        functools.partial(_prenorm_kernel, n_prompt_tiles=npt),
        grid=(m // tr,),
        in_specs=[xp_spec, xs_spec, g_spec],
        out_specs=row_spec,
        out_shape=jax.ShapeDtypeStruct((m, dm), BF16),
        compiler_params=_cparams("parallel"),
        name="prenorm1",
    )(xp2, xs2, row2(ffn1_pre_g))

    bm = _pick(m, (1536, 1024, 768, 512, 384, 256, 128))
    f1 = _half_ffn_matmuls(h1, ffn1_w_gate[0], ffn1_w_up[0], ffn1_w_down[0], bm=bm)

    x1, u = pl.pallas_call(
        functools.partial(_post_ffn1_kernel, n_prompt_tiles=npt),
        grid=(m // tr,),
        in_specs=[xp_spec, xs_spec, row_spec, g_spec, g_spec],
        out_specs=[row_spec, row_spec],
        out_shape=[jax.ShapeDtypeStruct((m, dm), F32), jax.ShapeDtypeStruct((m, dm), BF16)],
        compiler_params=_cparams("parallel"),
        name="post_ffn1",
    )(xp2, xs2, f1, row2(ffn1_post_g), row2(mix_pre_g))

    bn_in = _pick(math.gcd(d_lru, dm), (512, 256, 128))
    n_a = 2 * d_lru + d_ssm
    z = _matmul(u, w_in[0], bm=bm, bn=bn_in, out_dtype=F32, name="in_proj", n=n_a)

    cb = _pick(d_lru, (512, 256, 128))
    ncb = d_lru // cb
    hpb = cb // LRU_BLK
    wrg = bf(w_rg)
    wig = bf(w_ig)
    lru_params = (conv_w[0], row2(conv_b), wrg, row2(b_rg), wig, row2(b_ig), row2(lru_lambda))
    assert d_lru <= dm and d_ssm <= dm
    any_spec = pl.BlockSpec(memory_space=pl.ANY)
    assert mp % ms == 0, "sample rows must tile the unified row axis"
    s_blk = mp // ms

    n_lags = 2
    abr, abi, cfr, cfi = pl.pallas_call(
        _s5_disc_kernel,
        out_shape=[jax.ShapeDtypeStruct((n_grp, ssm_p), F32)] * 2
                  + [jax.ShapeDtypeStruct((n_lags, n_grp, ssm_p), F32)] * 2,
        name="s5_discretise",
    )(ssm_a_re[0], ssm_a_im[0], ssm_log_dt[0].reshape(n_grp, 1))
    bb_re = cfr[..., None] * ssm_b_re - cfi[..., None] * ssm_b_im
    bb_im = cfr[..., None] * ssm_b_im + cfi[..., None] * ssm_b_re

    gpb = LANES // ssm_cg
    nj = n_grp // gpb
    hw = gpb * ssm_p
    bb =jnp.stack([bb_re, bb_im]).reshape(2, n_lags, nj, gpb, ssm_p, ssm_cg).astype(BF16)
    bb = jnp.transpose(bb, (2, 1, 3, 5, 0, 4)).reshape(nj, n_lags * LANES, 2, ssm_p)
    assert LANES % ssm_p == 0 and gpb % (LANES // ssm_p) == 0
    hpl = LANES // ssm_p
    bb = jnp.concatenate([bb] * hpl, axis=3)
    wb = jnp.concatenate([bb[:, :, ri, :] for ri in range(2) for _ in range(gpb // hpl)], axis=2)
    wb_shape = (1, n_lags * LANES, 2 * hw)
    g_row = (lax.broadcasted_iota(jnp.int32, wb_shape, 1) // ssm_cg) % gpb
    h_col = (lax.broadcasted_iota(jnp.int32, wb_shape, 2) // ssm_p) % gpb
    wb = jnp.where(g_row == h_col, wb, jnp.zeros((), BF16))
    cc = jnp.stack([ssm_c_re[0], -ssm_c_im[0]]).reshape(2, nj, gpb, ssm_cg, ssm_p).astype(BF16)
    cc = jnp.transpose(cc, (1, 0, 4, 2, 3)).reshape(nj, 2, ssm_p, LANES)
    wc = jnp.concatenate([cc[:, ri, :, :] for ri in range(2) for _ in range(gpb)], axis=1)
    wc_shape = (1, 2 * hw, LANES)
    wc_mask = ((lax.broadcasted_iota(jnp.int32, wc_shape, 1) // ssm_p) % gpb
               == lax.broadcasted_iota(jnp.int32, wc_shape, 2) // ssm_cg)
    wc = jnp.where(wc_mask, wc, jnp.zeros((), BF16))
    abr3 = abr.reshape(nj, 1, hw)
    abi3 = abi.reshape(nj, 1, hw)
    xb_blk = (2 * d_lru) // LANES
    dsk = row2(ssm_d)

    n_par = 2 if nj % 2 == 0 and xb_blk % 2 == 0 else 1
    par_shape = (1, n_par * 2 * hw, n_par * LANES)
    par_mask = (lax.broadcasted_iota(jnp.int32, par_shape, 1) // (2 * hw)
                == lax.broadcasted_iota(jnp.int32, par_shape, 2) // LANES)
    wc_par = jnp.concatenate([wc.reshape(nj // n_par, n_par * 2 * hw, LANES)] * n_par, axis=2)
    wc_par = jnp.where(par_mask, wc_par, jnp.zeros((), BF16))
    n_gi = nj // n_par
    bm_gl = m // n_gi
    assert m % n_gi == 0 and bm_gl % (2 * SUBLANES) == 0

    def gl_cols_ok(t):
        bn = (2 * dm) // (nb * (seq // t)) if (2 * dm) % (nb * (seq // t)) == 0 else 0
        return bn > 0 and bn % LANES == 0 and n_a % bn == 0

    tm5 = _pick(seq, tuple(t for t in (512, 256, 128, 64, 32, 16, 8) if seq % t == 0 and gl_cols_ok(t)))
    rp5 = seq // tm5
    bn_gl = (2 * dm) // (nb * rp5)
    gl_blk = n_a // bn_gl
    gl, g_all, sre_p, sim_p = pl.pallas_call(
        _gates_s5_prompt_kernel,
        grid=(n_gi, nb, rp5),
        in_specs=[pl.BlockSpec((bm_gl, dm), lambda j, n, r: (j, 0)),
                  pl.BlockSpec((dm, bn_gl), lambda j, n, r: (0, gl_blk + n * rp5 + r)),
                  pl.BlockSpec((tm5, n_par * LANES), lambda j, n, r: (n * rp5 + r, xb_blk // n_par + j)),
                  pl.BlockSpec((n_par, n_lags * LANES, 2 * hw), lambda j, n, r: (j, 0, 0)),
                  pl.BlockSpec((None, n_par * 2 * hw, n_par * LANES), lambda j, n, r: (j, 0, 0)),
                  pl.BlockSpec((n_par, 1, hw), lambda j, n, r: (j, 0, 0)),
                  pl.BlockSpec((n_par, 1, hw), lambda j, n, r: (j, 0, 0)),
                  pl.BlockSpec((1, n_par * LANES), lambda j, n, r: (0, j)), any_spec],
        out_specs=[pl.BlockSpec((bm_gl, bn_gl), lambda j, n, r: (j, n * rp5 + r)),
                   pl.BlockSpec((tm5, n_par * LANES), lambda j, n, r: (n * rp5 + r, j)),
                   pl.BlockSpec((None, 1, n_par * hw), lambda j, n, r: (n, 0, j)),
                   pl.BlockSpec((None, 1, n_par * hw), lambda j, n, r: (n, 0, j))],
        out_shape=[jax.ShapeDtypeStruct((m, 2 * dm), F32),
                   jax.ShapeDtypeStruct((m, dm), F32),
                   jax.ShapeDtypeStruct((nb, 1, n_grp * ssm_p), F32),
                   jax.ShapeDtypeStruct((nb, 1, n_grp * ssm_p), F32)],
        scratch_shapes=[pltpu.VMEM((n_par, 1, hw), F32), pltpu.VMEM((n_par, 1, hw), F32)],
        input_output_aliases={8: 1},
        compiler_params=_cparams("arbitrary", "arbitrary", "arbitrary"),
        name="gates_s5_prompt",
    )(u, w_in[0], z, wb, wc_par, abr3, abi3, dsk, f1)

    s5_w_specs1 = [pl.BlockSpec((None, LANES, 2 * hw), lambda j: (j, 0, 0)),
                   pl.BlockSpec((None, 2 * hw, LANES), lambda j: (j, 0, 0)),
                   pl.BlockSpec((None, 1, hw), lambda j: (j, 0, 0)),
                   pl.BlockSpec((None, 1, hw), lambda j: (j, 0, 0)),
                   pl.BlockSpec((1, LANES), lambda j: (0, j))]
    g_all, sre_s, sim_s = pl.pallas_call(
        functools.partial(_s5_sample_kernel, n_seq=db, n_steps=dseq),
        grid=(nj,),
        in_specs=[pl.BlockSpec((ms, LANES), lambda j: (s_blk, xb_blk + j))] + s5_w_specs1
                 + [pl.BlockSpec((db, hw), lambda j: (0, j)), pl.BlockSpec((db, hw), lambda j: (0, j)), any_spec],
        out_specs=[pl.BlockSpec((ms, LANES), lambda j: (s_blk, j)),
                   pl.BlockSpec((db, hw), lambda j: (0, j)),
                   pl.BlockSpec((db, hw), lambda j: (0, j))],
        out_shape=[jax.ShapeDtypeStruct((m, dm), F32),
                   jax.ShapeDtypeStruct((db, n_grp * ssm_p), F32),
                   jax.ShapeDtypeStruct((db, n_grp * ssm_p), F32)],
        scratch_shapes=[pltpu.VMEM((ms, 2 * hw), F32), pltpu.VMEM((ms, 2 * hw), BF16)],
        input_output_aliases={8: 0},
        compiler_params=_cparams("parallel"),
        name="s5_sample",
    )(z, wb, wc, abr3, abi3, dsk,
      state_ssm_re[0].reshape(db, n_grp * ssm_p), state_ssm_im[0].reshape(db, n_grp * ssm_p), g_all)

    assert d_ssm // cb == ncb
    tm = _pick(seq, tuple(t for t in (512, 256, 128, 64, 32, 16, 8)
                          if seq % t == 0 and (m * t) % mp == 0 and ((m * t) // mp) % (2 * SUBLANES) == 0))
    rpt = seq // tm
    bm_u = (m * tm) // mp
    col = lambda shape: pl.BlockSpec(shape, lambda i, j: (0, j))
    head_spec = pl.BlockSpec((hpb, LRU_BLK, LRU_BLK), lambda i, j: (j, 0, 0))
    yb, ya, hl_p = pl.pallas_call(
        functools.partial(_glu_rglru_prompt_kernel, tiles_per_seq=rpt),
        grid=(nb * rpt, ncb),
        in_specs=[pl.BlockSpec((bm_u, d_ssm), lambda i, j: (i, 0)),
                  pl.BlockSpec((bm_u, cb), lambda i, j: (i, j)),
                  pl.BlockSpec((d_ssm, cb), lambda i, j: (0, j)),
                  col((1, cb)),
                  pl.BlockSpec((tm, cb), lambda i, j: (i, j)),
                  pl.BlockSpec((tm, cb), lambda i, j: (i, ncb + j)),
                  col((CONV_W, cb)), col((1, cb)), head_spec, col((1, cb)), head_spec, col((1, cb)),
                  col((1, cb)), any_spec],
        out_specs=[pl.BlockSpec((bm_u, cb), lambda i, j: (i, j)),
                   pl.BlockSpec((tm, cb), lambda i, j: (i, j)),
                   pl.BlockSpec((None, 1, cb), lambda i, j: (i, 0, j))],
        out_shape=[jax.ShapeDtypeStruct((m, d_ssm), BF16),
                   jax.ShapeDtypeStruct((m, dm), BF16),
                   jax.ShapeDtypeStruct((nb * rpt, 1, d_lru), F32)],
        scratch_shapes=[pltpu.VMEM((bm_u, d_ssm), BF16),
                        pltpu.VMEM((ncb, SUBLANES, cb), F32), pltpu.VMEM((ncb, 1, cb), F32)],
        input_output_aliases={4 + 2 + len(lru_params): 1},
        compiler_params=_cparams("arbitrary", "arbitrary"),
        name="glu_rglru_prompt",
    )(g_all, g_all, w_glu[0], row2(b_glu), z, z, *lru_params, h1)

    par1 = lambda shape: pl.BlockSpec(shape, lambda c: (0, c))
    cst = jnp.swapaxes(state_conv[0], 0, 1)
    ya, hl_s = pl.pallas_call(
        functools.partial(_rglru_sample_kernel, n_seq=db, n_steps=dseq),
        grid=(ncb,),
        in_specs=[pl.BlockSpec((ms, cb), lambda c: (s_blk, c)),
                  pl.BlockSpec((ms, cb), lambda c: (s_blk, ncb + c)),
                  pl.BlockSpec((CONV_W - 1, db, cb), lambda c: (0, 0, c)),
                  par1((db, cb)),
                  par1((CONV_W, cb)), par1((1, cb)),
                  pl.BlockSpec((hpb, LRU_BLK, LRU_BLK), lambda c: (c, 0, 0)), par1((1, cb)),
                  pl.BlockSpec((hpb, LRU_BLK, LRU_BLK), lambda c: (c, 0, 0)), par1((1, cb)),
                  par1((1, cb)), any_spec],
        out_specs=[pl.BlockSpec((ms, cb), lambda c: (s_blk, c)), par1((db, cb))],
        out_shape=[jax.ShapeDtypeStruct((m, dm), BF16), jax.ShapeDtypeStruct((db, d_lru), F32)],
        input_output_aliases={4 + len(lru_params): 0},
        compiler_params=_cparams("parallel"),
        name="rglru_sample",
    )(z, z, cst, state_lru_h[0], *lru_params, ya)

    bm_g = _pick(m, (1024, 512, 256, 128))
    bn_m = bn_in
    gla_blk = 0
    glb_blk = dm // bn_m
    merged = pl.pallas_call(
        _merge_kernel,
        grid=(m // bm_g, dm // bn_m),
        in_specs=[pl.BlockSpec((bm_g, d_lru), lambda i, j: (i, 0)),
                  pl.BlockSpec((bm_g, d_ssm), lambda i, j: (i, 0)),
                  pl.BlockSpec((d_lru, bn_m), lambda i, j: (0, j)),
                  pl.BlockSpec((d_ssm, bn_m), lambda i, j: (0, j)),
                  pl.BlockSpec((bm_g, bn_m), lambda i, j: (i, gla_blk + j)),
                  pl.BlockSpec((bm_g, bn_m), lambda i, j: (i, glb_blk + j))],
        out_specs=pl.BlockSpec((bm_g, bn_m), lambda i, j: (i, j)),
        out_shape=jax.ShapeDtypeStruct((m, dm), BF16),
        compiler_params=_cparams("parallel", "arbitrary"),
        name="gated_merge",
    )(ya, yb, w_out_a[0], w_out_b[0], gl, gl)

    o = _matmul(merged, w_o[0], bm=bm, bn=bn_in, out_dtype=F32, name="o_proj")

    x2, h2 = pl.pallas_call(
        _post_mix_kernel,
        grid=(m // tr,),
        in_specs=[row_spec, row_spec, g_spec, g_spec],
        out_specs=[row_spec, row_spec],
        out_shape=[jax.ShapeDtypeStruct((m, dm), F32), jax.ShapeDtypeStruct((m, dm), BF16)],
        compiler_params=_cparams("parallel"),
        name="post_mix",
    )(x1, o, row2(mix_post_g), row2(ffn2_pre_g))

    f2 = _half_ffn_matmuls(h2, ffn2_w_gate[0], ffn2_w_up[0], ffn2_w_down[0], bm=bm)

    def final(rows, first_tile):
        return pl.pallas_call(
            _final_kernel,
            grid=(rows // tr,),
            in_specs=[pl.BlockSpec((tr, dm), lambda i: (first_tile + i, 0)),
                      pl.BlockSpec((tr, dm), lambda i: (first_tile + i, 0)),
                      g_spec],
            out_specs=row_spec,
            out_shape=jax.ShapeDtypeStruct((rows, dm), F32),
            compiler_params=_cparams("parallel"),
            name="final_residual",
        )(x2, f2, row2(ffn2_post_g))

    y_prompt = final(mp, 0).reshape(nb, seq, dm)
    y_sample = jnp.swapaxes(final(ms, npt).reshape(dseq, db, dm), 0, 1)

    nk = CONV_W - 1
    prompt_conv = jnp.stack([lax.slice(z, ((n + 1) * seq - nk, 0), ((n + 1) * seq, d_lru)) for n in range(nb)])
    sample_conv = jnp.swapaxes(lax.slice(z, (m - nk * db, 0), (m, d_lru)).reshape(nk, db, d_lru), 0, 1)
    st = lambda v, n: v.reshape(1, n, n_grp, ssm_p).astype(sdt)
    return (y_prompt, y_sample,
            hl_p.reshape(nb, rpt, d_lru)[:, rpt - 1][None].astype(sdt), prompt_conv[None].astype(sdt),
            st(sre_p, nb), st(sim_p, nb),
            hl_s.reshape(1, db, d_lru).astype(sdt), sample_conv[None].astype(sdt),
            st(sre_s, db), st(sim_s, db))
```

```python
import functools
import math

import jax
import jax.numpy as jnp
from jax import lax
from jax.experimental import pallas as pl
from jax.experimental.pallas import tpu as pltpu

F32 = jnp.float32
BF16 = jnp.bfloat16

EPS = 1e-6
C_RG = 8.0
CONV_W = 4
LANES = 128
SUBLANES = 8
LRU_BLK = 128
VMEM_LIMIT = 56 * 1024 * 1024


def _cparams(*sem):
    return pltpu.CompilerParams(dimension_semantics=sem, vmem_limit_bytes=VMEM_LIMIT)


def _pick(n, candidates):
    for c in candidates:
        if n % c == 0:
            return c
    raise ValueError(f"no tile in {candidates} divides {n}")


def _rms(x, g):
    return x * lax.rsqrt(jnp.mean(x * x, axis=-1, keepdims=True) + EPS) * g


def _softplus(x):
    return jnp.maximum(x, 0.0) + jnp.log1p(jnp.exp(-jnp.abs(x)))


def _sigmoid(x):
    return 0.5 * jnp.tanh(0.5 * x) + 0.5


def _two_group_rows(i, n_prompt_tiles, xp_ref, xs_ref, body):
    @pl.when(i < n_prompt_tiles)
    def _():
        body(xp_ref[...])

    @pl.when(i >= n_prompt_tiles)
    def _():
        body(xs_ref[...])


def _prenorm_kernel(xp_ref, xs_ref, g_ref, h_ref, *, n_prompt_tiles):
    def body(x):
        h_ref[...] = _rms(x, g_ref[...]).astype(h_ref.dtype)

    _two_group_rows(pl.program_id(0), n_prompt_tiles, xp_ref, xs_ref, body)


def _post_ffn1_kernel(xp_ref, xs_ref, f_ref, gpost_ref, gpre_ref, x1_ref, u_ref, *, n_prompt_tiles):
    def body(x):
        x1 = x + 0.5 * _rms(f_ref[...], gpost_ref[...])
        x1_ref[...] = x1
        u_ref[...] = _rms(x1, gpre_ref[...]).astype(u_ref.dtype)

    _two_group_rows(pl.program_id(0), n_prompt_tiles, xp_ref, xs_ref, body)


def _post_mix_kernel(x1_ref, o_ref, gpost_ref, gpre_ref, x2_ref, h_ref):
    x2 = x1_ref[...] + _rms(o_ref[...], gpost_ref[...])
    x2_ref[...] = x2
    h_ref[...] = _rms(x2, gpre_ref[...]).astype(h_ref.dtype)


def _final_kernel(x2_ref, f_ref, g_ref, y_ref):
    y_ref[...] = x2_ref[...] + 0.5 * _rms(f_ref[...], g_ref[...])


def _ffn_up_kernel(h_ref, wg_ref, wu_ref, wd_ref, a_ref, wd16_ref):
    wg = wg_ref[...].astype(BF16)
    wu = wu_ref[...].astype(BF16)
    half = h_ref.shape[0] // 2
    for rows in (slice(0, half), slice(half, None)):
        h = h_ref[rows, :]
        g = jnp.dot(h, wg, preferred_element_type=F32)
        u = jnp.dot(h, wu, preferred_element_type=F32)
        a_ref[rows, :] = (g * _sigmoid(g) * u).astype(a_ref.dtype)
    wd16_ref[...] = wd_ref[...].astype(wd16_ref.dtype)


def _ffn_down_kernel(a_ref, wd_ref, o_ref):
    @pl.when(pl.program_id(2) == 0)
    def _():
        o_ref[...] = jnp.dot(a_ref[...], wd_ref[...], preferred_element_type=F32)

    @pl.when(pl.program_id(2) != 0)
    def _():
        o_ref[...] += jnp.dot(a_ref[...], wd_ref[...], preferred_element_type=F32)


def _mm_kernel(x_ref, w_ref, o_ref):
    w = w_ref[...].astype(BF16)
    o_ref[...] = jnp.dot(x_ref[...], w, preferred_element_type=F32).astype(o_ref.dtype)


def _merge_kernel(ya_ref, yb_ref, wa_ref, wb_ref, gla_ref, glb_ref, o_ref):
    wa = wa_ref[...].astype(BF16)
    wb = wb_ref[...].astype(BF16)
    half = o_ref.shape[0] // 2
    for rows in (slice(0, half), slice(half, None)):
        pa = jnp.dot(ya_ref[rows, :], wa, preferred_element_type=F32)
        pb = jnp.dot(yb_ref[rows, :], wb, preferred_element_type=F32)
        m = _sigmoid(gla_ref[rows, :]) * pa + _sigmoid(glb_ref[rows, :]) * pb
        o_ref[rows, :] = m.astype(o_ref.dtype)


def _lru_gates(xc, wrg_ref, brg_ref, wig_ref, big_ref, lam_ref):
    xcb = xc.astype(BF16)
    rs, gs = [], []
    for hh in range(wrg_ref.shape[0]):
        xh = xcb[:, hh * LRU_BLK:(hh + 1) * LRU_BLK]
        rs.append(jnp.dot(xh, wrg_ref[hh], preferred_element_type=F32))
        gs.append(jnp.dot(xh, wig_ref[hh], preferred_element_type=F32))
    r = _sigmoid(jnp.concatenate(rs, axis=1) + brg_ref[...])
    i = _sigmoid(jnp.concatenate(gs, axis=1) + big_ref[...])
    log_a = -C_RG * r * _softplus(-lam_ref[...])
    a = jnp.exp(log_a)
    mult = jnp.sqrt(-jnp.tanh(log_a) * (a * a + 1.0))
    return a, mult * (i * xc)


def _glu_rglru_prompt_kernel(g_ref, gcol_ref, w_ref, b_ref,
                             xa_ref, ga_ref, cw_ref, cb_ref, wrg_ref, brg_ref, wig_ref, big_ref, lam_ref,
                             ya_all_ref, o_ref, ya_ref, hl_ref, g_scr, xcars, hcars, *, tiles_per_seq):
    del ya_all_ref
    i, j = pl.program_id(0), pl.program_id(1)

    @pl.when(j == 0)
    def _():
        g_scr[...] = g_ref[...].astype(g_scr.dtype)

    _rglru_prompt_tile(xa_ref, ga_ref, cw_ref, cb_ref, wrg_ref, brg_ref, wig_ref, big_ref, lam_ref,
                       ya_ref, hl_ref, xcars.at[j], hcars.at[j], i % tiles_per_seq == 0)
    s = jnp.dot(g_scr[...], w_ref[...].astype(BF16), preferred_element_type=F32) + b_ref[...]
    o_ref[...] = (gcol_ref[...] * _sigmoid(s)).astype(o_ref.dtype)


def _rglru_prompt_tile(xa_ref, ga_ref, cw_ref, cb_ref, wrg_ref, brg_ref, wig_ref, big_ref, lam_ref,
                       ya_ref, hl_ref, xcar, hcar, starts_sequence):
    tm = xa_ref.shape[0]

    @pl.when(starts_sequence)
    def _():
        xcar[...] = jnp.zeros_like(xcar)
        hcar[...] = jnp.zeros_like(hcar)

    x = xa_ref[...]
    xfull = jnp.concatenate([xcar[...], x], axis=0)
    xcar[...] = x[tm - SUBLANES:, :]
    cw = cw_ref[...]
    xc = cb_ref[...]
    for k in range(CONV_W - 1):
        xs = pltpu.roll(xfull, CONV_W - 1 - k, 0)[SUBLANES:, :]
        xc = xc + xs * cw[k:k + 1, :]
    xc = xc + x * cw[CONV_W - 1:CONV_W, :]

    a, b = _lru_gates(xc, wrg_ref, brg_ref, wig_ref, big_ref, lam_ref)
    cb = a.shape[1]
    n_slab = tm // SUBLANES
    a = a.reshape(n_slab, SUBLANES, cb)
    b = b.reshape(n_slab, SUBLANES, cb)
    sub = lax.broadcasted_iota(jnp.int32, (1, SUBLANES, cb), 1)
    d = 1
    while d < SUBLANES:
        m = sub >= d
        a_s = jnp.where(m, pltpu.roll(a, d, 1), 1.0)
        b_s = jnp.where(m, pltpu.roll(b, d, 1), 0.0)
        b = a * b_s + b
        a = a * a_s
        d *= 2
    h = hcar[...]
    hs = []
    for s in range(n_slab):
        h_slab = b[s] + a[s] * jnp.broadcast_to(h, (SUBLANES, cb))
        h = h_slab[SUBLANES - 1:SUBLANES, :]
        hs.append(h_slab)
    hcar[...] = h
    hl_ref[...] = h
    ya_ref[...] = (jnp.concatenate(hs, axis=0) * jax.nn.gelu(ga_ref[...])).astype(ya_ref.dtype)


def _rglru_sample_kernel(xa_ref, ga_ref, cst_ref, h0_ref, cw_ref, cb_ref, wrg_ref, brg_ref, wig_ref,
                         big_ref, lam_ref, ya_all_ref, ya_ref, hl_ref, *, n_seq, n_steps):
    del ya_all_ref
    x = xa_ref[...]
    xp = jnp.concatenate([cst_ref[k] for k in range(CONV_W - 1)] + [x], axis=0)
    rows = n_seq * n_steps
    cw = cw_ref[...]
    xc = cb_ref[...]
    for k in range(CONV_W):
        xc = xc + xp[k * n_seq:k * n_seq + rows, :] * cw[k:k + 1, :]
    a, b = _lru_gates(xc, wrg_ref, brg_ref, wig_ref, big_ref, lam_ref)
    h = h0_ref[...]
    for t in range(n_steps):
        sl = slice(t * n_seq, (t + 1) * n_seq)
        h = a[sl, :] * h + b[sl, :]
        ya_ref[sl, :] = (h * jax.nn.gelu(ga_ref[sl, :])).astype(ya_ref.dtype)
    hl_ref[...] = h


def _s5_disc_kernel(are_ref, aim_ref, ldt_ref, abr_ref, abi_ref, cfr_ref, cfi_ref):
    a_re = are_ref[...]
    a_im = aim_ref[...]
    dt = jnp.exp(ldt_ref[...])
    mag = jnp.exp(a_re * dt)
    abr = mag * jnp.cos(a_im * dt)
    abi = mag * jnp.sin(a_im * dt)
    den = a_re * a_re + a_im * a_im
    nr = abr - 1.0
    abr_ref[...] = abr
    abi_ref[...] = abi
    cr = (nr * a_re + abi * a_im) / den
    ci = (abi * a_re - nr * a_im) / den
    for q in range(cfr_ref.shape[0]):
        cfr_ref[q] = cr
        cfi_ref[q] = ci
        cr, ci = abr * cr - abi * ci, abr * ci + abi * cr


def _gates_s5_prompt_kernel(u_ref, w_ref, xb_ref, wb_ref, wc_ref, ar_ref, ai_ref, d_ref, g_all_ref,
                            gl_ref, g_ref, sre_ref, sim_ref, cre, cim):
    del g_all_ref
    n_par, _, hw = ar_ref.shape

    @pl.when(pl.program_id(2) == 0)
    def _():
        cre[...] = jnp.zeros_like(cre)
        cim[...] = jnp.zeros_like(cim)

    split = (gl_ref.shape[0] * 9 // 16) // (2 * SUBLANES) * (2 * SUBLANES)
    w = w_ref[...].astype(BF16)
    bus = [_s5_project_in(xb_ref[:, p * LANES:(p + 1) * LANES], wb_ref.at[p]) for p in range(n_par)]
    gl_ref[:split, :] = jnp.dot(u_ref[:split, :], w, preferred_element_type=F32)
    scans = [_s5_slab_scan(bus[p], ar_ref[p], ai_ref[p], cre[p], cim[p]) for p in range(n_par)]
    y = jnp.dot(jnp.concatenate([s[0] for s in scans], axis=1), wc_ref[...], preferred_element_type=F32)
    gl_ref[split:, :] = jnp.dot(u_ref[split:, :], w, preferred_element_type=F32)
    g_ref[...] = jax.nn.gelu(y + d_ref[...] * xb_ref[...])
    for p in range(n_par):
        _, cr, ci = scans[p]
        cre[p] = cr
        cim[p] = ci
        sre_ref[:, p * hw:(p + 1) * hw] = cr
        sim_ref[:, p * hw:(p + 1) * hw] = ci


def _s5_project_in(xb, wb_ref):
    tm, width = xb.shape
    n_slab = tm // SUBLANES
    n_lags = wb_ref.shape[0] // width
    xb3 = xb.reshape(n_slab, SUBLANES, width)
    subx = lax.broadcasted_iota(jnp.int32, (1, SUBLANES, width), 1)
    lagged = [xb3] + [jnp.where(subx >= q, pltpu.roll(xb3, q, 1), 0.0) for q in range(1, n_lags)]
    xs = jnp.concatenate(lagged, axis=2).reshape(tm, n_lags * width)
    return jnp.dot(xs.astype(BF16), wb_ref[...], preferred_element_type=F32), n_lags


def _s5_slab_scan(bu_lags, ar, ai, cr, ci):
    bu, n_lags = bu_lags
    tm = bu.shape[0]
    hw = ar.shape[1]
    n_slab = tm // SUBLANES
    sub = lax.broadcasted_iota(jnp.int32, (SUBLANES, hw), 0)
    pr, pi = ar, ai
    tr = jnp.where(sub == 0, pr, 0.0)
    ti = jnp.where(sub == 0, pi, 0.0)
    steps = []
    d = 1
    while d < SUBLANES:
        mr = jnp.where(sub >= d, pr, 0.0)
        mi = jnp.where(sub >= d, pi, 0.0)
        if d >= n_lags:
            steps.append((d, mr, mi))
        sr = pltpu.roll(tr, d, 0)
        si = pltpu.roll(ti, d, 0)
        tr, ti = tr + (mr * sr - mi * si), ti + (mr * si + mi * sr)
        pr, pi = pr * pr - pi * pi, 2.0 * (pr * pi)
        d *= 2
    slabs = []
    for s in range(n_slab):
        rows = slice(s * SUBLANES, (s + 1) * SUBLANES)
        hr = bu[rows, :hw]
        hi = bu[rows, hw:]
        for d, mr, mi in steps:
            sr = pltpu.roll(hr, d, 0)
            si = pltpu.roll(hi, d, 0)
            hr, hi = hr + (mr * sr - mi * si), hi + (mr * si + mi * sr)
        cbr = jnp.broadcast_to(cr, (SUBLANES, hw))
        cbi = jnp.broadcast_to(ci, (SUBLANES, hw))
        hr, hi = hr + (tr * cbr - ti * cbi), hi + (tr * cbi + ti * cbr)
        cr = hr[SUBLANES - 1:SUBLANES, :]
        ci = hi[SUBLANES - 1:SUBLANES, :]
        slabs.append(jnp.concatenate([hr, hi], axis=1))
    return jnp.concatenate(slabs, axis=0).astype(BF16), cr, ci


def _s5_sample_kernel(xb_ref, wb_ref, wc_ref, ar_ref, ai_ref, d_ref, s0r_ref, s0i_ref, g_all_ref,
                      g_ref, sre_ref, sim_ref, bu_scr, h_scr, *, n_seq, n_steps):
    del g_all_ref
    hw = ar_ref.shape[1]
    xb = xb_ref[...]
    bu_scr[...] = jnp.dot(xb.astype(BF16), wb_ref[...], preferred_element_type=F32)
    ar = ar_ref[...]
    ai = ai_ref[...]
    hr = s0r_ref[...]
    hi = s0i_ref[...]
    for t in range(n_steps):
        sl = slice(t * n_seq, (t + 1) * n_seq)
        hr, hi = (ar * hr - ai * hi) + bu_scr[sl, :hw], (ar * hi + ai * hr) + bu_scr[sl, hw:]
        h_scr[sl, :hw] = hr.astype(h_scr.dtype)
        h_scr[sl, hw:] = hi.astype(h_scr.dtype)
    sre_ref[...] = hr
    sim_ref[...] = hi
    y = jnp.dot(h_scr[...], wc_ref[...], preferred_element_type=F32)
    g_ref[...] = jax.nn.gelu(y + d_ref[...] * xb)


def _half_ffn_matmuls(h, wg, wu, wd, *, bm):
    m, dm = h.shape
    dff = wg.shape[1]
    bn_up = _pick(dff, (256, 128))
    n_i, n_j = m // bm, dff // bn_up
    rb = _pick(dff, tuple(r for r in (64, 128, 256, 512) if n_i * n_j >= dff // r))
    n_rb = dff // rb
    assert n_i * n_j >= n_rb, "not enough up-projection steps to cover the down-projection weight"
    wd_blk = lambda i, j: (jnp.minimum(i * n_j + j, n_rb - 1), 0)
    a, wd = pl.pallas_call(
        _ffn_up_kernel,
        grid=(n_i, n_j),
        in_specs=[pl.BlockSpec((bm, dm), lambda i, j: (i, 0)),
                  pl.BlockSpec((dm, bn_up), lambda i, j: (0, j)),
                  pl.BlockSpec((dm, bn_up), lambda i, j: (0, j)),
                  pl.BlockSpec((rb, dm), wd_blk)],
        out_specs=[pl.BlockSpec((bm, bn_up), lambda i, j: (i, j)),
                   pl.BlockSpec((rb, dm), wd_blk)],
        out_shape=[jax.ShapeDtypeStruct((m, dff), BF16), jax.ShapeDtypeStruct((dff, dm), BF16)],
        compiler_params=_cparams("arbitrary", "arbitrary"),
        name="ffn_up",
    )(h, wg, wu, wd)

    bm_d = _pick(m, (1024, 512, 256, 128))
    bn_d = _pick(dm, (1024, 512, 256, 128))
    bk = dff // 2 if (dff // 2) % LANES == 0 else dff
    return pl.pallas_call(
        _ffn_down_kernel,
        grid=(m // bm_d, dm // bn_d, dff // bk),
        in_specs=[pl.BlockSpec((bm_d, bk), lambda i, j, k: (i, k)),
                  pl.BlockSpec((bk, bn_d), lambda i, j, k: (k, j))],
        out_specs=pl.BlockSpec((bm_d, bn_d), lambda i, j, k: (i, j)),
        out_shape=jax.ShapeDtypeStruct((m, dm), F32),
        compiler_params=_cparams("parallel", "parallel", "arbitrary"),
        name="ffn_down",
    )(a, wd)


def _matmul(x, w, *, bm, bn, out_dtype, name, n=None):
    m, k = x.shape
    n = w.shape[1] if n is None else n
    return pl.pallas_call(
        _mm_kernel,
        grid=(m // bm, n // bn),
        in_specs=[pl.BlockSpec((bm, k), lambda i, j: (i, 0)),
                  pl.BlockSpec((k, bn), lambda i, j: (0, j))],
        out_specs=pl.BlockSpec((bm, bn), lambda i, j: (i, j)),
        out_shape=jax.ShapeDtypeStruct((m, n), out_dtype),
        compiler_params=_cparams("parallel", "arbitrary"),
        name=name,
    )(x, w)


def kernel(x_prompt, x_sample, state_lru_h, state_conv, state_ssm_re, state_ssm_im, ffn1_pre_g, ffn1_post_g, ffn1_w_gate, ffn1_w_up, ffn1_w_down, mix_pre_g, mix_post_g, w_in, conv_w, conv_b, w_rg, b_rg, w_ig, b_ig, lru_lambda, ssm_a_re, ssm_a_im, ssm_log_dt, ssm_b_re, ssm_b_im, ssm_c_re, ssm_c_im, ssm_d, w_glu, b_glu, w_out_a, w_out_b, w_o, ffn2_pre_g, ffn2_post_g, ffn2_w_gate, ffn2_w_up, ffn2_w_down):
    nb, seq, dm = x_prompt.shape
    db, dseq, _ = x_sample.shape
    depth = state_lru_h.shape[0]
    assert depth == 1, "one decoder layer"
    d_lru = state_lru_h.shape[2]
    n_grp, ssm_p = state_ssm_re.shape[2], state_ssm_re.shape[3]
    d_ssm = ssm_d.shape[1]
    ssm_cg = d_ssm // n_grp
    d_in = w_in.shape[2]
    assert w_rg.shape[2] == LRU_BLK and conv_w.shape[1] == CONV_W
    assert d_in == 2 * d_lru + d_ssm + 2 * dm and seq >= CONV_W - 1 and dseq >= CONV_W - 1

    mp = nb * seq
    ms = db * dseq
    m = mp + ms
    sdt = state_lru_h.dtype

    row2 = lambda v: v.reshape(1, -1)
    bf = lambda v: v[0].astype(BF16)

    xp2 = x_prompt.reshape(mp, dm)
    xs2 = jnp.swapaxes(x_sample, 0, 1).reshape(ms, dm)

    tr = _pick(math.gcd(mp, ms), (256, 128, 64, 32, 16, 8))
    npt = mp // tr
    xp_spec = pl.BlockSpec((tr, dm), lambda i: (jnp.minimum(i, npt - 1), 0))
    xs_spec = pl.BlockSpec((tr, dm), lambda i: (jnp.maximum(i - npt, 0), 0))
    row_spec = pl.BlockSpec((tr, dm), lambda i: (i, 0))
    g_spec = pl.BlockSpec((1, dm), lambda i: (0, 0))

    h1 = pl.pallas_call(
        functools.partial(_prenorm_kernel, n_prompt_tiles=npt),
        grid=(m // tr,),
        in_specs=[xp_spec, xs_spec, g_spec],
        out_specs=row_spec,
        out_shape=jax.ShapeDtypeStruct((m, dm), BF16),
        compiler_params=_cparams("parallel"),
        name="prenorm1",
    )(xp2, xs2, row2(ffn1_pre_g))

    bm = _pick(m, (1536, 1024, 768, 512, 384, 256, 128))
    f1 = _half_ffn_matmuls(h1, ffn1_w_gate[0], ffn1_w_up[0], ffn1_w_down[0], bm=bm)

    x1, u = pl.pallas_call(
        functools.partial(_post_ffn1_kernel, n_prompt_tiles=npt),
        grid=(m // tr,),
        in_specs=[xp_spec, xs_spec, row_spec, g_spec, g_spec],
        out_specs=[row_spec, row_spec],
        out_shape=[jax.ShapeDtypeStruct((m, dm), F32), jax.ShapeDtypeStruct((m, dm), BF16)],
        compiler_params=_cparams("parallel"),
        name="post_ffn1",
    )(xp2, xs2, f1, row2(ffn1_post_g), row2(mix_pre_g))

    bn_in = _pick(math.gcd(d_lru, dm), (512, 256, 128))
    n_a = 2 * d_lru + d_ssm
    z = _matmul(u, w_in[0], bm=bm, bn=bn_in, out_dtype=F32, name="in_proj", n=n_a)

    cb = _pick(d_lru, (512, 256, 128))
    ncb = d_lru // cb
    hpb = cb // LRU_BLK
    wrg = bf(w_rg)
    wig = bf(w_ig)
    lru_params = (conv_w[0], row2(conv_b), wrg, row2(b_rg), wig, row2(b_ig), row2(lru_lambda))
    assert d_lru <= dm and d_ssm <= dm
    any_spec = pl.BlockSpec(memory_space=pl.ANY)
    assert mp % ms == 0, "sample rows must tile the unified row axis"
    s_blk = mp // ms

    n_lags = 2
    abr, abi, cfr, cfi = pl.pallas_call(
        _s5_disc_kernel,
        out_shape=[jax.ShapeDtypeStruct((n_grp, ssm_p), F32)] * 2
                  + [jax.ShapeDtypeStruct((n_lags, n_grp, ssm_p), F32)] * 2,
        name="s5_discretise",
    )(ssm_a_re[0], ssm_a_im[0], ssm_log_dt[0].reshape(n_grp, 1))
    bb_re = cfr[..., None] * ssm_b_re - cfi[..., None] * ssm_b_im
    bb_im = cfr[..., None] * ssm_b_im + cfi[..., None] * ssm_b_re

    gpb = LANES // ssm_cg
    nj = n_grp // gpb
    hw = gpb * ssm_p
    bb =jnp.stack([bb_re, bb_im]).reshape(2, n_lags, nj, gpb, ssm_p, ssm_cg).astype(BF16)
    bb = jnp.transpose(bb, (2, 1, 3, 5, 0, 4)).reshape(nj, n_lags * LANES, 2, ssm_p)
    assert LANES % ssm_p == 0 and gpb % (LANES // ssm_p) == 0
    hpl = LANES // ssm_p
    bb = jnp.concatenate([bb] * hpl, axis=3)
    wb = jnp.concatenate([bb[:, :, ri, :] for ri in range(2) for _ in range(gpb // hpl)], axis=2)
    wb_shape = (1, n_lags * LANES, 2 * hw)
    g_row = (lax.broadcasted_iota(jnp.int32, wb_shape, 1) // ssm_cg) % gpb
    h_col = (lax.broadcasted_iota(jnp.int32, wb_shape, 2) // ssm_p) % gpb
    wb = jnp.where(g_row == h_col, wb, jnp.zeros((), BF16))
    cc = jnp.stack([ssm_c_re[0], -ssm_c_im[0]]).reshape(2, nj, gpb, ssm_cg, ssm_p).astype(BF16)
    cc = jnp.transpose(cc, (1, 0, 4, 2, 3)).reshape(nj, 2, ssm_p, LANES)
    wc = jnp.concatenate([cc[:, ri, :, :] for ri in range(2) for _ in range(gpb)], axis=1)
    wc_shape = (1, 2 * hw, LANES)
    wc_mask = ((lax.broadcasted_iota(jnp.int32, wc_shape, 1) // ssm_p) % gpb
               == lax.broadcasted_iota(jnp.int32, wc_shape, 2) // ssm_cg)
    wc = jnp.where(wc_mask, wc, jnp.zeros((), BF16))
    abr3 = abr.reshape(nj, 1, hw)
    abi3 = abi.reshape(nj, 1, hw)
    xb_blk = (2 * d_lru) // LANES
    dsk = row2(ssm_d)

    n_par = 2 if nj % 2 == 0 and xb_blk % 2 == 0 else 1
    par_shape = (1, n_par * 2 * hw, n_par * LANES)
    par_mask = (lax.broadcasted_iota(jnp.int32, par_shape, 1) // (2 * hw)
                == lax.broadcasted_iota(jnp.int32, par_shape, 2) // LANES)
    wc_par = jnp.concatenate([wc.reshape(nj // n_par, n_par * 2 * hw, LANES)] * n_par, axis=2)
    wc_par = jnp.where(par_mask, wc_par, jnp.zeros((), BF16))
    n_gi = nj // n_par
    bm_gl = m // n_gi
    assert m % n_gi == 0 and bm_gl % (2 * SUBLANES) == 0

    def gl_cols_ok(t):
        bn = (2 * dm) // (nb * (seq // t)) if (2 * dm) % (nb * (seq // t)) == 0 else 0
        return bn > 0 and bn % LANES == 0 and n_a % bn == 0

    tm5 = _pick(seq, tuple(t for t in (512, 256, 128, 64, 32, 16, 8) if seq % t == 0 and gl_cols_ok(t)))
    rp5 = seq // tm5
    bn_gl = (2 * dm) // (nb * rp5)
    gl_blk = n_a // bn_gl
    gl, g_all, sre_p, sim_p = pl.pallas_call(
        _gates_s5_prompt_kernel,
        grid=(n_gi, nb, rp5),
        in_specs=[pl.BlockSpec((bm_gl, dm), lambda j, n, r: (j, 0)),
                  pl.BlockSpec((dm, bn_gl), lambda j, n, r: (0, gl_blk + n * rp5 + r)),
                  pl.BlockSpec((tm5, n_par * LANES), lambda j, n, r: (n * rp5 + r, xb_blk // n_par + j)),
                  pl.BlockSpec((n_par, n_lags * LANES, 2 * hw), lambda j, n, r: (j, 0, 0)),
                  pl.BlockSpec((None, n_par * 2 * hw, n_par * LANES), lambda j, n, r: (j, 0, 0)),
                  pl.BlockSpec((n_par, 1, hw), lambda j, n, r: (j, 0, 0)),
                  pl.BlockSpec((n_par, 1, hw), lambda j, n, r: (j, 0, 0)),
                  pl.BlockSpec((1, n_par * LANES), lambda j, n, r: (0, j)), any_spec],
        out_specs=[pl.BlockSpec((None, bm_gl, bn_gl), lambda j, n, r: (n * rp5 + r, j, 0)),
                   pl.BlockSpec((tm5, n_par * LANES), lambda j, n, r: (n * rp5 + r, j)),
                   pl.BlockSpec((None, 1, n_par * hw), lambda j, n, r: (n, 0, j)),
                   pl.BlockSpec((None, 1, n_par * hw), lambda j, n, r: (n, 0, j))],
        out_shape=[jax.ShapeDtypeStruct(((2 * dm) // bn_gl, m, bn_gl), F32),
                   jax.ShapeDtypeStruct((m, dm), F32),
                   jax.ShapeDtypeStruct((nb, 1, n_grp * ssm_p), F32),
                   jax.ShapeDtypeStruct((nb, 1, n_grp * ssm_p), F32)],
        scratch_shapes=[pltpu.VMEM((n_par, 1, hw), F32), pltpu.VMEM((n_par, 1, hw), F32)],
        input_output_aliases={8: 1},
        compiler_params=_cparams("arbitrary", "arbitrary", "arbitrary"),
        name="gates_s5_prompt",
    )(u, w_in[0], z, wb, wc_par, abr3, abi3, dsk, f1)

    s5_w_specs1 = [pl.BlockSpec((None, LANES, 2 * hw), lambda j: (j, 0, 0)),
                   pl.BlockSpec((None, 2 * hw, LANES), lambda j: (j, 0, 0)),
                   pl.BlockSpec((None, 1, hw), lambda j: (j, 0, 0)),
                   pl.BlockSpec((None, 1, hw), lambda j: (j, 0, 0)),
                   pl.BlockSpec((1, LANES), lambda j: (0, j))]
    g_all, sre_s, sim_s = pl.pallas_call(
        functools.partial(_s5_sample_kernel, n_seq=db, n_steps=dseq),
        grid=(nj,),
        in_specs=[pl.BlockSpec((ms, LANES), lambda j: (s_blk, xb_blk + j))] + s5_w_specs1
                 + [pl.BlockSpec((db, hw), lambda j: (0, j)), pl.BlockSpec((db, hw), lambda j: (0, j)), any_spec],
        out_specs=[pl.BlockSpec((ms, LANES), lambda j: (s_blk, j)),
                   pl.BlockSpec((db, hw), lambda j: (0, j)),
                   pl.BlockSpec((db, hw), lambda j: (0, j))],
        out_shape=[jax.ShapeDtypeStruct((m, dm), F32),
                   jax.ShapeDtypeStruct((db, n_grp * ssm_p), F32),
                   jax.ShapeDtypeStruct((db, n_grp * ssm_p), F32)],
        scratch_shapes=[pltpu.VMEM((ms, 2 * hw), F32), pltpu.VMEM((ms, 2 * hw), BF16)],
        input_output_aliases={8: 0},
        compiler_params=_cparams("parallel"),
        name="s5_sample",
    )(z, wb, wc, abr3, abi3, dsk,
      state_ssm_re[0].reshape(db, n_grp * ssm_p), state_ssm_im[0].reshape(db, n_grp * ssm_p), g_all)

    assert d_ssm // cb == ncb
    tm = _pick(seq, tuple(t for t in (512, 256, 128, 64, 32, 16, 8)
                          if seq % t == 0 and (m * t) % mp == 0 and ((m * t) // mp) % (2 * SUBLANES) == 0))
    rpt = seq // tm
    bm_u = (m * tm) // mp
    col = lambda shape: pl.BlockSpec(shape, lambda i, j: (0, j))
    head_spec = pl.BlockSpec((hpb, LRU_BLK, LRU_BLK), lambda i, j: (j, 0, 0))
    yb, ya, hl_p = pl.pallas_call(
        functools.partial(_glu_rglru_prompt_kernel, tiles_per_seq=rpt),
        grid=(nb * rpt, ncb),
        in_specs=[pl.BlockSpec((bm_u, d_ssm), lambda i, j: (i, 0)),
                  pl.BlockSpec((bm_u, cb), lambda i, j: (i, j)),
                  pl.BlockSpec((d_ssm, cb), lambda i, j: (0, j)),
                  col((1, cb)),
                  pl.BlockSpec((tm, cb), lambda i, j: (i, j)),
                  pl.BlockSpec((tm, cb), lambda i, j: (i, ncb + j)),
                  col((CONV_W, cb)), col((1, cb)), head_spec, col((1, cb)), head_spec, col((1, cb)),
                  col((1, cb)), any_spec],
        out_specs=[pl.BlockSpec((bm_u, cb), lambda i, j: (i, j)),
                   pl.BlockSpec((tm, cb), lambda i, j: (i, j)),
                   pl.BlockSpec((None, 1, cb), lambda i, j: (i, 0, j))],
        out_shape=[jax.ShapeDtypeStruct((m, d_ssm), BF16),
                   jax.ShapeDtypeStruct((m, dm), BF16),
                   jax.ShapeDtypeStruct((nb * rpt, 1, d_lru), F32)],
        scratch_shapes=[pltpu.VMEM((bm_u, d_ssm), BF16),
                        pltpu.VMEM((ncb, SUBLANES, cb), F32), pltpu.VMEM((ncb, 1, cb), F32)],
        input_output_aliases={4 + 2 + len(lru_params): 1},
        compiler_params=_cparams("arbitrary", "arbitrary"),
        name="glu_rglru_prompt",
    )(g_all, g_all, w_glu[0], row2(b_glu), z, z, *lru_params, h1)

    par1 = lambda shape: pl.BlockSpec(shape, lambda c: (0, c))
    cst = jnp.swapaxes(state_conv[0], 0, 1)
    ya, hl_s = pl.pallas_call(
        functools.partial(_rglru_sample_kernel, n_seq=db, n_steps=dseq),
        grid=(ncb,),
        in_specs=[pl.BlockSpec((ms, cb), lambda c: (s_blk, c)),
                  pl.BlockSpec((ms, cb), lambda c: (s_blk, ncb + c)),
                  pl.BlockSpec((CONV_W - 1, db, cb), lambda c: (0, 0, c)),
                  par1((db, cb)),
                  par1((CONV_W, cb)), par1((1, cb)),
                  pl.BlockSpec((hpb, LRU_BLK, LRU_BLK), lambda c: (c, 0, 0)), par1((1, cb)),
                  pl.BlockSpec((hpb, LRU_BLK, LRU_BLK), lambda c: (c, 0, 0)), par1((1, cb)),
                  par1((1, cb)), any_spec],
        out_specs=[pl.BlockSpec((ms, cb), lambda c: (s_blk, c)), par1((db, cb))],
        out_shape=[jax.ShapeDtypeStruct((m, dm), BF16), jax.ShapeDtypeStruct((db, d_lru), F32)],
        input_output_aliases={4 + len(lru_params): 0},
        compiler_params=_cparams("parallel"),
        name="rglru_sample",
    )(z, z, cst, state_lru_h[0], *lru_params, ya)

    bm_g = _pick(m, (1024, 512, 256, 128))
    bn_m = bn_gl
    gla_blk = 0
    glb_blk = dm // bn_m
    merged = pl.pallas_call(
        _merge_kernel,
        grid=(m // bm_g, dm // bn_m),
        in_specs=[pl.BlockSpec((bm_g, d_lru), lambda i, j: (i, 0)),
                  pl.BlockSpec((bm_g, d_ssm), lambda i, j: (i, 0)),
                  pl.BlockSpec((d_lru, bn_m), lambda i, j: (0, j)),
                  pl.BlockSpec((d_ssm, bn_m), lambda i, j: (0, j)),
                  pl.BlockSpec((None, bm_g, bn_m), lambda i, j: (gla_blk + j, i, 0)),
                  pl.BlockSpec((None, bm_g, bn_m), lambda i, j: (glb_blk + j, i, 0))],
        out_specs=pl.BlockSpec((bm_g, bn_m), lambda i, j: (i, j)),
        out_shape=jax.ShapeDtypeStruct((m, dm), BF16),
        compiler_params=_cparams("parallel", "arbitrary"),
        name="gated_merge",
    )(ya, yb, w_out_a[0], w_out_b[0], gl, gl)

    o = _matmul(merged, w_o[0], bm=bm, bn=bn_in, out_dtype=F32, name="o_proj")

    x2, h2 = pl.pallas_call(
        _post_mix_kernel,
        grid=(m // tr,),
        in_specs=[row_spec, row_spec, g_spec, g_spec],
        out_specs=[row_spec, row_spec],
        out_shape=[jax.ShapeDtypeStruct((m, dm), F32), jax.ShapeDtypeStruct((m, dm), BF16)],
        compiler_params=_cparams("parallel"),
        name="post_mix",
    )(x1, o, row2(mix_post_g), row2(ffn2_pre_g))

    f2 = _half_ffn_matmuls(h2, ffn2_w_gate[0], ffn2_w_up[0], ffn2_w_down[0], bm=bm)

    def final(rows, first_tile):
        return pl.pallas_call(
            _final_kernel,
            grid=(rows // tr,),
            in_specs=[pl.BlockSpec((tr, dm), lambda i: (first_tile + i, 0)),
                      pl.BlockSpec((tr, dm), lambda i: (first_tile + i, 0)),
                      g_spec],
            out_specs=row_spec,
            out_shape=jax.ShapeDtypeStruct((rows, dm), F32),
            compiler_params=_cparams("parallel"),
            name="final_residual",
        )(x2, f2, row2(ffn2_post_g))

    y_prompt = final(mp, 0).reshape(nb, seq, dm)
    y_sample = jnp.swapaxes(final(ms, npt).reshape(dseq, db, dm), 0, 1)

    nk = CONV_W - 1
    prompt_conv = jnp.stack([lax.slice(z, ((n + 1) * seq - nk, 0), ((n + 1) * seq, d_lru)) for n in range(nb)])
    sample_conv = jnp.swapaxes(lax.slice(z, (m - nk * db, 0), (m, d_lru)).reshape(nk, db, d_lru), 0, 1)
    st = lambda v, n: v.reshape(1, n, n_grp, ssm_p).astype(sdt)
    return (y_prompt, y_sample,
            hl_p.reshape(nb, rpt, d_lru)[:, rpt - 1][None].astype(sdt), prompt_conv[None].astype(sdt),
            st(sre_p, nb), st(sim_p, nb),
            hl_s.reshape(1, db, d_lru).astype(sdt), sample_conv[None].astype(sdt),
            st(sre_s, db), st(sim_s, db))
```

```python
import functools
import math

import jax
import jax.numpy as jnp
from jax import lax
from jax.experimental import pallas as pl
from jax.experimental.pallas import tpu as pltpu

F32 = jnp.float32
BF16 = jnp.bfloat16

EPS = 1e-6
C_RG = 8.0
CONV_W = 4
LANES = 128
SUBLANES = 8
LRU_BLK = 128
VMEM_LIMIT = 56 * 1024 * 1024


def _cparams(*sem):
    return pltpu.CompilerParams(dimension_semantics=sem, vmem_limit_bytes=VMEM_LIMIT)


def _pick(n, candidates):
    for c in candidates:
        if n % c == 0:
            return c
    raise ValueError(f"no tile in {candidates} divides {n}")


def _rms(x, g):
    return x * lax.rsqrt(jnp.mean(x * x, axis=-1, keepdims=True) + EPS) * g


def _softplus(x):
    return jnp.maximum(x, 0.0) + jnp.log1p(jnp.exp(-jnp.abs(x)))


def _sigmoid(x):
    return 0.5 * jnp.tanh(0.5 * x) + 0.5


def _two_group_rows(i, n_prompt_tiles, xp_ref, xs_ref, body):
    @pl.when(i < n_prompt_tiles)
    def _():
        body(xp_ref[...])

    @pl.when(i >= n_prompt_tiles)
    def _():
        body(xs_ref[...])


def _prenorm_kernel(xp_ref, xs_ref, g_ref, h_ref, *, n_prompt_tiles):
    def body(x):
        h_ref[...] = _rms(x, g_ref[...]).astype(h_ref.dtype)

    _two_group_rows(pl.program_id(0), n_prompt_tiles, xp_ref, xs_ref, body)


def _post_ffn1_kernel(xp_ref, xs_ref, f_ref, gpost_ref, gpre_ref, x1_ref, u_ref, *, n_prompt_tiles):
    def body(x):
        x1 = x + 0.5 * _rms(f_ref[...], gpost_ref[...])
        x1_ref[...] = x1
        u_ref[...] = _rms(x1, gpre_ref[...]).astype(u_ref.dtype)

    _two_group_rows(pl.program_id(0), n_prompt_tiles, xp_ref, xs_ref, body)


def _post_mix_kernel(x1_ref, o_ref, gpost_ref, gpre_ref, x2_ref, h_ref):
    x2 = x1_ref[...] + _rms(o_ref[...], gpost_ref[...])
    x2_ref[...] = x2
    h_ref[...] = _rms(x2, gpre_ref[...]).astype(h_ref.dtype)


def _final_kernel(x2_ref, f_ref, g_ref, y_ref):
    y_ref[...] = x2_ref[...] + 0.5 * _rms(f_ref[...], g_ref[...])


def _ffn_up_kernel(h_ref, wg_ref, wu_ref, wd_ref, a_ref, wd16_ref):
    wg = wg_ref[...].astype(BF16)
    wu = wu_ref[...].astype(BF16)
    half = h_ref.shape[0] // 2
    for rows in (slice(0, half), slice(half, None)):
        h = h_ref[rows, :]
        g = jnp.dot(h, wg, preferred_element_type=F32)
        u = jnp.dot(h, wu, preferred_element_type=F32)
        a_ref[rows, :] = (g * _sigmoid(g) * u).astype(a_ref.dtype)
    wd16_ref[...] = wd_ref[...].astype(wd16_ref.dtype)


def _ffn_down_kernel(a_ref, wd_ref, o_ref):
    @pl.when(pl.program_id(2) == 0)
    def _():
        o_ref[...] = jnp.dot(a_ref[...], wd_ref[...], preferred_element_type=F32)

    @pl.when(pl.program_id(2) != 0)
    def _():
        o_ref[...] += jnp.dot(a_ref[...], wd_ref[...], preferred_element_type=F32)


def _mm_kernel(x_ref, w_ref, o_ref):
    w = w_ref[...].astype(BF16)
    o_ref[...] = jnp.dot(x_ref[...], w, preferred_element_type=F32).astype(o_ref.dtype)


def _merge_kernel(ya_ref, yb_ref, wa_ref, wb_ref, gla_ref, glb_ref, o_ref):
    wa = wa_ref[...].astype(BF16)
    wb = wb_ref[...].astype(BF16)
    half = o_ref.shape[0] // 2
    for rows in (slice(0, half), slice(half, None)):
        pa = jnp.dot(ya_ref[rows, :], wa, preferred_element_type=F32)
        pb = jnp.dot(yb_ref[rows, :], wb, preferred_element_type=F32)
        m = _sigmoid(gla_ref[rows, :]) * pa + _sigmoid(glb_ref[rows, :]) * pb
        o_ref[rows, :] = m.astype(o_ref.dtype)


def _lru_gates(xc, wrg_ref, brg_ref, wig_ref, big_ref, lam_ref):
    xcb = xc.astype(BF16)
    rs, gs = [], []
    for hh in range(wrg_ref.shape[0]):
        xh = xcb[:, hh * LRU_BLK:(hh + 1) * LRU_BLK]
        rs.append(jnp.dot(xh, wrg_ref[hh], preferred_element_type=F32))
        gs.append(jnp.dot(xh, wig_ref[hh], preferred_element_type=F32))
    r = _sigmoid(jnp.concatenate(rs, axis=1) + brg_ref[...])
    i = _sigmoid(jnp.concatenate(gs, axis=1) + big_ref[...])
    log_a = -C_RG * r * _softplus(-lam_ref[...])
    a = jnp.exp(log_a)
    mult = jnp.sqrt(-jnp.tanh(log_a) * (a * a + 1.0))
    return a, mult * (i * xc)


def _glu_rglru_prompt_kernel(g_ref, gcol_ref, w_ref, b_ref,
                             xa_ref, ga_ref, cw_ref, cb_ref, wrg_ref, brg_ref, wig_ref, big_ref, lam_ref,
                             ya_all_ref, o_ref, ya_ref, hl_ref, g_scr, xcars, hcars, *, tiles_per_seq):
    del ya_all_ref
    i, j = pl.program_id(0), pl.program_id(1)

    @pl.when(j == 0)
    def _():
        g_scr[...] = g_ref[...].astype(g_scr.dtype)

    _rglru_prompt_tile(xa_ref, ga_ref, cw_ref, cb_ref, wrg_ref, brg_ref, wig_ref, big_ref, lam_ref,
                       ya_ref, hl_ref, xcars.at[j], hcars.at[j], i % tiles_per_seq == 0)
    s = jnp.dot(g_scr[...], w_ref[...].astype(BF16), preferred_element_type=F32) + b_ref[...]
    o_ref[...] = (gcol_ref[...] * _sigmoid(s)).astype(o_ref.dtype)


def _rglru_prompt_tile(xa_ref, ga_ref, cw_ref, cb_ref, wrg_ref, brg_ref, wig_ref, big_ref, lam_ref,
                       ya_ref, hl_ref, xcar, hcar, starts_sequence):
    tm = xa_ref.shape[0]

    @pl.when(starts_sequence)
    def _():
        xcar[...] = jnp.zeros_like(xcar)
        hcar[...] = jnp.zeros_like(hcar)

    x = xa_ref[...]
    xfull = jnp.concatenate([xcar[...], x], axis=0)
    xcar[...] = x[tm - SUBLANES:, :]
    cw = cw_ref[...]
    xc = cb_ref[...]
    for k in range(CONV_W - 1):
        xs = pltpu.roll(xfull, CONV_W - 1 - k, 0)[SUBLANES:, :]
        xc = xc + xs * cw[k:k + 1, :]
    xc = xc + x * cw[CONV_W - 1:CONV_W, :]

    a, b = _lru_gates(xc, wrg_ref, brg_ref, wig_ref, big_ref, lam_ref)
    cb = a.shape[1]
    n_slab = tm // SUBLANES
    a = a.reshape(n_slab, SUBLANES, cb)
    b = b.reshape(n_slab, SUBLANES, cb)
    sub = lax.broadcasted_iota(jnp.int32, (1, SUBLANES, cb), 1)
    d = 1
    while d < SUBLANES:
        m = sub >= d
        a_s = jnp.where(m, pltpu.roll(a, d, 1), 1.0)
        b_s = jnp.where(m, pltpu.roll(b, d, 1), 0.0)
        b = a * b_s + b
        a = a * a_s
        d *= 2
    h = hcar[...]
    hs = []
    for s in range(n_slab):
        h_slab = b[s] + a[s] * jnp.broadcast_to(h, (SUBLANES, cb))
        h = h_slab[SUBLANES - 1:SUBLANES, :]
        hs.append(h_slab)
    hcar[...] = h
    hl_ref[...] = h
    ya_ref[...] = (jnp.concatenate(hs, axis=0) * jax.nn.gelu(ga_ref[...])).astype(ya_ref.dtype)


def _rglru_sample_kernel(xa_ref, ga_ref, cst_ref, h0_ref, cw_ref, cb_ref, wrg_ref, brg_ref, wig_ref,
                         big_ref, lam_ref, ya_all_ref, ya_ref, hl_ref, *, n_seq, n_steps):
    del ya_all_ref
    x = xa_ref[...]
    xp = jnp.concatenate([cst_ref[k] for k in range(CONV_W - 1)] + [x], axis=0)
    rows = n_seq * n_steps
    cw = cw_ref[...]
    xc = cb_ref[...]
    for k in range(CONV_W):
        xc = xc + xp[k * n_seq:k * n_seq + rows, :] * cw[k:k + 1, :]
    a, b = _lru_gates(xc, wrg_ref, brg_ref, wig_ref, big_ref, lam_ref)
    h = h0_ref[...]
    for t in range(n_steps):
        sl = slice(t * n_seq, (t + 1) * n_seq)
        h = a[sl, :] * h + b[sl, :]
        ya_ref[sl, :] = (h * jax.nn.gelu(ga_ref[sl, :])).astype(ya_ref.dtype)
    hl_ref[...] = h


def _s5_disc_kernel(are_ref, aim_ref, ldt_ref, abr_ref, abi_ref, cfr_ref, cfi_ref):
    a_re = are_ref[...]
    a_im = aim_ref[...]
    dt = jnp.exp(ldt_ref[...])
    mag = jnp.exp(a_re * dt)
    abr = mag * jnp.cos(a_im * dt)
    abi = mag * jnp.sin(a_im * dt)
    den = a_re * a_re + a_im * a_im
    nr = abr - 1.0
    abr_ref[...] = abr
    abi_ref[...] = abi
    cr = (nr * a_re + abi * a_im) / den
    ci = (abi * a_re - nr * a_im) / den
    for q in range(cfr_ref.shape[0]):
        cfr_ref[q] = cr
        cfi_ref[q] = ci
        cr, ci = abr * cr - abi * ci, abr * ci + abi * cr


def _gates_s5_prompt_kernel(u_ref, w_ref, xb_ref, wb_ref, wc_ref, ar_ref, ai_ref, d_ref, g_all_ref,
                            gl_ref, g_ref, sre_ref, sim_ref, cre, cim):
    del g_all_ref
    n_par, _, hw = ar_ref.shape

    @pl.when(pl.program_id(2) == 0)
    def _():
        cre[...] = jnp.zeros_like(cre)
        cim[...] = jnp.zeros_like(cim)

    split = (gl_ref.shape[0] * 9 // 16) // (2 * SUBLANES) * (2 * SUBLANES)
    w = w_ref[...].astype(BF16)
    bus = [_s5_project_in(xb_ref[:, p * LANES:(p + 1) * LANES], wb_ref.at[p]) for p in range(n_par)]
    gl_ref[:split, :] = jnp.dot(u_ref[:split, :], w, preferred_element_type=F32)
    scans = [_s5_slab_scan(bus[p], ar_ref[p], ai_ref[p], cre[p], cim[p]) for p in range(n_par)]
    y = jnp.dot(jnp.concatenate([s[0] for s in scans], axis=1), wc_ref[...], preferred_element_type=F32)
    gl_ref[split:, :] = jnp.dot(u_ref[split:, :], w, preferred_element_type=F32)
    g_ref[...] = jax.nn.gelu(y + d_ref[...] * xb_ref[...])
    for p in range(n_par):
        _, cr, ci = scans[p]
        cre[p] = cr
        cim[p] = ci
        sre_ref[:, p * hw:(p + 1) * hw] = cr
        sim_ref[:, p * hw:(p + 1) * hw] = ci


def _s5_project_in(xb, wb_ref):
    tm, width = xb.shape
    n_slab = tm // SUBLANES
    n_lags = wb_ref.shape[0] // width
    xb3 = xb.reshape(n_slab, SUBLANES, width)
    subx = lax.broadcasted_iota(jnp.int32, (1, SUBLANES, width), 1)
    lagged = [xb3] + [jnp.where(subx >= q, pltpu.roll(xb3, q, 1), 0.0) for q in range(1, n_lags)]
    xs = jnp.concatenate(lagged, axis=2).reshape(tm, n_lags * width)
    return jnp.dot(xs.astype(BF16), wb_ref[...], preferred_element_type=F32), n_lags


def _s5_slab_scan(bu_lags, ar, ai, cr, ci):
    bu, n_lags = bu_lags
    tm = bu.shape[0]
    hw = ar.shape[1]
    n_slab = tm // SUBLANES
    sub = lax.broadcasted_iota(jnp.int32, (SUBLANES, hw), 0)
    pr, pi = ar, ai
    tr = jnp.where(sub == 0, pr, 0.0)
    ti = jnp.where(sub == 0, pi, 0.0)
    steps = []
    d = 1
    while d < SUBLANES:
        mr = jnp.where(sub >= d, pr, 0.0)
        mi = jnp.where(sub >= d, pi, 0.0)
        if d >= n_lags:
            steps.append((d, mr, mi))
        sr = pltpu.roll(tr, d, 0)
        si = pltpu.roll(ti, d, 0)
        tr, ti = tr + (mr * sr - mi * si), ti + (mr * si + mi * sr)
        pr, pi = pr * pr - pi * pi, 2.0 * (pr * pi)
        d *= 2
    slabs = []
    for s in range(n_slab):
        rows = slice(s * SUBLANES, (s + 1) * SUBLANES)
        hr = bu[rows, :hw]
        hi = bu[rows, hw:]
        for d, mr, mi in steps:
            sr = pltpu.roll(hr, d, 0)
            si = pltpu.roll(hi, d, 0)
            hr, hi = hr + (mr * sr - mi * si), hi + (mr * si + mi * sr)
        cbr = jnp.broadcast_to(cr, (SUBLANES, hw))
        cbi = jnp.broadcast_to(ci, (SUBLANES, hw))
        hr, hi = hr + (tr * cbr - ti * cbi), hi + (tr * cbi + ti * cbr)
        cr = hr[SUBLANES - 1:SUBLANES, :]
        ci = hi[SUBLANES - 1:SUBLANES, :]
        slabs.append(jnp.concatenate([hr, hi], axis=1))
    return jnp.concatenate(slabs, axis=0).astype(BF16), cr, ci


def _s5_sample_kernel(xb_ref, wb_ref, wc_ref, ar_ref, ai_ref, d_ref, s0r_ref, s0i_ref, g_all_ref,
                      g_ref, sre_ref, sim_ref, bu_scr, h_scr, *, n_seq, n_steps):
    del g_all_ref
    hw = ar_ref.shape[1]
    xb = xb_ref[...]
    bu_scr[...] = jnp.dot(xb.astype(BF16), wb_ref[...], preferred_element_type=F32)
    ar = ar_ref[...]
    ai = ai_ref[...]
    hr = s0r_ref[...]
    hi = s0i_ref[...]
    for t in range(n_steps):
        sl = slice(t * n_seq, (t + 1) * n_seq)
        hr, hi = (ar * hr - ai * hi) + bu_scr[sl, :hw], (ar * hi + ai * hr) + bu_scr[sl, hw:]
        h_scr[sl, :hw] = hr.astype(h_scr.dtype)
        h_scr[sl, hw:] = hi.astype(h_scr.dtype)
    sre_ref[...] = hr
    sim_ref[...] = hi
    y = jnp.dot(h_scr[...], wc_ref[...], preferred_element_type=F32)
    g_ref[...] = jax.nn.gelu(y + d_ref[...] * xb)


def _half_ffn_matmuls(h, wg, wu, wd, *, bm):
    m, dm = h.shape
    dff = wg.shape[1]
    bn_up = _pick(dff, (256, 128))
    n_i, n_j = m // bm, dff // bn_up
    rb = _pick(dff, tuple(r for r in (64, 128, 256, 512) if n_i * n_j >= dff // r))
    n_rb = dff // rb
    assert n_i * n_j >= n_rb, "not enough up-projection steps to cover the down-projection weight"
    wd_blk = lambda i, j: (jnp.minimum(i * n_j + j, n_rb - 1), 0)
    stream = pltpu.emit_pipeline(
        _ffn_up_kernel,
        grid=(n_i, n_j),
        in_specs=[pl.BlockSpec((bm, dm), lambda i, j: (i, 0)),
                  pl.BlockSpec((dm, bn_up), lambda i, j: (0, j)),
                  pl.BlockSpec((dm, bn_up), lambda i, j: (0, j)),
                  pl.BlockSpec((rb, dm), wd_blk)],
        out_specs=[pl.BlockSpec((bm, bn_up), lambda i, j: (i, j)),
                   pl.BlockSpec((rb, dm), wd_blk)],
    )
    hbm = pl.BlockSpec(memory_space=pl.ANY)
    a, wd = pl.pallas_call(
        lambda *refs: stream(*refs),
        in_specs=[hbm] * 4,
        out_specs=[hbm] * 2,
        out_shape=[jax.ShapeDtypeStruct((m, dff), BF16), jax.ShapeDtypeStruct((dff, dm), BF16)],
        compiler_params=pltpu.CompilerParams(vmem_limit_bytes=VMEM_LIMIT),
        name="ffn_up",
    )(h, wg, wu, wd)

    bm_d = _pick(m, (1024, 512, 256, 128))
    bn_d = _pick(dm, (1024, 512, 256, 128))
    bk = dff // 2 if (dff // 2) % LANES == 0 else dff
    return pl.pallas_call(
        _ffn_down_kernel,
        grid=(m // bm_d, dm // bn_d, dff // bk),
        in_specs=[pl.BlockSpec((bm_d, bk), lambda i, j, k: (i, k)),
                  pl.BlockSpec((bk, bn_d), lambda i, j, k: (k, j))],
        out_specs=pl.BlockSpec((bm_d, bn_d), lambda i, j, k: (i, j)),
        out_shape=jax.ShapeDtypeStruct((m, dm), F32),
        compiler_params=_cparams("parallel", "parallel", "arbitrary"),
        name="ffn_down",
    )(a, wd)


def _matmul(x, w, *, bm, bn, out_dtype, name, n=None):
    m, k = x.shape
    n = w.shape[1] if n is None else n
    return pl.pallas_call(
        _mm_kernel,
        grid=(m // bm, n // bn),
        in_specs=[pl.BlockSpec((bm, k), lambda i, j: (i, 0)),
                  pl.BlockSpec((k, bn), lambda i, j: (0, j))],
        out_specs=pl.BlockSpec((bm, bn), lambda i, j: (i, j)),
        out_shape=jax.ShapeDtypeStruct((m, n), out_dtype),
        compiler_params=_cparams("parallel", "arbitrary"),
        name=name,
    )(x, w)


def kernel(x_prompt, x_sample, state_lru_h, state_conv, state_ssm_re, state_ssm_im, ffn1_pre_g, ffn1_post_g, ffn1_w_gate, ffn1_w_up, ffn1_w_down, mix_pre_g, mix_post_g, w_in, conv_w, conv_b, w_rg, b_rg, w_ig, b_ig, lru_lambda, ssm_a_re, ssm_a_im, ssm_log_dt, ssm_b_re, ssm_b_im, ssm_c_re, ssm_c_im, ssm_d, w_glu, b_glu, w_out_a, w_out_b, w_o, ffn2_pre_g, ffn2_post_g, ffn2_w_gate, ffn2_w_up, ffn2_w_down):
    nb, seq, dm = x_prompt.shape
    db, dseq, _ = x_sample.shape
    depth = state_lru_h.shape[0]
    assert depth == 1, "one decoder layer"
    d_lru = state_lru_h.shape[2]
    n_grp, ssm_p = state_ssm_re.shape[2], state_ssm_re.shape[3]
    d_ssm = ssm_d.shape[1]
    ssm_cg = d_ssm // n_grp
    d_in = w_in.shape[2]
    assert w_rg.shape[2] == LRU_BLK and conv_w.shape[1] == CONV_W
    assert d_in == 2 * d_lru + d_ssm + 2 * dm and seq >= CONV_W - 1 and dseq >= CONV_W - 1

    mp = nb * seq
    ms = db * dseq
    m = mp + ms
    sdt = state_lru_h.dtype

    row2 = lambda v: v.reshape(1, -1)
    bf = lambda v: v[0].astype(BF16)

    xp2 = x_prompt.reshape(mp, dm)
    xs2 = jnp.swapaxes(x_sample, 0, 1).reshape(ms, dm)

    tr = _pick(math.gcd(mp, ms), (256, 128, 64, 32, 16, 8))
    npt = mp // tr
    xp_spec = pl.BlockSpec((tr, dm), lambda i: (jnp.minimum(i, npt - 1), 0))
    xs_spec = pl.BlockSpec((tr, dm), lambda i: (jnp.maximum(i - npt, 0), 0))
    row_spec = pl.BlockSpec((tr, dm), lambda i: (i, 0))
    g_spec = pl.BlockSpec((1, dm), lambda i: (0, 0))

    h1 = pl.pallas_call(
        functools.partial(_prenorm_kernel, n_prompt_tiles=npt),
        grid=(m // tr,),
        in_specs=[xp_spec, xs_spec, g_spec],
        out_specs=row_spec,
        out_shape=jax.ShapeDtypeStruct((m, dm), BF16),
        compiler_params=_cparams("parallel"),
        name="prenorm1",
    )(xp2, xs2, row2(ffn1_pre_g))

    bm = _pick(m, (1536, 1024, 768, 512, 384, 256, 128))
    f1 = _half_ffn_matmuls(h1, ffn1_w_gate[0], ffn1_w_up[0], ffn1_w_down[0], bm=bm)

    x1, u = pl.pallas_call(
        functools.partial(_post_ffn1_kernel, n_prompt_tiles=npt),
        grid=(m // tr,),
        in_specs=[xp_spec, xs_spec, row_spec, g_spec, g_spec],
        out_specs=[row_spec, row_spec],
        out_shape=[jax.ShapeDtypeStruct((m, dm), F32), jax.ShapeDtypeStruct((m, dm), BF16)],
        compiler_params=_cparams("parallel"),
        name="post_ffn1",
    )(xp2, xs2, f1, row2(ffn1_post_g), row2(mix_pre_g))

    bn_in = _pick(math.gcd(d_lru, dm), (512, 256, 128))
    n_a = 2 * d_lru + d_ssm
    z = _matmul(u, w_in[0], bm=bm, bn=bn_in, out_dtype=F32, name="in_proj", n=n_a)

    cb = _pick(d_lru, (512, 256, 128))
    ncb = d_lru // cb
    hpb = cb // LRU_BLK
    wrg = bf(w_rg)
    wig = bf(w_ig)
    lru_params = (conv_w[0], row2(conv_b), wrg, row2(b_rg), wig, row2(b_ig), row2(lru_lambda))
    assert d_lru <= dm and d_ssm <= dm
    any_spec = pl.BlockSpec(memory_space=pl.ANY)
    assert mp % ms == 0, "sample rows must tile the unified row axis"
    s_blk = mp // ms

    n_lags = 2
    abr, abi, cfr, cfi = pl.pallas_call(
        _s5_disc_kernel,
        out_shape=[jax.ShapeDtypeStruct((n_grp, ssm_p), F32)] * 2
                  + [jax.ShapeDtypeStruct((n_lags, n_grp, ssm_p), F32)] * 2,
        name="s5_discretise",
    )(ssm_a_re[0], ssm_a_im[0], ssm_log_dt[0].reshape(n_grp, 1))
    bb_re = cfr[..., None] * ssm_b_re - cfi[..., None] * ssm_b_im
    bb_im = cfr[..., None] * ssm_b_im + cfi[..., None] * ssm_b_re

    gpb = LANES // ssm_cg
    nj = n_grp // gpb
    hw = gpb * ssm_p
    bb =jnp.stack([bb_re, bb_im]).reshape(2, n_lags, nj, gpb, ssm_p, ssm_cg).astype(BF16)
    bb = jnp.transpose(bb, (2, 1, 3, 5, 0, 4)).reshape(nj, n_lags * LANES, 2, ssm_p)
    assert LANES % ssm_p == 0 and gpb % (LANES // ssm_p) == 0
    hpl = LANES // ssm_p
    bb = jnp.concatenate([bb] * hpl, axis=3)
    wb = jnp.concatenate([bb[:, :, ri, :] for ri in range(2) for _ in range(gpb // hpl)], axis=2)
    wb_shape = (1, n_lags * LANES, 2 * hw)
    g_row = (lax.broadcasted_iota(jnp.int32, wb_shape, 1) // ssm_cg) % gpb
    h_col = (lax.broadcasted_iota(jnp.int32, wb_shape, 2) // ssm_p) % gpb
    wb = jnp.where(g_row == h_col, wb, jnp.zeros((), BF16))
    cc = jnp.stack([ssm_c_re[0], -ssm_c_im[0]]).reshape(2, nj, gpb, ssm_cg, ssm_p).astype(BF16)
    cc = jnp.transpose(cc, (1, 0, 4, 2, 3)).reshape(nj, 2, ssm_p, LANES)
    wc = jnp.concatenate([cc[:, ri, :, :] for ri in range(2) for _ in range(gpb)], axis=1)
    wc_shape = (1, 2 * hw, LANES)
    wc_mask = ((lax.broadcasted_iota(jnp.int32, wc_shape, 1) // ssm_p) % gpb
               == lax.broadcasted_iota(jnp.int32, wc_shape, 2) // ssm_cg)
    wc = jnp.where(wc_mask, wc, jnp.zeros((), BF16))
    abr3 = abr.reshape(nj, 1, hw)
    abi3 = abi.reshape(nj, 1, hw)
    xb_blk = (2 * d_lru) // LANES
    dsk = row2(ssm_d)

    n_par = 2 if nj % 2 == 0 and xb_blk % 2 == 0 else 1
    par_shape = (1, n_par * 2 * hw, n_par * LANES)
    par_mask = (lax.broadcasted_iota(jnp.int32, par_shape, 1) // (2 * hw)
                == lax.broadcasted_iota(jnp.int32, par_shape, 2) // LANES)
    wc_par = jnp.concatenate([wc.reshape(nj // n_par, n_par * 2 * hw, LANES)] * n_par, axis=2)
    wc_par = jnp.where(par_mask, wc_par, jnp.zeros((), BF16))
    n_gi = nj // n_par
    bm_gl = m // n_gi
    assert m % n_gi == 0 and bm_gl % (2 * SUBLANES) == 0

    def gl_cols_ok(t):
        bn = (2 * dm) // (nb * (seq // t)) if (2 * dm) % (nb * (seq // t)) == 0 else 0
        return bn > 0 and bn % LANES == 0 and n_a % bn == 0

    tm5 = _pick(seq, tuple(t for t in (512, 256, 128, 64, 32, 16, 8) if seq % t == 0 and gl_cols_ok(t)))
    rp5 = seq // tm5
    bn_gl = (2 * dm) // (nb * rp5)
    gl_blk = n_a // bn_gl
    gl, g_all, sre_p, sim_p = pl.pallas_call(
        _gates_s5_prompt_kernel,
        grid=(n_gi, nb, rp5),
        in_specs=[pl.BlockSpec((bm_gl, dm), lambda j, n, r: (j, 0)),
                  pl.BlockSpec((dm, bn_gl), lambda j, n, r: (0, gl_blk + n * rp5 + r)),
                  pl.BlockSpec((tm5, n_par * LANES), lambda j, n, r: (n * rp5 + r, xb_blk // n_par + j)),
                  pl.BlockSpec((n_par, n_lags * LANES, 2 * hw), lambda j, n, r: (j, 0, 0)),
                  pl.BlockSpec((None, n_par * 2 * hw, n_par * LANES), lambda j, n, r: (j, 0, 0)),
                  pl.BlockSpec((n_par, 1, hw), lambda j, n, r: (j, 0, 0)),
                  pl.BlockSpec((n_par, 1, hw), lambda j, n, r: (j, 0, 0)),
                  pl.BlockSpec((1, n_par * LANES), lambda j, n, r: (0, j)), any_spec],
        out_specs=[pl.BlockSpec((bm_gl, bn_gl), lambda j, n, r: (j, n * rp5 + r)),
                   pl.BlockSpec((tm5, n_par * LANES), lambda j, n, r: (n * rp5 + r, j)),
                   pl.BlockSpec((None, 1, n_par * hw), lambda j, n, r: (n, 0, j)),
                   pl.BlockSpec((None, 1, n_par * hw), lambda j, n, r: (n, 0, j))],
        out_shape=[jax.ShapeDtypeStruct((m, 2 * dm), F32),
                   jax.ShapeDtypeStruct((m, dm), F32),
                   jax.ShapeDtypeStruct((nb, 1, n_grp * ssm_p), F32),
                   jax.ShapeDtypeStruct((nb, 1, n_grp * ssm_p), F32)],
        scratch_shapes=[pltpu.VMEM((n_par, 1, hw), F32), pltpu.VMEM((n_par, 1, hw), F32)],
        input_output_aliases={8: 1},
        compiler_params=_cparams("arbitrary", "arbitrary", "arbitrary"),
        name="gates_s5_prompt",
    )(u, w_in[0], z, wb, wc_par, abr3, abi3, dsk, f1)

    s5_w_specs1 = [pl.BlockSpec((None, LANES, 2 * hw), lambda j: (j, 0, 0)),
                   pl.BlockSpec((None, 2 * hw, LANES), lambda j: (j, 0, 0)),
                   pl.BlockSpec((None, 1, hw), lambda j: (j, 0, 0)),
                   pl.BlockSpec((None, 1, hw), lambda j: (j, 0, 0)),
                   pl.BlockSpec((1, LANES), lambda j: (0, j))]
    g_all, sre_s, sim_s = pl.pallas_call(
        functools.partial(_s5_sample_kernel, n_seq=db, n_steps=dseq),
        grid=(nj,),
        in_specs=[pl.BlockSpec((ms, LANES), lambda j: (s_blk, xb_blk + j))] + s5_w_specs1
                 + [pl.BlockSpec((db, hw), lambda j: (0, j)), pl.BlockSpec((db, hw), lambda j: (0, j)), any_spec],
        out_specs=[pl.BlockSpec((ms, LANES), lambda j: (s_blk, j)),
                   pl.BlockSpec((db, hw), lambda j: (0, j)),
                   pl.BlockSpec((db, hw), lambda j: (0, j))],
        out_shape=[jax.ShapeDtypeStruct((m, dm), F32),
                   jax.ShapeDtypeStruct((db, n_grp * ssm_p), F32),
                   jax.ShapeDtypeStruct((db, n_grp * ssm_p), F32)],
        scratch_shapes=[pltpu.VMEM((ms, 2 * hw), F32), pltpu.VMEM((ms, 2 * hw), BF16)],
        input_output_aliases={8: 0},
        compiler_params=_cparams("parallel"),
        name="s5_sample",
    )(z, wb, wc, abr3, abi3, dsk,
      state_ssm_re[0].reshape(db, n_grp * ssm_p), state_ssm_im[0].reshape(db, n_grp * ssm_p), g_all)

    assert d_ssm // cb == ncb
    tm = _pick(seq, tuple(t for t in (512, 256, 128, 64, 32, 16, 8)
                          if seq % t == 0 and (m * t) % mp == 0 and ((m * t) // mp) % (2 * SUBLANES) == 0))
    rpt = seq // tm
    bm_u = (m * tm) // mp
    col = lambda shape: pl.BlockSpec(shape, lambda i, j: (0, j))
    head_spec = pl.BlockSpec((hpb, LRU_BLK, LRU_BLK), lambda i, j: (j, 0, 0))
    yb, ya, hl_p = pl.pallas_call(
        functools.partial(_glu_rglru_prompt_kernel, tiles_per_seq=rpt),
        grid=(nb * rpt, ncb),
        in_specs=[pl.BlockSpec((bm_u, d_ssm), lambda i, j: (i, 0)),
                  pl.BlockSpec((bm_u, cb), lambda i, j: (i, j)),
                  pl.BlockSpec((d_ssm, cb), lambda i, j: (0, j)),
                  col((1, cb)),
                  pl.BlockSpec((tm, cb), lambda i, j: (i, j)),
                  pl.BlockSpec((tm, cb), lambda i, j: (i, ncb + j)),
                  col((CONV_W, cb)), col((1, cb)), head_spec, col((1, cb)), head_spec, col((1, cb)),
                  col((1, cb)), any_spec],
        out_specs=[pl.BlockSpec((bm_u, cb), lambda i, j: (i, j)),
                   pl.BlockSpec((tm, cb), lambda i, j: (i, j)),
                   pl.BlockSpec((None, 1, cb), lambda i, j: (i, 0, j))],
        out_shape=[jax.ShapeDtypeStruct((m, d_ssm), BF16),
                   jax.ShapeDtypeStruct((m, dm), BF16),
                   jax.ShapeDtypeStruct((nb * rpt, 1, d_lru), F32)],
        scratch_shapes=[pltpu.VMEM((bm_u, d_ssm), BF16),
                        pltpu.VMEM((ncb, SUBLANES, cb), F32), pltpu.VMEM((ncb, 1, cb), F32)],
        input_output_aliases={4 + 2 + len(lru_params): 1},
        compiler_params=_cparams("arbitrary", "arbitrary"),
        name="glu_rglru_prompt",
    )(g_all, g_all, w_glu[0], row2(b_glu), z, z, *lru_params, h1)

    par1 = lambda shape: pl.BlockSpec(shape, lambda c: (0, c))
    cst = jnp.swapaxes(state_conv[0], 0, 1)
    ya, hl_s = pl.pallas_call(
        functools.partial(_rglru_sample_kernel, n_seq=db, n_steps=dseq),
        grid=(ncb,),
        in_specs=[pl.BlockSpec((ms, cb), lambda c: (s_blk, c)),
                  pl.BlockSpec((ms, cb), lambda c: (s_blk, ncb + c)),
                  pl.BlockSpec((CONV_W - 1, db, cb), lambda c: (0, 0, c)),
                  par1((db, cb)),
                  par1((CONV_W, cb)), par1((1, cb)),
                  pl.BlockSpec((hpb, LRU_BLK, LRU_BLK), lambda c: (c, 0, 0)), par1((1, cb)),
                  pl.BlockSpec((hpb, LRU_BLK, LRU_BLK), lambda c: (c, 0, 0)), par1((1, cb)),
                  par1((1, cb)), any_spec],
        out_specs=[pl.BlockSpec((ms, cb), lambda c: (s_blk, c)), par1((db, cb))],
        out_shape=[jax.ShapeDtypeStruct((m, dm), BF16), jax.ShapeDtypeStruct((db, d_lru), F32)],
        input_output_aliases={4 + len(lru_params): 0},
        compiler_params=_cparams("parallel"),
        name="rglru_sample",
    )(z, z, cst, state_lru_h[0], *lru_params, ya)

    bm_g = _pick(m, (1024, 512, 256, 128))
    bn_m = bn_in
    gla_blk = 0
    glb_blk = dm // bn_m
    merged = pl.pallas_call(
        _merge_kernel,
        grid=(m // bm_g, dm // bn_m),
        in_specs=[pl.BlockSpec((bm_g, d_lru), lambda i, j: (i, 0)),
                  pl.BlockSpec((bm_g, d_ssm), lambda i, j: (i, 0)),
                  pl.BlockSpec((d_lru, bn_m), lambda i, j: (0, j)),
                  pl.BlockSpec((d_ssm, bn_m), lambda i, j: (0, j)),
                  pl.BlockSpec((bm_g, bn_m), lambda i, j: (i, gla_blk + j)),
                  pl.BlockSpec((bm_g, bn_m), lambda i, j: (i, glb_blk + j))],
        out_specs=pl.BlockSpec((bm_g, bn_m), lambda i, j: (i, j)),
        out_shape=jax.ShapeDtypeStruct((m, dm), BF16),
        compiler_params=_cparams("parallel", "arbitrary"),
        name="gated_merge",
    )(ya, yb, w_out_a[0], w_out_b[0], gl, gl)

    o = _matmul(merged, w_o[0], bm=bm, bn=bn_in, out_dtype=F32, name="o_proj")

    x2, h2 = pl.pallas_call(
        _post_mix_kernel,
        grid=(m // tr,),
        in_specs=[row_spec, row_spec, g_spec, g_spec],
        out_specs=[row_spec, row_spec],
        out_shape=[jax.ShapeDtypeStruct((m, dm), F32), jax.ShapeDtypeStruct((m, dm), BF16)],
        compiler_params=_cparams("parallel"),
        name="post_mix",
    )(x1, o, row2(mix_post_g), row2(ffn2_pre_g))

    f2 = _half_ffn_matmuls(h2, ffn2_w_gate[0], ffn2_w_up[0], ffn2_w_down[0], bm=bm)

    def final(rows, first_tile):
        return pl.pallas_call(
            _final_kernel,
            grid=(rows // tr,),
            in_specs=[pl.BlockSpec((tr, dm), lambda i: (first_tile + i, 0)),
                      pl.BlockSpec((tr, dm), lambda i: (first_tile + i, 0)),
                      g_spec],
            out_specs=row_spec,
            out_shape=jax.ShapeDtypeStruct((rows, dm), F32),
            compiler_params=_cparams("parallel"),
            name="final_residual",
        )(x2, f2, row2(ffn2_post_g))

    y_prompt = final(mp, 0).reshape(nb, seq, dm)
    y_sample = jnp.swapaxes(final(ms, npt).reshape(dseq, db, dm), 0, 1)

    nk = CONV_W - 1
    prompt_conv = jnp.stack([lax.slice(z, ((n + 1) * seq - nk, 0), ((n + 1) * seq, d_lru)) for n in range(nb)])
    sample_conv = jnp.swapaxes(lax.slice(z, (m - nk * db, 0), (m, d_lru)).reshape(nk, db, d_lru), 0, 1)
    st = lambda v, n: v.reshape(1, n, n_grp, ssm_p).astype(sdt)
    return (y_prompt, y_sample,
            hl_p.reshape(nb, rpt, d_lru)[:, rpt - 1][None].astype(sdt), prompt_conv[None].astype(sdt),
            st(sre_p, nb), st(sim_p, nb),
            hl_s.reshape(1, db, d_lru).astype(sdt), sample_conv[None].astype(sdt),
            st(sre_s, db), st(sim_s, db))
```

```python
import functools
import math

import jax
import jax.numpy as jnp
from jax import lax
from jax.experimental import pallas as pl
from jax.experimental.pallas import tpu as pltpu

F32 = jnp.float32
BF16 = jnp.bfloat16

EPS = 1e-6
C_RG = 8.0
CONV_W = 4
LANES = 128
SUBLANES = 8
LRU_BLK = 128
VMEM_LIMIT = 56 * 1024 * 1024


def _cparams(*sem):
    return pltpu.CompilerParams(dimension_semantics=sem, vmem_limit_bytes=VMEM_LIMIT)


def _pick(n, candidates):
    for c in candidates:
        if n % c == 0:
            return c
    raise ValueError(f"no tile in {candidates} divides {n}")


def _rms(x, g):
    return x * lax.rsqrt(jnp.mean(x * x, axis=-1, keepdims=True) + EPS) * g


def _softplus(x):
    return jnp.maximum(x, 0.0) + jnp.log1p(jnp.exp(-jnp.abs(x)))


def _sigmoid(x):
    return 0.5 * jnp.tanh(0.5 * x) + 0.5


def _two_group_rows(i, n_prompt_tiles, xp_ref, xs_ref, body):
    @pl.when(i < n_prompt_tiles)
    def _():
        body(xp_ref[...])

    @pl.when(i >= n_prompt_tiles)
    def _():
        body(xs_ref[...])


def _prenorm_kernel(xp_ref, xs_ref, g_ref, h_ref, *, n_prompt_tiles):
    def body(x):
        h_ref[...] = _rms(x, g_ref[...]).astype(h_ref.dtype)

    _two_group_rows(pl.program_id(0), n_prompt_tiles, xp_ref, xs_ref, body)


def _post_ffn1_kernel(xp_ref, xs_ref, f_ref, gpost_ref, gpre_ref, x1_ref, u_ref, *, n_prompt_tiles):
    def body(x):
        x1 = x + 0.5 * _rms(f_ref[...], gpost_ref[...])
        x1_ref[...] = x1
        u_ref[...] = _rms(x1, gpre_ref[...]).astype(u_ref.dtype)

    _two_group_rows(pl.program_id(0), n_prompt_tiles, xp_ref, xs_ref, body)


def _post_mix_kernel(x1_ref, o_ref, gpost_ref, gpre_ref, x2_ref, h_ref):
    x2 = x1_ref[...] + _rms(o_ref[...], gpost_ref[...])
    x2_ref[...] = x2
    h_ref[...] = _rms(x2, gpre_ref[...]).astype(h_ref.dtype)


def _final_kernel(x2_ref, f_ref, g_ref, y_ref):
    y_ref[...] = x2_ref[...] + 0.5 * _rms(f_ref[...], g_ref[...])


def _ffn_up_kernel(h_ref, wg_ref, wu_ref, wd_ref, a_ref, wd16_ref):
    wg = wg_ref[...].astype(BF16)
    wu = wu_ref[...].astype(BF16)
    half = h_ref.shape[0] // 2
    for rows in (slice(0, half), slice(half, None)):
        h = h_ref[rows, :]
        g = jnp.dot(h, wg, preferred_element_type=F32)
        u = jnp.dot(h, wu, preferred_element_type=F32)
        a_ref[rows, :] = (g * _sigmoid(g) * u).astype(a_ref.dtype)
    wd16_ref[...] = wd_ref[...].astype(wd16_ref.dtype)


def _ffn_down_kernel(a_ref, wd_ref, o_ref):
    @pl.when(pl.program_id(2) == 0)
    def _():
        o_ref[...] = jnp.dot(a_ref[...], wd_ref[...], preferred_element_type=F32)

    @pl.when(pl.program_id(2) != 0)
    def _():
        o_ref[...] += jnp.dot(a_ref[...], wd_ref[...], preferred_element_type=F32)


def _mm_kernel(x_ref, w_ref, o_ref):
    w = w_ref[...].astype(BF16)
    o_ref[...] = jnp.dot(x_ref[...], w, preferred_element_type=F32).astype(o_ref.dtype)


def _merge_kernel(ya_ref, yb_ref, wa_ref, wb_ref, gla_ref, glb_ref, o_ref):
    wa = wa_ref[...].astype(BF16)
    wb = wb_ref[...].astype(BF16)
    half = o_ref.shape[0] // 2
    for rows in (slice(0, half), slice(half, None)):
        pa = jnp.dot(ya_ref[rows, :], wa, preferred_element_type=F32)
        pb = jnp.dot(yb_ref[rows, :], wb, preferred_element_type=F32)
        m = _sigmoid(gla_ref[rows, :]) * pa + _sigmoid(glb_ref[rows, :]) * pb
        o_ref[rows, :] = m.astype(o_ref.dtype)


def _lru_gates(xc, wrg_ref, brg_ref, wig_ref, big_ref, lam_ref):
    xcb = xc.astype(BF16)
    rs, gs = [], []
    for hh in range(wrg_ref.shape[0]):
        xh = xcb[:, hh * LRU_BLK:(hh + 1) * LRU_BLK]
        rs.append(jnp.dot(xh, wrg_ref[hh], preferred_element_type=F32))
        gs.append(jnp.dot(xh, wig_ref[hh], preferred_element_type=F32))
    r = _sigmoid(jnp.concatenate(rs, axis=1) + brg_ref[...])
    i = _sigmoid(jnp.concatenate(gs, axis=1) + big_ref[...])
    log_a = -C_RG * r * _softplus(-lam_ref[...])
    a = jnp.exp(log_a)
    mult = jnp.sqrt(-jnp.tanh(log_a) * (a * a + 1.0))
    return a, mult * (i * xc)


def _glu_rglru_prompt_kernel(g_ref, gcol_ref, w_ref, b_ref,
                             xa_ref, ga_ref, cw_ref, cb_ref, wrg_ref, brg_ref, wig_ref, big_ref, lam_ref,
                             ya_all_ref, o_ref, ya_ref, hl_ref, g_scr, xcars, hcars, *, tiles_per_seq):
    del ya_all_ref
    i, j = pl.program_id(0), pl.program_id(1)

    @pl.when(j == 0)
    def _():
        g_scr[...] = g_ref[...].astype(g_scr.dtype)

    _rglru_prompt_tile(xa_ref, ga_ref, cw_ref, cb_ref, wrg_ref, brg_ref, wig_ref, big_ref, lam_ref,
                       ya_ref, hl_ref, xcars.at[j], hcars.at[j], i % tiles_per_seq == 0)
    s = jnp.dot(g_scr[...], w_ref[...].astype(BF16), preferred_element_type=F32) + b_ref[...]
    o_ref[...] = (gcol_ref[...] * _sigmoid(s)).astype(o_ref.dtype)


def _rglru_prompt_tile(xa_ref, ga_ref, cw_ref, cb_ref, wrg_ref, brg_ref, wig_ref, big_ref, lam_ref,
                       ya_ref, hl_ref, xcar, hcar, starts_sequence):
    tm = xa_ref.shape[0]

    @pl.when(starts_sequence)
    def _():
        xcar[...] = jnp.zeros_like(xcar)
        hcar[...] = jnp.zeros_like(hcar)

    x = xa_ref[...]
    xfull = jnp.concatenate([xcar[...], x], axis=0)
    xcar[...] = x[tm - SUBLANES:, :]
    cw = cw_ref[...]
    xc = cb_ref[...]
    for k in range(CONV_W - 1):
        xs = pltpu.roll(xfull, CONV_W - 1 - k, 0)[SUBLANES:, :]
        xc = xc + xs * cw[k:k + 1, :]
    xc = xc + x * cw[CONV_W - 1:CONV_W, :]

    a, b = _lru_gates(xc, wrg_ref, brg_ref, wig_ref, big_ref, lam_ref)
    cb = a.shape[1]
    n_slab = tm // SUBLANES
    a = a.reshape(n_slab, SUBLANES, cb)
    b = b.reshape(n_slab, SUBLANES, cb)
    sub = lax.broadcasted_iota(jnp.int32, (1, SUBLANES, cb), 1)
    d = 1
    while d < SUBLANES:
        m = sub >= d
        a_s = jnp.where(m, pltpu.roll(a, d, 1), 1.0)
        b_s = jnp.where(m, pltpu.roll(b, d, 1), 0.0)
        b = a * b_s + b
        a = a * a_s
        d *= 2
    h = hcar[...]
    hs = []
    for s in range(n_slab):
        h_slab = b[s] + a[s] * jnp.broadcast_to(h, (SUBLANES, cb))
        h = h_slab[SUBLANES - 1:SUBLANES, :]
        hs.append(h_slab)
    hcar[...] = h
    hl_ref[...] = h
    ya_ref[...] = (jnp.concatenate(hs, axis=0) * jax.nn.gelu(ga_ref[...])).astype(ya_ref.dtype)


def _rglru_sample_kernel(xa_ref, ga_ref, cst_ref, h0_ref, cw_ref, cb_ref, wrg_ref, brg_ref, wig_ref,
                         big_ref, lam_ref, ya_all_ref, ya_ref, hl_ref, *, n_seq, n_steps):
    del ya_all_ref
    x = xa_ref[...]
    xp = jnp.concatenate([cst_ref[k] for k in range(CONV_W - 1)] + [x], axis=0)
    rows = n_seq * n_steps
    cw = cw_ref[...]
    xc = cb_ref[...]
    for k in range(CONV_W):
        xc = xc + xp[k * n_seq:k * n_seq + rows, :] * cw[k:k + 1, :]
    a, b = _lru_gates(xc, wrg_ref, brg_ref, wig_ref, big_ref, lam_ref)
    h = h0_ref[...]
    for t in range(n_steps):
        sl = slice(t * n_seq, (t + 1) * n_seq)
        h = a[sl, :] * h + b[sl, :]
        ya_ref[sl, :] = (h * jax.nn.gelu(ga_ref[sl, :])).astype(ya_ref.dtype)
    hl_ref[...] = h


def _s5_disc_kernel(are_ref, aim_ref, ldt_ref, abr_ref, abi_ref, cfr_ref, cfi_ref):
    a_re = are_ref[...]
    a_im = aim_ref[...]
    dt = jnp.exp(ldt_ref[...])
    mag = jnp.exp(a_re * dt)
    abr = mag * jnp.cos(a_im * dt)
    abi = mag * jnp.sin(a_im * dt)
    den = a_re * a_re + a_im * a_im
    nr = abr - 1.0
    abr_ref[...] = abr
    abi_ref[...] = abi
    cr = (nr * a_re + abi * a_im) / den
    ci = (abi * a_re - nr * a_im) / den
    for q in range(cfr_ref.shape[0]):
        cfr_ref[q] = cr
        cfi_ref[q] = ci
        cr, ci = abr * cr - abi * ci, abr * ci + abi * cr


def _gates_s5_prompt_kernel(u_ref, w_ref, xb_ref, wb_ref, wc_ref, ar_ref, ai_ref, d_ref, g_all_ref,
                            gl_ref, g_ref, sre_ref, sim_ref, cre, cim):
    del g_all_ref
    n_par, _, hw = ar_ref.shape

    @pl.when(pl.program_id(2) == 0)
    def _():
        cre[...] = jnp.zeros_like(cre)
        cim[...] = jnp.zeros_like(cim)

    split = (gl_ref.shape[0] * 9 // 16) // (2 * SUBLANES) * (2 * SUBLANES)
    w = w_ref[...].astype(BF16)
    bus = [_s5_project_in(xb_ref[:, p * LANES:(p + 1) * LANES], wb_ref.at[p]) for p in range(n_par)]
    gl_ref[:split, :] = jnp.dot(u_ref[:split, :], w, preferred_element_type=F32)
    scans = [_s5_slab_scan(bus[p], ar_ref[p], ai_ref[p], cre[p], cim[p]) for p in range(n_par)]
    y = jnp.dot(jnp.concatenate([s[0] for s in scans], axis=1), wc_ref[...], preferred_element_type=F32)
    gl_ref[split:, :] = jnp.dot(u_ref[split:, :], w, preferred_element_type=F32)
    g_ref[...] = jax.nn.gelu(y + d_ref[...] * xb_ref[...])
    for p in range(n_par):
        _, cr, ci = scans[p]
        cre[p] = cr
        cim[p] = ci
        sre_ref[:, p * hw:(p + 1) * hw] = cr
        sim_ref[:, p * hw:(p + 1) * hw] = ci


def _s5_project_in(xb, wb_ref):
    tm, width = xb.shape
    n_slab = tm // SUBLANES
    n_lags = wb_ref.shape[0] // width
    xb3 = xb.reshape(n_slab, SUBLANES, width)
    subx = lax.broadcasted_iota(jnp.int32, (1, SUBLANES, width), 1)
    lagged = [xb3] + [jnp.where(subx >= q, pltpu.roll(xb3, q, 1), 0.0) for q in range(1, n_lags)]
    xs = jnp.concatenate(lagged, axis=2).reshape(tm, n_lags * width)
    return jnp.dot(xs.astype(BF16), wb_ref[...], preferred_element_type=F32), n_lags


def _s5_slab_scan(bu_lags, ar, ai, cr, ci):
    bu, n_lags = bu_lags
    tm = bu.shape[0]
    hw = ar.shape[1]
    n_slab = tm // SUBLANES
    sub = lax.broadcasted_iota(jnp.int32, (SUBLANES, hw), 0)
    pr, pi = ar, ai
    tr = jnp.where(sub == 0, pr, 0.0)
    ti = jnp.where(sub == 0, pi, 0.0)
    steps = []
    d = 1
    while d < SUBLANES:
        mr = jnp.where(sub >= d, pr, 0.0)
        mi = jnp.where(sub >= d, pi, 0.0)
        if d >= n_lags:
            steps.append((d, mr, mi))
        sr = pltpu.roll(tr, d, 0)
        si = pltpu.roll(ti, d, 0)
        tr, ti = tr + (mr * sr - mi * si), ti + (mr * si + mi * sr)
        pr, pi = pr * pr - pi * pi, 2.0 * (pr * pi)
        d *= 2
    slabs = []
    for s in range(n_slab):
        rows = slice(s * SUBLANES, (s + 1) * SUBLANES)
        hr = bu[rows, :hw]
        hi = bu[rows, hw:]
        for d, mr, mi in steps:
            sr = pltpu.roll(hr, d, 0)
            si = pltpu.roll(hi, d, 0)
            hr, hi = hr + (mr * sr - mi * si), hi + (mr * si + mi * sr)
        cbr = jnp.broadcast_to(cr, (SUBLANES, hw))
        cbi = jnp.broadcast_to(ci, (SUBLANES, hw))
        hr, hi = hr + (tr * cbr - ti * cbi), hi + (tr * cbi + ti * cbr)
        cr = hr[SUBLANES - 1:SUBLANES, :]
        ci = hi[SUBLANES - 1:SUBLANES, :]
        slabs.append(jnp.concatenate([hr, hi], axis=1))
    return jnp.concatenate(slabs, axis=0).astype(BF16), cr, ci


def _s5_sample_kernel(xb_ref, wb_ref, wc_ref, ar_ref, ai_ref, d_ref, s0r_ref, s0i_ref, g_all_ref,
                      g_ref, sre_ref, sim_ref, bu_scr, h_scr, *, n_seq, n_steps):
    del g_all_ref
    hw = ar_ref.shape[1]
    xb = xb_ref[...]
    bu_scr[...] = jnp.dot(xb.astype(BF16), wb_ref[...], preferred_element_type=F32)
    ar = ar_ref[...]
    ai = ai_ref[...]
    hr = s0r_ref[...]
    hi = s0i_ref[...]
    for t in range(n_steps):
        sl = slice(t * n_seq, (t + 1) * n_seq)
        hr, hi = (ar * hr - ai * hi) + bu_scr[sl, :hw], (ar * hi + ai * hr) + bu_scr[sl, hw:]
        h_scr[sl, :hw] = hr.astype(h_scr.dtype)
        h_scr[sl, hw:] = hi.astype(h_scr.dtype)
    sre_ref[...] = hr
    sim_ref[...] = hi
    y = jnp.dot(h_scr[...], wc_ref[...], preferred_element_type=F32)
    g_ref[...] = jax.nn.gelu(y + d_ref[...] * xb)


def _half_ffn_matmuls(h, wg, wu, wd, *, bm):
    m, dm = h.shape
    dff = wg.shape[1]
    bn_up = _pick(dff, (256, 128))
    n_i, n_j = m // bm, dff // bn_up
    rb = _pick(dff, tuple(r for r in (64, 128, 256, 512) if n_i * n_j >= dff // r))
    n_rb = dff // rb
    assert n_i * n_j >= n_rb, "not enough up-projection steps to cover the down-projection weight"
    wd_blk = lambda i, j: (jnp.minimum(i * n_j + j, n_rb - 1), 0)
    a, wd = pl.pallas_call(
        _ffn_up_kernel,
        grid=(n_i, n_j),
        in_specs=[pl.BlockSpec((bm, dm), lambda i, j: (i, 0)),
                  pl.BlockSpec((dm, bn_up), lambda i, j: (0, j)),
                  pl.BlockSpec((dm, bn_up), lambda i, j: (0, j)),
                  pl.BlockSpec((rb, dm), wd_blk)],
        out_specs=[pl.BlockSpec((bm, bn_up), lambda i, j: (i, j)),
                   pl.BlockSpec((rb, dm), wd_blk)],
        out_shape=[jax.ShapeDtypeStruct((m, dff), BF16), jax.ShapeDtypeStruct((dff, dm), BF16)],
        compiler_params=_cparams("arbitrary", "arbitrary"),
        name="ffn_up",
    )(h, wg, wu, wd)

    bm_d = _pick(m, (1024, 512, 256, 128))
    bn_d = _pick(dm, (1024, 512, 256, 128))
    bk = dff // 2 if (dff // 2) % LANES == 0 else dff
    return pl.pallas_call(
        _ffn_down_kernel,
        grid=(m // bm_d, dm // bn_d, dff // bk),
        in_specs=[pl.BlockSpec((bm_d, bk), lambda i, j, k: (i, k)),
                  pl.BlockSpec((bk, bn_d), lambda i, j, k: (k, j))],
        out_specs=pl.BlockSpec((bm_d, bn_d), lambda i, j, k: (i, j)),
        out_shape=jax.ShapeDtypeStruct((m, dm), F32),
        compiler_params=_cparams("parallel", "parallel", "arbitrary"),
        name="ffn_down",
    )(a, wd)


def _matmul(x, w, *, bm, bn, out_dtype, name, n=None):
    m, k = x.shape
    n = w.shape[1] if n is None else n
    return pl.pallas_call(
        _mm_kernel,
        grid=(m // bm, n // bn),
        in_specs=[pl.BlockSpec((bm, k), lambda i, j: (i, 0)),
                  pl.BlockSpec((k, bn), lambda i, j: (0, j))],
        out_specs=pl.BlockSpec((bm, bn), lambda i, j: (i, j)),
        out_shape=jax.ShapeDtypeStruct((m, n), out_dtype),
        compiler_params=_cparams("parallel", "arbitrary"),
        name=name,
    )(x, w)


def kernel(x_prompt, x_sample, state_lru_h, state_conv, state_ssm_re, state_ssm_im, ffn1_pre_g, ffn1_post_g, ffn1_w_gate, ffn1_w_up, ffn1_w_down, mix_pre_g, mix_post_g, w_in, conv_w, conv_b, w_rg, b_rg, w_ig, b_ig, lru_lambda, ssm_a_re, ssm_a_im, ssm_log_dt, ssm_b_re, ssm_b_im, ssm_c_re, ssm_c_im, ssm_d, w_glu, b_glu, w_out_a, w_out_b, w_o, ffn2_pre_g, ffn2_post_g, ffn2_w_gate, ffn2_w_up, ffn2_w_down):
    nb, seq, dm = x_prompt.shape
    db, dseq, _ = x_sample.shape
    depth = state_lru_h.shape[0]
    assert depth == 1, "one decoder layer"
    d_lru = state_lru_h.shape[2]
    n_grp, ssm_p = state_ssm_re.shape[2], state_ssm_re.shape[3]
    d_ssm = ssm_d.shape[1]
    ssm_cg = d_ssm // n_grp
    d_in = w_in.shape[2]
    assert w_rg.shape[2] == LRU_BLK and conv_w.shape[1] == CONV_W
    assert d_in == 2 * d_lru + d_ssm + 2 * dm and seq >= CONV_W - 1 and dseq >= CONV_W - 1

    mp = nb * seq
    ms = db * dseq
    m = mp + ms
    sdt = state_lru_h.dtype

    row2 = lambda v: v.reshape(1, -1)
    bf = lambda v: v[0].astype(BF16)

    xp2 = x_prompt.reshape(mp, dm)
    xs2 = jnp.swapaxes(x_sample, 0, 1).reshape(ms, dm)

    tr = _pick(math.gcd(mp, ms), (256, 128, 64, 32, 16, 8))
    npt = mp // tr
    xp_spec = pl.BlockSpec((tr, dm), lambda i: (jnp.minimum(i, npt - 1), 0))
    xs_spec = pl.BlockSpec((tr, dm), lambda i: (jnp.maximum(i - npt, 0), 0))
    row_spec = pl.BlockSpec((tr, dm), lambda i: (i, 0))
    g_spec = pl.BlockSpec((1, dm), lambda i: (0, 0))

    tr2 = _pick(math.gcd(mp, ms), (512, 256, 128, 64, 32, 16, 8))
    npt2 = mp // tr2
    h1 = pl.pallas_call(
        functools.partial(_prenorm_kernel, n_prompt_tiles=npt2),
        grid=(m // tr2,),
        in_specs=[pl.BlockSpec((tr2, dm), lambda i: (jnp.minimum(i, npt2 - 1), 0)),
                  pl.BlockSpec((tr2, dm), lambda i: (jnp.maximum(i - npt2, 0), 0)), g_spec],
        out_specs=pl.BlockSpec((tr2, dm), lambda i: (i, 0)),
        out_shape=jax.ShapeDtypeStruct((m, dm), BF16),
        compiler_params=_cparams("parallel"),
        name="prenorm1",
    )(xp2, xs2, row2(ffn1_pre_g))

    bm = _pick(m, (1536, 1024, 768, 512, 384, 256, 128))
    f1 = _half_ffn_matmuls(h1, ffn1_w_gate[0], ffn1_w_up[0], ffn1_w_down[0], bm=bm)

    x1, u = pl.pallas_call(
        functools.partial(_post_ffn1_kernel, n_prompt_tiles=npt),
        grid=(m // tr,),
        in_specs=[xp_spec, xs_spec, row_spec, g_spec, g_spec],
        out_specs=[row_spec, row_spec],
        out_shape=[jax.ShapeDtypeStruct((m, dm), F32), jax.ShapeDtypeStruct((m, dm), BF16)],
        compiler_params=_cparams("parallel"),
        name="post_ffn1",
    )(xp2, xs2, f1, row2(ffn1_post_g), row2(mix_pre_g))

    bn_in = _pick(math.gcd(d_lru, dm), (512, 256, 128))
    n_a = 2 * d_lru + d_ssm
    z = _matmul(u, w_in[0], bm=bm, bn=bn_in, out_dtype=F32, name="in_proj", n=n_a)

    cb = _pick(d_lru, (512, 256, 128))
    ncb = d_lru // cb
    hpb = cb // LRU_BLK
    wrg = bf(w_rg)
    wig = bf(w_ig)
    lru_params = (conv_w[0], row2(conv_b), wrg, row2(b_rg), wig, row2(b_ig), row2(lru_lambda))
    assert d_lru <= dm and d_ssm <= dm
    any_spec = pl.BlockSpec(memory_space=pl.ANY)
    assert mp % ms == 0, "sample rows must tile the unified row axis"
    s_blk = mp // ms

    n_lags = 2
    abr, abi, cfr, cfi = pl.pallas_call(
        _s5_disc_kernel,
        out_shape=[jax.ShapeDtypeStruct((n_grp, ssm_p), F32)] * 2
                  + [jax.ShapeDtypeStruct((n_lags, n_grp, ssm_p), F32)] * 2,
        name="s5_discretise",
    )(ssm_a_re[0], ssm_a_im[0], ssm_log_dt[0].reshape(n_grp, 1))
    bb_re = cfr[..., None] * ssm_b_re - cfi[..., None] * ssm_b_im
    bb_im = cfr[..., None] * ssm_b_im + cfi[..., None] * ssm_b_re

    gpb = LANES // ssm_cg
    nj = n_grp // gpb
    hw = gpb * ssm_p
    bb =jnp.stack([bb_re, bb_im]).reshape(2, n_lags, nj, gpb, ssm_p, ssm_cg).astype(BF16)
    bb = jnp.transpose(bb, (2, 1, 3, 5, 0, 4)).reshape(nj, n_lags * LANES, 2, ssm_p)
    assert LANES % ssm_p == 0 and gpb % (LANES // ssm_p) == 0
    hpl = LANES // ssm_p
    bb = jnp.concatenate([bb] * hpl, axis=3)
    wb = jnp.concatenate([bb[:, :, ri, :] for ri in range(2) for _ in range(gpb // hpl)], axis=2)
    wb_shape = (1, n_lags * LANES, 2 * hw)
    g_row = (lax.broadcasted_iota(jnp.int32, wb_shape, 1) // ssm_cg) % gpb
    h_col = (lax.broadcasted_iota(jnp.int32, wb_shape, 2) // ssm_p) % gpb
    wb = jnp.where(g_row == h_col, wb, jnp.zeros((), BF16))
    cc = jnp.stack([ssm_c_re[0], -ssm_c_im[0]]).reshape(2, nj, gpb, ssm_cg, ssm_p).astype(BF16)
    cc = jnp.transpose(cc, (1, 0, 4, 2, 3)).reshape(nj, 2, ssm_p, LANES)
    wc = jnp.concatenate([cc[:, ri, :, :] for ri in range(2) for _ in range(gpb)], axis=1)
    wc_shape = (1, 2 * hw, LANES)
    wc_mask = ((lax.broadcasted_iota(jnp.int32, wc_shape, 1) // ssm_p) % gpb
               == lax.broadcasted_iota(jnp.int32, wc_shape, 2) // ssm_cg)
    wc = jnp.where(wc_mask, wc, jnp.zeros((), BF16))
    abr3 = abr.reshape(nj, 1, hw)
    abi3 = abi.reshape(nj, 1, hw)
    xb_blk = (2 * d_lru) // LANES
    dsk = row2(ssm_d)

    n_par = 2 if nj % 2 == 0 and xb_blk % 2 == 0 else 1
    par_shape = (1, n_par * 2 * hw, n_par * LANES)
    par_mask = (lax.broadcasted_iota(jnp.int32, par_shape, 1) // (2 * hw)
                == lax.broadcasted_iota(jnp.int32, par_shape, 2) // LANES)
    wc_par = jnp.concatenate([wc.reshape(nj // n_par, n_par * 2 * hw, LANES)] * n_par, axis=2)
    wc_par = jnp.where(par_mask, wc_par, jnp.zeros((), BF16))
    n_gi = nj // n_par
    bm_gl = m // n_gi
    assert m % n_gi == 0 and bm_gl % (2 * SUBLANES) == 0

    def gl_cols_ok(t):
        bn = (2 * dm) // (nb * (seq // t)) if (2 * dm) % (nb * (seq // t)) == 0 else 0
        return bn > 0 and bn % LANES == 0 and n_a % bn == 0

    tm5 = _pick(seq, tuple(t for t in (512, 256, 128, 64, 32, 16, 8) if seq % t == 0 and gl_cols_ok(t)))
    rp5 = seq // tm5
    bn_gl = (2 * dm) // (nb * rp5)
    gl_blk = n_a // bn_gl
    gl, g_all, sre_p, sim_p = pl.pallas_call(
        _gates_s5_prompt_kernel,
        grid=(n_gi, nb, rp5),
        in_specs=[pl.BlockSpec((bm_gl, dm), lambda j, n, r: (j, 0)),
                  pl.BlockSpec((dm, bn_gl), lambda j, n, r: (0, gl_blk + n * rp5 + r)),
                  pl.BlockSpec((tm5, n_par * LANES), lambda j, n, r: (n * rp5 + r, xb_blk // n_par + j)),
                  pl.BlockSpec((n_par, n_lags * LANES, 2 * hw), lambda j, n, r: (j, 0, 0)),
                  pl.BlockSpec((None, n_par * 2 * hw, n_par * LANES), lambda j, n, r: (j, 0, 0)),
                  pl.BlockSpec((n_par, 1, hw), lambda j, n, r: (j, 0, 0)),
                  pl.BlockSpec((n_par, 1, hw), lambda j, n, r: (j, 0, 0)),
                  pl.BlockSpec((1, n_par * LANES), lambda j, n, r: (0, j)), any_spec],
        out_specs=[pl.BlockSpec((bm_gl, bn_gl), lambda j, n, r: (j, n * rp5 + r)),
                   pl.BlockSpec((tm5, n_par * LANES), lambda j, n, r: (n * rp5 + r, j)),
                   pl.BlockSpec((None, 1, n_par * hw), lambda j, n, r: (n, 0, j)),
                   pl.BlockSpec((None, 1, n_par * hw), lambda j, n, r: (n, 0, j))],
        out_shape=[jax.ShapeDtypeStruct((m, 2 * dm), F32),
                   jax.ShapeDtypeStruct((m, dm), F32),
                   jax.ShapeDtypeStruct((nb, 1, n_grp * ssm_p), F32),
                   jax.ShapeDtypeStruct((nb, 1, n_grp * ssm_p), F32)],
        scratch_shapes=[pltpu.VMEM((n_par, 1, hw), F32), pltpu.VMEM((n_par, 1, hw), F32)],
        input_output_aliases={8: 1},
        compiler_params=_cparams("arbitrary", "arbitrary", "arbitrary"),
        name="gates_s5_prompt",
    )(u, w_in[0], z, wb, wc_par, abr3, abi3, dsk, f1)

    s5_w_specs1 = [pl.BlockSpec((None, LANES, 2 * hw), lambda j: (j, 0, 0)),
                   pl.BlockSpec((None, 2 * hw, LANES), lambda j: (j, 0, 0)),
                   pl.BlockSpec((None, 1, hw), lambda j: (j, 0, 0)),
                   pl.BlockSpec((None, 1, hw), lambda j: (j, 0, 0)),
                   pl.BlockSpec((1, LANES), lambda j: (0, j))]
    g_all, sre_s, sim_s = pl.pallas_call(
        functools.partial(_s5_sample_kernel, n_seq=db, n_steps=dseq),
        grid=(nj,),
        in_specs=[pl.BlockSpec((ms, LANES), lambda j: (s_blk, xb_blk + j))] + s5_w_specs1
                 + [pl.BlockSpec((db, hw), lambda j: (0, j)), pl.BlockSpec((db, hw), lambda j: (0, j)), any_spec],
        out_specs=[pl.BlockSpec((ms, LANES), lambda j: (s_blk, j)),
                   pl.BlockSpec((db, hw), lambda j: (0, j)),
                   pl.BlockSpec((db, hw), lambda j: (0, j))],
        out_shape=[jax.ShapeDtypeStruct((m, dm), F32),
                   jax.ShapeDtypeStruct((db, n_grp * ssm_p), F32),
                   jax.ShapeDtypeStruct((db, n_grp * ssm_p), F32)],
        scratch_shapes=[pltpu.VMEM((ms, 2 * hw), F32), pltpu.VMEM((ms, 2 * hw), BF16)],
        input_output_aliases={8: 0},
        compiler_params=_cparams("parallel"),
        name="s5_sample",
    )(z, wb, wc, abr3, abi3, dsk,
      state_ssm_re[0].reshape(db, n_grp * ssm_p), state_ssm_im[0].reshape(db, n_grp * ssm_p), g_all)

    assert d_ssm // cb == ncb
    tm = _pick(seq, tuple(t for t in (512, 256, 128, 64, 32, 16, 8)
                          if seq % t == 0 and (m * t) % mp == 0 and ((m * t) // mp) % (2 * SUBLANES) == 0))
    rpt = seq // tm
    bm_u = (m * tm) // mp
    col = lambda shape: pl.BlockSpec(shape, lambda i, j: (0, j))
    head_spec = pl.BlockSpec((hpb, LRU_BLK, LRU_BLK), lambda i, j: (j, 0, 0))
    yb, ya, hl_p = pl.pallas_call(
        functools.partial(_glu_rglru_prompt_kernel, tiles_per_seq=rpt),
        grid=(nb * rpt, ncb),
        in_specs=[pl.BlockSpec((bm_u, d_ssm), lambda i, j: (i, 0)),
                  pl.BlockSpec((bm_u, cb), lambda i, j: (i, j)),
                  pl.BlockSpec((d_ssm, cb), lambda i, j: (0, j)),
                  col((1, cb)),
                  pl.BlockSpec((tm, cb), lambda i, j: (i, j)),
                  pl.BlockSpec((tm, cb), lambda i, j: (i, ncb + j)),
                  col((CONV_W, cb)), col((1, cb)), head_spec, col((1, cb)), head_spec, col((1, cb)),
                  col((1, cb)), any_spec],
        out_specs=[pl.BlockSpec((bm_u, cb), lambda i, j: (i, j)),
                   pl.BlockSpec((tm, cb), lambda i, j: (i, j)),
                   pl.BlockSpec((None, 1, cb), lambda i, j: (i, 0, j))],
        out_shape=[jax.ShapeDtypeStruct((m, d_ssm), BF16),
                   jax.ShapeDtypeStruct((m, dm), BF16),
                   jax.ShapeDtypeStruct((nb * rpt, 1, d_lru), F32)],
        scratch_shapes=[pltpu.VMEM((bm_u, d_ssm), BF16),
                        pltpu.VMEM((ncb, SUBLANES, cb), F32), pltpu.VMEM((ncb, 1, cb), F32)],
        input_output_aliases={4 + 2 + len(lru_params): 1},
        compiler_params=_cparams("arbitrary", "arbitrary"),
        name="glu_rglru_prompt",
    )(g_all, g_all, w_glu[0], row2(b_glu), z, z, *lru_params, h1)

    par1 = lambda shape: pl.BlockSpec(shape, lambda c: (0, c))
    cst = jnp.swapaxes(state_conv[0], 0, 1)
    ya, hl_s = pl.pallas_call(
        functools.partial(_rglru_sample_kernel, n_seq=db, n_steps=dseq),
        grid=(ncb,),
        in_specs=[pl.BlockSpec((ms, cb), lambda c: (s_blk, c)),
                  pl.BlockSpec((ms, cb), lambda c: (s_blk, ncb + c)),
                  pl.BlockSpec((CONV_W - 1, db, cb), lambda c: (0, 0, c)),
                  par1((db, cb)),
                  par1((CONV_W, cb)), par1((1, cb)),
                  pl.BlockSpec((hpb, LRU_BLK, LRU_BLK), lambda c: (c, 0, 0)), par1((1, cb)),
                  pl.BlockSpec((hpb, LRU_BLK, LRU_BLK), lambda c: (c, 0, 0)), par1((1, cb)),
                  par1((1, cb)), any_spec],
        out_specs=[pl.BlockSpec((ms, cb), lambda c: (s_blk, c)), par1((db, cb))],
        out_shape=[jax.ShapeDtypeStruct((m, dm), BF16), jax.ShapeDtypeStruct((db, d_lru), F32)],
        input_output_aliases={4 + len(lru_params): 0},
        compiler_params=_cparams("parallel"),
        name="rglru_sample",
    )(z, z, cst, state_lru_h[0], *lru_params, ya)

    bm_g = _pick(m, (1024, 512, 256, 128))
    bn_m = bn_in
    gla_blk = 0
    glb_blk = dm // bn_m
    merged = pl.pallas_call(
        _merge_kernel,
        grid=(m // bm_g, dm // bn_m),
        in_specs=[pl.BlockSpec((bm_g, d_lru), lambda i, j: (i, 0)),
                  pl.BlockSpec((bm_g, d_ssm), lambda i, j: (i, 0)),
                  pl.BlockSpec((d_lru, bn_m), lambda i, j: (0, j)),
                  pl.BlockSpec((d_ssm, bn_m), lambda i, j: (0, j)),
                  pl.BlockSpec((bm_g, bn_m), lambda i, j: (i, gla_blk + j)),
                  pl.BlockSpec((bm_g, bn_m), lambda i, j: (i, glb_blk + j))],
        out_specs=pl.BlockSpec((bm_g, bn_m), lambda i, j: (i, j)),
        out_shape=jax.ShapeDtypeStruct((m, dm), BF16),
        compiler_params=_cparams("parallel", "arbitrary"),
        name="gated_merge",
    )(ya, yb, w_out_a[0], w_out_b[0], gl, gl)

    o = _matmul(merged, w_o[0], bm=bm, bn=bn_in, out_dtype=F32, name="o_proj")

    x2, h2 = pl.pallas_call(
        _post_mix_kernel,
        grid=(m // tr,),
        in_specs=[row_spec, row_spec, g_spec, g_spec],
        out_specs=[row_spec, row_spec],
        out_shape=[jax.ShapeDtypeStruct((m, dm), F32), jax.ShapeDtypeStruct((m, dm), BF16)],
        compiler_params=_cparams("parallel"),
        name="post_mix",
    )(x1, o, row2(mix_post_g), row2(ffn2_pre_g))

    f2 = _half_ffn_matmuls(h2, ffn2_w_gate[0], ffn2_w_up[0], ffn2_w_down[0], bm=bm)

    def final(rows, first_tile):
        return pl.pallas_call(
            _final_kernel,
            grid=(rows // tr2,),
            in_specs=[pl.BlockSpec((tr2, dm), lambda i: (first_tile + i, 0)),
                      pl.BlockSpec((tr2, dm), lambda i: (first_tile + i, 0)),
                      g_spec],
            out_specs=pl.BlockSpec((tr2, dm), lambda i: (i, 0)),
            out_shape=jax.ShapeDtypeStruct((rows, dm), F32),
            compiler_params=_cparams("parallel"),
            name="final_residual",
        )(x2, f2, row2(ffn2_post_g))

    y_prompt = final(mp, 0).reshape(nb, seq, dm)
    y_sample = jnp.swapaxes(final(ms, npt2).reshape(dseq, db, dm), 0, 1)

    nk = CONV_W - 1
    prompt_conv = jnp.stack([lax.slice(z, ((n + 1) * seq - nk, 0), ((n + 1) * seq, d_lru)) for n in range(nb)])
    sample_conv = jnp.swapaxes(lax.slice(z, (m - nk * db, 0), (m, d_lru)).reshape(nk, db, d_lru), 0, 1)
    st = lambda v, n: v.reshape(1, n, n_grp, ssm_p).astype(sdt)
    return (y_prompt, y_sample,
            hl_p.reshape(nb, rpt, d_lru)[:, rpt - 1][None].astype(sdt), prompt_conv[None].astype(sdt),
            st(sre_p, nb), st(sim_p, nb),
            hl_s.reshape(1, db, d_lru).astype(sdt), sample_conv[None].astype(sdt),
            st(sre_s, db), st(sim_s, db))
```
